```python
import math
import jax
import jax.numpy as jnp
from jax import lax
import numpy as np

D_MODEL = 1024
BATCH = 8
SEQ = 2048
DEPTH = 4

GRID_W = 64
CTX_LEN = 256
EPS = 1e-6

GROUP_W = D_MODEL // 4
D_MIX = 4 * GROUP_W
HEAD_DIM = 64
N_HEADS = GROUP_W // HEAD_DIM

HGRN_DK = 64
HGRN_DV = 64
HGRN_CHUNK = 64

HYENA_ORDER = 2
HYENA_SHORT = 3
HYENA_BANDS = 16
HYENA_EMB = 1 + 2 * HYENA_BANDS
HYENA_FFN = 64
HYENA_DECAY_MIN = 3.07
HYENA_DECAY_MAX = 15.35

NA_ROWS = 8
NA_COLS = 16

MLA_Q_RANK = 256
MLA_KV_RANK = 128
MLA_NOPE = 64
MLA_ROPE = 32
MLA_V = 64
MLA_QBLOCK = 128
ROPE_BASE = 10000.0

N_GROUPS = 4
EXPERTS_PER_GROUP = 8
N_EXPERTS = N_GROUPS * EXPERTS_PER_GROUP
TOP_K = 2
D_EXPERT = 512
MOE_BLOCK = 128

IN_SPLITS = (GROUP_W, GROUP_W, GROUP_W, GROUP_W, GROUP_W, 3 * GROUP_W, 3 * GROUP_W, MLA_Q_RANK, MLA_KV_RANK, MLA_ROPE)
D_IN = sum(IN_SPLITS)

kernel_name = "hybrid_parallel_heads_diffusion_trunk"


def rmsnorm(x, g):
    xf = x.astype(jnp.float32)
    y = xf * lax.rsqrt(jnp.mean(xf * xf, axis=-1, keepdims=True) + EPS)
    return (y * g.astype(jnp.float32)).astype(x.dtype)


def _heads(a, d):
    return a.reshape(a.shape[0], a.shape[1], -1, d)


def dense_attention(q, k, v):
    s = jnp.einsum('bqhd,bkhd->bhqk', q.astype(jnp.float32), k.astype(jnp.float32)) * (q.shape[-1] ** -0.5)
    p = jax.nn.softmax(s, axis=-1)
    return jnp.einsum('bhqk,bkhd->bqhd', p, v.astype(jnp.float32)).astype(v.dtype)


def blocked_attention(q, k, v):
    B, L, H, dq = q.shape
    nb = L // MLA_QBLOCK
    qb = q.reshape(B, nb, MLA_QBLOCK, H, dq).swapaxes(0, 1)
    o = lax.map(lambda qi: dense_attention(qi, k, v), qb)
    return o.swapaxes(0, 1).reshape(B, L, H, v.shape[-1])


def rope_2d(x):
    L = x.shape[1]
    t = jnp.arange(L)
    row = (t // GRID_W).astype(jnp.float32)
    col = (t % GRID_W).astype(jnp.float32)
    half = MLA_ROPE // 2
    inv = ROPE_BASE ** (-jnp.arange(0, half, 2, dtype=jnp.float32) / half)

    def rot(part, pos):
        ang = pos[:, None] * inv[None, :]
        cos = jnp.cos(ang)[None, :, None, :]
        sin = jnp.sin(ang)[None, :, None, :]
        a, b = part[..., :half // 2], part[..., half // 2:]
        return jnp.concatenate([a * cos - b * sin, a * sin + b * cos], axis=-1)

    xf = x.astype(jnp.float32)
    return jnp.concatenate([rot(xf[..., :half], row), rot(xf[..., half:], col)], axis=-1).astype(x.dtype)


def hgrn_gates(z, lb):
    zf = z.astype(jnp.float32)
    lbf = lb.astype(jnp.float32)
    logf = jnp.logaddexp(jnp.log(lbf), jnp.log1p(-lbf) + jax.nn.log_sigmoid(zf))
    k = -jnp.expm1(logf)
    return _heads(logf, HGRN_DK), _heads(k, HGRN_DK)


def hgrn_chunk_scan(q, k, logf, v, s0):
    B, L, H, dk = q.shape
    dv = v.shape[-1]
    C = HGRN_CHUNK
    n = L // C

    def chunks(a):
        return a.reshape(B, n, C, H, a.shape[-1]).transpose(1, 0, 3, 2, 4)

    causal = jnp.tril(jnp.ones((C, C), dtype=bool))

    def step(S, inp):
        qc, kc, fc, vc = inp
        b = jnp.cumsum(fc, axis=2)
        o_inter = jnp.einsum('bhtk,bhkv->bhtv', qc * jnp.exp(b), S)
        diff = b[:, :, :, None, :] - b[:, :, None, :, :]
        decay = jnp.exp(jnp.where(causal[None, None, :, :, None], diff, -jnp.inf))
        attn = jnp.einsum('bhtk,bhsk,bhtsk->bhts', qc, kc, decay)
        o_intra = jnp.einsum('bhts,bhsv->bhtv', attn, vc)
        b_last = b[:, :, -1:, :]
        S_new = jnp.exp(b_last[:, :, 0, :])[..., None] * S + jnp.einsum('bhsk,bhsv->bhkv', kc * jnp.exp(b_last - b), vc)
        return S_new, o_inter + o_intra

    S_fin, o = lax.scan(step, s0, (chunks(q), chunks(k), chunks(logf), chunks(v)))
    return o.transpose(1, 0, 3, 2, 4).reshape(B, L, H, dv), S_fin


def hgrn_readout(o, g, norm_g):
    B, L = g.shape[0], g.shape[1]
    o = rmsnorm(o, norm_g).reshape(B, L, -1)
    return (o * jax.nn.silu(g.astype(jnp.float32))).astype(g.dtype)


def hgrn_mixer(q, zf, zb, i, g, qc, zfc, zbc, ic, gc, lb, norm_g, need_ctx):
    f32 = jnp.float32
    B = q.shape[0]
    qh, ih = _heads(q.astype(f32), HGRN_DK), _heads(i.astype(f32), HGRN_DV)
    qch, ich = _heads(qc.astype(f32), HGRN_DK), _heads(ic.astype(f32), HGRN_DV)
    lf_f, k_f = hgrn_gates(zf, lb[0])
    lf_b, k_b = hgrn_gates(zb, lb[1])
    lfc_f, kc_f = hgrn_gates(zfc, lb[0])
    lfc_b, kc_b = hgrn_gates(zbc, lb[1])
    s0 = jnp.zeros((B, N_HEADS, HGRN_DK, HGRN_DV), f32)
    rev = lambda a: jnp.flip(a, axis=1)
    oc_f, s_cf = hgrn_chunk_scan(qch, kc_f, lfc_f, ich, s0)
    o_f, _ = hgrn_chunk_scan(qh, k_f, lf_f, ih, s_cf)
    oc_b, s_cb = hgrn_chunk_scan(rev(qch), rev(kc_b), rev(lfc_b), rev(ich), s0)
    o_b, _ = hgrn_chunk_scan(rev(qh), rev(k_b), rev(lf_b), rev(ih), s_cb)
    out = hgrn_readout(o_f + rev(o_b), g, norm_g)
    outc = hgrn_readout(oc_f + rev(oc_b), gc, norm_g) if need_ctx else None
    return out, outc


def hyena_filters(L, w1, b1, freq, w2, b2, w3, b3, decay):
    f32 = jnp.float32
    t = jnp.arange(L, dtype=f32)
    t_unit = jnp.linspace(0.0, 1.0, L, dtype=f32)
    bands = jnp.linspace(1e-4, HYENA_BANDS - 1, HYENA_BANDS, dtype=f32)
    ang = (2.0 * math.pi / L) * t[:, None] * bands[None, :]
    feats = jnp.concatenate([t_unit[:, None], jnp.cos(ang), -jnp.sin(ang)], axis=-1)
    fr = freq.astype(f32)
    hdn = jnp.sin(fr * (feats @ w1.astype(f32) + b1.astype(f32)))
    hdn = jnp.sin(fr * (hdn @ w2.astype(f32) + b2.astype(f32)))
    filt = (hdn @ w3.astype(f32) + b3.astype(f32)).reshape(L, HYENA_ORDER, 2, GROUP_W)
    filt = filt * jnp.exp(-t_unit[:, None, None, None] * decay.astype(f32)[None])
    fwd, bwd = filt[:, :, 0], filt[:, :, 1]
    kern = jnp.concatenate([fwd, jnp.zeros((1, HYENA_ORDER, GROUP_W), f32), bwd[:0:-1]], axis=0)
    return kern * lax.rsqrt(jnp.sum(kern * kern, axis=0, keepdims=True) + EPS)


def hyena_mixer(u, short_w, short_b, w1, b1, freq, w2, b2, w3, b3, decay, dbias):
    B, L, Cin = u.shape
    uc = lax.conv_general_dilated(u, short_w[:, None, :].astype(u.dtype), (1,), 'SAME',
                                  dimension_numbers=('NWC', 'WIO', 'NWC'),
                                  feature_group_count=Cin) + short_b.astype(u.dtype)
    v, x1, x2 = jnp.split(uc.astype(jnp.float32), 3, axis=-1)
    kern = hyena_filters(L, w1, b1, freq, w2, b2, w3, b3, decay)
    kf = jnp.fft.rfft(kern, axis=0)
    db = dbias.astype(jnp.float32)
    z = v
    for o, gate in enumerate((x1, x2)):
        zf = jnp.fft.rfft(z, n=2 * L, axis=1)
        y = jnp.fft.irfft(zf * kf[None, :, o], n=2 * L, axis=1)[:, :L]
        z = gate * (y + db[o] * z)
    return z.astype(u.dtype)


def neighborhood_attention(q, k, v, k_ctx, v_ctx, rpb):
    B, L, H, d = q.shape
    rows = L // GRID_W
    wr = min(NA_ROWS, rows)
    f32 = jnp.float32
    qg = q.astype(f32).reshape(B, rows, GRID_W, H, d)
    kg = k.astype(f32).reshape(B, rows, GRID_W, H, d)
    vg = v.astype(f32).reshape(B, rows, GRID_W, H, d)
    r = jnp.arange(rows)
    rs = jnp.clip(r - wr // 2, 0, rows - wr)
    row_idx = rs[:, None] + jnp.arange(wr)[None, :]
    kb = kg[:, row_idx]
    vb = vg[:, row_idx]
    cq = jnp.arange(GRID_W)
    cs = jnp.clip(cq - NA_COLS // 2, 0, GRID_W - NA_COLS)
    col_ok = (cq[None, :] >= cs[:, None]) & (cq[None, :] < cs[:, None] + NA_COLS)
    dr = row_idx - r[:, None] + (NA_ROWS - 1)
    dc = jnp.clip(cq[None, :] - cq[:, None] + (NA_COLS - 1), 0, 2 * NA_COLS - 2)
    bias = rpb.astype(f32)[:, dr[:, None, :, None], dc[None, :, None, :]]
    scale = d ** -0.5
    s_loc = jnp.einsum('brqhd,brjkhd->bhrqjk', qg, kb) * scale + bias[None]
    s_loc = jnp.where(col_ok[None, None, None, :, None, :], s_loc, -jnp.inf)
    s_ctx = jnp.einsum('brqhd,bchd->bhrqc', qg, k_ctx.astype(f32)) * scale
    s = jnp.concatenate([s_loc.reshape(B, H, rows, GRID_W, wr * GRID_W), s_ctx], axis=-1)
    p = jax.nn.softmax(s, axis=-1)
    p_loc = p[..., :wr * GRID_W].reshape(B, H, rows, GRID_W, wr, GRID_W)
    p_ctx = p[..., wr * GRID_W:]
    out = jnp.einsum('bhrqjk,brjkhd->brqhd', p_loc, vb) + jnp.einsum('bhrqc,bchd->brqhd', p_ctx, v_ctx.astype(f32))
    return out.reshape(B, L, H, d).astype(v.dtype)


def na_mixer(qkv, qkvc, rpb, q_g, k_g, need_ctx):
    q, k, v = [_heads(a, HEAD_DIM) for a in jnp.split(qkv, 3, axis=-1)]
    qc, kc, vc = [_heads(a, HEAD_DIM) for a in jnp.split(qkvc, 3, axis=-1)]
    q, k = rmsnorm(q, q_g), rmsnorm(k, k_g)
    qc, kc = rmsnorm(qc, q_g), rmsnorm(kc, k_g)
    B, L = qkv.shape[0], qkv.shape[1]
    out = neighborhood_attention(q, k, v, kc, vc, rpb).reshape(B, L, GROUP_W)
    outc = dense_attention(qc, kc, vc).reshape(B, qkvc.shape[1], GROUP_W) if need_ctx else None
    return out, outc


def mla_project(cq, ckv, kr, q_a_g, kv_a_g, w_uq, w_ukv, q_g, k_g, use_rope):
    B, L = cq.shape[0], cq.shape[1]
    q = (rmsnorm(cq, q_a_g) @ w_uq).reshape(B, L, N_HEADS, MLA_NOPE + MLA_ROPE)
    kv = (rmsnorm(ckv, kv_a_g) @ w_ukv).reshape(B, L, N_HEADS, MLA_NOPE + MLA_V)
    k_nope, v = kv[..., :MLA_NOPE], kv[..., MLA_NOPE:]
    k = jnp.concatenate([k_nope, jnp.broadcast_to(kr[:, :, None, :], (B, L, N_HEADS, MLA_ROPE))], axis=-1)
    q, k = rmsnorm(q, q_g), rmsnorm(k, k_g)
    if use_rope:
        q = jnp.concatenate([q[..., :MLA_NOPE], rope_2d(q[..., MLA_NOPE:])], axis=-1)
        k = jnp.concatenate([k[..., :MLA_NOPE], rope_2d(k[..., MLA_NOPE:])], axis=-1)
    return q, k, v


def mla_mixer(cq, ckv, kr, cqc, ckvc, krc, q_a_g, kv_a_g, w_uq, w_ukv, q_g, k_g, need_ctx):
    q, k, v = mla_project(cq, ckv, kr, q_a_g, kv_a_g, w_uq, w_ukv, q_g, k_g, True)
    qc, kc, vc = mla_project(cqc, ckvc, krc, q_a_g, kv_a_g, w_uq, w_ukv, q_g, k_g, False)
    B, L = cq.shape[0], cq.shape[1]
    k_all = jnp.concatenate([k, kc], axis=1)
    v_all = jnp.concatenate([v, vc], axis=1)
    out = blocked_attention(q, k_all, v_all).reshape(B, L, N_HEADS * MLA_V)
    outc = dense_attention(qc, kc, vc).reshape(B, cqc.shape[1], N_HEADS * MLA_V) if need_ctx else None
    return out, outc


def moe_ffn(h, wg, bg, we, be, w_gate, w_up, w_down):
    T, D = h.shape
    f32 = jnp.float32
    hf = h.astype(f32)
    lg = hf @ wg.astype(f32) + bg.astype(f32)
    pg = jax.nn.softmax(lg, axis=-1)
    g_sel = jnp.argmax(lg, axis=-1).astype(jnp.int32)
    p_sel = jnp.take_along_axis(pg, g_sel[:, None], axis=-1)[:, 0]
    le = (hf @ we.astype(f32) + be.astype(f32)).reshape(T, N_GROUPS, EXPERTS_PER_GROUP)
    le_sel = le[jnp.arange(T), g_sel]
    top_v, top_i = lax.top_k(le_sel, TOP_K)
    wts = jax.nn.softmax(top_v, axis=-1) * p_sel[:, None]
    eid = g_sel[:, None] * EXPERTS_PER_GROUP + top_i.astype(jnp.int32)
    N = T * TOP_K
    flat_e = eid.reshape(N)
    flat_t = jnp.repeat(jnp.arange(T, dtype=jnp.int32), TOP_K)
    flat_w = wts.reshape(N).astype(h.dtype)
    order = jnp.argsort(flat_e)
    se = flat_e[order]
    counts = jnp.bincount(flat_e, length=N_EXPERTS)
    starts = jnp.cumsum(counts) - counts
    pcounts = ((counts + MOE_BLOCK - 1) // MOE_BLOCK) * MOE_BLOCK
    pends = jnp.cumsum(pcounts)
    pstarts = pends - pcounts
    dest = pstarts[se] + (jnp.arange(N) - starts[se])
    NB = -(-N // MOE_BLOCK) + N_EXPERTS
    P = NB * MOE_BLOCK
    buf_t = jnp.full((P,), T, dtype=jnp.int32).at[dest].set(flat_t[order])
    buf_w = jnp.zeros((P,), h.dtype).at[dest].set(flat_w[order])
    block_e = jnp.minimum(jnp.searchsorted(pends, jnp.arange(NB) * MOE_BLOCK, side='right'), N_EXPERTS - 1)
    xpad = jnp.concatenate([h, jnp.zeros((1, D), h.dtype)], axis=0)
    xb = xpad[buf_t].reshape(NB, MOE_BLOCK, D)

    def run(args):
        xblk, e = args
        return (jax.nn.silu(xblk @ w_gate[e]) * (xblk @ w_up[e])) @ w_down[e]

    yb = lax.map(run, (xb, block_e)).reshape(P, D)
    return jax.ops.segment_sum(yb * buf_w[:, None], buf_t, num_segments=T + 1)[:T]


def setup_inputs(seed: int = 0) -> dict:
    key = jax.random.key(seed)
    ks = iter(jax.random.split(key, 48))
    f32 = jnp.float32
    nrm = lambda shape, scale: jax.random.normal(next(ks), shape, f32) * scale
    gain = lambda shape: 1.0 + 0.02 * jax.random.normal(next(ks), shape, f32)
    D = D_MODEL
    C = GROUP_W
    return {
        "x": nrm((BATCH, SEQ, D), 1.0),
        "c": nrm((BATCH, D), 1.0),
        "ctx": nrm((BATCH, CTX_LEN, D), 1.0),
        "c_ctx": nrm((D,), 1.0),
        "w_ada": nrm((DEPTH, D, 6 * D), 0.3 * D ** -0.5),
        "b_ada": nrm((DEPTH, 6 * D), 0.02),
        "norm1_g": gain((DEPTH, D)),
        "norm2_g": gain((DEPTH, D)),
        "w_in": nrm((DEPTH, D, D_IN), D ** -0.5),
        "w_out": nrm((DEPTH, D_MIX, D), D_MIX ** -0.5),
        "hgrn_lb_logits": nrm((DEPTH, 2, N_HEADS * HGRN_DK), 1.0),
        "hgrn_norm_g": gain((DEPTH, HGRN_DV)),
        "hy_short_w": nrm((DEPTH, HYENA_SHORT, 3 * C), HYENA_SHORT ** -0.5),
        "hy_short_b": nrm((DEPTH, 3 * C), 0.02),
        "hy_w1": nrm((DEPTH, HYENA_EMB, HYENA_FFN), HYENA_EMB ** -0.5),
        "hy_b1": nrm((DEPTH, HYENA_FFN), 0.02),
        "hy_freq": gain((DEPTH, HYENA_FFN)),
        "hy_w2": nrm((DEPTH, HYENA_FFN, HYENA_FFN), HYENA_FFN ** -0.5),
        "hy_b2": nrm((DEPTH, HYENA_FFN), 0.02),
        "hy_w3": nrm((DEPTH, HYENA_FFN, HYENA_ORDER * 2 * C), HYENA_FFN ** -0.5),
        "hy_b3": nrm((DEPTH, HYENA_ORDER * 2 * C), 0.02),
        "hy_decay": jnp.exp(jax.random.uniform(next(ks), (DEPTH, HYENA_ORDER, 2, C), f32,
                                               math.log(HYENA_DECAY_MIN), math.log(HYENA_DECAY_MAX))),
        "hy_bias": nrm((DEPTH, HYENA_ORDER, C), 0.1),
        "na_rpb": nrm((DEPTH, N_HEADS, 2 * NA_ROWS - 1, 2 * NA_COLS - 1), 0.02),
        "na_q_g": gain((DEPTH, HEAD_DIM)),
        "na_k_g": gain((DEPTH, HEAD_DIM)),
        "mla_q_a_g": gain((DEPTH, MLA_Q_RANK)),
        "mla_kv_a_g": gain((DEPTH, MLA_KV_RANK)),
        "mla_w_uq": nrm((DEPTH, MLA_Q_RANK, N_HEADS * (MLA_NOPE + MLA_ROPE)), MLA_Q_RANK ** -0.5),
        "mla_w_ukv": nrm((DEPTH, MLA_KV_RANK, N_HEADS * (MLA_NOPE + MLA_V)), MLA_KV_RANK ** -0.5),
        "mla_q_g": gain((DEPTH, MLA_NOPE + MLA_ROPE)),
        "mla_k_g": gain((DEPTH, MLA_NOPE + MLA_ROPE)),
        "moe_wg": nrm((DEPTH, D, N_GROUPS), D ** -0.5),
        "moe_bg": nrm((DEPTH, N_GROUPS), 0.01),
        "moe_we": nrm((DEPTH, D, N_EXPERTS), D ** -0.5),
        "moe_be": nrm((DEPTH, N_EXPERTS), 0.01),
        "moe_w_gate": nrm((DEPTH, N_EXPERTS, D, D_EXPERT), D ** -0.5),
        "moe_w_up": nrm((DEPTH, N_EXPERTS, D, D_EXPERT), D ** -0.5),
        "moe_w_down": nrm((DEPTH, N_EXPERTS, D_EXPERT, D), D_EXPERT ** -0.5),
    }


def reference(x, c, ctx, c_ctx, w_ada, b_ada, norm1_g, norm2_g, w_in, w_out,
              hgrn_lb_logits, hgrn_norm_g,
              hy_short_w, hy_short_b, hy_w1, hy_b1, hy_freq, hy_w2, hy_b2, hy_w3, hy_b3, hy_decay, hy_bias,
              na_rpb, na_q_g, na_k_g,
              mla_q_a_g, mla_kv_a_g, mla_w_uq, mla_w_ukv, mla_q_g, mla_k_g,
              moe_wg, moe_bg, moe_we, moe_be, moe_w_gate, moe_w_up, moe_w_down):
    B, L, D = x.shape
    lb_cum = jnp.cumsum(jax.nn.softmax(hgrn_lb_logits.astype(jnp.float32), axis=0), axis=0)
    lower_bounds = lb_cum - lb_cum[0:1]
    sc = jax.nn.silu(c)
    scc = jax.nn.silu(c_ctx)
    split_at = [int(s) for s in np.cumsum(IN_SPLITS)[:-1]]
    xc = ctx
    for l in range(DEPTH):
        need_ctx = l < DEPTH - 1
        mod = sc @ w_ada[l] + b_ada[l]
        modc = scc @ w_ada[l] + b_ada[l]
        sh1, s1, g1, sh2, s2, g2 = jnp.split(mod[:, None, :], 6, axis=-1)
        sh1c, s1c, g1c, sh2c, s2c, g2c = jnp.split(modc, 6, axis=-1)
        h = rmsnorm(x, norm1_g[l]) * (1 + s1) + sh1
        hc = rmsnorm(xc, norm1_g[l]) * (1 + s1c) + sh1c
        p = jnp.split(h @ w_in[l], split_at, axis=-1)
        pc = jnp.split(hc @ w_in[l], split_at, axis=-1)
        o_a, oc_a = hgrn_mixer(p[0], p[1], p[2], p[3], p[4], pc[0], pc[1], pc[2], pc[3], pc[4],
                               lower_bounds[l], hgrn_norm_g[l], need_ctx)
        hy_args = (hy_short_w[l], hy_short_b[l], hy_w1[l], hy_b1[l], hy_freq[l], hy_w2[l], hy_b2[l],
                   hy_w3[l], hy_b3[l], hy_decay[l], hy_bias[l])
        o_b = hyena_mixer(p[5], *hy_args)
        o_c, oc_c = na_mixer(p[6], pc[6], na_rpb[l], na_q_g[l], na_k_g[l], need_ctx)
        o_d, oc_d = mla_mixer(p[7], p[8], p[9], pc[7], pc[8], pc[9], mla_q_a_g[l], mla_kv_a_g[l],
                              mla_w_uq[l], mla_w_ukv[l], mla_q_g[l], mla_k_g[l], need_ctx)
        x = x + g1 * (jnp.concatenate([o_a, o_b, o_c, o_d], axis=-1) @ w_out[l])
        h2 = rmsnorm(x, norm2_g[l]) * (1 + s2) + sh2
        moe_args = (moe_wg[l], moe_bg[l], moe_we[l], moe_be[l], moe_w_gate[l], moe_w_up[l], moe_w_down[l])
        if need_ctx:
            oc_b = hyena_mixer(pc[5], *hy_args)
            xc = xc + g1c * (jnp.concatenate([oc_a, oc_b, oc_c, oc_d], axis=-1) @ w_out[l])
            h2c = rmsnorm(xc, norm2_g[l]) * (1 + s2c) + sh2c
            y = moe_ffn(jnp.concatenate([h2.reshape(-1, D), h2c.reshape(-1, D)], axis=0), *moe_args)
            x = x + g2 * y[:B * L].reshape(B, L, D)
            xc = xc + g2c * y[B * L:].reshape(xc.shape)
        else:
            x = x + g2 * moe_ffn(h2.reshape(-1, D), *moe_args).reshape(B, L, D)
    return x
```

```python
import functools
import math

import jax
import jax.numpy as jnp
from jax import lax
from jax.experimental import pallas as pl
from jax.experimental.pallas import tpu as pltpu

F32 = jnp.float32
BF16 = jnp.bfloat16

D = 1024
B = 8
L = 2048
CTX = 256
DEPTH = 4
GRID_W = 64
EPS = 1e-6
GW = 256
NH = 4
HYENA_BANDS = 16
HYENA_EMB = 1 + 2 * HYENA_BANDS
HYENA_FFN = 64
NA_ROWS = 8
NA_COLS = 16
MLA_Q_RANK = 256
MLA_KV_RANK = 128
MLA_NOPE = 64
MLA_ROPE = 32
MLA_QK = MLA_NOPE + MLA_ROPE
ROPE_BASE = 10000.0
N_GROUPS = 4
EPG = 8
N_EXPERTS = N_GROUPS * EPG
D_EXPERT = 512
D_IN = 3232
D_IN_PAD = 3328

T_LAT = B * L
T_CTX = B * CTX
T_ALL = T_LAT + T_CTX

TM = 256
N_TILES = T_ALL // TM
LAT_TILES = T_LAT // TM
TILES_PER_SEQ = L // TM
HCHUNK = 64
HSUB = 16
TMOE = 256
VMEM_LIMIT_BYTES = 56 * 1024 * 1024
NEG = -1e30

HI = lax.Precision.HIGHEST


def _cparams(sem, vmem=VMEM_LIMIT_BYTES):
    return pltpu.CompilerParams(dimension_semantics=sem, vmem_limit_bytes=vmem)


def _seg_of_tile(i):
    return jnp.where(i < LAT_TILES, i // TILES_PER_SEQ, B)


def _dot(a, b):
    return jnp.dot(a, b, preferred_element_type=F32)


def _dot_nt(a, b):
    return lax.dot_general(a, b, (((1,), (1,)), ((), ())), preferred_element_type=F32)


def _dot_tn(a, b):
    return lax.dot_general(a, b, (((0,), (0,)), ((), ())), preferred_element_type=F32)


def _split2(x):
    hi = x.astype(BF16)
    lo = (x - hi.astype(F32)).astype(BF16)
    return hi, lo


def _split3(x):
    h1 = x.astype(BF16)
    r1 = x - h1.astype(F32)
    h2 = r1.astype(BF16)
    h3 = (r1 - h2.astype(F32)).astype(BF16)
    return h1, h2, h3


def _group_sum(x, gm):
    hi, lo = _split2(x)
    return _dot(hi, gm) + _dot(lo, gm)


def _ada_kernel(c_ref, w_ref, b_ref, o_ref):
    cc = c_ref[...]
    sc = cc * jax.nn.sigmoid(cc)
    o_ref[0] = jnp.dot(sc, w_ref[0], preferred_element_type=F32, precision=HI) + b_ref[0]


def _adaln(cmat, w_ada, b_ada):
    tn = 1536
    return pl.pallas_call(
        _ada_kernel,
        grid=(DEPTH, 6 * D // tn),
        in_specs=[
            pl.BlockSpec((16, D), lambda l, j: (0, 0)),
            pl.BlockSpec((1, D, tn), lambda l, j: (l, 0, j)),
            pl.BlockSpec((1, 1, tn), lambda l, j: (l, 0, j)),
        ],
        out_specs=pl.BlockSpec((1, 16, tn), lambda l, j: (l, 0, j)),
        out_shape=jax.ShapeDtypeStruct((DEPTH, 16, 6 * D), F32),
        compiler_params=_cparams(("arbitrary", "arbitrary")),
        name="adaln",
    )(cmat, w_ada, b_ada.reshape(DEPTH, 1, 6 * D))


def _inproj_kernel(x_ref, mod_ref, g_ref, w_ref, o_hg, o_hy, o_na, o_mla):
    x = x_ref[...]
    ms = jnp.mean(x * x, axis=-1, keepdims=True)
    y = x * lax.rsqrt(ms + EPS) * g_ref[...]
    h = y * (1.0 + mod_ref[1:2, :]) + mod_ref[0:1, :]
    p = _dot(h.astype(BF16), w_ref[...])
    o_hg[...] = p[:, 0:1280]
    o_hy[...] = p[:, 1280:2048]
    o_na[...] = p[:, 2048:2816]
    o_mla[...] = p[:, 2816:3328]


def _inproj(X, mod_l, g, w_bf):
    return pl.pallas_call(
        _inproj_kernel,
        grid=(N_TILES,),
        in_specs=[
            pl.BlockSpec((TM, D), lambda i: (i, 0)),
            pl.BlockSpec((None, 6, D), lambda i: (_seg_of_tile(i), 0, 0)),
            pl.BlockSpec((1, D), lambda i: (0, 0)),
            pl.BlockSpec((D, D_IN_PAD), lambda i: (0, 0)),
        ],
        out_specs=[
            pl.BlockSpec((TM, 1280), lambda i: (i, 0)),
            pl.BlockSpec((TM, 768), lambda i: (i, 0)),
            pl.BlockSpec((TM, 768), lambda i: (i, 0)),
            pl.BlockSpec((TM, 512), lambda i: (i, 0)),
        ],
        out_shape=[
            jax.ShapeDtypeStruct((T_ALL, 1280), F32),
            jax.ShapeDtypeStruct((T_ALL, 768), F32),
            jax.ShapeDtypeStruct((T_ALL, 768), F32),
            jax.ShapeDtypeStruct((T_ALL, 512), F32),
        ],
        compiler_params=_cparams(("parallel",)),
        name="inproj",
    )(X, mod_l, g, w_bf)


def _hgrn_direction(p_ref, zcol, c_ref, crow, tri_ref, gm_ref, gmb_ref, st_ref, o_ref, reverse):
    q = p_ref[:, 0:GW]
    z = p_ref[:, zcol:zcol + GW]
    v = p_ref[:, 3 * GW:4 * GW]
    la = c_ref[crow:crow + 1, :]
    l1 = c_ref[crow + 1:crow + 2, :]
    oml = c_ref[crow + 2:crow + 3, :]
    ls = jnp.minimum(z, 0.0) - jnp.log1p(jnp.exp(-jnp.abs(z)))
    c2 = l1 + ls
    logf = jnp.maximum(la, c2) + jnp.log1p(jnp.exp(-jnp.abs(la - c2)))
    kk = oml * jax.nn.sigmoid(-z)
    tri = tri_ref[...]
    h1, h2, h3 = _split3(logf)
    bsub = _dot(tri, h1) + _dot(tri, h2) + _dot(tri, h3)
    row = lax.broadcasted_iota(jnp.int32, (HSUB, GW), 0)
    st = st_ref[...]
    gm = gm_ref[...]
    gmb = gmb_ref[...]
    order = range(HCHUNK // HSUB - 1, -1, -1) if reverse else range(HCHUNK // HSUB)
    for blk in order:
        r0 = blk * HSUB
        b_i = bsub[r0:r0 + HSUB]
        q_i = q[r0:r0 + HSUB]
        k_i = kk[r0:r0 + HSUB]
        v_i = v[r0:r0 + HSUB]
        btot = b_i[0:1] if reverse else b_i[HSUB - 1:HSUB]
        qe = (q_i * jnp.exp(b_i)).astype(BF16)
        o_inter = _dot_nt(qe, st.astype(BF16))
        parts = []
        for tl in range(HSUB):
            dlt = b_i[tl:tl + 1] - b_i
            valid = (row >= tl) if reverse else (row <= tl)
            w = jnp.exp(jnp.where(valid, dlt, NEG))
            parts.append((q_i[tl:tl + 1] * w) * k_i)
        pmat = jnp.concatenate(parts, axis=0).astype(BF16)
        abar = _dot(pmat, gmb)
        o_diag = jnp.sum(abar.reshape(HSUB, HSUB, GW) * v_i[None], axis=1)
        o_ref[r0:r0 + HSUB, :] = o_inter + o_diag
        kd = (k_i * jnp.exp(btot - b_i)).astype(BF16)
        upd = _dot_tn(v_i.astype(BF16), kd)
        st = st * jnp.exp(btot) + upd * gm
    st_ref[...] = st


def _hgrn_kernel(pf_ref, pb_ref, c_ref, trif_ref, trib_ref, gm_ref, gmb_ref, of_ref, ob_ref, stf, stb):
    @pl.when(pl.program_id(1) == 0)
    def _():
        stf[...] = jnp.zeros_like(stf)
        stb[...] = jnp.zeros_like(stb)

    _hgrn_direction(pf_ref, GW, c_ref, 0, trif_ref, gm_ref, gmb_ref, stf, of_ref, False)
    _hgrn_direction(pb_ref, 2 * GW, c_ref, 3, trib_ref, gm_ref, gmb_ref, stb, ob_ref, True)


def _hgrn_block(b, n, reverse):
    nctx = CTX // HCHUNK
    nlat = L // HCHUNK
    jc = (nctx - 1 - n) if reverse else n
    jl = (nlat - 1 - (n - nctx)) if reverse else (n - nctx)
    return jnp.where(n < nctx, T_LAT // HCHUNK + b * nctx + jc, b * nlat + jl)


def _hgrn(p_hg, consts, trif, trib, gm, gmb):
    nsteps = (CTX + L) // HCHUNK
    full = lambda shape: pl.BlockSpec(shape, lambda b, n: (0,) * len(shape))
    return pl.pallas_call(
        _hgrn_kernel,
        grid=(B, nsteps),
        in_specs=[
            pl.BlockSpec((HCHUNK, 1280), lambda b, n: (_hgrn_block(b, n, False), 0)),
            pl.BlockSpec((HCHUNK, 1280), lambda b, n: (_hgrn_block(b, n, True), 0)),
            full((8, GW)),
            full((HCHUNK, HCHUNK)),
            full((HCHUNK, HCHUNK)),
            full((GW, GW)),
            full((GW, GW)),
        ],
        out_specs=[
            pl.BlockSpec((HCHUNK, GW), lambda b, n: (_hgrn_block(b, n, False), 0)),
            pl.BlockSpec((HCHUNK, GW), lambda b, n: (_hgrn_block(b, n, True), 0)),
        ],
        out_shape=[jax.ShapeDtypeStruct((T_ALL, GW), F32), jax.ShapeDtypeStruct((T_ALL, GW), F32)],
        scratch_shapes=[pltpu.VMEM((GW, GW), F32), pltpu.VMEM((GW, GW), F32)],
        compiler_params=_cparams(("arbitrary", "arbitrary")),
        name="hgrn",
    )(p_hg, p_hg, consts, trif, trib, gm, gmb)


def _alt_sum(x):
    n, c = x.shape
    sgn = jnp.where((lax.broadcasted_iota(jnp.int32, (n, c), 0) & 1) == 0, 1.0, -1.0)
    return jnp.sum(x * sgn, axis=0, keepdims=True)


def _hyfilt_kernel(feats_ref, w1_ref, b1_ref, fr_ref, w2_ref, b2_ref, w3_ref, b3_ref, dec_ref,
                   e_ref, o_ref, nq_ref):
    fr = fr_ref[...]
    feats = feats_ref[...]
    h = jnp.sin(fr * (jnp.dot(feats, w1_ref[...], preferred_element_type=F32, precision=HI) + b1_ref[...]))
    h = jnp.sin(fr * (jnp.dot(h, w2_ref[...], preferred_element_type=F32, precision=HI) + b2_ref[...]))
    filt = jnp.dot(h, w3_ref[...], preferred_element_type=F32, precision=HI) + b3_ref[...]
    filt = filt * jnp.exp(-feats[:, 0:1] * dec_ref[...])
    n = filt.shape[0]
    row = lax.broadcasted_iota(jnp.int32, (n, GW), 0)
    for o in range(2):
        fwd = filt[:, (2 * o) * GW:(2 * o + 1) * GW]
        bwd = jnp.where(row >= 1, filt[:, (2 * o + 1) * GW:(2 * o + 2) * GW], 0.0)
        ssq = jnp.sum(fwd * fwd + bwd * bwd, axis=0, keepdims=True)
        scale = lax.rsqrt(ssq + EPS)
        ev = (fwd + bwd) * scale
        e_ref[:, o * GW:(o + 1) * GW] = ev
        o_ref[:, o * GW:(o + 1) * GW] = (fwd - bwd) * scale
        nq_ref[:, o * GW:(o + 1) * GW] = _alt_sum(ev) * (0.5 / n)


def _hyfilt(feats, w1p, b1, fr, w2, b2, w3, b3, dec):
    n = feats.shape[0]
    return pl.pallas_call(
        _hyfilt_kernel,
        out_shape=[jax.ShapeDtypeStruct((n, 2 * GW), F32), jax.ShapeDtypeStruct((n, 2 * GW), F32),
                   jax.ShapeDtypeStruct((1, 2 * GW), F32)],
        compiler_params=_cparams(None),
        name="hyfilt",
    )(feats, w1p, b1, fr, w2, b2, w3, b3, dec)


def _hyspec_kernel(chi_ref, clo_ref, shi_ref, slo_ref, e_ref, o_ref, kre_ref, kim_ref, *, n):
    eh, el = _split2(e_ref[...])
    oh, ol = _split2(o_ref[...])
    kre = _dot(chi_ref[...], eh) + _dot(chi_ref[...], el) + _dot(clo_ref[...], eh)
    kim = _dot(shi_ref[...], oh) + _dot(shi_ref[...], ol) + _dot(slo_ref[...], oh)
    tr = kre.shape[0]
    grow = lax.broadcasted_iota(jnp.int32, kre.shape, 0) + pl.program_id(0) * tr
    s2 = 1.0 / n
    kre_ref[...] = kre * jnp.where(grow == 0, 0.5 * s2, s2)
    kim_ref[...] = kim * s2


def _hyspec(chi, clo, shi, slo, e, o):
    n = e.shape[0]
    tr = min(256, n)
    rows = pl.BlockSpec((tr, n), lambda i: (i, 0))
    full = pl.BlockSpec((n, 2 * GW), lambda i: (0, 0))
    outb = pl.BlockSpec((tr, 2 * GW), lambda i: (i, 0))
    return pl.pallas_call(
        functools.partial(_hyspec_kernel, n=n),
        grid=(n // tr,),
        in_specs=[rows, rows, rows, rows, full, full],
        out_specs=[outb, outb],
        out_shape=[jax.ShapeDtypeStruct((n, 2 * GW), F32)] * 2,
        compiler_params=_cparams(("parallel",)),
        name="hyspec",
    )(chi, clo, shi, slo, e, o)


def _hyena_kernel(u_ref, sw_ref, sb_ref, db_ref, c_ref, s_ref, kre_ref, kim_ref, knq_ref, o_ref,
                  z_scr, zb_scr, y_scr):
    n = u_ref.shape[0]
    ft = min(512, n)
    rc = min(256, n)
    nchunks = n // rc
    lrow = lax.broadcasted_iota(jnp.int32, (rc, GW), 0)
    sgn = jnp.where((lrow & 1) == 0, 1.0, -1.0)

    def short_conv(part, c):
        sl = slice(part * GW, (part + 1) * GW)
        r0 = c * rc
        u = u_ref[r0:r0 + rc, sl]
        prev = u_ref[r0 - 1:r0, sl] if c > 0 else jnp.zeros((1, GW), F32)
        nxt = u_ref[r0 + rc:r0 + rc + 1, sl] if c < nchunks - 1 else jnp.zeros((1, GW), F32)
        up = jnp.where(lrow == 0, prev, pltpu.roll(u, 1, 0))
        un = jnp.where(lrow == rc - 1, nxt, pltpu.roll(u, rc - 1, 0))
        return sw_ref[0:1, sl] * up + sw_ref[1:2, sl] * u + sw_ref[2:3, sl] * un + sb_ref[:, sl]

    for c in range(nchunks):
        z_scr[c * rc:(c + 1) * rc, :] = short_conv(0, c)
    for o in range(2):
        cols = slice(o * GW, (o + 1) * GW)
        znq = jnp.zeros((1, GW), F32)
        for c in range(nchunks):
            zc = z_scr[c * rc:(c + 1) * rc, :]
            zb_scr[c * rc:(c + 1) * rc, :] = zc.astype(BF16)
            znq = znq + jnp.sum(zc * sgn, axis=0, keepdims=True)
        ynq = znq * knq_ref[:, cols]
        for c in range(nchunks):
            y_scr[c * rc:(c + 1) * rc, :] = sgn * ynq
        for f in range(n // ft):
            rs = slice(f * ft, (f + 1) * ft)
            zre = _dot(c_ref[rs, :], zb_scr[...])
            zim = _dot(s_ref[rs, :], zb_scr[...])
            kre = kre_ref[rs, cols]
            kim = kim_ref[rs, cols]
            yre = (zre * kre - zim * kim).astype(BF16)
            yim = (zre * kim + zim * kre).astype(BF16)
            y_scr[...] += _dot(c_ref[:, rs], yre) + _dot(s_ref[:, rs], yim)
        dst = o_ref if o == 1 else z_scr
        for c in range(nchunks):
            rows = slice(c * rc, (c + 1) * rc)
            dst[rows, :] = short_conv(o + 1, c) * (y_scr[rows, :] + db_ref[o:o + 1, :] * z_scr[rows, :])


def _hyena(u, nb, n, sw, sb, db, cm, sm, kre, kim, knq):
    whole = pl.BlockSpec(memory_space=pltpu.VMEM)
    return pl.pallas_call(
        _hyena_kernel,
        grid=(nb,),
        in_specs=[
            pl.BlockSpec((n, 3 * GW), lambda b: (b, 0)),
            pl.BlockSpec((3, 3 * GW), lambda b: (0, 0)),
            pl.BlockSpec((1, 3 * GW), lambda b: (0, 0)),
            pl.BlockSpec((2, GW), lambda b: (0, 0)),
            whole, whole, whole, whole, whole,
        ],
        out_specs=pl.BlockSpec((n, GW), lambda b: (b, 0)),
        out_shape=jax.ShapeDtypeStruct((nb * n, GW), F32),
        scratch_shapes=[pltpu.VMEM((n, GW), F32), pltpu.VMEM((n, GW), BF16), pltpu.VMEM((n, GW), F32)],
        compiler_params=_cparams(("arbitrary",)),
        name="hyena",
    )(u, sw, sb, db, cm, sm, kre, kim, knq)


def _dft_consts(n):
    kk = jnp.arange(n, dtype=jnp.int32)
    ph = (kk[:, None] * kk[None, :]) % (2 * n)
    ang = ph.astype(F32) * (math.pi / n)
    return jnp.cos(ang), -jnp.sin(ang)


def _hyena_feats(n):
    t = jnp.arange(n, dtype=F32)
    t_unit = jnp.linspace(0.0, 1.0, n, dtype=F32)
    bands = jnp.linspace(1e-4, HYENA_BANDS - 1, HYENA_BANDS, dtype=F32)
    ang = (2.0 * math.pi / n) * t[:, None] * bands[None, :]
    feats = jnp.concatenate([t_unit[:, None], jnp.cos(ang), -jnp.sin(ang)], axis=-1)
    return jnp.pad(feats, ((0, 0), (0, 128 - HYENA_EMB)))


def _naprep_kernel(p_ref, qg_ref, kg_ref, gm_ref, q_ref, k_ref, v_ref):
    p = p_ref[...]
    q = p[:, 0:GW]
    k = p[:, GW:2 * GW]
    gm = gm_ref[...]
    qn = q * lax.rsqrt(_group_sum(q * q, gm) * (1.0 / 64) + EPS) * qg_ref[...]
    kn = k * lax.rsqrt(_group_sum(k * k, gm) * (1.0 / 64) + EPS) * kg_ref[...]
    q_ref[...] = (qn * (64 ** -0.5)).astype(BF16)
    k_ref[...] = kn.astype(BF16)
    v_ref[...] = p[:, 2 * GW:3 * GW].astype(BF16)


def _naprep(p_na, qg, kg, gmb):
    tok = lambda w: pl.BlockSpec((TM, w), lambda i: (i, 0))
    full = lambda shape: pl.BlockSpec(shape, lambda i: (0,) * len(shape))
    return pl.pallas_call(
        _naprep_kernel,
        grid=(N_TILES,),
        in_specs=[tok(768), full((1, GW)), full((1, GW)), full((GW, GW))],
        out_specs=[tok(GW), tok(GW), tok(GW)],
        out_shape=[jax.ShapeDtypeStruct((T_ALL, GW), BF16)] * 3,
        compiler_params=_cparams(("parallel",)),
        name="naprep",
    )(p_na, qg, kg, gmb)


def _na_kernel(q_ref, k_ref, v_ref, kc_ref, vc_ref, bias_ref, o_ref):
    r = pl.program_id(1)
    rows = L // GRID_W
    rs = jnp.clip(r - NA_ROWS // 2, 0, rows - NA_ROWS)
    start = pl.multiple_of(rs * GRID_W, GRID_W)
    kw = k_ref[pl.ds(start, NA_ROWS * GRID_W), :]
    vw = v_ref[pl.ds(start, NA_ROWS * GRID_W), :]
    q = q_ref[...]
    kc = kc_ref[...]
    vc = vc_ref[...]
    outs = []
    for h in range(NH):
        sl = slice(h * 64, (h + 1) * 64)
        s_loc = _dot_nt(q[:, sl], kw[:, sl]) + bias_ref[h]
        s_ctx = _dot_nt(q[:, sl], kc[:, sl])
        m = jnp.maximum(jnp.max(s_loc, axis=-1, keepdims=True), jnp.max(s_ctx, axis=-1, keepdims=True))
        p_loc = jnp.exp(s_loc - m)
        p_ctx = jnp.exp(s_ctx - m)
        den = jnp.sum(p_loc, axis=-1, keepdims=True) + jnp.sum(p_ctx, axis=-1, keepdims=True)
        o = _dot(p_loc.astype(BF16), vw[:, sl]) + _dot(p_ctx.astype(BF16), vc[:, sl])
        outs.append(o / den)
    o_ref[...] = jnp.concatenate(outs, axis=-1)


def _na_variant(r):
    rows = L // GRID_W
    return r - jnp.clip(r - NA_ROWS // 2, 0, rows - NA_ROWS)


def _na(qn, kn, vn, bias_t):
    rows = L // GRID_W
    ctx_blk = T_LAT // CTX
    return pl.pallas_call(
        _na_kernel,
        grid=(B, rows),
        in_specs=[
            pl.BlockSpec((GRID_W, GW), lambda b, r: (b * rows + r, 0)),
            pl.BlockSpec((L, GW), lambda b, r: (b, 0)),
            pl.BlockSpec((L, GW), lambda b, r: (b, 0)),
            pl.BlockSpec((CTX, GW), lambda b, r: (ctx_blk + b, 0)),
            pl.BlockSpec((CTX, GW), lambda b, r: (ctx_blk + b, 0)),
            pl.BlockSpec((None, NH, GRID_W, NA_ROWS * GRID_W), lambda b, r: (_na_variant(r), 0, 0, 0)),
        ],
        out_specs=pl.BlockSpec((GRID_W, GW), lambda b, r: (b * rows + r, 0)),
        out_shape=jax.ShapeDtypeStruct((T_LAT, GW), F32),
        compiler_params=_cparams(("arbitrary", "arbitrary")),
        name="na",
    )(qn, kn, vn, kn, vn, bias_t)


def _na_bias_table(rpb):
    a = jnp.arange(NA_ROWS)
    j = jnp.arange(NA_ROWS)
    dr = j[None, :] - a[:, None] + (NA_ROWS - 1)
    cq = jnp.arange(GRID_W)
    cs = jnp.clip(cq - NA_COLS // 2, 0, GRID_W - NA_COLS)
    col_ok = (cq[None, :] >= cs[:, None]) & (cq[None, :] < cs[:, None] + NA_COLS)
    dc = jnp.clip(cq[None, :] - cq[:, None] + (NA_COLS - 1), 0, 2 * NA_COLS - 2)
    bias = rpb.astype(F32)[:, dr[:, None, :, None], dc[None, :, None, :]]
    bias = jnp.where(col_ok[None, None, :, None, :], bias, NEG)
    return bias.transpose(1, 0, 2, 3, 4).reshape(NA_ROWS, NH, GRID_W, NA_ROWS * GRID_W)


def _attn_kernel(*refs, nkv, dq, dv):
    q = refs[0][...]
    ks = [refs[1 + 2 * j][...] for j in range(nkv)]
    vs = [refs[2 + 2 * j][...] for j in range(nkv)]
    o_ref = refs[1 + 2 * nkv]
    outs = []
    for h in range(NH):
        qh = q[:, h * dq:(h + 1) * dq]
        ss = [_dot_nt(qh, k[:, h * dq:(h + 1) * dq]) for k in ks]
        m = functools.reduce(jnp.maximum, [jnp.max(s, axis=-1, keepdims=True) for s in ss])
        ps = [jnp.exp(s - m) for s in ss]
        den = functools.reduce(lambda a, b2: a + b2, [jnp.sum(p, axis=-1, keepdims=True) for p in ps])
        o = functools.reduce(lambda a, b2: a + b2,
                             [_dot(p.astype(BF16), v[:, h * dv:(h + 1) * dv]) for p, v in zip(ps, vs)])
        outs.append(o / den)
    o_ref[...] = jnp.concatenate(outs, axis=-1)


def _attn_latent(q, k, v, dq, dv, tq):
    nq = L // tq
    ctx_blk = T_LAT // CTX
    return pl.pallas_call(
        functools.partial(_attn_kernel, nkv=2, dq=dq, dv=dv),
        grid=(B, nq),
        in_specs=[
            pl.BlockSpec((tq, NH * dq), lambda b, i: (b * nq + i, 0)),
            pl.BlockSpec((L, NH * dq), lambda b, i: (b, 0)),
            pl.BlockSpec((L, NH * dv), lambda b, i: (b, 0)),
            pl.BlockSpec((CTX, NH * dq), lambda b, i: (ctx_blk + b, 0)),
            pl.BlockSpec((CTX, NH * dv), lambda b, i: (ctx_blk + b, 0)),
        ],
        out_specs=pl.BlockSpec((tq, NH * dv), lambda b, i: (b * nq + i, 0)),
        out_shape=jax.ShapeDtypeStruct((T_LAT, NH * dv), F32),
        compiler_params=_cparams(("arbitrary", "arbitrary")),
        name="attn_latent",
    )(q, k, v, k, v)


def _attn_ctx(q, k, v, dq, dv):
    ctx_blk = T_LAT // CTX
    return pl.pallas_call(
        functools.partial(_attn_kernel, nkv=1, dq=dq, dv=dv),
        grid=(B,),
        in_specs=[
            pl.BlockSpec((CTX, NH * dq), lambda b: (ctx_blk + b, 0)),
            pl.BlockSpec((CTX, NH * dq), lambda b: (ctx_blk + b, 0)),
            pl.BlockSpec((CTX, NH * dv), lambda b: (ctx_blk + b, 0)),
        ],
        out_specs=pl.BlockSpec((CTX, NH * dv), lambda b: (b, 0)),
        out_shape=jax.ShapeDtypeStruct((T_CTX, NH * dv), F32),
        compiler_params=_cparams(("arbitrary",)),
        name="attn_ctx",
    )(q, k, v)


def _mlaprep_kernel(p_ref, qag_ref, kvag_ref, wq_ref, wk_ref, wv_ref, qg_ref, kg_ref, gm_ref, pm_ref,
                    cos_ref, sin_ref, q_ref, k_ref, v_ref):
    p = p_ref[...]
    cq = p[:, 0:MLA_Q_RANK]
    ckv = p[:, MLA_Q_RANK:MLA_Q_RANK + MLA_KV_RANK]
    krp = p[:, MLA_Q_RANK + MLA_KV_RANK:]
    cqn = cq * lax.rsqrt(jnp.mean(cq * cq, axis=-1, keepdims=True) + EPS) * qag_ref[...]
    ckvn = ckv * lax.rsqrt(jnp.mean(ckv * ckv, axis=-1, keepdims=True) + EPS) * kvag_ref[...]
    ckvb = ckvn.astype(BF16)
    q = _dot(cqn.astype(BF16), wq_ref[...])
    k = _dot(jnp.concatenate([ckvb, krp.astype(BF16)], axis=-1), wk_ref[...])
    v = _dot(ckvb, wv_ref[...])
    gm = gm_ref[...]
    q = q * lax.rsqrt(_group_sum(q * q, gm) * (1.0 / MLA_QK) + EPS) * qg_ref[...]
    k = k * lax.rsqrt(_group_sum(k * k, gm) * (1.0 / MLA_QK) + EPS) * kg_ref[...]
    cos = cos_ref[...]
    sin = sin_ref[...]
    pm = pm_ref[...]
    q = q * cos + _dot(q.astype(BF16), pm) * sin
    k = k * cos + _dot(k.astype(BF16), pm) * sin
    q_ref[...] = (q * (MLA_QK ** -0.5)).astype(BF16)
    k_ref[...] = k.astype(BF16)
    v_ref[...] = v.astype(BF16)


def _mlaprep(p_mla, qag, kvag, wq, wk, wv, qg, kg, gm, pm, cos_t, sin_t):
    tok = lambda w: pl.BlockSpec((TM, w), lambda i: (i, 0))
    full = lambda shape: pl.BlockSpec(shape, lambda i: (0,) * len(shape))
    pos = pl.BlockSpec((TM, 512), lambda i: (jnp.where(i < LAT_TILES, i % TILES_PER_SEQ, TILES_PER_SEQ), 0))
    return pl.pallas_call(
        _mlaprep_kernel,
        grid=(N_TILES,),
        in_specs=[tok(512), full((1, 256)), full((1, 128)), full((256, 512)), full((256, 512)),
                  full((128, 256)), full((1, 512)), full((1, 512)), full((512, 512)), full((512, 512)),
                  pos, pos],
        out_specs=[tok(512), tok(512), tok(GW)],
        out_shape=[jax.ShapeDtypeStruct((T_ALL, 512), BF16), jax.ShapeDtypeStruct((T_ALL, 512), BF16),
                   jax.ShapeDtypeStruct((T_ALL, GW), BF16)],
        compiler_params=_cparams(("parallel",)),
        name="mlaprep",
    )(p_mla, qag, kvag, wq, wk, wv, qg, kg, gm, pm, cos_t, sin_t)


def _rope_tables():
    t = jnp.arange(L)
    rowp = (t // GRID_W).astype(F32)
    colp = (t % GRID_W).astype(F32)
    half = MLA_ROPE // 2
    inv = ROPE_BASE ** (-jnp.arange(0, half, 2, dtype=F32) / half)
    j = jnp.arange(MLA_ROPE)
    pos = jnp.where(j[None, :] < half, rowp[:, None], colp[:, None])
    ang = pos * inv[j % (half // 2)][None, :]
    first = (j % half) < (half // 2)
    cos32 = jnp.cos(ang)
    sin32 = jnp.where(first[None, :], -jnp.sin(ang), jnp.sin(ang))
    cos_h = jnp.concatenate([jnp.ones((L, MLA_NOPE), F32), cos32, jnp.ones((L, 32), F32)], axis=-1)
    sin_h = jnp.concatenate([jnp.zeros((L, MLA_NOPE), F32), sin32, jnp.zeros((L, 32), F32)], axis=-1)
    cos_t = jnp.concatenate([jnp.tile(cos_h, (1, NH)), jnp.ones((TM, 512), F32)], axis=0)
    sin_t = jnp.concatenate([jnp.tile(sin_h, (1, NH)), jnp.zeros((TM, 512), F32)], axis=0)
    lane = jnp.arange(512)
    jj = lane % 128 - MLA_NOPE
    is_rope = (jj >= 0) & (jj < MLA_ROPE)
    partner = jnp.where(is_rope, jnp.where((jj % half) < (half // 2), lane + half // 2, lane - half // 2), lane)
    pm = (lane[:, None] == partner[None, :]).astype(BF16)
    return cos_t, sin_t, pm


def _outproj_kernel(x_ref, of_ref, ob_ref, g_ref, hy_ref, na_ref, mla_ref, mod_ref, ng_ref, gm_ref,
                    w_ref, n2_ref, wr_ref, x1_ref, h2_ref, lg_ref):
    oa = of_ref[...] + ob_ref[...]
    ms = _group_sum(oa * oa, gm_ref[...]) * (1.0 / 64)
    g = g_ref[...]
    oa = oa * lax.rsqrt(ms + EPS) * ng_ref[...] * (g * jax.nn.sigmoid(g))
    mix = jnp.concatenate([oa, hy_ref[...], na_ref[...], mla_ref[...]], axis=-1).astype(BF16)
    x1 = x_ref[...] + mod_ref[2:3, :] * _dot(mix, w_ref[...])
    x1_ref[...] = x1
    ms2 = jnp.mean(x1 * x1, axis=-1, keepdims=True)
    h2 = x1 * lax.rsqrt(ms2 + EPS) * n2_ref[...] * (1.0 + mod_ref[4:5, :]) + mod_ref[3:4, :]
    h2_ref[...] = h2
    hh, hl = _split2(h2)
    wr = wr_ref[...]
    lg_ref[...] = _dot(hh, wr[:, 0:128]) + _dot(hl, wr[:, 0:128]) + _dot(hh, wr[:, 128:256])


def _outproj(X, o_f, o_b, p_hg, o_hy, o_na, o_mla, mod_l, ng, gmb, w_bf, n2g, wr):
    tok = lambda w: pl.BlockSpec((TM, w), lambda i: (i, 0))
    full = lambda shape: pl.BlockSpec(shape, lambda i: (0,) * len(shape))
    return pl.pallas_call(
        _outproj_kernel,
        grid=(N_TILES,),
        in_specs=[tok(D), tok(GW), tok(GW), pl.BlockSpec((TM, GW), lambda i: (i, 4)), tok(GW), tok(GW), tok(GW),
                  pl.BlockSpec((None, 6, D), lambda i: (_seg_of_tile(i), 0, 0)),
                  full((1, GW)), full((GW, GW)), full((D, D)), full((1, D)), full((D, 256))],
        out_specs=[tok(D), tok(D), tok(128)],
        out_shape=[jax.ShapeDtypeStruct((T_ALL, D), F32), jax.ShapeDtypeStruct((T_ALL, D), F32),
                   jax.ShapeDtypeStruct((T_ALL, 128), F32)],
        compiler_params=_cparams(("parallel",)),
        name="outproj",
    )(X, o_f, o_b, p_hg, o_hy, o_na, o_mla, mod_l, ng, gmb, w_bf, n2g, wr)


def _route(logits, bg, be):
    t = logits.shape[0]
    lg = logits[:, 0:N_GROUPS] + bg
    le = (logits[:, N_GROUPS:N_GROUPS + N_EXPERTS] + be).reshape(t, N_GROUPS, EPG)
    pg = jax.nn.softmax(lg, axis=-1)
    g_sel = jnp.argmax(lg, axis=-1).astype(jnp.int32)
    p_sel = jnp.take_along_axis(pg, g_sel[:, None], axis=-1)[:, 0]
    le_sel = jnp.take_along_axis(le, g_sel[:, None, None], axis=1)[:, 0]
    top_v, top_i = lax.top_k(le_sel, 2)
    wts = jax.nn.softmax(top_v, axis=-1) * p_sel[:, None]
    eid = g_sel[:, None] * EPG + top_i.astype(jnp.int32)
    return eid, wts


def _dispatch_tables(eid, wts):
    t = eid.shape[0]
    n = 2 * t
    nb = n // TMOE + N_EXPERTS
    flat_e = eid.reshape(n)
    onehot = (flat_e[:, None] == jnp.arange(N_EXPERTS, dtype=jnp.int32)[None, :]).astype(jnp.int32)
    csum = jnp.cumsum(onehot, axis=0)
    pos = jnp.take_along_axis(csum, flat_e[:, None], axis=1)[:, 0] - 1
    counts = csum[-1]
    pcounts = ((counts + TMOE - 1) // TMOE) * TMOE
    pends = jnp.cumsum(pcounts)
    pstarts = pends - pcounts
    dest = pstarts[flat_e] + pos
    slots = nb * TMOE
    tok = jnp.zeros((slots,), jnp.int32).at[dest].set(jnp.arange(n, dtype=jnp.int32) // 2)
    orow = jnp.full((slots,), n, jnp.int32).at[dest].set(jnp.arange(n, dtype=jnp.int32))
    wslot = jnp.zeros((slots,), F32).at[dest].set(wts.reshape(n))
    block_e = jnp.minimum(jnp.searchsorted(pends, jnp.arange(nb, dtype=jnp.int32) * TMOE, side='right'),
                          N_EXPERTS - 1).astype(jnp.int32)
    nblk = (pends[-1] // TMOE).astype(jnp.int32).reshape(1)
    return block_e, nblk, tok, orow, wslot.reshape(nb, TMOE, 1)


def _moe_kernel(be_ref, nblk_ref, tok_ref, orow_ref, h_hbm, w_ref, wg_ref, wu_ref, wd_ref, y_hbm,
                xbuf, ybuf, gsem, ssem):
    i = pl.program_id(0)
    n_pairs = y_hbm.shape[0]

    @pl.when(i < nblk_ref[0])
    def _():
        base = i * TMOE

        def gather(r, carry):
            pltpu.make_async_copy(h_hbm.at[pl.ds(tok_ref[base + r], 1)], xbuf.at[pl.ds(r, 1)], gsem).start()
            return carry

        lax.fori_loop(0, TMOE, gather, 0)

        def gwait(r, carry):
            pltpu.make_async_copy(h_hbm.at[pl.ds(0, 1)], xbuf.at[pl.ds(r, 1)], gsem).wait()
            return carry

        lax.fori_loop(0, TMOE, gwait, 0)
        x = xbuf[...].astype(BF16)
        gate = _dot(x, wg_ref[0].astype(BF16))
        up = _dot(x, wu_ref[0].astype(BF16))
        act = (gate * jax.nn.sigmoid(gate)) * up
        y = _dot(act.astype(BF16), wd_ref[0].astype(BF16))
        ybuf[...] = y * w_ref[...]

        def scatter(r, carry):
            dst = orow_ref[base + r]

            @pl.when(dst < n_pairs)
            def _():
                pltpu.make_async_copy(ybuf.at[pl.ds(r, 1)], y_hbm.at[pl.ds(dst, 1)], ssem).start()

            return carry

        lax.fori_loop(0, TMOE, scatter, 0)

        def swait(r, carry):
            @pl.when(orow_ref[base + r] < n_pairs)
            def _():
                pltpu.make_async_copy(ybuf.at[pl.ds(r, 1)], y_hbm.at[pl.ds(0, 1)], ssem).wait()

            return carry

        lax.fori_loop(0, TMOE, swait, 0)


def _moe(h2, block_e, nblk, tok, orow, wslot, w_gate, w_up, w_down):
    t = h2.shape[0]
    nb = block_e.shape[0]
    grid_spec = pltpu.PrefetchScalarGridSpec(
        num_scalar_prefetch=4,
        grid=(nb,),
        in_specs=[
            pl.BlockSpec(memory_space=pl.ANY),
            pl.BlockSpec((None, TMOE, 1), lambda i, be, nk, tk, orw: (i, 0, 0)),
            pl.BlockSpec((1, D, D_EXPERT), lambda i, be, nk, tk, orw: (be[i], 0, 0)),
            pl.BlockSpec((1, D, D_EXPERT), lambda i, be, nk, tk, orw: (be[i], 0, 0)),
            pl.BlockSpec((1, D_EXPERT, D), lambda i, be, nk, tk, orw: (be[i], 0, 0)),
        ],
        out_specs=pl.BlockSpec(memory_space=pl.ANY),
        scratch_shapes=[pltpu.VMEM((TMOE, D), F32), pltpu.VMEM((TMOE, D), F32),
                        pltpu.SemaphoreType.DMA(()), pltpu.SemaphoreType.DMA(())],
    )
    return pl.pallas_call(
        _moe_kernel,
        grid_spec=grid_spec,
        out_shape=jax.ShapeDtypeStruct((2 * t, D), F32),
        compiler_params=_cparams(("arbitrary",)),
        name="moe",
    )(block_e, nblk, tok, orow, h2, wslot, w_gate, w_up, w_down)


def _combine_kernel(x_ref, y_ref, mod_ref, o_ref):
    y = y_ref[...]
    o_ref[...] = x_ref[...] + mod_ref[5:6, :] * (y[:, 0:D] + y[:, D:2 * D])


def _combine(X1, y2, mod_l):
    return pl.pallas_call(
        _combine_kernel,
        grid=(N_TILES,),
        in_specs=[pl.BlockSpec((TM, D), lambda i: (i, 0)), pl.BlockSpec((TM, 2 * D), lambda i: (i, 0)),
                  pl.BlockSpec((None, 6, D), lambda i: (_seg_of_tile(i), 0, 0))],
        out_specs=pl.BlockSpec((TM, D), lambda i: (i, 0)),
        out_shape=jax.ShapeDtypeStruct((T_ALL, D), F32),
        compiler_params=_cparams(("parallel",)),
        name="combine",
    )(X1, y2, mod_l)


def _group_mask(width, group):
    lane = jnp.arange(width)
    return (lane[:, None] // group == lane[None, :] // group)


def _hgrn_tri(reverse):
    t = jnp.arange(HCHUNK)
    same = (t[:, None] // HSUB) == (t[None, :] // HSUB)
    order = (t[None, :] >= t[:, None]) if reverse else (t[None, :] <= t[:, None])
    return (same & order).astype(BF16)


def _mla_weights(w_uq, w_ukv, q_g, k_g):
    wq = jnp.pad(w_uq.reshape(MLA_Q_RANK, NH, MLA_QK), ((0, 0), (0, 0), (0, 128 - MLA_QK))).reshape(MLA_Q_RANK, 512)
    kv = w_ukv.reshape(MLA_KV_RANK, NH, MLA_NOPE + 64)
    wk_top = jnp.pad(kv[:, :, :MLA_NOPE], ((0, 0), (0, 0), (0, 128 - MLA_NOPE))).reshape(MLA_KV_RANK, 512)
    lane = jnp.arange(512)
    src = jnp.arange(128)
    place = ((lane[None, :] % 128) == (src[:, None] + MLA_NOPE)) & (src[:, None] < MLA_ROPE)
    wk = jnp.concatenate([wk_top, place.astype(F32)], axis=0)
    wv = kv[:, :, MLA_NOPE:].reshape(MLA_KV_RANK, GW)
    pad_g = lambda g: jnp.tile(jnp.pad(g, (0, 128 - MLA_QK)), NH).reshape(1, 512)
    return wq.astype(BF16), wk.astype(BF16), wv.astype(BF16), pad_g(q_g), pad_g(k_g)


def kernel(x, c, ctx, c_ctx, w_ada, b_ada, norm1_g, norm2_g, w_in, w_out, hgrn_lb_logits, hgrn_norm_g,
           hy_short_w, hy_short_b, hy_w1, hy_b1, hy_freq, hy_w2, hy_b2, hy_w3, hy_b3, hy_decay, hy_bias,
           na_rpb, na_q_g, na_k_g, mla_q_a_g, mla_kv_a_g, mla_w_uq, mla_w_ukv, mla_q_g, mla_k_g,
           moe_wg, moe_bg, moe_we, moe_be, moe_w_gate, moe_w_up, moe_w_down):
    X = jnp.concatenate([x.reshape(T_LAT, D), ctx.reshape(T_CTX, D)], axis=0)
    cmat = jnp.concatenate([c, c_ctx[None, :], jnp.zeros((16 - B - 1, D), F32)], axis=0)
    mod = _adaln(cmat, w_ada, b_ada).reshape(DEPTH, 16, 6, D)

    lb_cum = jnp.cumsum(jax.nn.softmax(hgrn_lb_logits.astype(F32), axis=0), axis=0)
    lower = lb_cum - lb_cum[0:1]

    gm64 = _group_mask(GW, 64)
    gm64_f = gm64.astype(F32)
    gm64_b = gm64.astype(BF16)
    gm128_b = _group_mask(512, 128).astype(BF16)
    trif = _hgrn_tri(False)
    trib = _hgrn_tri(True)
    cos_t, sin_t, pm = _rope_tables()
    dft = {}
    for n in (L, CTX):
        cm, sm = _dft_consts(n)
        chi, clo = _split2(cm)
        shi, slo = _split2(sm)
        dft[n] = (chi, clo, shi, slo, _hyena_feats(n))

    for l in range(DEPTH):
        mod_l = mod[l]
        p_hg, p_hy, p_na, p_mla = _inproj(X, mod_l, norm1_g[l].reshape(1, D),
                                          jnp.pad(w_in[l], ((0, 0), (0, D_IN_PAD - D_IN))).astype(BF16))

        lb = lower[l]
        hconst = jnp.concatenate([
            jnp.stack([jnp.maximum(jnp.log(lb[d]), NEG), jnp.log1p(-lb[d]), 1.0 - lb[d]]) for d in range(2)
        ] + [jnp.zeros((2, GW), F32)], axis=0)
        o_f, o_b = _hgrn(p_hg, hconst, trif, trib, gm64_f, gm64_b)

        w1p = jnp.pad(hy_w1[l], ((0, 128 - HYENA_EMB), (0, 0)))
        o_hy_parts = []
        for n, nb, u in ((L, B, p_hy[:T_LAT]), (CTX, B, p_hy[T_LAT:])):
            chi, clo, shi, slo, feats = dft[n]
            e, o, knq = _hyfilt(feats, w1p, hy_b1[l].reshape(1, -1), hy_freq[l].reshape(1, -1), hy_w2[l],
                                hy_b2[l].reshape(1, -1), hy_w3[l], hy_b3[l].reshape(1, -1),
                                hy_decay[l].reshape(1, 4 * GW))
            kre, kim = _hyspec(chi, clo, shi, slo, e, o)
            o_hy_parts.append(_hyena(u, nb, n, hy_short_w[l], hy_short_b[l].reshape(1, -1), hy_bias[l],
                                     chi, shi, kre, kim, knq))
        o_hy = jnp.concatenate(o_hy_parts, axis=0)

        qn, kn, vn = _naprep(p_na, jnp.tile(na_q_g[l], NH).reshape(1, GW), jnp.tile(na_k_g[l], NH).reshape(1, GW),
                             gm64_b)
        o_na = jnp.concatenate([_na(qn, kn, vn, _na_bias_table(na_rpb[l])),
                                _attn_ctx(qn, kn, vn, 64, 64)], axis=0)

        wq, wk, wv, qg, kg = _mla_weights(mla_w_uq[l], mla_w_ukv[l], mla_q_g[l], mla_k_g[l])
        mq, mk, mv = _mlaprep(p_mla, mla_q_a_g[l].reshape(1, -1), mla_kv_a_g[l].reshape(1, -1), wq, wk, wv,
                              qg, kg, gm128_b, pm, cos_t, sin_t)
        o_mla = jnp.concatenate([_attn_latent(mq, mk, mv, 128, 64, 256), _attn_ctx(mq, mk, mv, 128, 64)], axis=0)

        wr = jnp.pad(jnp.concatenate([moe_wg[l], moe_we[l]], axis=1), ((0, 0), (0, 128 - N_GROUPS - N_EXPERTS)))
        wr_hi, wr_lo = _split2(wr)
        X1, h2, logits = _outproj(X, o_f, o_b, p_hg, o_hy, o_na, o_mla, mod_l,
                                  jnp.tile(hgrn_norm_g[l], NH).reshape(1, GW), gm64_b,
                                  w_out[l].astype(BF16), norm2_g[l].reshape(1, D),
                                  jnp.concatenate([wr_hi, wr_lo], axis=1))

        eid, wts = _route(logits, moe_bg[l], moe_be[l])
        block_e, nblk, tok, orow, wslot = _dispatch_tables(eid, wts)
        y2 = _moe(h2, block_e, nblk, tok, orow, wslot, moe_w_gate[l], moe_w_up[l], moe_w_down[l])
        X = _combine(X1, y2.reshape(T_ALL, 2 * D), mod_l)

    return X[:T_LAT].reshape(B, L, D)
```

```python
import functools
import math

import jax
import jax.numpy as jnp
from jax import lax
from jax.experimental import pallas as pl
from jax.experimental.pallas import tpu as pltpu

F32 = jnp.float32
BF16 = jnp.bfloat16

D = 1024
B = 8
L = 2048
CTX = 256
DEPTH = 4
GRID_W = 64
EPS = 1e-6
GW = 256
NH = 4
HYENA_BANDS = 16
HYENA_EMB = 1 + 2 * HYENA_BANDS
HYENA_FFN = 64
NA_ROWS = 8
NA_COLS = 16
MLA_Q_RANK = 256
MLA_KV_RANK = 128
MLA_NOPE = 64
MLA_ROPE = 32
MLA_QK = MLA_NOPE + MLA_ROPE
ROPE_BASE = 10000.0
N_GROUPS = 4
EPG = 8
N_EXPERTS = N_GROUPS * EPG
D_EXPERT = 512
D_IN = 3232
D_IN_PAD = 3328

T_LAT = B * L
T_CTX = B * CTX
T_ALL = T_LAT + T_CTX

TM = 256
N_TILES = T_ALL // TM
LAT_TILES = T_LAT // TM
TILES_PER_SEQ = L // TM
HCHUNK = 64
HSUB = 16
TMOE = 256
VMEM_LIMIT_BYTES = 56 * 1024 * 1024
NEG = -1e30

HI = lax.Precision.HIGHEST


def _cparams(sem, vmem=VMEM_LIMIT_BYTES):
    return pltpu.CompilerParams(dimension_semantics=sem, vmem_limit_bytes=vmem)


def _seg_of_tile(i):
    return jnp.where(i < LAT_TILES, i // TILES_PER_SEQ, B)


def _dot(a, b):
    return jnp.dot(a, b, preferred_element_type=F32)


def _dot_nt(a, b):
    return lax.dot_general(a, b, (((1,), (1,)), ((), ())), preferred_element_type=F32)


def _dot_tn(a, b):
    return lax.dot_general(a, b, (((0,), (0,)), ((), ())), preferred_element_type=F32)


def _split2(x):
    hi = x.astype(BF16)
    lo = (x - hi.astype(F32)).astype(BF16)
    return hi, lo


def _split3(x):
    h1 = x.astype(BF16)
    r1 = x - h1.astype(F32)
    h2 = r1.astype(BF16)
    h3 = (r1 - h2.astype(F32)).astype(BF16)
    return h1, h2, h3


def _group_sum(x, gm):
    hi, lo = _split2(x)
    return _dot(hi, gm) + _dot(lo, gm)


def _ada_kernel(c_ref, w_ref, b_ref, o_ref):
    cc = c_ref[...]
    sc = cc * jax.nn.sigmoid(cc)
    o_ref[0] = jnp.dot(sc, w_ref[0], preferred_element_type=F32, precision=HI) + b_ref[0]


def _adaln(cmat, w_ada, b_ada):
    tn = 1536
    return pl.pallas_call(
        _ada_kernel,
        grid=(DEPTH, 6 * D // tn),
        in_specs=[
            pl.BlockSpec((16, D), lambda l, j: (0, 0)),
            pl.BlockSpec((1, D, tn), lambda l, j: (l, 0, j)),
            pl.BlockSpec((1, 1, tn), lambda l, j: (l, 0, j)),
        ],
        out_specs=pl.BlockSpec((1, 16, tn), lambda l, j: (l, 0, j)),
        out_shape=jax.ShapeDtypeStruct((DEPTH, 16, 6 * D), F32),
        compiler_params=_cparams(("arbitrary", "arbitrary")),
        name="adaln",
    )(cmat, w_ada, b_ada.reshape(DEPTH, 1, 6 * D))


def _inproj_kernel(x_ref, mod_ref, g_ref, w_ref, o_hg, o_hy, o_na, o_mla):
    x = x_ref[...]
    ms = jnp.mean(x * x, axis=-1, keepdims=True)
    y = x * lax.rsqrt(ms + EPS) * g_ref[...]
    h = y * (1.0 + mod_ref[1:2, :]) + mod_ref[0:1, :]
    p = _dot(h.astype(BF16), w_ref[...])
    o_hg[...] = p[:, 0:1280]
    o_hy[...] = p[:, 1280:2048]
    o_na[...] = p[:, 2048:2816]
    o_mla[...] = p[:, 2816:3328]


def _inproj(X, mod_l, g, w_bf):
    return pl.pallas_call(
        _inproj_kernel,
        grid=(N_TILES,),
        in_specs=[
            pl.BlockSpec((TM, D), lambda i: (i, 0)),
            pl.BlockSpec((None, 6, D), lambda i: (_seg_of_tile(i), 0, 0)),
            pl.BlockSpec((1, D), lambda i: (0, 0)),
            pl.BlockSpec((D, D_IN_PAD), lambda i: (0, 0)),
        ],
        out_specs=[
            pl.BlockSpec((TM, 1280), lambda i: (i, 0)),
            pl.BlockSpec((TM, 768), lambda i: (i, 0)),
            pl.BlockSpec((TM, 768), lambda i: (i, 0)),
            pl.BlockSpec((TM, 512), lambda i: (i, 0)),
        ],
        out_shape=[
            jax.ShapeDtypeStruct((T_ALL, 1280), F32),
            jax.ShapeDtypeStruct((T_ALL, 768), F32),
            jax.ShapeDtypeStruct((T_ALL, 768), F32),
            jax.ShapeDtypeStruct((T_ALL, 512), F32),
        ],
        compiler_params=_cparams(("parallel",)),
        name="inproj",
    )(X, mod_l, g, w_bf)


def _hgrn_direction(p_ref, zcol, c_ref, crow, tri_ref, gm_ref, gmb_ref, st_ref, o_ref, reverse):
    q = p_ref[:, 0:GW]
    z = p_ref[:, zcol:zcol + GW]
    v = p_ref[:, 3 * GW:4 * GW]
    la = c_ref[crow:crow + 1, :]
    l1 = c_ref[crow + 1:crow + 2, :]
    oml = c_ref[crow + 2:crow + 3, :]
    ls = jnp.minimum(z, 0.0) - jnp.log1p(jnp.exp(-jnp.abs(z)))
    c2 = l1 + ls
    logf = jnp.maximum(la, c2) + jnp.log1p(jnp.exp(-jnp.abs(la - c2)))
    kk = oml * jax.nn.sigmoid(-z)
    tri = tri_ref[...]
    h1, h2, h3 = _split3(logf)
    bsub = _dot(tri, h1) + _dot(tri, h2) + _dot(tri, h3)
    row = lax.broadcasted_iota(jnp.int32, (HSUB, GW), 0)
    st = st_ref[...]
    gm = gm_ref[...]
    gmb = gmb_ref[...]
    order = range(HCHUNK // HSUB - 1, -1, -1) if reverse else range(HCHUNK // HSUB)
    for blk in order:
        r0 = blk * HSUB
        b_i = bsub[r0:r0 + HSUB]
        q_i = q[r0:r0 + HSUB]
        k_i = kk[r0:r0 + HSUB]
        v_i = v[r0:r0 + HSUB]
        btot = b_i[0:1] if reverse else b_i[HSUB - 1:HSUB]
        qe = (q_i * jnp.exp(b_i)).astype(BF16)
        o_inter = _dot_nt(qe, st.astype(BF16))
        parts = []
        for tl in range(HSUB):
            dlt = b_i[tl:tl + 1] - b_i
            valid = (row >= tl) if reverse else (row <= tl)
            w = jnp.exp(jnp.where(valid, dlt, NEG))
            parts.append((q_i[tl:tl + 1] * w) * k_i)
        pmat = jnp.concatenate(parts, axis=0).astype(BF16)
        abar = _dot(pmat, gmb)
        o_diag = jnp.sum(abar.reshape(HSUB, HSUB, GW) * v_i[None], axis=1)
        o_ref[r0:r0 + HSUB, :] = o_inter + o_diag
        kd = (k_i * jnp.exp(btot - b_i)).astype(BF16)
        upd = _dot_tn(v_i.astype(BF16), kd)
        st = st * jnp.exp(btot) + upd * gm
    st_ref[...] = st


def _hgrn_kernel(pf_ref, pb_ref, c_ref, trif_ref, trib_ref, gm_ref, gmb_ref, of_ref, ob_ref, stf, stb):
    @pl.when(pl.program_id(1) == 0)
    def _():
        stf[...] = jnp.zeros_like(stf)
        stb[...] = jnp.zeros_like(stb)

    _hgrn_direction(pf_ref, GW, c_ref, 0, trif_ref, gm_ref, gmb_ref, stf, of_ref, False)
    _hgrn_direction(pb_ref, 2 * GW, c_ref, 3, trib_ref, gm_ref, gmb_ref, stb, ob_ref, True)


def _hgrn_block(b, n, reverse):
    nctx = CTX // HCHUNK
    nlat = L // HCHUNK
    jc = (nctx - 1 - n) if reverse else n
    jl = (nlat - 1 - (n - nctx)) if reverse else (n - nctx)
    return jnp.where(n < nctx, T_LAT // HCHUNK + b * nctx + jc, b * nlat + jl)


def _hgrn(p_hg, consts, trif, trib, gm, gmb):
    nsteps = (CTX + L) // HCHUNK
    full = lambda shape: pl.BlockSpec(shape, lambda b, n: (0,) * len(shape))
    return pl.pallas_call(
        _hgrn_kernel,
        grid=(B, nsteps),
        in_specs=[
            pl.BlockSpec((HCHUNK, 1280), lambda b, n: (_hgrn_block(b, n, False), 0)),
            pl.BlockSpec((HCHUNK, 1280), lambda b, n: (_hgrn_block(b, n, True), 0)),
            full((8, GW)),
            full((HCHUNK, HCHUNK)),
            full((HCHUNK, HCHUNK)),
            full((GW, GW)),
            full((GW, GW)),
        ],
        out_specs=[
            pl.BlockSpec((HCHUNK, GW), lambda b, n: (_hgrn_block(b, n, False), 0)),
            pl.BlockSpec((HCHUNK, GW), lambda b, n: (_hgrn_block(b, n, True), 0)),
        ],
        out_shape=[jax.ShapeDtypeStruct((T_ALL, GW), F32), jax.ShapeDtypeStruct((T_ALL, GW), F32)],
        scratch_shapes=[pltpu.VMEM((GW, GW), F32), pltpu.VMEM((GW, GW), F32)],
        compiler_params=_cparams(("arbitrary", "arbitrary")),
        name="hgrn",
    )(p_hg, p_hg, consts, trif, trib, gm, gmb)


def _alt_sum(x):
    n, c = x.shape
    sgn = jnp.where((lax.broadcasted_iota(jnp.int32, (n, c), 0) & 1) == 0, 1.0, -1.0)
    return jnp.sum(x * sgn, axis=0, keepdims=True)


def _hyfilt_kernel(feats_ref, w1_ref, b1_ref, fr_ref, w2_ref, b2_ref, w3_ref, b3_ref, dec_ref,
                   e_ref, o_ref, nq_ref):
    fr = fr_ref[...]
    feats = feats_ref[...]
    h = jnp.sin(fr * (jnp.dot(feats, w1_ref[...], preferred_element_type=F32, precision=HI) + b1_ref[...]))
    h = jnp.sin(fr * (jnp.dot(h, w2_ref[...], preferred_element_type=F32, precision=HI) + b2_ref[...]))
    filt = jnp.dot(h, w3_ref[...], preferred_element_type=F32, precision=HI) + b3_ref[...]
    filt = filt * jnp.exp(-feats[:, 0:1] * dec_ref[...])
    n = filt.shape[0]
    row = lax.broadcasted_iota(jnp.int32, (n, GW), 0)
    for o in range(2):
        fwd = filt[:, (2 * o) * GW:(2 * o + 1) * GW]
        bwd = jnp.where(row >= 1, filt[:, (2 * o + 1) * GW:(2 * o + 2) * GW], 0.0)
        ssq = jnp.sum(fwd * fwd + bwd * bwd, axis=0, keepdims=True)
        scale = lax.rsqrt(ssq + EPS)
        ev = (fwd + bwd) * scale
        e_ref[:, o * GW:(o + 1) * GW] = ev
        o_ref[:, o * GW:(o + 1) * GW] = (fwd - bwd) * scale
        nq_ref[:, o * GW:(o + 1) * GW] = _alt_sum(ev) * (0.5 / n)


def _hyfilt(feats, w1p, b1, fr, w2, b2, w3, b3, dec):
    n = feats.shape[0]
    return pl.pallas_call(
        _hyfilt_kernel,
        out_shape=[jax.ShapeDtypeStruct((n, 2 * GW), F32), jax.ShapeDtypeStruct((n, 2 * GW), F32),
                   jax.ShapeDtypeStruct((1, 2 * GW), F32)],
        compiler_params=_cparams(None),
        name="hyfilt",
    )(feats, w1p, b1, fr, w2, b2, w3, b3, dec)


def _hyspec_kernel(chi_ref, clo_ref, shi_ref, slo_ref, e_ref, o_ref, kre_ref, kim_ref, *, n):
    eh, el = _split2(e_ref[...])
    oh, ol = _split2(o_ref[...])
    kre = _dot(chi_ref[...], eh) + _dot(chi_ref[...], el) + _dot(clo_ref[...], eh)
    kim = _dot(shi_ref[...], oh) + _dot(shi_ref[...], ol) + _dot(slo_ref[...], oh)
    tr = kre.shape[0]
    grow = lax.broadcasted_iota(jnp.int32, kre.shape, 0) + pl.program_id(0) * tr
    s2 = 1.0 / n
    kre_ref[...] = kre * jnp.where(grow == 0, 0.5 * s2, s2)
    kim_ref[...] = kim * s2


def _hyspec(chi, clo, shi, slo, e, o):
    n = e.shape[0]
    tr = min(256, n)
    rows = pl.BlockSpec((tr, n), lambda i: (i, 0))
    full = pl.BlockSpec((n, 2 * GW), lambda i: (0, 0))
    outb = pl.BlockSpec((tr, 2 * GW), lambda i: (i, 0))
    return pl.pallas_call(
        functools.partial(_hyspec_kernel, n=n),
        grid=(n // tr,),
        in_specs=[rows, rows, rows, rows, full, full],
        out_specs=[outb, outb],
        out_shape=[jax.ShapeDtypeStruct((n, 2 * GW), F32)] * 2,
        compiler_params=_cparams(("parallel",)),
        name="hyspec",
    )(chi, clo, shi, slo, e, o)


def _hyena_kernel(u_ref, sw_ref, sb_ref, db_ref, c_ref, s_ref, kre_ref, kim_ref, knq_ref, o_ref,
                  z_scr, zb_scr, y_scr):
    n = u_ref.shape[0]
    ft = min(512, n)
    rc = min(256, n)
    nchunks = n // rc
    lrow = lax.broadcasted_iota(jnp.int32, (rc, GW), 0)
    sgn = jnp.where((lrow & 1) == 0, 1.0, -1.0)

    def short_conv(part, c):
        sl = slice(part * GW, (part + 1) * GW)
        r0 = c * rc
        u = u_ref[r0:r0 + rc, sl]
        prev = u_ref[r0 - 1:r0, sl] if c > 0 else jnp.zeros((1, GW), F32)
        nxt = u_ref[r0 + rc:r0 + rc + 1, sl] if c < nchunks - 1 else jnp.zeros((1, GW), F32)
        up = jnp.where(lrow == 0, prev, pltpu.roll(u, 1, 0))
        un = jnp.where(lrow == rc - 1, nxt, pltpu.roll(u, rc - 1, 0))
        return sw_ref[0:1, sl] * up + sw_ref[1:2, sl] * u + sw_ref[2:3, sl] * un + sb_ref[:, sl]

    for c in range(nchunks):
        z_scr[c * rc:(c + 1) * rc, :] = short_conv(0, c)
    for o in range(2):
        cols = slice(o * GW, (o + 1) * GW)
        znq = jnp.zeros((1, GW), F32)
        for c in range(nchunks):
            zc = z_scr[c * rc:(c + 1) * rc, :]
            zb_scr[c * rc:(c + 1) * rc, :] = zc.astype(BF16)
            znq = znq + jnp.sum(zc * sgn, axis=0, keepdims=True)
        ynq = znq * knq_ref[:, cols]
        for c in range(nchunks):
            y_scr[c * rc:(c + 1) * rc, :] = sgn * ynq
        for f in range(n // ft):
            rs = slice(f * ft, (f + 1) * ft)
            zre = _dot(c_ref[rs, :], zb_scr[...])
            zim = _dot(s_ref[rs, :], zb_scr[...])
            kre = kre_ref[rs, cols]
            kim = kim_ref[rs, cols]
            yre = (zre * kre - zim * kim).astype(BF16)
            yim = (zre * kim + zim * kre).astype(BF16)
            y_scr[...] += _dot(c_ref[:, rs], yre) + _dot(s_ref[:, rs], yim)
        dst = o_ref if o == 1 else z_scr
        for c in range(nchunks):
            rows = slice(c * rc, (c + 1) * rc)
            dst[rows, :] = short_conv(o + 1, c) * (y_scr[rows, :] + db_ref[o:o + 1, :] * z_scr[rows, :])


def _hyena(u, blk0, nb, n, sw, sb, db, cm, sm, kre, kim, knq):
    whole = pl.BlockSpec(memory_space=pltpu.VMEM)
    return pl.pallas_call(
        _hyena_kernel,
        grid=(nb,),
        in_specs=[
            pl.BlockSpec((n, 3 * GW), lambda b: (blk0 + b, 0)),
            pl.BlockSpec((3, 3 * GW), lambda b: (0, 0)),
            pl.BlockSpec((1, 3 * GW), lambda b: (0, 0)),
            pl.BlockSpec((2, GW), lambda b: (0, 0)),
            whole, whole, whole, whole, whole,
        ],
        out_specs=pl.BlockSpec((n, GW), lambda b: (b, 0)),
        out_shape=jax.ShapeDtypeStruct((nb * n, GW), F32),
        scratch_shapes=[pltpu.VMEM((n, GW), F32), pltpu.VMEM((n, GW), BF16), pltpu.VMEM((n, GW), F32)],
        compiler_params=_cparams(("arbitrary",)),
        name="hyena",
    )(u, sw, sb, db, cm, sm, kre, kim, knq)


def _dft_consts(n):
    kk = jnp.arange(n, dtype=jnp.int32)
    ph = (kk[:, None] * kk[None, :]) % (2 * n)
    ang = ph.astype(F32) * (math.pi / n)
    return jnp.cos(ang), -jnp.sin(ang)


def _hyena_feats(n):
    t = jnp.arange(n, dtype=F32)
    t_unit = jnp.linspace(0.0, 1.0, n, dtype=F32)
    bands = jnp.linspace(1e-4, HYENA_BANDS - 1, HYENA_BANDS, dtype=F32)
    ang = (2.0 * math.pi / n) * t[:, None] * bands[None, :]
    feats = jnp.concatenate([t_unit[:, None], jnp.cos(ang), -jnp.sin(ang)], axis=-1)
    return jnp.pad(feats, ((0, 0), (0, 128 - HYENA_EMB)))


def _naprep_kernel(p_ref, qg_ref, kg_ref, gm_ref, q_ref, k_ref, v_ref):
    p = p_ref[...]
    q = p[:, 0:GW]
    k = p[:, GW:2 * GW]
    gm = gm_ref[...]
    qn = q * lax.rsqrt(_group_sum(q * q, gm) * (1.0 / 64) + EPS) * qg_ref[...]
    kn = k * lax.rsqrt(_group_sum(k * k, gm) * (1.0 / 64) + EPS) * kg_ref[...]
    q_ref[...] = (qn * (64 ** -0.5)).astype(BF16)
    k_ref[...] = kn.astype(BF16)
    v_ref[...] = p[:, 2 * GW:3 * GW].astype(BF16)


def _naprep(p_na, qg, kg, gmb):
    tok = lambda w: pl.BlockSpec((TM, w), lambda i: (i, 0))
    full = lambda shape: pl.BlockSpec(shape, lambda i: (0,) * len(shape))
    return pl.pallas_call(
        _naprep_kernel,
        grid=(N_TILES,),
        in_specs=[tok(768), full((1, GW)), full((1, GW)), full((GW, GW))],
        out_specs=[tok(GW), tok(GW), tok(GW)],
        out_shape=[jax.ShapeDtypeStruct((T_ALL, GW), BF16)] * 3,
        compiler_params=_cparams(("parallel",)),
        name="naprep",
    )(p_na, qg, kg, gmb)


def _na_kernel(q_ref, k_ref, v_ref, kc_ref, vc_ref, bias_ref, o_ref):
    r = pl.program_id(1)
    rows = L // GRID_W
    rs = jnp.clip(r - NA_ROWS // 2, 0, rows - NA_ROWS)
    start = pl.multiple_of(rs * GRID_W, GRID_W)
    kw = k_ref[pl.ds(start, NA_ROWS * GRID_W), :]
    vw = v_ref[pl.ds(start, NA_ROWS * GRID_W), :]
    q = q_ref[...]
    kc = kc_ref[...]
    vc = vc_ref[...]
    outs = []
    for h in range(NH):
        sl = slice(h * 64, (h + 1) * 64)
        s_loc = _dot_nt(q[:, sl], kw[:, sl]) + bias_ref[h]
        s_ctx = _dot_nt(q[:, sl], kc[:, sl])
        m = jnp.maximum(jnp.max(s_loc, axis=-1, keepdims=True), jnp.max(s_ctx, axis=-1, keepdims=True))
        p_loc = jnp.exp(s_loc - m)
        p_ctx = jnp.exp(s_ctx - m)
        den = jnp.sum(p_loc, axis=-1, keepdims=True) + jnp.sum(p_ctx, axis=-1, keepdims=True)
        o = _dot(p_loc.astype(BF16), vw[:, sl]) + _dot(p_ctx.astype(BF16), vc[:, sl])
        outs.append(o / den)
    o_ref[...] = jnp.concatenate(outs, axis=-1)


def _na_variant(r):
    rows = L // GRID_W
    return r - jnp.clip(r - NA_ROWS // 2, 0, rows - NA_ROWS)


def _na(qn, kn, vn, bias_t):
    rows = L // GRID_W
    ctx_blk = T_LAT // CTX
    return pl.pallas_call(
        _na_kernel,
        grid=(B, rows),
        in_specs=[
            pl.BlockSpec((GRID_W, GW), lambda b, r: (b * rows + r, 0)),
            pl.BlockSpec((L, GW), lambda b, r: (b, 0)),
            pl.BlockSpec((L, GW), lambda b, r: (b, 0)),
            pl.BlockSpec((CTX, GW), lambda b, r: (ctx_blk + b, 0)),
            pl.BlockSpec((CTX, GW), lambda b, r: (ctx_blk + b, 0)),
            pl.BlockSpec((None, NH, GRID_W, NA_ROWS * GRID_W), lambda b, r: (_na_variant(r), 0, 0, 0)),
        ],
        out_specs=pl.BlockSpec((GRID_W, GW), lambda b, r: (b * rows + r, 0)),
        out_shape=jax.ShapeDtypeStruct((T_LAT, GW), F32),
        compiler_params=_cparams(("arbitrary", "arbitrary")),
        name="na",
    )(qn, kn, vn, kn, vn, bias_t)


def _na_bias_table(rpb):
    cq = jnp.arange(GRID_W)
    cs = jnp.clip(cq - NA_COLS // 2, 0, GRID_W - NA_COLS)
    col_ok = (cq[None, :] >= cs[:, None]) & (cq[None, :] < cs[:, None] + NA_COLS)
    dc = jnp.clip(cq[None, :] - cq[:, None] + (NA_COLS - 1), 0, 2 * NA_COLS - 2)
    onehot = (dc[:, :, None] == jnp.arange(2 * NA_COLS - 1)[None, None, :]).astype(F32)
    full = jnp.einsum('qkc,hrc->hrqk', onehot, rpb.astype(F32), precision=HI)
    full = jnp.where(col_ok[None, None], full, NEG)
    tab = jnp.stack([full[:, NA_ROWS - 1 - a:2 * NA_ROWS - 1 - a] for a in range(NA_ROWS)], axis=0)
    return tab.transpose(0, 1, 3, 2, 4).reshape(NA_ROWS, NH, GRID_W, NA_ROWS * GRID_W)


def _attn_kernel(*refs, nkv, dq, dv):
    q = refs[0][...]
    ks = [refs[1 + 2 * j][...] for j in range(nkv)]
    vs = [refs[2 + 2 * j][...] for j in range(nkv)]
    o_ref = refs[1 + 2 * nkv]
    outs = []
    for h in range(NH):
        qh = q[:, h * dq:(h + 1) * dq]
        ss = [_dot_nt(qh, k[:, h * dq:(h + 1) * dq]) for k in ks]
        m = functools.reduce(jnp.maximum, [jnp.max(s, axis=-1, keepdims=True) for s in ss])
        ps = [jnp.exp(s - m) for s in ss]
        den = functools.reduce(lambda a, b2: a + b2, [jnp.sum(p, axis=-1, keepdims=True) for p in ps])
        o = functools.reduce(lambda a, b2: a + b2,
                             [_dot(p.astype(BF16), v[:, h * dv:(h + 1) * dv]) for p, v in zip(ps, vs)])
        outs.append(o / den)
    o_ref[...] = jnp.concatenate(outs, axis=-1)


def _attn_latent(q, k, v, dq, dv, tq):
    nq = L // tq
    ctx_blk = T_LAT // CTX
    return pl.pallas_call(
        functools.partial(_attn_kernel, nkv=2, dq=dq, dv=dv),
        grid=(B, nq),
        in_specs=[
            pl.BlockSpec((tq, NH * dq), lambda b, i: (b * nq + i, 0)),
            pl.BlockSpec((L, NH * dq), lambda b, i: (b, 0)),
            pl.BlockSpec((L, NH * dv), lambda b, i: (b, 0)),
            pl.BlockSpec((CTX, NH * dq), lambda b, i: (ctx_blk + b, 0)),
            pl.BlockSpec((CTX, NH * dv), lambda b, i: (ctx_blk + b, 0)),
        ],
        out_specs=pl.BlockSpec((tq, NH * dv), lambda b, i: (b * nq + i, 0)),
        out_shape=jax.ShapeDtypeStruct((T_LAT, NH * dv), F32),
        compiler_params=_cparams(("arbitrary", "arbitrary")),
        name="attn_latent",
    )(q, k, v, k, v)


def _attn_ctx(q, k, v, dq, dv):
    ctx_blk = T_LAT // CTX
    return pl.pallas_call(
        functools.partial(_attn_kernel, nkv=1, dq=dq, dv=dv),
        grid=(B,),
        in_specs=[
            pl.BlockSpec((CTX, NH * dq), lambda b: (ctx_blk + b, 0)),
            pl.BlockSpec((CTX, NH * dq), lambda b: (ctx_blk + b, 0)),
            pl.BlockSpec((CTX, NH * dv), lambda b: (ctx_blk + b, 0)),
        ],
        out_specs=pl.BlockSpec((CTX, NH * dv), lambda b: (b, 0)),
        out_shape=jax.ShapeDtypeStruct((T_CTX, NH * dv), F32),
        compiler_params=_cparams(("arbitrary",)),
        name="attn_ctx",
    )(q, k, v)


def _mlaprep_kernel(p_ref, qag_ref, kvag_ref, wq_ref, wk_ref, wv_ref, qg_ref, kg_ref, gm_ref, pm_ref,
                    cos_ref, sin_ref, q_ref, k_ref, v_ref):
    p = p_ref[...]
    cq = p[:, 0:MLA_Q_RANK]
    ckv = p[:, MLA_Q_RANK:MLA_Q_RANK + MLA_KV_RANK]
    krp = p[:, MLA_Q_RANK + MLA_KV_RANK:]
    cqn = cq * lax.rsqrt(jnp.mean(cq * cq, axis=-1, keepdims=True) + EPS) * qag_ref[...]
    ckvn = ckv * lax.rsqrt(jnp.mean(ckv * ckv, axis=-1, keepdims=True) + EPS) * kvag_ref[...]
    ckvb = ckvn.astype(BF16)
    q = _dot(cqn.astype(BF16), wq_ref[...])
    k = _dot(jnp.concatenate([ckvb, krp.astype(BF16)], axis=-1), wk_ref[...])
    v = _dot(ckvb, wv_ref[...])
    gm = gm_ref[...]
    q = q * lax.rsqrt(_group_sum(q * q, gm) * (1.0 / MLA_QK) + EPS) * qg_ref[...]
    k = k * lax.rsqrt(_group_sum(k * k, gm) * (1.0 / MLA_QK) + EPS) * kg_ref[...]
    cos = cos_ref[...]
    sin = sin_ref[...]
    pm = pm_ref[...]
    q = q * cos + _dot(q.astype(BF16), pm) * sin
    k = k * cos + _dot(k.astype(BF16), pm) * sin
    q_ref[...] = (q * (MLA_QK ** -0.5)).astype(BF16)
    k_ref[...] = k.astype(BF16)
    v_ref[...] = v.astype(BF16)


def _mlaprep(p_mla, qag, kvag, wq, wk, wv, qg, kg, gm, pm, cos_t, sin_t):
    tok = lambda w: pl.BlockSpec((TM, w), lambda i: (i, 0))
    full = lambda shape: pl.BlockSpec(shape, lambda i: (0,) * len(shape))
    pos = pl.BlockSpec((TM, 512), lambda i: (jnp.where(i < LAT_TILES, i % TILES_PER_SEQ, TILES_PER_SEQ), 0))
    return pl.pallas_call(
        _mlaprep_kernel,
        grid=(N_TILES,),
        in_specs=[tok(512), full((1, 256)), full((1, 128)), full((256, 512)), full((256, 512)),
                  full((128, 256)), full((1, 512)), full((1, 512)), full((512, 512)), full((512, 512)),
                  pos, pos],
        out_specs=[tok(512), tok(512), tok(GW)],
        out_shape=[jax.ShapeDtypeStruct((T_ALL, 512), BF16), jax.ShapeDtypeStruct((T_ALL, 512), BF16),
                   jax.ShapeDtypeStruct((T_ALL, GW), BF16)],
        compiler_params=_cparams(("parallel",)),
        name="mlaprep",
    )(p_mla, qag, kvag, wq, wk, wv, qg, kg, gm, pm, cos_t, sin_t)


def _rope_tables():
    t = jnp.arange(L)
    rowp = (t // GRID_W).astype(F32)
    colp = (t % GRID_W).astype(F32)
    half = MLA_ROPE // 2
    inv = ROPE_BASE ** (-jnp.arange(0, half, 2, dtype=F32) / half)
    j = jnp.arange(MLA_ROPE)
    pos = jnp.where(j[None, :] < half, rowp[:, None], colp[:, None])
    ang = pos * inv[j % (half // 2)][None, :]
    first = (j % half) < (half // 2)
    cos32 = jnp.cos(ang)
    sin32 = jnp.where(first[None, :], -jnp.sin(ang), jnp.sin(ang))
    cos_h = jnp.concatenate([jnp.ones((L, MLA_NOPE), F32), cos32, jnp.ones((L, 32), F32)], axis=-1)
    sin_h = jnp.concatenate([jnp.zeros((L, MLA_NOPE), F32), sin32, jnp.zeros((L, 32), F32)], axis=-1)
    cos_t = jnp.concatenate([jnp.tile(cos_h, (1, NH)), jnp.ones((TM, 512), F32)], axis=0)
    sin_t = jnp.concatenate([jnp.tile(sin_h, (1, NH)), jnp.zeros((TM, 512), F32)], axis=0)
    lane = jnp.arange(512)
    jj = lane % 128 - MLA_NOPE
    is_rope = (jj >= 0) & (jj < MLA_ROPE)
    partner = jnp.where(is_rope, jnp.where((jj % half) < (half // 2), lane + half // 2, lane - half // 2), lane)
    pm = (lane[:, None] == partner[None, :]).astype(BF16)
    return cos_t, sin_t, pm


def _outproj_kernel(x_ref, of_ref, ob_ref, g_ref, hyl_ref, hyc_ref, nal_ref, nac_ref, mll_ref, mlc_ref,
                    mod_ref, ng_ref, gm_ref, w_ref, n2_ref, wr_ref, x1_ref, h2_ref, lg_ref):
    oa = of_ref[...] + ob_ref[...]
    ms = _group_sum(oa * oa, gm_ref[...]) * (1.0 / 64)
    g = g_ref[...]
    oa = oa * lax.rsqrt(ms + EPS) * ng_ref[...] * (g * jax.nn.sigmoid(g))
    lat = pl.program_id(0) < LAT_TILES
    hy = jnp.where(lat, hyl_ref[...], hyc_ref[...])
    na = jnp.where(lat, nal_ref[...], nac_ref[...])
    mla = jnp.where(lat, mll_ref[...], mlc_ref[...])
    mix = jnp.concatenate([oa, hy, na, mla], axis=-1).astype(BF16)
    x1 = x_ref[...] + mod_ref[2:3, :] * _dot(mix, w_ref[...])
    x1_ref[...] = x1
    ms2 = jnp.mean(x1 * x1, axis=-1, keepdims=True)
    h2 = x1 * lax.rsqrt(ms2 + EPS) * n2_ref[...] * (1.0 + mod_ref[4:5, :]) + mod_ref[3:4, :]
    h2_ref[...] = h2
    hh, hl = _split2(h2)
    wr = wr_ref[...]
    lg_ref[...] = _dot(hh, wr[:, 0:128]) + _dot(hl, wr[:, 0:128]) + _dot(hh, wr[:, 128:256])


def _outproj(n_tiles, X, o_f, o_b, p_hg, hy, na, mla, mod_l, ng, gmb, w_bf, n2g, wr):
    tok = lambda w: pl.BlockSpec((TM, w), lambda i: (i, 0))
    full = lambda shape: pl.BlockSpec(shape, lambda i: (0,) * len(shape))
    latb = pl.BlockSpec((TM, GW), lambda i: (jnp.minimum(i, LAT_TILES - 1), 0))
    ctxb = pl.BlockSpec((TM, GW), lambda i: (jnp.maximum(i - LAT_TILES, 0), 0))
    nt = n_tiles * TM
    return pl.pallas_call(
        _outproj_kernel,
        grid=(n_tiles,),
        in_specs=[tok(D), tok(GW), tok(GW), pl.BlockSpec((TM, GW), lambda i: (i, 4)),
                  latb, ctxb, latb, ctxb, latb, ctxb,
                  pl.BlockSpec((None, 6, D), lambda i: (_seg_of_tile(i), 0, 0)),
                  full((1, GW)), full((GW, GW)), full((D, D)), full((1, D)), full((D, 256))],
        out_specs=[tok(D), tok(D), tok(128)],
        out_shape=[jax.ShapeDtypeStruct((nt, D), F32), jax.ShapeDtypeStruct((nt, D), F32),
                   jax.ShapeDtypeStruct((nt, 128), F32)],
        compiler_params=_cparams(("parallel",)),
        name="outproj",
    )(X, o_f, o_b, p_hg, hy[0], hy[1], na[0], na[1], mla[0], mla[1], mod_l, ng, gmb, w_bf, n2g, wr)


def _route(logits, bg, be):
    t = logits.shape[0]
    lg = logits[:, 0:N_GROUPS] + bg
    le = (logits[:, N_GROUPS:N_GROUPS + N_EXPERTS] + be).reshape(t, N_GROUPS, EPG)
    pg = jax.nn.softmax(lg, axis=-1)
    g_sel = jnp.argmax(lg, axis=-1).astype(jnp.int32)
    p_sel = jnp.take_along_axis(pg, g_sel[:, None], axis=-1)[:, 0]
    le_sel = jnp.take_along_axis(le, g_sel[:, None, None], axis=1)[:, 0]
    top_v, top_i = lax.top_k(le_sel, 2)
    wts = jax.nn.softmax(top_v, axis=-1) * p_sel[:, None]
    eid = g_sel[:, None] * EPG + top_i.astype(jnp.int32)
    return eid, wts


def _dispatch_tables(eid):
    t = eid.shape[0]
    n = 2 * t
    nb = n // TMOE + N_EXPERTS
    flat_e = eid.reshape(n)
    onehot = (flat_e[:, None] == jnp.arange(N_EXPERTS, dtype=jnp.int32)[None, :]).astype(jnp.int32)
    csum = jnp.cumsum(onehot, axis=0)
    pos = jnp.sum(csum * onehot, axis=1) - 1
    counts = csum[-1]
    pcounts = ((counts + TMOE - 1) // TMOE) * TMOE
    pends = jnp.cumsum(pcounts)
    pstarts = pends - pcounts
    dest = (jnp.sum(jnp.where(onehot > 0, pstarts[None, :], 0), axis=1) + pos).astype(jnp.int32)
    blk_start = jnp.arange(nb, dtype=jnp.int32) * TMOE
    block_e = jnp.minimum(jnp.sum((pends[None, :] <= blk_start[:, None]).astype(jnp.int32), axis=1),
                          N_EXPERTS - 1).astype(jnp.int32)
    nblk = (pends[-1] // TMOE).astype(jnp.int32).reshape(1)
    return block_e, nblk, dest


def _dispatch_kernel(dest_ref, h_ref, xs_in, xs_out, sem):
    del xs_in
    base = pl.program_id(0) * (2 * TM)

    def issue(r, carry):
        for k in range(2):
            pltpu.make_async_copy(h_ref.at[pl.ds(r, 1)], xs_out.at[pl.ds(dest_ref[base + 2 * r + k], 1)], sem).start()
        return carry

    lax.fori_loop(0, TM, issue, 0, unroll=8)

    def drain(r, carry):
        for k in range(2):
            pltpu.make_async_copy(h_ref.at[pl.ds(r, 1)], xs_out.at[pl.ds(0, 1)], sem).wait()
        return carry

    lax.fori_loop(0, TM, drain, 0, unroll=8)


def _dispatch(n_tiles, dest, h2, xs0):
    grid_spec = pltpu.PrefetchScalarGridSpec(
        num_scalar_prefetch=1,
        grid=(n_tiles,),
        in_specs=[pl.BlockSpec((TM, D), lambda i, dst: (i, 0)), pl.BlockSpec(memory_space=pl.ANY)],
        out_specs=pl.BlockSpec(memory_space=pl.ANY),
        scratch_shapes=[pltpu.SemaphoreType.DMA(())],
    )
    return pl.pallas_call(
        _dispatch_kernel,
        grid_spec=grid_spec,
        out_shape=jax.ShapeDtypeStruct(xs0.shape, F32),
        input_output_aliases={2: 0},
        compiler_params=_cparams(("arbitrary",)),
        name="dispatch",
    )(dest, h2, xs0)


def _experts_kernel(be_ref, nblk_ref, xs_ref, wg_ref, wu_ref, wd_ref, ys_ref, wgb, wub, wdb):
    i = pl.program_id(0)

    @pl.when((i == 0) | (be_ref[i] != be_ref[jnp.maximum(i - 1, 0)]))
    def _():
        wgb[...] = wg_ref[...].astype(BF16)
        wub[...] = wu_ref[...].astype(BF16)
        wdb[...] = wd_ref[...].astype(BF16)

    @pl.when(i < nblk_ref[0])
    def _():
        x = xs_ref[...].astype(BF16)
        gate = _dot(x, wgb[...])
        up = _dot(x, wub[...])
        act = (gate * jax.nn.sigmoid(gate)) * up
        ys_ref[...] = _dot(act.astype(BF16), wdb[...])

    @pl.when(i >= nblk_ref[0])
    def _():
        ys_ref[...] = jnp.zeros_like(ys_ref)


def _experts(layer, block_e, nblk, xs, w_gate, w_up, w_down):
    nb = block_e.shape[0]
    grid_spec = pltpu.PrefetchScalarGridSpec(
        num_scalar_prefetch=2,
        grid=(nb,),
        in_specs=[
            pl.BlockSpec((TMOE, D), lambda i, be, nk: (i, 0)),
            pl.BlockSpec((None, None, D, D_EXPERT), lambda i, be, nk: (layer, be[i], 0, 0)),
            pl.BlockSpec((None, None, D, D_EXPERT), lambda i, be, nk: (layer, be[i], 0, 0)),
            pl.BlockSpec((None, None, D_EXPERT, D), lambda i, be, nk: (layer, be[i], 0, 0)),
        ],
        out_specs=pl.BlockSpec((TMOE, D), lambda i, be, nk: (i, 0)),
        scratch_shapes=[pltpu.VMEM((D, D_EXPERT), BF16), pltpu.VMEM((D, D_EXPERT), BF16),
                        pltpu.VMEM((D_EXPERT, D), BF16)],
    )
    return pl.pallas_call(
        _experts_kernel,
        grid_spec=grid_spec,
        out_shape=jax.ShapeDtypeStruct(xs.shape, F32),
        compiler_params=_cparams(("arbitrary",)),
        name="experts",
    )(block_e, nblk, xs, w_gate, w_up, w_down)


def _combine_kernel(dest_ref, x_ref, w_ref, mod_ref, ys_hbm, o_ref, ybuf, sem):
    base = pl.program_id(0) * (2 * TM)

    def issue(r, carry):
        for k in range(2):
            pltpu.make_async_copy(ys_hbm.at[pl.ds(dest_ref[base + 2 * r + k], 1)], ybuf.at[k, pl.ds(r, 1)], sem).start()
        return carry

    lax.fori_loop(0, TM, issue, 0, unroll=8)

    def drain(r, carry):
        for k in range(2):
            pltpu.make_async_copy(ys_hbm.at[pl.ds(0, 1)], ybuf.at[k, pl.ds(r, 1)], sem).wait()
        return carry

    lax.fori_loop(0, TM, drain, 0, unroll=8)
    w = w_ref[...]
    y = w[:, 0:1] * ybuf[0] + w[:, 1:2] * ybuf[1]
    o_ref[...] = x_ref[...] + mod_ref[5:6, :] * y


def _combine(n_tiles, dest, X1, wts, mod_l, ys):
    grid_spec = pltpu.PrefetchScalarGridSpec(
        num_scalar_prefetch=1,
        grid=(n_tiles,),
        in_specs=[pl.BlockSpec((TM, D), lambda i, dst: (i, 0)),
                  pl.BlockSpec((TM, 2), lambda i, dst: (i, 0)),
                  pl.BlockSpec((None, 6, D), lambda i, dst: (_seg_of_tile(i), 0, 0)),
                  pl.BlockSpec(memory_space=pl.ANY)],
        out_specs=pl.BlockSpec((TM, D), lambda i, dst: (i, 0)),
        scratch_shapes=[pltpu.VMEM((2, TM, D), F32), pltpu.SemaphoreType.DMA(())],
    )
    return pl.pallas_call(
        _combine_kernel,
        grid_spec=grid_spec,
        out_shape=jax.ShapeDtypeStruct((n_tiles * TM, D), F32),
        compiler_params=_cparams(("arbitrary",)),
        name="combine",
    )(dest, X1, wts, mod_l, ys)


def _group_mask(width, group):
    lane = jnp.arange(width)
    return (lane[:, None] // group == lane[None, :] // group)


def _hgrn_tri(reverse):
    t = jnp.arange(HCHUNK)
    same = (t[:, None] // HSUB) == (t[None, :] // HSUB)
    order = (t[None, :] >= t[:, None]) if reverse else (t[None, :] <= t[:, None])
    return (same & order).astype(BF16)


def _mla_weights(w_uq, w_ukv, q_g, k_g):
    wq = jnp.pad(w_uq.reshape(MLA_Q_RANK, NH, MLA_QK), ((0, 0), (0, 0), (0, 128 - MLA_QK))).reshape(MLA_Q_RANK, 512)
    kv = w_ukv.reshape(MLA_KV_RANK, NH, MLA_NOPE + 64)
    wk_top = jnp.pad(kv[:, :, :MLA_NOPE], ((0, 0), (0, 0), (0, 128 - MLA_NOPE))).reshape(MLA_KV_RANK, 512)
    lane = jnp.arange(512)
    src = jnp.arange(128)
    place = ((lane[None, :] % 128) == (src[:, None] + MLA_NOPE)) & (src[:, None] < MLA_ROPE)
    wk = jnp.concatenate([wk_top, place.astype(F32)], axis=0)
    wv = kv[:, :, MLA_NOPE:].reshape(MLA_KV_RANK, GW)
    pad_g = lambda g: jnp.tile(jnp.pad(g, (0, 128 - MLA_QK)), NH).reshape(1, 512)
    return wq.astype(BF16), wk.astype(BF16), wv.astype(BF16), pad_g(q_g), pad_g(k_g)


def kernel(x, c, ctx, c_ctx, w_ada, b_ada, norm1_g, norm2_g, w_in, w_out, hgrn_lb_logits, hgrn_norm_g,
           hy_short_w, hy_short_b, hy_w1, hy_b1, hy_freq, hy_w2, hy_b2, hy_w3, hy_b3, hy_decay, hy_bias,
           na_rpb, na_q_g, na_k_g, mla_q_a_g, mla_kv_a_g, mla_w_uq, mla_w_ukv, mla_q_g, mla_k_g,
           moe_wg, moe_bg, moe_we, moe_be, moe_w_gate, moe_w_up, moe_w_down):
    X = jnp.concatenate([x.reshape(T_LAT, D), ctx.reshape(T_CTX, D)], axis=0)
    cmat = jnp.concatenate([c, c_ctx[None, :], jnp.zeros((16 - B - 1, D), F32)], axis=0)
    mod = _adaln(cmat, w_ada, b_ada).reshape(DEPTH, 16, 6, D)

    lb_cum = jnp.cumsum(jax.nn.softmax(hgrn_lb_logits.astype(F32), axis=0), axis=0)
    lower = lb_cum - lb_cum[0:1]

    gm64 = _group_mask(GW, 64)
    gm64_f = gm64.astype(F32)
    gm64_b = gm64.astype(BF16)
    gm128_b = _group_mask(512, 128).astype(BF16)
    trif = _hgrn_tri(False)
    trib = _hgrn_tri(True)
    cos_t, sin_t, pm = _rope_tables()
    dft = {}
    for n in (L, CTX):
        cm, sm = _dft_consts(n)
        chi, clo = _split2(cm)
        shi, slo = _split2(sm)
        dft[n] = (chi, clo, shi, slo, _hyena_feats(n))

    for l in range(DEPTH):
        mod_l = mod[l]
        p_hg, p_hy, p_na, p_mla = _inproj(X, mod_l, norm1_g[l].reshape(1, D),
                                          jnp.pad(w_in[l], ((0, 0), (0, D_IN_PAD - D_IN))).astype(BF16))

        lb = lower[l]
        hconst = jnp.concatenate([
            jnp.stack([jnp.maximum(jnp.log(lb[d]), NEG), jnp.log1p(-lb[d]), 1.0 - lb[d]]) for d in range(2)
        ] + [jnp.zeros((2, GW), F32)], axis=0)
        o_f, o_b = _hgrn(p_hg, hconst, trif, trib, gm64_f, gm64_b)

        need_ctx = l < DEPTH - 1
        w1p = jnp.pad(hy_w1[l], ((0, 128 - HYENA_EMB), (0, 0)))
        o_hy = []
        for n, blk0 in ((L, 0), (CTX, T_LAT // CTX)):
            if n == CTX and not need_ctx:
                continue
            chi, clo, shi, slo, feats = dft[n]
            e, o, knq = _hyfilt(feats, w1p, hy_b1[l].reshape(1, -1), hy_freq[l].reshape(1, -1), hy_w2[l],
                                hy_b2[l].reshape(1, -1), hy_w3[l], hy_b3[l].reshape(1, -1),
                                hy_decay[l].reshape(1, 4 * GW))
            kre, kim = _hyspec(chi, clo, shi, slo, e, o)
            o_hy.append(_hyena(p_hy, blk0, B, n, hy_short_w[l], hy_short_b[l].reshape(1, -1), hy_bias[l],
                               chi, shi, kre, kim, knq))

        qn, kn, vn = _naprep(p_na, jnp.tile(na_q_g[l], NH).reshape(1, GW), jnp.tile(na_k_g[l], NH).reshape(1, GW),
                             gm64_b)
        o_na = [_na(qn, kn, vn, _na_bias_table(na_rpb[l]))]

        wq, wk, wv, qg, kg = _mla_weights(mla_w_uq[l], mla_w_ukv[l], mla_q_g[l], mla_k_g[l])
        mq, mk, mv = _mlaprep(p_mla, mla_q_a_g[l].reshape(1, -1), mla_kv_a_g[l].reshape(1, -1), wq, wk, wv,
                              qg, kg, gm128_b, pm, cos_t, sin_t)
        o_mla = [_attn_latent(mq, mk, mv, 128, 64, 256)]
        if need_ctx:
            o_na.append(_attn_ctx(qn, kn, vn, 64, 64))
            o_mla.append(_attn_ctx(mq, mk, mv, 128, 64))
        else:
            o_hy.append(o_hy[0])
            o_na.append(o_na[0])
            o_mla.append(o_mla[0])
        n_tiles = N_TILES if need_ctx else LAT_TILES

        wr = jnp.pad(jnp.concatenate([moe_wg[l], moe_we[l]], axis=1), ((0, 0), (0, 128 - N_GROUPS - N_EXPERTS)))
        wr_hi, wr_lo = _split2(wr)
        X1, h2, logits = _outproj(n_tiles, X, o_f, o_b, p_hg, o_hy, o_na, o_mla, mod_l,
                                  jnp.tile(hgrn_norm_g[l], NH).reshape(1, GW), gm64_b,
                                  w_out[l].astype(BF16), norm2_g[l].reshape(1, D),
                                  jnp.concatenate([wr_hi, wr_lo], axis=1))

        eid, wts = _route(logits, moe_bg[l], moe_be[l])
        block_e, nblk, dest = _dispatch_tables(eid)
        xs = _dispatch(n_tiles, dest, h2, jnp.zeros((block_e.shape[0] * TMOE, D), F32))
        ys = _experts(l, block_e, nblk, xs, moe_w_gate, moe_w_up, moe_w_down)
        X = _combine(n_tiles, dest, X1, wts, mod_l, ys)

    return X.reshape(B, L, D)
```

```python
import functools
import math

import jax
import jax.numpy as jnp
from jax import lax
from jax.experimental import pallas as pl
from jax.experimental.pallas import tpu as pltpu

F32 = jnp.float32
BF16 = jnp.bfloat16

D = 1024
B = 8
L = 2048
CTX = 256
DEPTH = 4
GRID_W = 64
EPS = 1e-6
GW = 256
NH = 4
HYENA_BANDS = 16
HYENA_EMB = 1 + 2 * HYENA_BANDS
HYENA_FFN = 64
NA_ROWS = 8
NA_COLS = 16
MLA_Q_RANK = 256
MLA_KV_RANK = 128
MLA_NOPE = 64
MLA_ROPE = 32
MLA_QK = MLA_NOPE + MLA_ROPE
ROPE_BASE = 10000.0
N_GROUPS = 4
EPG = 8
N_EXPERTS = N_GROUPS * EPG
D_EXPERT = 512
D_IN = 3232
D_IN_PAD = 3328

T_LAT = B * L
T_CTX = B * CTX
T_ALL = T_LAT + T_CTX

TM = 256
N_TILES = T_ALL // TM
LAT_TILES = T_LAT // TM
TILES_PER_SEQ = L // TM
HCHUNK = 64
HSUB = 16
TMOE = 256
VMEM_LIMIT_BYTES = 56 * 1024 * 1024
NEG = -1e30

HI = lax.Precision.HIGHEST


def _cparams(sem, vmem=VMEM_LIMIT_BYTES):
    return pltpu.CompilerParams(dimension_semantics=sem, vmem_limit_bytes=vmem)


def _seg_of_tile(i):
    return jnp.where(i < LAT_TILES, i // TILES_PER_SEQ, B)


def _dot(a, b):
    return jnp.dot(a, b, preferred_element_type=F32)


def _dot_nt(a, b):
    return lax.dot_general(a, b, (((1,), (1,)), ((), ())), preferred_element_type=F32)


def _dot_tn(a, b):
    return lax.dot_general(a, b, (((0,), (0,)), ((), ())), preferred_element_type=F32)


def _split2(x):
    hi = x.astype(BF16)
    lo = (x - hi.astype(F32)).astype(BF16)
    return hi, lo


def _split3(x):
    h1 = x.astype(BF16)
    r1 = x - h1.astype(F32)
    h2 = r1.astype(BF16)
    h3 = (r1 - h2.astype(F32)).astype(BF16)
    return h1, h2, h3


def _group_sum(x, gm):
    hi, lo = _split2(x)
    return _dot(hi, gm) + _dot(lo, gm)


def _ada_kernel(c_ref, w_ref, b_ref, o_ref):
    cc = c_ref[...]
    sc = cc * jax.nn.sigmoid(cc)
    o_ref[0] = jnp.dot(sc, w_ref[0], preferred_element_type=F32, precision=HI) + b_ref[0]


def _adaln(cmat, w_ada, b_ada):
    tn = 1536
    return pl.pallas_call(
        _ada_kernel,
        grid=(DEPTH, 6 * D // tn),
        in_specs=[
            pl.BlockSpec((16, D), lambda l, j: (0, 0)),
            pl.BlockSpec((1, D, tn), lambda l, j: (l, 0, j)),
            pl.BlockSpec((1, 1, tn), lambda l, j: (l, 0, j)),
        ],
        out_specs=pl.BlockSpec((1, 16, tn), lambda l, j: (l, 0, j)),
        out_shape=jax.ShapeDtypeStruct((DEPTH, 16, 6 * D), F32),
        compiler_params=_cparams(("arbitrary", "arbitrary")),
        name="adaln",
    )(cmat, w_ada, b_ada.reshape(DEPTH, 1, 6 * D))


def _inproj_kernel(x_ref, mod_ref, g_ref, w_ref, o_hg, o_hy, o_na, o_mla):
    x = x_ref[...]
    ms = jnp.mean(x * x, axis=-1, keepdims=True)
    y = x * lax.rsqrt(ms + EPS) * g_ref[...]
    h = y * (1.0 + mod_ref[1:2, :]) + mod_ref[0:1, :]
    p = _dot(h.astype(BF16), w_ref[...])
    o_hg[...] = p[:, 0:1280]
    o_hy[...] = p[:, 1280:2048]
    o_na[...] = p[:, 2048:2816]
    o_mla[...] = p[:, 2816:3328]


def _inproj(X, mod_l, g, w_bf):
    return pl.pallas_call(
        _inproj_kernel,
        grid=(N_TILES,),
        in_specs=[
            pl.BlockSpec((TM, D), lambda i: (i, 0)),
            pl.BlockSpec((None, 6, D), lambda i: (_seg_of_tile(i), 0, 0)),
            pl.BlockSpec((1, D), lambda i: (0, 0)),
            pl.BlockSpec((D, D_IN_PAD), lambda i: (0, 0)),
        ],
        out_specs=[
            pl.BlockSpec((TM, 1280), lambda i: (i, 0)),
            pl.BlockSpec((TM, 768), lambda i: (i, 0)),
            pl.BlockSpec((TM, 768), lambda i: (i, 0)),
            pl.BlockSpec((TM, 512), lambda i: (i, 0)),
        ],
        out_shape=[
            jax.ShapeDtypeStruct((T_ALL, 1280), F32),
            jax.ShapeDtypeStruct((T_ALL, 768), F32),
            jax.ShapeDtypeStruct((T_ALL, 768), F32),
            jax.ShapeDtypeStruct((T_ALL, 512), F32),
        ],
        compiler_params=_cparams(("parallel",)),
        name="inproj",
    )(X, mod_l, g, w_bf)


def _hgrn_direction(p_ref, zcol, c_ref, crow, tri_ref, gm_ref, gmb_ref, rsel_ref, st_ref, o_ref, reverse):
    q = p_ref[:, 0:GW]
    z = p_ref[:, zcol:zcol + GW]
    v = p_ref[:, 3 * GW:4 * GW]
    la = c_ref[crow:crow + 1, :]
    l1 = c_ref[crow + 1:crow + 2, :]
    oml = c_ref[crow + 2:crow + 3, :]
    ls = jnp.minimum(z, 0.0) - jnp.log1p(jnp.exp(-jnp.abs(z)))
    c2 = l1 + ls
    logf = jnp.maximum(la, c2) + jnp.log1p(jnp.exp(-jnp.abs(la - c2)))
    kk = oml * jax.nn.sigmoid(-z)
    tri = tri_ref[...]
    h1, h2, h3 = _split3(logf)
    bsub = _dot(tri, h1) + _dot(tri, h2) + _dot(tri, h3)
    row = lax.broadcasted_iota(jnp.int32, (HSUB, GW), 0)
    st = st_ref[...]
    gm = gm_ref[...]
    gmb = gmb_ref[...]
    rsel = rsel_ref[...]
    order = range(HCHUNK // HSUB - 1, -1, -1) if reverse else range(HCHUNK // HSUB)
    for blk in order:
        r0 = blk * HSUB
        b_i = bsub[r0:r0 + HSUB]
        q_i = q[r0:r0 + HSUB]
        k_i = kk[r0:r0 + HSUB]
        v_i = v[r0:r0 + HSUB]
        btot = b_i[0:1] if reverse else b_i[HSUB - 1:HSUB]
        qe = (q_i * jnp.exp(b_i)).astype(BF16)
        o_inter = _dot_nt(qe, st.astype(BF16))
        parts = []
        for tl in range(HSUB):
            dlt = b_i[tl:tl + 1] - b_i
            valid = (row >= tl) if reverse else (row <= tl)
            w = jnp.exp(jnp.where(valid, dlt, NEG))
            parts.append((q_i[tl:tl + 1] * w) * k_i)
        pmat = jnp.concatenate(parts, axis=0).astype(BF16)
        abar = _dot(pmat, gmb)
        av = (abar.reshape(HSUB, HSUB, GW) * v_i[None]).reshape(HSUB * HSUB, GW)
        o_diag = _dot(rsel, av.astype(BF16))
        o_ref[r0:r0 + HSUB, :] = o_inter + o_diag
        kd = (k_i * jnp.exp(btot - b_i)).astype(BF16)
        upd = _dot_tn(v_i.astype(BF16), kd)
        st = st * jnp.exp(btot) + upd * gm
    st_ref[...] = st


def _hgrn_kernel(pf_ref, pb_ref, c_ref, trif_ref, trib_ref, gm_ref, gmb_ref, rsel_ref, of_ref, ob_ref, stf, stb):
    @pl.when(pl.program_id(1) == 0)
    def _():
        stf[...] = jnp.zeros_like(stf)
        stb[...] = jnp.zeros_like(stb)

    _hgrn_direction(pf_ref, GW, c_ref, 0, trif_ref, gm_ref, gmb_ref, rsel_ref, stf, of_ref, False)
    _hgrn_direction(pb_ref, 2 * GW, c_ref, 3, trib_ref, gm_ref, gmb_ref, rsel_ref, stb, ob_ref, True)


def _hgrn_block(b, n, reverse):
    nctx = CTX // HCHUNK
    nlat = L // HCHUNK
    jc = (nctx - 1 - n) if reverse else n
    jl = (nlat - 1 - (n - nctx)) if reverse else (n - nctx)
    return jnp.where(n < nctx, T_LAT // HCHUNK + b * nctx + jc, b * nlat + jl)


def _hgrn(p_hg, consts, trif, trib, gm, gmb):
    nsteps = (CTX + L) // HCHUNK
    full = lambda shape: pl.BlockSpec(shape, lambda b, n: (0,) * len(shape))
    return pl.pallas_call(
        _hgrn_kernel,
        grid=(B, nsteps),
        in_specs=[
            pl.BlockSpec((HCHUNK, 1280), lambda b, n: (_hgrn_block(b, n, False), 0)),
            pl.BlockSpec((HCHUNK, 1280), lambda b, n: (_hgrn_block(b, n, True), 0)),
            full((8, GW)),
            full((HCHUNK, HCHUNK)),
            full((HCHUNK, HCHUNK)),
            full((GW, GW)),
            full((GW, GW)),
            full((HSUB, HSUB * HSUB)),
        ],
        out_specs=[
            pl.BlockSpec((HCHUNK, GW), lambda b, n: (_hgrn_block(b, n, False), 0)),
            pl.BlockSpec((HCHUNK, GW), lambda b, n: (_hgrn_block(b, n, True), 0)),
        ],
        out_shape=[jax.ShapeDtypeStruct((T_ALL, GW), F32), jax.ShapeDtypeStruct((T_ALL, GW), F32)],
        scratch_shapes=[pltpu.VMEM((GW, GW), F32), pltpu.VMEM((GW, GW), F32)],
        compiler_params=_cparams(("arbitrary", "arbitrary")),
        name="hgrn",
    )(p_hg, p_hg, consts, trif, trib, gm, gmb,
      (jnp.arange(HSUB)[:, None] == jnp.arange(HSUB * HSUB)[None, :] // HSUB).astype(BF16))


def _alt_sum(x):
    n, c = x.shape
    sgn = jnp.where((lax.broadcasted_iota(jnp.int32, (n, c), 0) & 1) == 0, 1.0, -1.0)
    return jnp.sum(x * sgn, axis=0, keepdims=True)


def _hyfilt_kernel(feats_ref, w1_ref, b1_ref, fr_ref, w2_ref, b2_ref, w3_ref, b3_ref, dec_ref,
                   e_ref, o_ref, nq_ref):
    fr = fr_ref[...]
    feats = feats_ref[...]
    h = jnp.sin(fr * (jnp.dot(feats, w1_ref[...], preferred_element_type=F32, precision=HI) + b1_ref[...]))
    h = jnp.sin(fr * (jnp.dot(h, w2_ref[...], preferred_element_type=F32, precision=HI) + b2_ref[...]))
    filt = jnp.dot(h, w3_ref[...], preferred_element_type=F32, precision=HI) + b3_ref[...]
    filt = filt * jnp.exp(-feats[:, 0:1] * dec_ref[...])
    n = filt.shape[0]
    row = lax.broadcasted_iota(jnp.int32, (n, GW), 0)
    for o in range(2):
        fwd = filt[:, (2 * o) * GW:(2 * o + 1) * GW]
        bwd = jnp.where(row >= 1, filt[:, (2 * o + 1) * GW:(2 * o + 2) * GW], 0.0)
        ssq = jnp.sum(fwd * fwd + bwd * bwd, axis=0, keepdims=True)
        scale = lax.rsqrt(ssq + EPS)
        ev = (fwd + bwd) * scale
        e_ref[:, o * GW:(o + 1) * GW] = ev
        o_ref[:, o * GW:(o + 1) * GW] = (fwd - bwd) * scale
        nq_ref[:, o * GW:(o + 1) * GW] = _alt_sum(ev) * (0.5 / n)


def _hyfilt(feats, w1p, b1, fr, w2, b2, w3, b3, dec):
    n = feats.shape[0]
    return pl.pallas_call(
        _hyfilt_kernel,
        out_shape=[jax.ShapeDtypeStruct((n, 2 * GW), F32), jax.ShapeDtypeStruct((n, 2 * GW), F32),
                   jax.ShapeDtypeStruct((1, 2 * GW), F32)],
        compiler_params=_cparams(None),
        name="hyfilt",
    )(feats, w1p, b1, fr, w2, b2, w3, b3, dec)


def _hyspec_kernel(chi_ref, clo_ref, shi_ref, slo_ref, e_ref, o_ref, kre_ref, kim_ref, *, n):
    eh, el = _split2(e_ref[...])
    oh, ol = _split2(o_ref[...])
    kre = _dot(chi_ref[...], eh) + _dot(chi_ref[...], el) + _dot(clo_ref[...], eh)
    kim = _dot(shi_ref[...], oh) + _dot(shi_ref[...], ol) + _dot(slo_ref[...], oh)
    tr = kre.shape[0]
    grow = lax.broadcasted_iota(jnp.int32, kre.shape, 0) + pl.program_id(0) * tr
    s2 = 1.0 / n
    kre_ref[...] = kre * jnp.where(grow == 0, 0.5 * s2, s2)
    kim_ref[...] = kim * s2


def _hyspec(chi, clo, shi, slo, e, o):
    n = e.shape[0]
    tr = min(256, n)
    rows = pl.BlockSpec((tr, n), lambda i: (i, 0))
    full = pl.BlockSpec((n, 2 * GW), lambda i: (0, 0))
    outb = pl.BlockSpec((tr, 2 * GW), lambda i: (i, 0))
    return pl.pallas_call(
        functools.partial(_hyspec_kernel, n=n),
        grid=(n // tr,),
        in_specs=[rows, rows, rows, rows, full, full],
        out_specs=[outb, outb],
        out_shape=[jax.ShapeDtypeStruct((n, 2 * GW), F32)] * 2,
        compiler_params=_cparams(("parallel",)),
        name="hyspec",
    )(chi, clo, shi, slo, e, o)


def _hyena_kernel(u_ref, sw_ref, sb_ref, db_ref, c_ref, s_ref, kre_ref, kim_ref, knq_ref, o_ref,
                  z_scr, zb_scr, y_scr):
    n = u_ref.shape[0]
    ft = min(512, n)
    rc = min(256, n)
    nchunks = n // rc
    lrow = lax.broadcasted_iota(jnp.int32, (rc, GW), 0)
    sgn = jnp.where((lrow & 1) == 0, 1.0, -1.0)

    def short_conv(part, c):
        sl = slice(part * GW, (part + 1) * GW)
        r0 = c * rc
        u = u_ref[r0:r0 + rc, sl]
        prev = u_ref[r0 - 1:r0, sl] if c > 0 else jnp.zeros((1, GW), F32)
        nxt = u_ref[r0 + rc:r0 + rc + 1, sl] if c < nchunks - 1 else jnp.zeros((1, GW), F32)
        up = jnp.where(lrow == 0, prev, pltpu.roll(u, 1, 0))
        un = jnp.where(lrow == rc - 1, nxt, pltpu.roll(u, rc - 1, 0))
        return sw_ref[0:1, sl] * up + sw_ref[1:2, sl] * u + sw_ref[2:3, sl] * un + sb_ref[:, sl]

    for c in range(nchunks):
        z_scr[c * rc:(c + 1) * rc, :] = short_conv(0, c)
    for o in range(2):
        cols = slice(o * GW, (o + 1) * GW)
        znq = jnp.zeros((1, GW), F32)
        for c in range(nchunks):
            zc = z_scr[c * rc:(c + 1) * rc, :]
            zb_scr[c * rc:(c + 1) * rc, :] = zc.astype(BF16)
            znq = znq + jnp.sum(zc * sgn, axis=0, keepdims=True)
        ynq = znq * knq_ref[:, cols]
        for c in range(nchunks):
            y_scr[c * rc:(c + 1) * rc, :] = sgn * ynq
        for f in range(n // ft):
            rs = slice(f * ft, (f + 1) * ft)
            zre = _dot(c_ref[rs, :], zb_scr[...])
            zim = _dot(s_ref[rs, :], zb_scr[...])
            kre = kre_ref[rs, cols]
            kim = kim_ref[rs, cols]
            yre = (zre * kre - zim * kim).astype(BF16)
            yim = (zre * kim + zim * kre).astype(BF16)
            y_scr[...] += _dot(c_ref[:, rs], yre) + _dot(s_ref[:, rs], yim)
        dst = o_ref if o == 1 else z_scr
        for c in range(nchunks):
            rows = slice(c * rc, (c + 1) * rc)
            dst[rows, :] = short_conv(o + 1, c) * (y_scr[rows, :] + db_ref[o:o + 1, :] * z_scr[rows, :])


def _hyena(u, blk0, nb, n, sw, sb, db, cm, sm, kre, kim, knq):
    whole = pl.BlockSpec(memory_space=pltpu.VMEM)
    return pl.pallas_call(
        _hyena_kernel,
        grid=(nb,),
        in_specs=[
            pl.BlockSpec((n, 3 * GW), lambda b: (blk0 + b, 0)),
            pl.BlockSpec((3, 3 * GW), lambda b: (0, 0)),
            pl.BlockSpec((1, 3 * GW), lambda b: (0, 0)),
            pl.BlockSpec((2, GW), lambda b: (0, 0)),
            whole, whole, whole, whole, whole,
        ],
        out_specs=pl.BlockSpec((n, GW), lambda b: (b, 0)),
        out_shape=jax.ShapeDtypeStruct((nb * n, GW), F32),
        scratch_shapes=[pltpu.VMEM((n, GW), F32), pltpu.VMEM((n, GW), BF16), pltpu.VMEM((n, GW), F32)],
        compiler_params=_cparams(("arbitrary",)),
        name="hyena",
    )(u, sw, sb, db, cm, sm, kre, kim, knq)


def _dft_consts(n):
    kk = jnp.arange(n, dtype=jnp.int32)
    ph = (kk[:, None] * kk[None, :]) % (2 * n)
    ang = ph.astype(F32) * (math.pi / n)
    return jnp.cos(ang), -jnp.sin(ang)


def _hyena_feats(n):
    t = jnp.arange(n, dtype=F32)
    t_unit = jnp.linspace(0.0, 1.0, n, dtype=F32)
    bands = jnp.linspace(1e-4, HYENA_BANDS - 1, HYENA_BANDS, dtype=F32)
    ang = (2.0 * math.pi / n) * t[:, None] * bands[None, :]
    feats = jnp.concatenate([t_unit[:, None], jnp.cos(ang), -jnp.sin(ang)], axis=-1)
    return jnp.pad(feats, ((0, 0), (0, 128 - HYENA_EMB)))


def _naprep_kernel(p_ref, qg_ref, kg_ref, gm_ref, q_ref, k_ref, v_ref):
    p = p_ref[...]
    q = p[:, 0:GW]
    k = p[:, GW:2 * GW]
    gm = gm_ref[...]
    qn = q * lax.rsqrt(_group_sum(q * q, gm) * (1.0 / 64) + EPS) * qg_ref[...]
    kn = k * lax.rsqrt(_group_sum(k * k, gm) * (1.0 / 64) + EPS) * kg_ref[...]
    q_ref[...] = (qn * (64 ** -0.5)).astype(BF16)
    k_ref[...] = kn.astype(BF16)
    v_ref[...] = p[:, 2 * GW:3 * GW].astype(BF16)


def _naprep(p_na, qg, kg, gmb):
    tok = lambda w: pl.BlockSpec((TM, w), lambda i: (i, 0))
    full = lambda shape: pl.BlockSpec(shape, lambda i: (0,) * len(shape))
    return pl.pallas_call(
        _naprep_kernel,
        grid=(N_TILES,),
        in_specs=[tok(768), full((1, GW)), full((1, GW)), full((GW, GW))],
        out_specs=[tok(GW), tok(GW), tok(GW)],
        out_shape=[jax.ShapeDtypeStruct((T_ALL, GW), BF16)] * 3,
        compiler_params=_cparams(("parallel",)),
        name="naprep",
    )(p_na, qg, kg, gmb)


NA_RPS = 4


def _na_kernel(q_ref, k_ref, v_ref, kc_ref, vc_ref, bias_ref, o_ref):
    rows = L // GRID_W
    kc = kc_ref[...]
    vc = vc_ref[...]
    for j in range(NA_RPS):
        r = pl.program_id(1) * NA_RPS + j
        rs = jnp.clip(r - NA_ROWS // 2, 0, rows - NA_ROWS)
        variant = r - rs
        start = pl.multiple_of(rs * GRID_W, GRID_W)
        kw = k_ref[pl.ds(start, NA_ROWS * GRID_W), :]
        vw = v_ref[pl.ds(start, NA_ROWS * GRID_W), :]
        q = q_ref[j * GRID_W:(j + 1) * GRID_W, :]
        outs = []
        for h in range(NH):
            sl = slice(h * 64, (h + 1) * 64)
            s_loc = _dot_nt(q[:, sl], kw[:, sl]) + bias_ref[variant, h]
            s_ctx = _dot_nt(q[:, sl], kc[:, sl])
            m = jnp.maximum(jnp.max(s_loc, axis=-1, keepdims=True), jnp.max(s_ctx, axis=-1, keepdims=True))
            p_loc = jnp.exp(s_loc - m)
            p_ctx = jnp.exp(s_ctx - m)
            den = jnp.sum(p_loc, axis=-1, keepdims=True) + jnp.sum(p_ctx, axis=-1, keepdims=True)
            o = _dot(p_loc.astype(BF16), vw[:, sl]) + _dot(p_ctx.astype(BF16), vc[:, sl])
            outs.append(o / den)
        o_ref[j * GRID_W:(j + 1) * GRID_W, :] = jnp.concatenate(outs, axis=-1)


def _na(qn, kn, vn, bias_t):
    steps = L // GRID_W // NA_RPS
    tq = NA_RPS * GRID_W
    ctx_blk = T_LAT // CTX
    return pl.pallas_call(
        _na_kernel,
        grid=(B, steps),
        in_specs=[
            pl.BlockSpec((tq, GW), lambda b, r: (b * steps + r, 0)),
            pl.BlockSpec((L, GW), lambda b, r: (b, 0)),
            pl.BlockSpec((L, GW), lambda b, r: (b, 0)),
            pl.BlockSpec((CTX, GW), lambda b, r: (ctx_blk + b, 0)),
            pl.BlockSpec((CTX, GW), lambda b, r: (ctx_blk + b, 0)),
            pl.BlockSpec((NA_ROWS, NH, GRID_W, NA_ROWS * GRID_W), lambda b, r: (0, 0, 0, 0)),
        ],
        out_specs=pl.BlockSpec((tq, GW), lambda b, r: (b * steps + r, 0)),
        out_shape=jax.ShapeDtypeStruct((T_LAT, GW), F32),
        compiler_params=_cparams(("arbitrary", "arbitrary")),
        name="na",
    )(qn, kn, vn, kn, vn, bias_t)


def _na_bias_table(rpb):
    cq = jnp.arange(GRID_W)
    cs = jnp.clip(cq - NA_COLS // 2, 0, GRID_W - NA_COLS)
    col_ok = (cq[None, :] >= cs[:, None]) & (cq[None, :] < cs[:, None] + NA_COLS)
    dc = jnp.clip(cq[None, :] - cq[:, None] + (NA_COLS - 1), 0, 2 * NA_COLS - 2)
    onehot = (dc[:, :, None] == jnp.arange(2 * NA_COLS - 1)[None, None, :]).astype(F32)
    full = jnp.einsum('qkc,hrc->hrqk', onehot, rpb.astype(F32), precision=HI)
    full = jnp.where(col_ok[None, None], full, NEG)
    tab = jnp.stack([full[:, NA_ROWS - 1 - a:2 * NA_ROWS - 1 - a] for a in range(NA_ROWS)], axis=0)
    return tab.transpose(0, 1, 3, 2, 4).reshape(NA_ROWS, NH, GRID_W, NA_ROWS * GRID_W)


def _attn_kernel(*refs, nkv, dq, dv):
    q = refs[0][...]
    ks = [refs[1 + 2 * j][...] for j in range(nkv)]
    vs = [refs[2 + 2 * j][...] for j in range(nkv)]
    o_ref = refs[1 + 2 * nkv]
    outs = []
    for h in range(NH):
        qh = q[:, h * dq:(h + 1) * dq]
        ss = [_dot_nt(qh, k[:, h * dq:(h + 1) * dq]) for k in ks]
        m = functools.reduce(jnp.maximum, [jnp.max(s, axis=-1, keepdims=True) for s in ss])
        ps = [jnp.exp(s - m) for s in ss]
        den = functools.reduce(lambda a, b2: a + b2, [jnp.sum(p, axis=-1, keepdims=True) for p in ps])
        o = functools.reduce(lambda a, b2: a + b2,
                             [_dot(p.astype(BF16), v[:, h * dv:(h + 1) * dv]) for p, v in zip(ps, vs)])
        outs.append(o / den)
    o_ref[...] = jnp.concatenate(outs, axis=-1)


def _attn_latent(q, k, v, dq, dv, tq):
    nq = L // tq
    ctx_blk = T_LAT // CTX
    return pl.pallas_call(
        functools.partial(_attn_kernel, nkv=2, dq=dq, dv=dv),
        grid=(B, nq),
        in_specs=[
            pl.BlockSpec((tq, NH * dq), lambda b, i: (b * nq + i, 0)),
            pl.BlockSpec((L, NH * dq), lambda b, i: (b, 0)),
            pl.BlockSpec((L, NH * dv), lambda b, i: (b, 0)),
            pl.BlockSpec((CTX, NH * dq), lambda b, i: (ctx_blk + b, 0)),
            pl.BlockSpec((CTX, NH * dv), lambda b, i: (ctx_blk + b, 0)),
        ],
        out_specs=pl.BlockSpec((tq, NH * dv), lambda b, i: (b * nq + i, 0)),
        out_shape=jax.ShapeDtypeStruct((T_LAT, NH * dv), F32),
        compiler_params=_cparams(("arbitrary", "arbitrary")),
        name="attn_latent",
    )(q, k, v, k, v)


def _attn_ctx(q, k, v, dq, dv):
    ctx_blk = T_LAT // CTX
    return pl.pallas_call(
        functools.partial(_attn_kernel, nkv=1, dq=dq, dv=dv),
        grid=(B,),
        in_specs=[
            pl.BlockSpec((CTX, NH * dq), lambda b: (ctx_blk + b, 0)),
            pl.BlockSpec((CTX, NH * dq), lambda b: (ctx_blk + b, 0)),
            pl.BlockSpec((CTX, NH * dv), lambda b: (ctx_blk + b, 0)),
        ],
        out_specs=pl.BlockSpec((CTX, NH * dv), lambda b: (b, 0)),
        out_shape=jax.ShapeDtypeStruct((T_CTX, NH * dv), F32),
        compiler_params=_cparams(("arbitrary",)),
        name="attn_ctx",
    )(q, k, v)


def _mlaprep_kernel(p_ref, qag_ref, kvag_ref, wq_ref, wk_ref, wv_ref, qg_ref, kg_ref, gm_ref, pm_ref,
                    cos_ref, sin_ref, q_ref, k_ref, v_ref):
    p = p_ref[...]
    cq = p[:, 0:MLA_Q_RANK]
    ckv = p[:, MLA_Q_RANK:MLA_Q_RANK + MLA_KV_RANK]
    krp = p[:, MLA_Q_RANK + MLA_KV_RANK:]
    cqn = cq * lax.rsqrt(jnp.mean(cq * cq, axis=-1, keepdims=True) + EPS) * qag_ref[...]
    ckvn = ckv * lax.rsqrt(jnp.mean(ckv * ckv, axis=-1, keepdims=True) + EPS) * kvag_ref[...]
    ckvb = ckvn.astype(BF16)
    q = _dot(cqn.astype(BF16), wq_ref[...])
    k = _dot(jnp.concatenate([ckvb, krp.astype(BF16)], axis=-1), wk_ref[...])
    v = _dot(ckvb, wv_ref[...])
    gm = gm_ref[...]
    q = q * lax.rsqrt(_group_sum(q * q, gm) * (1.0 / MLA_QK) + EPS) * qg_ref[...]
    k = k * lax.rsqrt(_group_sum(k * k, gm) * (1.0 / MLA_QK) + EPS) * kg_ref[...]
    cos = cos_ref[...]
    sin = sin_ref[...]
    pm = pm_ref[...]
    q = q * cos + _dot(q.astype(BF16), pm) * sin
    k = k * cos + _dot(k.astype(BF16), pm) * sin
    q_ref[...] = (q * (MLA_QK ** -0.5)).astype(BF16)
    k_ref[...] = k.astype(BF16)
    v_ref[...] = v.astype(BF16)


def _mlaprep(p_mla, qag, kvag, wq, wk, wv, qg, kg, gm, pm, cos_t, sin_t):
    tok = lambda w: pl.BlockSpec((TM, w), lambda i: (i, 0))
    full = lambda shape: pl.BlockSpec(shape, lambda i: (0,) * len(shape))
    pos = pl.BlockSpec((TM, 512), lambda i: (jnp.where(i < LAT_TILES, i % TILES_PER_SEQ, TILES_PER_SEQ), 0))
    return pl.pallas_call(
        _mlaprep_kernel,
        grid=(N_TILES,),
        in_specs=[tok(512), full((1, 256)), full((1, 128)), full((256, 512)), full((256, 512)),
                  full((128, 256)), full((1, 512)), full((1, 512)), full((512, 512)), full((512, 512)),
                  pos, pos],
        out_specs=[tok(512), tok(512), tok(GW)],
        out_shape=[jax.ShapeDtypeStruct((T_ALL, 512), BF16), jax.ShapeDtypeStruct((T_ALL, 512), BF16),
                   jax.ShapeDtypeStruct((T_ALL, GW), BF16)],
        compiler_params=_cparams(("parallel",)),
        name="mlaprep",
    )(p_mla, qag, kvag, wq, wk, wv, qg, kg, gm, pm, cos_t, sin_t)


def _rope_tables():
    t = jnp.arange(L)
    rowp = (t // GRID_W).astype(F32)
    colp = (t % GRID_W).astype(F32)
    half = MLA_ROPE // 2
    inv = ROPE_BASE ** (-jnp.arange(0, half, 2, dtype=F32) / half)
    j = jnp.arange(MLA_ROPE)
    pos = jnp.where(j[None, :] < half, rowp[:, None], colp[:, None])
    ang = pos * inv[j % (half // 2)][None, :]
    first = (j % half) < (half // 2)
    cos32 = jnp.cos(ang)
    sin32 = jnp.where(first[None, :], -jnp.sin(ang), jnp.sin(ang))
    cos_h = jnp.concatenate([jnp.ones((L, MLA_NOPE), F32), cos32, jnp.ones((L, 32), F32)], axis=-1)
    sin_h = jnp.concatenate([jnp.zeros((L, MLA_NOPE), F32), sin32, jnp.zeros((L, 32), F32)], axis=-1)
    cos_t = jnp.concatenate([jnp.tile(cos_h, (1, NH)), jnp.ones((TM, 512), F32)], axis=0)
    sin_t = jnp.concatenate([jnp.tile(sin_h, (1, NH)), jnp.zeros((TM, 512), F32)], axis=0)
    lane = jnp.arange(512)
    jj = lane % 128 - MLA_NOPE
    is_rope = (jj >= 0) & (jj < MLA_ROPE)
    partner = jnp.where(is_rope, jnp.where((jj % half) < (half // 2), lane + half // 2, lane - half // 2), lane)
    pm = (lane[:, None] == partner[None, :]).astype(BF16)
    return cos_t, sin_t, pm


def _outproj_kernel(x_ref, of_ref, ob_ref, g_ref, hyl_ref, hyc_ref, nal_ref, nac_ref, mll_ref, mlc_ref,
                    mod_ref, ng_ref, gm_ref, w_ref, n2_ref, wr_ref, x1_ref, h2_ref, lg_ref):
    oa = of_ref[...] + ob_ref[...]
    ms = _group_sum(oa * oa, gm_ref[...]) * (1.0 / 64)
    g = g_ref[...]
    oa = oa * lax.rsqrt(ms + EPS) * ng_ref[...] * (g * jax.nn.sigmoid(g))
    lat = pl.program_id(0) < LAT_TILES
    hy = jnp.where(lat, hyl_ref[...], hyc_ref[...])
    na = jnp.where(lat, nal_ref[...], nac_ref[...])
    mla = jnp.where(lat, mll_ref[...], mlc_ref[...])
    mix = jnp.concatenate([oa, hy, na, mla], axis=-1).astype(BF16)
    x1 = x_ref[...] + mod_ref[2:3, :] * _dot(mix, w_ref[...])
    x1_ref[...] = x1
    ms2 = jnp.mean(x1 * x1, axis=-1, keepdims=True)
    h2 = x1 * lax.rsqrt(ms2 + EPS) * n2_ref[...] * (1.0 + mod_ref[4:5, :]) + mod_ref[3:4, :]
    h2_ref[...] = h2
    hh, hl = _split2(h2)
    wr = wr_ref[...]
    lg_ref[...] = _dot(hh, wr[:, 0:128]) + _dot(hl, wr[:, 0:128]) + _dot(hh, wr[:, 128:256])


def _outproj(n_tiles, X, o_f, o_b, p_hg, hy, na, mla, mod_l, ng, gmb, w_bf, n2g, wr):
    tok = lambda w: pl.BlockSpec((TM, w), lambda i: (i, 0))
    full = lambda shape: pl.BlockSpec(shape, lambda i: (0,) * len(shape))
    latb = pl.BlockSpec((TM, GW), lambda i: (jnp.minimum(i, LAT_TILES - 1), 0))
    ctxb = pl.BlockSpec((TM, GW), lambda i: (jnp.maximum(i - LAT_TILES, 0), 0))
    nt = n_tiles * TM
    return pl.pallas_call(
        _outproj_kernel,
        grid=(n_tiles,),
        in_specs=[tok(D), tok(GW), tok(GW), pl.BlockSpec((TM, GW), lambda i: (i, 4)),
                  latb, ctxb, latb, ctxb, latb, ctxb,
                  pl.BlockSpec((None, 6, D), lambda i: (_seg_of_tile(i), 0, 0)),
                  full((1, GW)), full((GW, GW)), full((D, D)), full((1, D)), full((D, 256))],
        out_specs=[tok(D), tok(D), tok(128)],
        out_shape=[jax.ShapeDtypeStruct((nt, D), F32), jax.ShapeDtypeStruct((nt, D), F32),
                   jax.ShapeDtypeStruct((nt, 128), F32)],
        compiler_params=_cparams(("parallel",)),
        name="outproj",
    )(X, o_f, o_b, p_hg, hy[0], hy[1], na[0], na[1], mla[0], mla[1], mod_l, ng, gmb, w_bf, n2g, wr)


def _route(logits, bg, be):
    t = logits.shape[0]
    lg = logits[:, 0:N_GROUPS] + bg
    le = (logits[:, N_GROUPS:N_GROUPS + N_EXPERTS] + be).reshape(t, N_GROUPS, EPG)
    pg = jax.nn.softmax(lg, axis=-1)
    g_sel = jnp.argmax(lg, axis=-1).astype(jnp.int32)
    p_sel = jnp.take_along_axis(pg, g_sel[:, None], axis=-1)[:, 0]
    le_sel = jnp.take_along_axis(le, g_sel[:, None, None], axis=1)[:, 0]
    top_v, top_i = lax.top_k(le_sel, 2)
    wts = jax.nn.softmax(top_v, axis=-1) * p_sel[:, None]
    eid = g_sel[:, None] * EPG + top_i.astype(jnp.int32)
    return eid, wts


def _dispatch_tables(eid):
    t = eid.shape[0]
    n = 2 * t
    nb = n // TMOE + N_EXPERTS
    flat_e = eid.reshape(n)
    onehot = (flat_e[:, None] == jnp.arange(N_EXPERTS, dtype=jnp.int32)[None, :]).astype(jnp.int32)
    csum = jnp.cumsum(onehot, axis=0)
    pos = jnp.sum(csum * onehot, axis=1) - 1
    counts = csum[-1]
    pcounts = ((counts + TMOE - 1) // TMOE) * TMOE
    pends = jnp.cumsum(pcounts)
    pstarts = pends - pcounts
    dest = (jnp.sum(jnp.where(onehot > 0, pstarts[None, :], 0), axis=1) + pos).astype(jnp.int32)
    blk_start = jnp.arange(nb, dtype=jnp.int32) * TMOE
    block_e = jnp.minimum(jnp.sum((pends[None, :] <= blk_start[:, None]).astype(jnp.int32), axis=1),
                          N_EXPERTS - 1).astype(jnp.int32)
    nblk = (pends[-1] // TMOE).astype(jnp.int32).reshape(1)
    pads = jnp.concatenate([pstarts + counts, pcounts - counts, nblk]).astype(jnp.int32)
    return block_e, nblk, dest, pads


ROW_TILE = 8
PAD_PIECES = tuple(ROW_TILE << b for b in range((TMOE // ROW_TILE).bit_length() - 1))


def _dispatch_kernel(dest_ref, pad_ref, h_ref, xs_out, zbuf, sem, zsem):
    base = pl.program_id(0) * (2 * TM)

    @pl.when(pl.program_id(0) == 0)
    def _():
        zbuf[...] = jnp.zeros_like(zbuf)
        for phase in range(2):
            for e in range(N_EXPERTS):
                start = pad_ref[e]
                npad = pad_ref[N_EXPERTS + e]
                head = (ROW_TILE - start % ROW_TILE) % ROW_TILE
                for j in range(ROW_TILE - 1):
                    @pl.when(j < head)
                    def _(j=j):
                        cp = pltpu.make_async_copy(zbuf.at[pl.ds(0, 1)], xs_out.at[pl.ds(start + j, 1)], zsem)
                        cp.start() if phase == 0 else cp.wait()

                off = start + head
                rest = npad - head
                for piece in PAD_PIECES:
                    has = (rest & piece) != 0

                    @pl.when(has)
                    def _(off=off, piece=piece):
                        cp = pltpu.make_async_copy(zbuf.at[pl.ds(0, piece)],
                                                   xs_out.at[pl.ds(pl.multiple_of(off, ROW_TILE), piece)], zsem)
                        cp.start() if phase == 0 else cp.wait()

                    off = off + jnp.where(has, piece, 0)

            zrows = zbuf.shape[0]
            first = pad_ref[2 * N_EXPERTS] * (TMOE // zrows)

            def tail(j, carry):
                cp = pltpu.make_async_copy(zbuf, xs_out.at[pl.ds(pl.multiple_of(j * zrows, zrows), zrows)], zsem)
                cp.start() if phase == 0 else cp.wait()
                return carry

            lax.fori_loop(first, xs_out.shape[0] // zrows, tail, 0)

    def issue(r, carry):
        for k in range(2):
            pltpu.make_async_copy(h_ref.at[pl.ds(r, 1)], xs_out.at[pl.ds(dest_ref[base + 2 * r + k], 1)], sem).start()
        return carry

    lax.fori_loop(0, TM, issue, 0, unroll=8)
    pltpu.make_async_copy(xs_out.at[pl.ds(0, 2 * TM)], xs_out.at[pl.ds(0, 2 * TM)], sem).wait()


def _dispatch(n_tiles, n_slots, dest, pads, h2):
    grid_spec = pltpu.PrefetchScalarGridSpec(
        num_scalar_prefetch=2,
        grid=(n_tiles,),
        in_specs=[pl.BlockSpec((TM, D), lambda i, dst, pd: (i, 0))],
        out_specs=pl.BlockSpec(memory_space=pl.ANY),
        scratch_shapes=[pltpu.VMEM((TMOE // 2, D), F32), pltpu.SemaphoreType.DMA(()), pltpu.SemaphoreType.DMA(())],
    )
    return pl.pallas_call(
        _dispatch_kernel,
        grid_spec=grid_spec,
        out_shape=jax.ShapeDtypeStruct((n_slots, D), F32),
        compiler_params=_cparams(("arbitrary",)),
        name="dispatch",
    )(dest, pads, h2)


def _experts_kernel(be_ref, nblk_ref, xs_ref, wg_ref, wu_ref, wd_ref, ys_ref, wgb, wub, wdb):
    i = pl.program_id(0)

    @pl.when((i == 0) | (be_ref[i] != be_ref[jnp.maximum(i - 1, 0)]))
    def _():
        wgb[...] = wg_ref[...].astype(BF16)
        wub[...] = wu_ref[...].astype(BF16)
        wdb[...] = wd_ref[...].astype(BF16)

    @pl.when(i < nblk_ref[0])
    def _():
        x = xs_ref[...].astype(BF16)
        gate = _dot(x, wgb[...])
        up = _dot(x, wub[...])
        act = (gate * jax.nn.sigmoid(gate)) * up
        ys_ref[...] = _dot(act.astype(BF16), wdb[...])

    @pl.when(i >= nblk_ref[0])
    def _():
        ys_ref[...] = jnp.zeros_like(ys_ref)


def _experts(layer, block_e, nblk, xs, w_gate, w_up, w_down):
    nb = block_e.shape[0]
    used = lambda i, nk: jnp.minimum(i, nk[0] - 1)
    grid_spec = pltpu.PrefetchScalarGridSpec(
        num_scalar_prefetch=2,
        grid=(nb,),
        in_specs=[
            pl.BlockSpec((TMOE, D), lambda i, be, nk: (used(i, nk), 0)),
            pl.BlockSpec((None, None, D, D_EXPERT), lambda i, be, nk: (layer, be[i], 0, 0)),
            pl.BlockSpec((None, None, D, D_EXPERT), lambda i, be, nk: (layer, be[i], 0, 0)),
            pl.BlockSpec((None, None, D_EXPERT, D), lambda i, be, nk: (layer, be[i], 0, 0)),
        ],
        out_specs=pl.BlockSpec((TMOE, D), lambda i, be, nk: (i, 0)),
        scratch_shapes=[pltpu.VMEM((D, D_EXPERT), BF16), pltpu.VMEM((D, D_EXPERT), BF16),
                        pltpu.VMEM((D_EXPERT, D), BF16)],
    )
    return pl.pallas_call(
        _experts_kernel,
        grid_spec=grid_spec,
        out_shape=jax.ShapeDtypeStruct(xs.shape, F32),
        compiler_params=_cparams(("arbitrary",)),
        name="experts",
    )(block_e, nblk, xs, w_gate, w_up, w_down)


def _combine_kernel(dest_ref, x_ref, w_ref, mod_ref, ys_hbm, o_ref, ybuf, sem):
    i = pl.program_id(0)

    def fetch(tile, slot):
        base = tile * (2 * TM)

        def issue(r, carry):
            for k in range(2):
                pltpu.make_async_copy(ys_hbm.at[pl.ds(dest_ref[base + 2 * r + k], 1)],
                                      ybuf.at[slot, k, pl.ds(r, 1)], sem.at[slot]).start()
            return carry

        lax.fori_loop(0, TM, issue, 0, unroll=8)

    @pl.when(i == 0)
    def _():
        fetch(0, 0)

    @pl.when(i + 1 < pl.num_programs(0))
    def _():
        fetch(i + 1, (i + 1) % 2)

    slot = i % 2
    for k in range(2):
        pltpu.make_async_copy(ys_hbm.at[pl.ds(0, TM)], ybuf.at[slot, k], sem.at[slot]).wait()
    w = w_ref[...]
    y = w[:, 0:1] * ybuf[slot, 0] + w[:, 1:2] * ybuf[slot, 1]
    o_ref[...] = x_ref[...] + mod_ref[5:6, :] * y


def _combine(n_tiles, dest, X1, wts, mod_l, ys):
    grid_spec = pltpu.PrefetchScalarGridSpec(
        num_scalar_prefetch=1,
        grid=(n_tiles,),
        in_specs=[pl.BlockSpec((TM, D), lambda i, dst: (i, 0)),
                  pl.BlockSpec((TM, 2), lambda i, dst: (i, 0)),
                  pl.BlockSpec((None, 6, D), lambda i, dst: (_seg_of_tile(i), 0, 0)),
                  pl.BlockSpec(memory_space=pl.ANY)],
        out_specs=pl.BlockSpec((TM, D), lambda i, dst: (i, 0)),
        scratch_shapes=[pltpu.VMEM((2, 2, TM, D), F32), pltpu.SemaphoreType.DMA((2,))],
    )
    return pl.pallas_call(
        _combine_kernel,
        grid_spec=grid_spec,
        out_shape=jax.ShapeDtypeStruct((n_tiles * TM, D), F32),
        compiler_params=_cparams(("arbitrary",)),
        name="combine",
    )(dest, X1, wts, mod_l, ys)


def _group_mask(width, group):
    lane = jnp.arange(width)
    return (lane[:, None] // group == lane[None, :] // group)


def _hgrn_tri(reverse):
    t = jnp.arange(HCHUNK)
    same = (t[:, None] // HSUB) == (t[None, :] // HSUB)
    order = (t[None, :] >= t[:, None]) if reverse else (t[None, :] <= t[:, None])
    return (same & order).astype(BF16)


def _mla_weights(w_uq, w_ukv, q_g, k_g):
    wq = jnp.pad(w_uq.reshape(MLA_Q_RANK, NH, MLA_QK), ((0, 0), (0, 0), (0, 128 - MLA_QK))).reshape(MLA_Q_RANK, 512)
    kv = w_ukv.reshape(MLA_KV_RANK, NH, MLA_NOPE + 64)
    wk_top = jnp.pad(kv[:, :, :MLA_NOPE], ((0, 0), (0, 0), (0, 128 - MLA_NOPE))).reshape(MLA_KV_RANK, 512)
    lane = jnp.arange(512)
    src = jnp.arange(128)
    place = ((lane[None, :] % 128) == (src[:, None] + MLA_NOPE)) & (src[:, None] < MLA_ROPE)
    wk = jnp.concatenate([wk_top, place.astype(F32)], axis=0)
    wv = kv[:, :, MLA_NOPE:].reshape(MLA_KV_RANK, GW)
    pad_g = lambda g: jnp.tile(jnp.pad(g, (0, 128 - MLA_QK)), NH).reshape(1, 512)
    return wq.astype(BF16), wk.astype(BF16), wv.astype(BF16), pad_g(q_g), pad_g(k_g)


def kernel(x, c, ctx, c_ctx, w_ada, b_ada, norm1_g, norm2_g, w_in, w_out, hgrn_lb_logits, hgrn_norm_g,
           hy_short_w, hy_short_b, hy_w1, hy_b1, hy_freq, hy_w2, hy_b2, hy_w3, hy_b3, hy_decay, hy_bias,
           na_rpb, na_q_g, na_k_g, mla_q_a_g, mla_kv_a_g, mla_w_uq, mla_w_ukv, mla_q_g, mla_k_g,
           moe_wg, moe_bg, moe_we, moe_be, moe_w_gate, moe_w_up, moe_w_down):
    X = jnp.concatenate([x.reshape(T_LAT, D), ctx.reshape(T_CTX, D)], axis=0)
    cmat = jnp.concatenate([c, c_ctx[None, :], jnp.zeros((16 - B - 1, D), F32)], axis=0)
    mod = _adaln(cmat, w_ada, b_ada).reshape(DEPTH, 16, 6, D)

    lb_cum = jnp.cumsum(jax.nn.softmax(hgrn_lb_logits.astype(F32), axis=0), axis=0)
    lower = lb_cum - lb_cum[0:1]

    gm64 = _group_mask(GW, 64)
    gm64_f = gm64.astype(F32)
    gm64_b = gm64.astype(BF16)
    gm128_b = _group_mask(512, 128).astype(BF16)
    trif = _hgrn_tri(False)
    trib = _hgrn_tri(True)
    cos_t, sin_t, pm = _rope_tables()
    dft = {}
    for n in (L, CTX):
        cm, sm = _dft_consts(n)
        chi, clo = _split2(cm)
        shi, slo = _split2(sm)
        dft[n] = (chi, clo, shi, slo, _hyena_feats(n))

    for l in range(DEPTH):
        mod_l = mod[l]
        p_hg, p_hy, p_na, p_mla = _inproj(X, mod_l, norm1_g[l].reshape(1, D),
                                          jnp.pad(w_in[l], ((0, 0), (0, D_IN_PAD - D_IN))).astype(BF16))

        lb = lower[l]
        hconst = jnp.concatenate([
            jnp.stack([jnp.maximum(jnp.log(lb[d]), NEG), jnp.log1p(-lb[d]), 1.0 - lb[d]]) for d in range(2)
        ] + [jnp.zeros((2, GW), F32)], axis=0)
        o_f, o_b = _hgrn(p_hg, hconst, trif, trib, gm64_f, gm64_b)

        need_ctx = l < DEPTH - 1
        w1p = jnp.pad(hy_w1[l], ((0, 128 - HYENA_EMB), (0, 0)))
        o_hy = []
        for n, blk0 in ((L, 0), (CTX, T_LAT // CTX)):
            if n == CTX and not need_ctx:
                continue
            chi, clo, shi, slo, feats = dft[n]
            e, o, knq = _hyfilt(feats, w1p, hy_b1[l].reshape(1, -1), hy_freq[l].reshape(1, -1), hy_w2[l],
                                hy_b2[l].reshape(1, -1), hy_w3[l], hy_b3[l].reshape(1, -1),
                                hy_decay[l].reshape(1, 4 * GW))
            kre, kim = _hyspec(chi, clo, shi, slo, e, o)
            o_hy.append(_hyena(p_hy, blk0, B, n, hy_short_w[l], hy_short_b[l].reshape(1, -1), hy_bias[l],
                               chi, shi, kre, kim, knq))

        qn, kn, vn = _naprep(p_na, jnp.tile(na_q_g[l], NH).reshape(1, GW), jnp.tile(na_k_g[l], NH).reshape(1, GW),
                             gm64_b)
        o_na = [_na(qn, kn, vn, _na_bias_table(na_rpb[l]))]

        wq, wk, wv, qg, kg = _mla_weights(mla_w_uq[l], mla_w_ukv[l], mla_q_g[l], mla_k_g[l])
        mq, mk, mv = _mlaprep(p_mla, mla_q_a_g[l].reshape(1, -1), mla_kv_a_g[l].reshape(1, -1), wq, wk, wv,
                              qg, kg, gm128_b, pm, cos_t, sin_t)
        o_mla = [_attn_latent(mq, mk, mv, 128, 64, 256)]
        if need_ctx:
            o_na.append(_attn_ctx(qn, kn, vn, 64, 64))
            o_mla.append(_attn_ctx(mq, mk, mv, 128, 64))
        else:
            o_hy.append(o_hy[0])
            o_na.append(o_na[0])
            o_mla.append(o_mla[0])
        n_tiles = N_TILES if need_ctx else LAT_TILES

        wr = jnp.pad(jnp.concatenate([moe_wg[l], moe_we[l]], axis=1), ((0, 0), (0, 128 - N_GROUPS - N_EXPERTS)))
        wr_hi, wr_lo = _split2(wr)
        X1, h2, logits = _outproj(n_tiles, X, o_f, o_b, p_hg, o_hy, o_na, o_mla, mod_l,
                                  jnp.tile(hgrn_norm_g[l], NH).reshape(1, GW), gm64_b,
                                  w_out[l].astype(BF16), norm2_g[l].reshape(1, D),
                                  jnp.concatenate([wr_hi, wr_lo], axis=1))

        eid, wts = _route(logits, moe_bg[l], moe_be[l])
        block_e, nblk, dest, pads = _dispatch_tables(eid)
        xs = _dispatch(n_tiles, block_e.shape[0] * TMOE, dest, pads, h2)
        ys = _experts(l, block_e, nblk, xs, moe_w_gate, moe_w_up, moe_w_down)
        X = _combine(n_tiles, dest, X1, wts, mod_l, ys)

    return X.reshape(B, L, D)
```

```python
import functools
import math

import jax
import jax.numpy as jnp
from jax import lax
from jax.experimental import pallas as pl
from jax.experimental.pallas import tpu as pltpu

F32 = jnp.float32
BF16 = jnp.bfloat16

D = 1024
B = 8
L = 2048
CTX = 256
DEPTH = 4
GRID_W = 64
EPS = 1e-6
GW = 256
NH = 4
HYENA_BANDS = 16
HYENA_EMB = 1 + 2 * HYENA_BANDS
HYENA_FFN = 64
NA_ROWS = 8
NA_COLS = 16
MLA_Q_RANK = 256
MLA_KV_RANK = 128
MLA_NOPE = 64
MLA_ROPE = 32
MLA_QK = MLA_NOPE + MLA_ROPE
ROPE_BASE = 10000.0
N_GROUPS = 4
EPG = 8
N_EXPERTS = N_GROUPS * EPG
D_EXPERT = 512
D_IN = 3232
D_IN_PAD = 3328

T_LAT = B * L
T_CTX = B * CTX
T_ALL = T_LAT + T_CTX

TM = 256
N_TILES = T_ALL // TM
LAT_TILES = T_LAT // TM
TILES_PER_SEQ = L // TM
HCHUNK = 64
HSUB = 16
TMOE = 256
VMEM_LIMIT_BYTES = 56 * 1024 * 1024
NEG = -1e30

HI = lax.Precision.HIGHEST


def _cparams(sem, vmem=VMEM_LIMIT_BYTES):
    return pltpu.CompilerParams(dimension_semantics=sem, vmem_limit_bytes=vmem)


def _seg_of_tile(i):
    return jnp.where(i < LAT_TILES, i // TILES_PER_SEQ, B)


def _dot(a, b):
    return jnp.dot(a, b, preferred_element_type=F32)


def _dot_nt(a, b):
    return lax.dot_general(a, b, (((1,), (1,)), ((), ())), preferred_element_type=F32)


def _dot_tn(a, b):
    return lax.dot_general(a, b, (((0,), (0,)), ((), ())), preferred_element_type=F32)


def _split2(x):
    hi = x.astype(BF16)
    lo = (x - hi.astype(F32)).astype(BF16)
    return hi, lo


def _split3(x):
    h1 = x.astype(BF16)
    r1 = x - h1.astype(F32)
    h2 = r1.astype(BF16)
    h3 = (r1 - h2.astype(F32)).astype(BF16)
    return h1, h2, h3


def _group_sum(x, gm):
    hi, lo = _split2(x)
    return _dot(hi, gm) + _dot(lo, gm)


def _ada_kernel(c_ref, w_ref, b_ref, o_ref):
    cc = c_ref[...]
    sc = cc * jax.nn.sigmoid(cc)
    o_ref[0] = jnp.dot(sc, w_ref[0], preferred_element_type=F32, precision=HI) + b_ref[0]


def _adaln(cmat, w_ada, b_ada):
    tn = 1536
    return pl.pallas_call(
        _ada_kernel,
        grid=(DEPTH, 6 * D // tn),
        in_specs=[
            pl.BlockSpec((16, D), lambda l, j: (0, 0)),
            pl.BlockSpec((1, D, tn), lambda l, j: (l, 0, j)),
            pl.BlockSpec((1, 1, tn), lambda l, j: (l, 0, j)),
        ],
        out_specs=pl.BlockSpec((1, 16, tn), lambda l, j: (l, 0, j)),
        out_shape=jax.ShapeDtypeStruct((DEPTH, 16, 6 * D), F32),
        compiler_params=_cparams(("arbitrary", "arbitrary")),
        name="adaln",
    )(cmat, w_ada, b_ada.reshape(DEPTH, 1, 6 * D))


def _inproj_kernel(x_ref, mod_ref, g_ref, w_ref, o_hg, o_hy, o_na, o_mla):
    x = x_ref[...]
    ms = jnp.mean(x * x, axis=-1, keepdims=True)
    y = x * lax.rsqrt(ms + EPS) * g_ref[...]
    h = y * (1.0 + mod_ref[1:2, :]) + mod_ref[0:1, :]
    p = _dot(h.astype(BF16), w_ref[...])
    o_hg[...] = p[:, 0:1280]
    o_hy[...] = p[:, 1280:2048]
    o_na[...] = p[:, 2048:2816]
    o_mla[...] = p[:, 2816:3328]


def _inproj(X, mod_l, g, w_bf):
    return pl.pallas_call(
        _inproj_kernel,
        grid=(N_TILES,),
        in_specs=[
            pl.BlockSpec((TM, D), lambda i: (i, 0)),
            pl.BlockSpec((None, 6, D), lambda i: (_seg_of_tile(i), 0, 0)),
            pl.BlockSpec((1, D), lambda i: (0, 0)),
            pl.BlockSpec((D, D_IN_PAD), lambda i: (0, 0)),
        ],
        out_specs=[
            pl.BlockSpec((TM, 1280), lambda i: (i, 0)),
            pl.BlockSpec((TM, 768), lambda i: (i, 0)),
            pl.BlockSpec((TM, 768), lambda i: (i, 0)),
            pl.BlockSpec((TM, 512), lambda i: (i, 0)),
        ],
        out_shape=[
            jax.ShapeDtypeStruct((T_ALL, 1280), F32),
            jax.ShapeDtypeStruct((T_ALL, 768), F32),
            jax.ShapeDtypeStruct((T_ALL, 768), F32),
            jax.ShapeDtypeStruct((T_ALL, 512), F32),
        ],
        compiler_params=_cparams(("parallel",)),
        name="inproj",
    )(X, mod_l, g, w_bf)


def _hgrn_direction(p_ref, zcol, c_ref, crow, tri_ref, gm_ref, gmb_ref, st_ref, o_ref, reverse):
    q = p_ref[:, 0:GW]
    z = p_ref[:, zcol:zcol + GW]
    v = p_ref[:, 3 * GW:4 * GW]
    la = c_ref[crow:crow + 1, :]
    l1 = c_ref[crow + 1:crow + 2, :]
    oml = c_ref[crow + 2:crow + 3, :]
    ls = jnp.minimum(z, 0.0) - jnp.log1p(jnp.exp(-jnp.abs(z)))
    c2 = l1 + ls
    logf = jnp.maximum(la, c2) + jnp.log1p(jnp.exp(-jnp.abs(la - c2)))
    kk = oml * jax.nn.sigmoid(-z)
    tri = tri_ref[...]
    h1, h2, h3 = _split3(logf)
    bsub = _dot(tri, h1) + _dot(tri, h2) + _dot(tri, h3)
    row = lax.broadcasted_iota(jnp.int32, (HSUB, GW), 0)
    st = st_ref[...]
    gm = gm_ref[...]
    gmb = gmb_ref[...]
    order = range(HCHUNK // HSUB - 1, -1, -1) if reverse else range(HCHUNK // HSUB)
    for blk in order:
        r0 = blk * HSUB
        b_i = bsub[r0:r0 + HSUB]
        q_i = q[r0:r0 + HSUB]
        k_i = kk[r0:r0 + HSUB]
        v_i = v[r0:r0 + HSUB]
        btot = b_i[0:1] if reverse else b_i[HSUB - 1:HSUB]
        qe = (q_i * jnp.exp(b_i)).astype(BF16)
        o_inter = _dot_nt(qe, st.astype(BF16))
        parts = []
        for tl in range(HSUB):
            dlt = b_i[tl:tl + 1] - b_i
            valid = (row >= tl) if reverse else (row <= tl)
            w = jnp.exp(jnp.where(valid, dlt, NEG))
            parts.append((q_i[tl:tl + 1] * w) * k_i)
        pmat = jnp.concatenate(parts, axis=0).astype(BF16)
        abar = _dot(pmat, gmb)
        o_diag = jnp.sum(abar.reshape(HSUB, HSUB, GW) * v_i[None], axis=1)
        o_ref[r0:r0 + HSUB, :] = o_inter + o_diag
        kd = (k_i * jnp.exp(btot - b_i)).astype(BF16)
        upd = _dot_tn(v_i.astype(BF16), kd)
        st = st * jnp.exp(btot) + upd * gm
    st_ref[...] = st


def _hgrn_kernel(pf_ref, pb_ref, c_ref, trif_ref, trib_ref, gm_ref, gmb_ref, of_ref, ob_ref, stf, stb):
    @pl.when(pl.program_id(1) == 0)
    def _():
        stf[...] = jnp.zeros_like(stf)
        stb[...] = jnp.zeros_like(stb)

    _hgrn_direction(pf_ref, GW, c_ref, 0, trif_ref, gm_ref, gmb_ref, stf, of_ref, False)
    _hgrn_direction(pb_ref, 2 * GW, c_ref, 3, trib_ref, gm_ref, gmb_ref, stb, ob_ref, True)


def _hgrn_block(b, n, reverse):
    nctx = CTX // HCHUNK
    nlat = L // HCHUNK
    jc = (nctx - 1 - n) if reverse else n
    jl = (nlat - 1 - (n - nctx)) if reverse else (n - nctx)
    return jnp.where(n < nctx, T_LAT // HCHUNK + b * nctx + jc, b * nlat + jl)


def _hgrn(p_hg, consts, trif, trib, gm, gmb):
    nsteps = (CTX + L) // HCHUNK
    full = lambda shape: pl.BlockSpec(shape, lambda b, n: (0,) * len(shape))
    return pl.pallas_call(
        _hgrn_kernel,
        grid=(B, nsteps),
        in_specs=[
            pl.BlockSpec((HCHUNK, 1280), lambda b, n: (_hgrn_block(b, n, False), 0)),
            pl.BlockSpec((HCHUNK, 1280), lambda b, n: (_hgrn_block(b, n, True), 0)),
            full((8, GW)),
            full((HCHUNK, HCHUNK)),
            full((HCHUNK, HCHUNK)),
            full((GW, GW)),
            full((GW, GW)),
        ],
        out_specs=[
            pl.BlockSpec((HCHUNK, GW), lambda b, n: (_hgrn_block(b, n, False), 0)),
            pl.BlockSpec((HCHUNK, GW), lambda b, n: (_hgrn_block(b, n, True), 0)),
        ],
        out_shape=[jax.ShapeDtypeStruct((T_ALL, GW), F32), jax.ShapeDtypeStruct((T_ALL, GW), F32)],
        scratch_shapes=[pltpu.VMEM((GW, GW), F32), pltpu.VMEM((GW, GW), F32)],
        compiler_params=_cparams(("arbitrary", "arbitrary")),
        name="hgrn",
    )(p_hg, p_hg, consts, trif, trib, gm, gmb)


def _alt_sum(x):
    n, c = x.shape
    sgn = jnp.where((lax.broadcasted_iota(jnp.int32, (n, c), 0) & 1) == 0, 1.0, -1.0)
    return jnp.sum(x * sgn, axis=0, keepdims=True)


def _hyfilt_kernel(feats_ref, w1_ref, b1_ref, fr_ref, w2_ref, b2_ref, w3_ref, b3_ref, dec_ref,
                   e_ref, o_ref, nq_ref):
    fr = fr_ref[...]
    feats = feats_ref[...]
    h = jnp.sin(fr * (jnp.dot(feats, w1_ref[...], preferred_element_type=F32, precision=HI) + b1_ref[...]))
    h = jnp.sin(fr * (jnp.dot(h, w2_ref[...], preferred_element_type=F32, precision=HI) + b2_ref[...]))
    filt = jnp.dot(h, w3_ref[...], preferred_element_type=F32, precision=HI) + b3_ref[...]
    filt = filt * jnp.exp(-feats[:, 0:1] * dec_ref[...])
    n = filt.shape[0]
    row = lax.broadcasted_iota(jnp.int32, (n, GW), 0)
    for o in range(2):
        fwd = filt[:, (2 * o) * GW:(2 * o + 1) * GW]
        bwd = jnp.where(row >= 1, filt[:, (2 * o + 1) * GW:(2 * o + 2) * GW], 0.0)
        ssq = jnp.sum(fwd * fwd + bwd * bwd, axis=0, keepdims=True)
        scale = lax.rsqrt(ssq + EPS)
        ev = (fwd + bwd) * scale
        e_ref[:, o * GW:(o + 1) * GW] = ev
        o_ref[:, o * GW:(o + 1) * GW] = (fwd - bwd) * scale
        nq_ref[:, o * GW:(o + 1) * GW] = _alt_sum(ev) * (0.5 / n)


def _hyfilt(feats, w1p, b1, fr, w2, b2, w3, b3, dec):
    n = feats.shape[0]
    return pl.pallas_call(
        _hyfilt_kernel,
        out_shape=[jax.ShapeDtypeStruct((n, 2 * GW), F32), jax.ShapeDtypeStruct((n, 2 * GW), F32),
                   jax.ShapeDtypeStruct((1, 2 * GW), F32)],
        compiler_params=_cparams(None),
        name="hyfilt",
    )(feats, w1p, b1, fr, w2, b2, w3, b3, dec)


def _hyspec_kernel(chi_ref, clo_ref, shi_ref, slo_ref, e_ref, o_ref, kre_ref, kim_ref, *, n):
    eh, el = _split2(e_ref[...])
    oh, ol = _split2(o_ref[...])
    kre = _dot(chi_ref[...], eh) + _dot(chi_ref[...], el) + _dot(clo_ref[...], eh)
    kim = _dot(shi_ref[...], oh) + _dot(shi_ref[...], ol) + _dot(slo_ref[...], oh)
    tr = kre.shape[0]
    grow = lax.broadcasted_iota(jnp.int32, kre.shape, 0) + pl.program_id(0) * tr
    s2 = 1.0 / n
    kre_ref[...] = kre * jnp.where(grow == 0, 0.5 * s2, s2)
    kim_ref[...] = kim * s2


def _hyspec(chi, clo, shi, slo, e, o):
    n = e.shape[0]
    tr = min(256, n)
    rows = pl.BlockSpec((tr, n), lambda i: (i, 0))
    full = pl.BlockSpec((n, 2 * GW), lambda i: (0, 0))
    outb = pl.BlockSpec((tr, 2 * GW), lambda i: (i, 0))
    return pl.pallas_call(
        functools.partial(_hyspec_kernel, n=n),
        grid=(n // tr,),
        in_specs=[rows, rows, rows, rows, full, full],
        out_specs=[outb, outb],
        out_shape=[jax.ShapeDtypeStruct((n, 2 * GW), F32)] * 2,
        compiler_params=_cparams(("parallel",)),
        name="hyspec",
    )(chi, clo, shi, slo, e, o)


def _hyena_kernel(u_ref, sw_ref, sb_ref, db_ref, c_ref, s_ref, kre_ref, kim_ref, knq_ref, o_ref,
                  z_scr, zb_scr, y_scr):
    n = u_ref.shape[0]
    ft = min(512, n)
    rc = min(256, n)
    nchunks = n // rc
    lrow = lax.broadcasted_iota(jnp.int32, (rc, GW), 0)
    sgn = jnp.where((lrow & 1) == 0, 1.0, -1.0)

    def short_conv(part, c):
        sl = slice(part * GW, (part + 1) * GW)
        r0 = c * rc
        u = u_ref[r0:r0 + rc, sl]
        prev = u_ref[r0 - 1:r0, sl] if c > 0 else jnp.zeros((1, GW), F32)
        nxt = u_ref[r0 + rc:r0 + rc + 1, sl] if c < nchunks - 1 else jnp.zeros((1, GW), F32)
        up = jnp.where(lrow == 0, prev, pltpu.roll(u, 1, 0))
        un = jnp.where(lrow == rc - 1, nxt, pltpu.roll(u, rc - 1, 0))
        return sw_ref[0:1, sl] * up + sw_ref[1:2, sl] * u + sw_ref[2:3, sl] * un + sb_ref[:, sl]

    for c in range(nchunks):
        z_scr[c * rc:(c + 1) * rc, :] = short_conv(0, c)
    for o in range(2):
        cols = slice(o * GW, (o + 1) * GW)
        znq = jnp.zeros((1, GW), F32)
        for c in range(nchunks):
            zc = z_scr[c * rc:(c + 1) * rc, :]
            zb_scr[c * rc:(c + 1) * rc, :] = zc.astype(BF16)
            znq = znq + jnp.sum(zc * sgn, axis=0, keepdims=True)
        ynq = znq * knq_ref[:, cols]
        for c in range(nchunks):
            y_scr[c * rc:(c + 1) * rc, :] = sgn * ynq
        for f in range(n // ft):
            rs = slice(f * ft, (f + 1) * ft)
            zre = _dot(c_ref[rs, :], zb_scr[...])
            zim = _dot(s_ref[rs, :], zb_scr[...])
            kre = kre_ref[rs, cols]
            kim = kim_ref[rs, cols]
            yre = (zre * kre - zim * kim).astype(BF16)
            yim = (zre * kim + zim * kre).astype(BF16)
            y_scr[...] += _dot(c_ref[:, rs], yre) + _dot(s_ref[:, rs], yim)
        dst = o_ref if o == 1 else z_scr
        for c in range(nchunks):
            rows = slice(c * rc, (c + 1) * rc)
            dst[rows, :] = short_conv(o + 1, c) * (y_scr[rows, :] + db_ref[o:o + 1, :] * z_scr[rows, :])


def _hyena(u, blk0, nb, n, sw, sb, db, cm, sm, kre, kim, knq):
    whole = pl.BlockSpec(memory_space=pltpu.VMEM)
    return pl.pallas_call(
        _hyena_kernel,
        grid=(nb,),
        in_specs=[
            pl.BlockSpec((n, 3 * GW), lambda b: (blk0 + b, 0)),
            pl.BlockSpec((3, 3 * GW), lambda b: (0, 0)),
            pl.BlockSpec((1, 3 * GW), lambda b: (0, 0)),
            pl.BlockSpec((2, GW), lambda b: (0, 0)),
            whole, whole, whole, whole, whole,
        ],
        out_specs=pl.BlockSpec((n, GW), lambda b: (b, 0)),
        out_shape=jax.ShapeDtypeStruct((nb * n, GW), F32),
        scratch_shapes=[pltpu.VMEM((n, GW), F32), pltpu.VMEM((n, GW), BF16), pltpu.VMEM((n, GW), F32)],
        compiler_params=_cparams(("arbitrary",)),
        name="hyena",
    )(u, sw, sb, db, cm, sm, kre, kim, knq)


def _dft_consts(n):
    kk = jnp.arange(n, dtype=jnp.int32)
    ph = (kk[:, None] * kk[None, :]) % (2 * n)
    ang = ph.astype(F32) * (math.pi / n)
    return jnp.cos(ang), -jnp.sin(ang)


def _hyena_feats(n):
    t = jnp.arange(n, dtype=F32)
    t_unit = jnp.linspace(0.0, 1.0, n, dtype=F32)
    bands = jnp.linspace(1e-4, HYENA_BANDS - 1, HYENA_BANDS, dtype=F32)
    ang = (2.0 * math.pi / n) * t[:, None] * bands[None, :]
    feats = jnp.concatenate([t_unit[:, None], jnp.cos(ang), -jnp.sin(ang)], axis=-1)
    return jnp.pad(feats, ((0, 0), (0, 128 - HYENA_EMB)))


def _naprep_kernel(p_ref, qg_ref, kg_ref, gm_ref, q_ref, k_ref, v_ref):
    p = p_ref[...]
    q = p[:, 0:GW]
    k = p[:, GW:2 * GW]
    gm = gm_ref[...]
    qn = q * lax.rsqrt(_group_sum(q * q, gm) * (1.0 / 64) + EPS) * qg_ref[...]
    kn = k * lax.rsqrt(_group_sum(k * k, gm) * (1.0 / 64) + EPS) * kg_ref[...]
    q_ref[...] = (qn * (64 ** -0.5)).astype(BF16)
    k_ref[...] = kn.astype(BF16)
    v_ref[...] = p[:, 2 * GW:3 * GW].astype(BF16)


def _naprep(p_na, qg, kg, gmb):
    tok = lambda w: pl.BlockSpec((TM, w), lambda i: (i, 0))
    full = lambda shape: pl.BlockSpec(shape, lambda i: (0,) * len(shape))
    return pl.pallas_call(
        _naprep_kernel,
        grid=(N_TILES,),
        in_specs=[tok(768), full((1, GW)), full((1, GW)), full((GW, GW))],
        out_specs=[tok(GW), tok(GW), tok(GW)],
        out_shape=[jax.ShapeDtypeStruct((T_ALL, GW), BF16)] * 3,
        compiler_params=_cparams(("parallel",)),
        name="naprep",
    )(p_na, qg, kg, gmb)


NA_RPS = 4


def _na_kernel(q_ref, k_ref, v_ref, kc_ref, vc_ref, bias_ref, o_ref):
    rows = L // GRID_W
    kc = kc_ref[...]
    vc = vc_ref[...]
    for j in range(NA_RPS):
        r = pl.program_id(1) * NA_RPS + j
        rs = jnp.clip(r - NA_ROWS // 2, 0, rows - NA_ROWS)
        variant = r - rs
        start = pl.multiple_of(rs * GRID_W, GRID_W)
        kw = k_ref[pl.ds(start, NA_ROWS * GRID_W), :]
        vw = v_ref[pl.ds(start, NA_ROWS * GRID_W), :]
        q = q_ref[j * GRID_W:(j + 1) * GRID_W, :]
        lane_head = lax.broadcasted_iota(jnp.int32, (GRID_W, GW), 1) // 64
        hmask = [lane_head == h for h in range(NH)]
        qx = jnp.concatenate([jnp.where(hmask[h], q, jnp.zeros_like(q)) for h in range(NH)], axis=0)
        s_loc = _dot_nt(qx, kw) + bias_ref[variant]
        s_ctx = _dot_nt(qx, kc)
        m = jnp.maximum(jnp.max(s_loc, axis=-1, keepdims=True), jnp.max(s_ctx, axis=-1, keepdims=True))
        p_loc = jnp.exp(s_loc - m)
        p_ctx = jnp.exp(s_ctx - m)
        den = jnp.sum(p_loc, axis=-1, keepdims=True) + jnp.sum(p_ctx, axis=-1, keepdims=True)
        o_all = (_dot(p_loc.astype(BF16), vw) + _dot(p_ctx.astype(BF16), vc)) / den
        o = jnp.zeros((GRID_W, GW), F32)
        for h in range(NH):
            o = o + jnp.where(hmask[h], o_all[h * GRID_W:(h + 1) * GRID_W, :], 0.0)
        o_ref[j * GRID_W:(j + 1) * GRID_W, :] = o


def _na(qn, kn, vn, bias_t):
    steps = L // GRID_W // NA_RPS
    tq = NA_RPS * GRID_W
    ctx_blk = T_LAT // CTX
    return pl.pallas_call(
        _na_kernel,
        grid=(B, steps),
        in_specs=[
            pl.BlockSpec((tq, GW), lambda b, r: (b * steps + r, 0)),
            pl.BlockSpec((L, GW), lambda b, r: (b, 0)),
            pl.BlockSpec((L, GW), lambda b, r: (b, 0)),
            pl.BlockSpec((CTX, GW), lambda b, r: (ctx_blk + b, 0)),
            pl.BlockSpec((CTX, GW), lambda b, r: (ctx_blk + b, 0)),
            pl.BlockSpec((NA_ROWS, NH * GRID_W, NA_ROWS * GRID_W), lambda b, r: (0, 0, 0)),
        ],
        out_specs=pl.BlockSpec((tq, GW), lambda b, r: (b * steps + r, 0)),
        out_shape=jax.ShapeDtypeStruct((T_LAT, GW), F32),
        compiler_params=_cparams(("arbitrary", "arbitrary")),
        name="na",
    )(qn, kn, vn, kn, vn, bias_t)


def _na_bias_table(rpb):
    cq = jnp.arange(GRID_W)
    cs = jnp.clip(cq - NA_COLS // 2, 0, GRID_W - NA_COLS)
    col_ok = (cq[None, :] >= cs[:, None]) & (cq[None, :] < cs[:, None] + NA_COLS)
    dc = jnp.clip(cq[None, :] - cq[:, None] + (NA_COLS - 1), 0, 2 * NA_COLS - 2)
    onehot = (dc[:, :, None] == jnp.arange(2 * NA_COLS - 1)[None, None, :]).astype(F32)
    full = jnp.einsum('qkc,hrc->hrqk', onehot, rpb.astype(F32), precision=HI)
    full = jnp.where(col_ok[None, None], full, NEG)
    tab = jnp.stack([full[:, NA_ROWS - 1 - a:2 * NA_ROWS - 1 - a] for a in range(NA_ROWS)], axis=0)
    return tab.transpose(0, 1, 3, 2, 4).reshape(NA_ROWS, NH * GRID_W, NA_ROWS * GRID_W)


def _attn_kernel(*refs, nkv, dq, dv):
    q = refs[0][...]
    ks = [refs[1 + 2 * j][...] for j in range(nkv)]
    vs = [refs[2 + 2 * j][...] for j in range(nkv)]
    o_ref = refs[1 + 2 * nkv]
    outs = []
    for h in range(NH):
        qh = q[:, h * dq:(h + 1) * dq]
        ss = [_dot_nt(qh, k[:, h * dq:(h + 1) * dq]) for k in ks]
        m = functools.reduce(jnp.maximum, [jnp.max(s, axis=-1, keepdims=True) for s in ss])
        ps = [jnp.exp(s - m) for s in ss]
        den = functools.reduce(lambda a, b2: a + b2, [jnp.sum(p, axis=-1, keepdims=True) for p in ps])
        o = functools.reduce(lambda a, b2: a + b2,
                             [_dot(p.astype(BF16), v[:, h * dv:(h + 1) * dv]) for p, v in zip(ps, vs)])
        outs.append(o / den)
    o_ref[...] = jnp.concatenate(outs, axis=-1)


def _attn_latent(q, k, v, dq, dv, tq):
    nq = L // tq
    ctx_blk = T_LAT // CTX
    return pl.pallas_call(
        functools.partial(_attn_kernel, nkv=2, dq=dq, dv=dv),
        grid=(B, nq),
        in_specs=[
            pl.BlockSpec((tq, NH * dq), lambda b, i: (b * nq + i, 0)),
            pl.BlockSpec((L, NH * dq), lambda b, i: (b, 0)),
            pl.BlockSpec((L, NH * dv), lambda b, i: (b, 0)),
            pl.BlockSpec((CTX, NH * dq), lambda b, i: (ctx_blk + b, 0)),
            pl.BlockSpec((CTX, NH * dv), lambda b, i: (ctx_blk + b, 0)),
        ],
        out_specs=pl.BlockSpec((tq, NH * dv), lambda b, i: (b * nq + i, 0)),
        out_shape=jax.ShapeDtypeStruct((T_LAT, NH * dv), F32),
        compiler_params=_cparams(("arbitrary", "arbitrary")),
        name="attn_latent",
    )(q, k, v, k, v)


def _attn_ctx(q, k, v, dq, dv):
    ctx_blk = T_LAT // CTX
    return pl.pallas_call(
        functools.partial(_attn_kernel, nkv=1, dq=dq, dv=dv),
        grid=(B,),
        in_specs=[
            pl.BlockSpec((CTX, NH * dq), lambda b: (ctx_blk + b, 0)),
            pl.BlockSpec((CTX, NH * dq), lambda b: (ctx_blk + b, 0)),
            pl.BlockSpec((CTX, NH * dv), lambda b: (ctx_blk + b, 0)),
        ],
        out_specs=pl.BlockSpec((CTX, NH * dv), lambda b: (b, 0)),
        out_shape=jax.ShapeDtypeStruct((T_CTX, NH * dv), F32),
        compiler_params=_cparams(("arbitrary",)),
        name="attn_ctx",
    )(q, k, v)


def _mlaprep_kernel(p_ref, qag_ref, kvag_ref, wq_ref, wk_ref, wv_ref, qg_ref, kg_ref, gm_ref, pm_ref,
                    cos_ref, sin_ref, q_ref, k_ref, v_ref):
    p = p_ref[...]
    cq = p[:, 0:MLA_Q_RANK]
    ckv = p[:, MLA_Q_RANK:MLA_Q_RANK + MLA_KV_RANK]
    krp = p[:, MLA_Q_RANK + MLA_KV_RANK:]
    cqn = cq * lax.rsqrt(jnp.mean(cq * cq, axis=-1, keepdims=True) + EPS) * qag_ref[...]
    ckvn = ckv * lax.rsqrt(jnp.mean(ckv * ckv, axis=-1, keepdims=True) + EPS) * kvag_ref[...]
    ckvb = ckvn.astype(BF16)
    q = _dot(cqn.astype(BF16), wq_ref[...])
    k = _dot(jnp.concatenate([ckvb, krp.astype(BF16)], axis=-1), wk_ref[...])
    v = _dot(ckvb, wv_ref[...])
    gm = gm_ref[...]
    q = q * lax.rsqrt(_group_sum(q * q, gm) * (1.0 / MLA_QK) + EPS) * qg_ref[...]
    k = k * lax.rsqrt(_group_sum(k * k, gm) * (1.0 / MLA_QK) + EPS) * kg_ref[...]
    cos = cos_ref[...]
    sin = sin_ref[...]
    pm = pm_ref[...]
    q = q * cos + _dot(q.astype(BF16), pm) * sin
    k = k * cos + _dot(k.astype(BF16), pm) * sin
    q_ref[...] = (q * (MLA_QK ** -0.5)).astype(BF16)
    k_ref[...] = k.astype(BF16)
    v_ref[...] = v.astype(BF16)


def _mlaprep(p_mla, qag, kvag, wq, wk, wv, qg, kg, gm, pm, cos_t, sin_t):
    tok = lambda w: pl.BlockSpec((TM, w), lambda i: (i, 0))
    full = lambda shape: pl.BlockSpec(shape, lambda i: (0,) * len(shape))
    pos = pl.BlockSpec((TM, 512), lambda i: (jnp.where(i < LAT_TILES, i % TILES_PER_SEQ, TILES_PER_SEQ), 0))
    return pl.pallas_call(
        _mlaprep_kernel,
        grid=(N_TILES,),
        in_specs=[tok(512), full((1, 256)), full((1, 128)), full((256, 512)), full((256, 512)),
                  full((128, 256)), full((1, 512)), full((1, 512)), full((512, 512)), full((512, 512)),
                  pos, pos],
        out_specs=[tok(512), tok(512), tok(GW)],
        out_shape=[jax.ShapeDtypeStruct((T_ALL, 512), BF16), jax.ShapeDtypeStruct((T_ALL, 512), BF16),
                   jax.ShapeDtypeStruct((T_ALL, GW), BF16)],
        compiler_params=_cparams(("parallel",)),
        name="mlaprep",
    )(p_mla, qag, kvag, wq, wk, wv, qg, kg, gm, pm, cos_t, sin_t)


def _rope_tables():
    t = jnp.arange(L)
    rowp = (t // GRID_W).astype(F32)
    colp = (t % GRID_W).astype(F32)
    half = MLA_ROPE // 2
    inv = ROPE_BASE ** (-jnp.arange(0, half, 2, dtype=F32) / half)
    j = jnp.arange(MLA_ROPE)
    pos = jnp.where(j[None, :] < half, rowp[:, None], colp[:, None])
    ang = pos * inv[j % (half // 2)][None, :]
    first = (j % half) < (half // 2)
    cos32 = jnp.cos(ang)
    sin32 = jnp.where(first[None, :], -jnp.sin(ang), jnp.sin(ang))
    cos_h = jnp.concatenate([jnp.ones((L, MLA_NOPE), F32), cos32, jnp.ones((L, 32), F32)], axis=-1)
    sin_h = jnp.concatenate([jnp.zeros((L, MLA_NOPE), F32), sin32, jnp.zeros((L, 32), F32)], axis=-1)
    cos_t = jnp.concatenate([jnp.tile(cos_h, (1, NH)), jnp.ones((TM, 512), F32)], axis=0)
    sin_t = jnp.concatenate([jnp.tile(sin_h, (1, NH)), jnp.zeros((TM, 512), F32)], axis=0)
    lane = jnp.arange(512)
    jj = lane % 128 - MLA_NOPE
    is_rope = (jj >= 0) & (jj < MLA_ROPE)
    partner = jnp.where(is_rope, jnp.where((jj % half) < (half // 2), lane + half // 2, lane - half // 2), lane)
    pm = (lane[:, None] == partner[None, :]).astype(BF16)
    return cos_t, sin_t, pm


def _outproj_kernel(x_ref, of_ref, ob_ref, g_ref, hyl_ref, hyc_ref, nal_ref, nac_ref, mll_ref, mlc_ref,
                    mod_ref, ng_ref, gm_ref, w_ref, n2_ref, wr_ref, x1_ref, h2_ref, lg_ref):
    oa = of_ref[...] + ob_ref[...]
    ms = _group_sum(oa * oa, gm_ref[...]) * (1.0 / 64)
    g = g_ref[...]
    oa = oa * lax.rsqrt(ms + EPS) * ng_ref[...] * (g * jax.nn.sigmoid(g))
    lat = pl.program_id(0) < LAT_TILES
    hy = jnp.where(lat, hyl_ref[...], hyc_ref[...])
    na = jnp.where(lat, nal_ref[...], nac_ref[...])
    mla = jnp.where(lat, mll_ref[...], mlc_ref[...])
    mix = jnp.concatenate([oa, hy, na, mla], axis=-1).astype(BF16)
    x1 = x_ref[...] + mod_ref[2:3, :] * _dot(mix, w_ref[...])
    x1_ref[...] = x1
    ms2 = jnp.mean(x1 * x1, axis=-1, keepdims=True)
    h2 = x1 * lax.rsqrt(ms2 + EPS) * n2_ref[...] * (1.0 + mod_ref[4:5, :]) + mod_ref[3:4, :]
    h2_ref[...] = h2
    hh, hl = _split2(h2)
    wr = wr_ref[...]
    lg_ref[...] = _dot(hh, wr[:, 0:128]) + _dot(hl, wr[:, 0:128]) + _dot(hh, wr[:, 128:256])


def _outproj(n_tiles, X, o_f, o_b, p_hg, hy, na, mla, mod_l, ng, gmb, w_bf, n2g, wr):
    tok = lambda w: pl.BlockSpec((TM, w), lambda i: (i, 0))
    full = lambda shape: pl.BlockSpec(shape, lambda i: (0,) * len(shape))
    latb = pl.BlockSpec((TM, GW), lambda i: (jnp.minimum(i, LAT_TILES - 1), 0))
    ctxb = pl.BlockSpec((TM, GW), lambda i: (jnp.maximum(i - LAT_TILES, 0), 0))
    nt = n_tiles * TM
    return pl.pallas_call(
        _outproj_kernel,
        grid=(n_tiles,),
        in_specs=[tok(D), tok(GW), tok(GW), pl.BlockSpec((TM, GW), lambda i: (i, 4)),
                  latb, ctxb, latb, ctxb, latb, ctxb,
                  pl.BlockSpec((None, 6, D), lambda i: (_seg_of_tile(i), 0, 0)),
                  full((1, GW)), full((GW, GW)), full((D, D)), full((1, D)), full((D, 256))],
        out_specs=[tok(D), tok(D), tok(128)],
        out_shape=[jax.ShapeDtypeStruct((nt, D), F32), jax.ShapeDtypeStruct((nt, D), F32),
                   jax.ShapeDtypeStruct((nt, 128), F32)],
        compiler_params=_cparams(("parallel",)),
        name="outproj",
    )(X, o_f, o_b, p_hg, hy[0], hy[1], na[0], na[1], mla[0], mla[1], mod_l, ng, gmb, w_bf, n2g, wr)


def _route(logits, bg, be):
    t = logits.shape[0]
    lg = logits[:, 0:N_GROUPS] + bg
    le = (logits[:, N_GROUPS:N_GROUPS + N_EXPERTS] + be).reshape(t, N_GROUPS, EPG)
    pg = jax.nn.softmax(lg, axis=-1)
    g_sel = jnp.argmax(lg, axis=-1).astype(jnp.int32)
    p_sel = jnp.take_along_axis(pg, g_sel[:, None], axis=-1)[:, 0]
    le_sel = jnp.take_along_axis(le, g_sel[:, None, None], axis=1)[:, 0]
    top_v, top_i = lax.top_k(le_sel, 2)
    wts = jax.nn.softmax(top_v, axis=-1) * p_sel[:, None]
    eid = g_sel[:, None] * EPG + top_i.astype(jnp.int32)
    return eid, wts


def _dispatch_tables(eid):
    t = eid.shape[0]
    n = 2 * t
    nb = n // TMOE + N_EXPERTS
    flat_e = eid.reshape(n)
    onehot = (flat_e[:, None] == jnp.arange(N_EXPERTS, dtype=jnp.int32)[None, :]).astype(jnp.int32)
    csum = jnp.cumsum(onehot, axis=0)
    pos = jnp.sum(csum * onehot, axis=1) - 1
    counts = csum[-1]
    pcounts = ((counts + TMOE - 1) // TMOE) * TMOE
    pends = jnp.cumsum(pcounts)
    pstarts = pends - pcounts
    dest = (jnp.sum(jnp.where(onehot > 0, pstarts[None, :], 0), axis=1) + pos).astype(jnp.int32)
    blk_start = jnp.arange(nb, dtype=jnp.int32) * TMOE
    block_e = jnp.minimum(jnp.sum((pends[None, :] <= blk_start[:, None]).astype(jnp.int32), axis=1),
                          N_EXPERTS - 1).astype(jnp.int32)
    nblk = (pends[-1] // TMOE).astype(jnp.int32).reshape(1)
    pads = jnp.concatenate([pstarts + counts, pcounts - counts, nblk]).astype(jnp.int32)
    return block_e, nblk, dest, pads


ROW_TILE = 8
PAD_PIECES = tuple(ROW_TILE << b for b in range((TMOE // ROW_TILE).bit_length() - 1))


def _dispatch_kernel(dest_ref, pad_ref, h_ref, xs_out, zbuf, sem, zsem):
    base = pl.program_id(0) * (2 * TM)

    @pl.when(pl.program_id(0) == 0)
    def _():
        zbuf[...] = jnp.zeros_like(zbuf)
        for phase in range(2):
            for e in range(N_EXPERTS):
                start = pad_ref[e]
                npad = pad_ref[N_EXPERTS + e]
                head = (ROW_TILE - start % ROW_TILE) % ROW_TILE
                for j in range(ROW_TILE - 1):
                    @pl.when(j < head)
                    def _(j=j):
                        cp = pltpu.make_async_copy(zbuf.at[pl.ds(0, 1)], xs_out.at[pl.ds(start + j, 1)], zsem)
                        cp.start() if phase == 0 else cp.wait()

                off = start + head
                rest = npad - head
                for piece in PAD_PIECES:
                    has = (rest & piece) != 0

                    @pl.when(has)
                    def _(off=off, piece=piece):
                        cp = pltpu.make_async_copy(zbuf.at[pl.ds(0, piece)],
                                                   xs_out.at[pl.ds(pl.multiple_of(off, ROW_TILE), piece)], zsem)
                        cp.start() if phase == 0 else cp.wait()

                    off = off + jnp.where(has, piece, 0)

            zrows = zbuf.shape[0]
            first = pad_ref[2 * N_EXPERTS] * (TMOE // zrows)

            def tail(j, carry):
                cp = pltpu.make_async_copy(zbuf, xs_out.at[pl.ds(pl.multiple_of(j * zrows, zrows), zrows)], zsem)
                cp.start() if phase == 0 else cp.wait()
                return carry

            lax.fori_loop(first, xs_out.shape[0] // zrows, tail, 0)

    def issue(r, carry):
        for k in range(2):
            pltpu.make_async_copy(h_ref.at[pl.ds(r, 1)], xs_out.at[pl.ds(dest_ref[base + 2 * r + k], 1)],
                                  sem).start(priority=k)
        return carry

    lax.fori_loop(0, TM, issue, 0, unroll=8)
    pltpu.make_async_copy(xs_out.at[pl.ds(0, 2 * TM)], xs_out.at[pl.ds(0, 2 * TM)], sem).wait()


def _dispatch(n_tiles, n_slots, dest, pads, h2):
    grid_spec = pltpu.PrefetchScalarGridSpec(
        num_scalar_prefetch=2,
        grid=(n_tiles,),
        in_specs=[pl.BlockSpec((TM, D), lambda i, dst, pd: (i, 0))],
        out_specs=pl.BlockSpec(memory_space=pl.ANY),
        scratch_shapes=[pltpu.VMEM((TMOE // 2, D), F32), pltpu.SemaphoreType.DMA(()), pltpu.SemaphoreType.DMA(())],
    )
    return pl.pallas_call(
        _dispatch_kernel,
        grid_spec=grid_spec,
        out_shape=jax.ShapeDtypeStruct((n_slots, D), F32),
        compiler_params=_cparams(("arbitrary",)),
        name="dispatch",
    )(dest, pads, h2)


def _experts_kernel(be_ref, nblk_ref, xs_ref, wg_ref, wu_ref, wd_ref, ys_ref, wgb, wub, wdb):
    i = pl.program_id(0)

    @pl.when((i == 0) | (be_ref[i] != be_ref[jnp.maximum(i - 1, 0)]))
    def _():
        wgb[...] = wg_ref[...].astype(BF16)
        wub[...] = wu_ref[...].astype(BF16)
        wdb[...] = wd_ref[...].astype(BF16)

    @pl.when(i < nblk_ref[0])
    def _():
        x = xs_ref[...].astype(BF16)
        gate = _dot(x, wgb[...])
        up = _dot(x, wub[...])
        act = (gate * jax.nn.sigmoid(gate)) * up
        ys_ref[...] = _dot(act.astype(BF16), wdb[...])

    @pl.when(i >= nblk_ref[0])
    def _():
        ys_ref[...] = jnp.zeros_like(ys_ref)


def _experts(layer, block_e, nblk, xs, w_gate, w_up, w_down):
    nb = block_e.shape[0]
    used = lambda i, nk: jnp.minimum(i, nk[0] - 1)
    grid_spec = pltpu.PrefetchScalarGridSpec(
        num_scalar_prefetch=2,
        grid=(nb,),
        in_specs=[
            pl.BlockSpec((TMOE, D), lambda i, be, nk: (used(i, nk), 0)),
            pl.BlockSpec((None, None, D, D_EXPERT), lambda i, be, nk: (layer, be[i], 0, 0)),
            pl.BlockSpec((None, None, D, D_EXPERT), lambda i, be, nk: (layer, be[i], 0, 0)),
            pl.BlockSpec((None, None, D_EXPERT, D), lambda i, be, nk: (layer, be[i], 0, 0)),
        ],
        out_specs=pl.BlockSpec((TMOE, D), lambda i, be, nk: (i, 0)),
        scratch_shapes=[pltpu.VMEM((D, D_EXPERT), BF16), pltpu.VMEM((D, D_EXPERT), BF16),
                        pltpu.VMEM((D_EXPERT, D), BF16)],
    )
    return pl.pallas_call(
        _experts_kernel,
        grid_spec=grid_spec,
        out_shape=jax.ShapeDtypeStruct(xs.shape, F32),
        compiler_params=_cparams(("arbitrary",)),
        name="experts",
    )(block_e, nblk, xs, w_gate, w_up, w_down)


def _combine_kernel(dest_ref, x_ref, w_ref, mod_ref, ys_hbm, o_ref, ybuf, sem):
    i = pl.program_id(0)

    def fetch(tile, slot):
        base = tile * (2 * TM)

        def issue(r, carry):
            for k in range(2):
                pltpu.make_async_copy(ys_hbm.at[pl.ds(dest_ref[base + 2 * r + k], 1)],
                                      ybuf.at[slot, k, pl.ds(r, 1)], sem.at[slot]).start(priority=k)
            return carry

        lax.fori_loop(0, TM, issue, 0, unroll=8)

    @pl.when(i == 0)
    def _():
        fetch(0, 0)

    @pl.when(i + 1 < pl.num_programs(0))
    def _():
        fetch(i + 1, (i + 1) % 2)

    slot = i % 2
    for k in range(2):
        pltpu.make_async_copy(ys_hbm.at[pl.ds(0, TM)], ybuf.at[slot, k], sem.at[slot]).wait()
    w = w_ref[...]
    y = w[:, 0:1] * ybuf[slot, 0] + w[:, 1:2] * ybuf[slot, 1]
    o_ref[...] = x_ref[...] + mod_ref[5:6, :] * y


def _combine(n_tiles, dest, X1, wts, mod_l, ys):
    grid_spec = pltpu.PrefetchScalarGridSpec(
        num_scalar_prefetch=1,
        grid=(n_tiles,),
        in_specs=[pl.BlockSpec((TM, D), lambda i, dst: (i, 0)),
                  pl.BlockSpec((TM, 2), lambda i, dst: (i, 0)),
                  pl.BlockSpec((None, 6, D), lambda i, dst: (_seg_of_tile(i), 0, 0)),
                  pl.BlockSpec(memory_space=pl.ANY)],
        out_specs=pl.BlockSpec((TM, D), lambda i, dst: (i, 0)),
        scratch_shapes=[pltpu.VMEM((2, 2, TM, D), F32), pltpu.SemaphoreType.DMA((2,))],
    )
    return pl.pallas_call(
        _combine_kernel,
        grid_spec=grid_spec,
        out_shape=jax.ShapeDtypeStruct((n_tiles * TM, D), F32),
        compiler_params=_cparams(("arbitrary",)),
        name="combine",
    )(dest, X1, wts, mod_l, ys)


def _group_mask(width, group):
    lane = jnp.arange(width)
    return (lane[:, None] // group == lane[None, :] // group)


def _hgrn_tri(reverse):
    t = jnp.arange(HCHUNK)
    same = (t[:, None] // HSUB) == (t[None, :] // HSUB)
    order = (t[None, :] >= t[:, None]) if reverse else (t[None, :] <= t[:, None])
    return (same & order).astype(BF16)


def _mla_weights(w_uq, w_ukv, q_g, k_g):
    wq = jnp.pad(w_uq.reshape(MLA_Q_RANK, NH, MLA_QK), ((0, 0), (0, 0), (0, 128 - MLA_QK))).reshape(MLA_Q_RANK, 512)
    kv = w_ukv.reshape(MLA_KV_RANK, NH, MLA_NOPE + 64)
    wk_top = jnp.pad(kv[:, :, :MLA_NOPE], ((0, 0), (0, 0), (0, 128 - MLA_NOPE))).reshape(MLA_KV_RANK, 512)
    lane = jnp.arange(512)
    src = jnp.arange(128)
    place = ((lane[None, :] % 128) == (src[:, None] + MLA_NOPE)) & (src[:, None] < MLA_ROPE)
    wk = jnp.concatenate([wk_top, place.astype(F32)], axis=0)
    wv = kv[:, :, MLA_NOPE:].reshape(MLA_KV_RANK, GW)
    pad_g = lambda g: jnp.tile(jnp.pad(g, (0, 128 - MLA_QK)), NH).reshape(1, 512)
    return wq.astype(BF16), wk.astype(BF16), wv.astype(BF16), pad_g(q_g), pad_g(k_g)


def kernel(x, c, ctx, c_ctx, w_ada, b_ada, norm1_g, norm2_g, w_in, w_out, hgrn_lb_logits, hgrn_norm_g,
           hy_short_w, hy_short_b, hy_w1, hy_b1, hy_freq, hy_w2, hy_b2, hy_w3, hy_b3, hy_decay, hy_bias,
           na_rpb, na_q_g, na_k_g, mla_q_a_g, mla_kv_a_g, mla_w_uq, mla_w_ukv, mla_q_g, mla_k_g,
           moe_wg, moe_bg, moe_we, moe_be, moe_w_gate, moe_w_up, moe_w_down):
    X = jnp.concatenate([x.reshape(T_LAT, D), ctx.reshape(T_CTX, D)], axis=0)
    cmat = jnp.concatenate([c, c_ctx[None, :], jnp.zeros((16 - B - 1, D), F32)], axis=0)
    mod = _adaln(cmat, w_ada, b_ada).reshape(DEPTH, 16, 6, D)

    lb_cum = jnp.cumsum(jax.nn.softmax(hgrn_lb_logits.astype(F32), axis=0), axis=0)
    lower = lb_cum - lb_cum[0:1]

    gm64 = _group_mask(GW, 64)
    gm64_f = gm64.astype(F32)
    gm64_b = gm64.astype(BF16)
    gm128_b = _group_mask(512, 128).astype(BF16)
    trif = _hgrn_tri(False)
    trib = _hgrn_tri(True)
    cos_t, sin_t, pm = _rope_tables()
    dft = {}
    for n in (L, CTX):
        cm, sm = _dft_consts(n)
        chi, clo = _split2(cm)
        shi, slo = _split2(sm)
        dft[n] = (chi, clo, shi, slo, _hyena_feats(n))

    for l in range(DEPTH):
        mod_l = mod[l]
        p_hg, p_hy, p_na, p_mla = _inproj(X, mod_l, norm1_g[l].reshape(1, D),
                                          jnp.pad(w_in[l], ((0, 0), (0, D_IN_PAD - D_IN))).astype(BF16))

        lb = lower[l]
        hconst = jnp.concatenate([
            jnp.stack([jnp.maximum(jnp.log(lb[d]), NEG), jnp.log1p(-lb[d]), 1.0 - lb[d]]) for d in range(2)
        ] + [jnp.zeros((2, GW), F32)], axis=0)
        o_f, o_b = _hgrn(p_hg, hconst, trif, trib, gm64_f, gm64_b)

        need_ctx = l < DEPTH - 1
        w1p = jnp.pad(hy_w1[l], ((0, 128 - HYENA_EMB), (0, 0)))
        o_hy = []
        for n, blk0 in ((L, 0), (CTX, T_LAT // CTX)):
            if n == CTX and not need_ctx:
                continue
            chi, clo, shi, slo, feats = dft[n]
            e, o, knq = _hyfilt(feats, w1p, hy_b1[l].reshape(1, -1), hy_freq[l].reshape(1, -1), hy_w2[l],
                                hy_b2[l].reshape(1, -1), hy_w3[l], hy_b3[l].reshape(1, -1),
                                hy_decay[l].reshape(1, 4 * GW))
            kre, kim = _hyspec(chi, clo, shi, slo, e, o)
            o_hy.append(_hyena(p_hy, blk0, B, n, hy_short_w[l], hy_short_b[l].reshape(1, -1), hy_bias[l],
                               chi, shi, kre, kim, knq))

        qn, kn, vn = _naprep(p_na, jnp.tile(na_q_g[l], NH).reshape(1, GW), jnp.tile(na_k_g[l], NH).reshape(1, GW),
                             gm64_b)
        o_na = [_na(qn, kn, vn, _na_bias_table(na_rpb[l]))]

        wq, wk, wv, qg, kg = _mla_weights(mla_w_uq[l], mla_w_ukv[l], mla_q_g[l], mla_k_g[l])
        mq, mk, mv = _mlaprep(p_mla, mla_q_a_g[l].reshape(1, -1), mla_kv_a_g[l].reshape(1, -1), wq, wk, wv,
                              qg, kg, gm128_b, pm, cos_t, sin_t)
        o_mla = [_attn_latent(mq, mk, mv, 128, 64, 256)]
        if need_ctx:
            o_na.append(_attn_ctx(qn, kn, vn, 64, 64))
            o_mla.append(_attn_ctx(mq, mk, mv, 128, 64))
        else:
            o_hy.append(o_hy[0])
            o_na.append(o_na[0])
            o_mla.append(o_mla[0])
        n_tiles = N_TILES if need_ctx else LAT_TILES

        wr = jnp.pad(jnp.concatenate([moe_wg[l], moe_we[l]], axis=1), ((0, 0), (0, 128 - N_GROUPS - N_EXPERTS)))
        wr_hi, wr_lo = _split2(wr)
        X1, h2, logits = _outproj(n_tiles, X, o_f, o_b, p_hg, o_hy, o_na, o_mla, mod_l,
                                  jnp.tile(hgrn_norm_g[l], NH).reshape(1, GW), gm64_b,
                                  w_out[l].astype(BF16), norm2_g[l].reshape(1, D),
                                  jnp.concatenate([wr_hi, wr_lo], axis=1))

        eid, wts = _route(logits, moe_bg[l], moe_be[l])
        block_e, nblk, dest, pads = _dispatch_tables(eid)
        xs = _dispatch(n_tiles, block_e.shape[0] * TMOE, dest, pads, h2)
        ys = _experts(l, block_e, nblk, xs, moe_w_gate, moe_w_up, moe_w_down)
        X = _combine(n_tiles, dest, X1, wts, mod_l, ys)

    return X.reshape(B, L, D)
```

```python
import functools
import math

import jax
import jax.numpy as jnp
from jax import lax
from jax.experimental import pallas as pl
from jax.experimental.pallas import tpu as pltpu

F32 = jnp.float32
BF16 = jnp.bfloat16

D = 1024
B = 8
L = 2048
CTX = 256
DEPTH = 4
GRID_W = 64
EPS = 1e-6
GW = 256
NH = 4
HYENA_BANDS = 16
HYENA_EMB = 1 + 2 * HYENA_BANDS
HYENA_FFN = 64
NA_ROWS = 8
NA_COLS = 16
MLA_Q_RANK = 256
MLA_KV_RANK = 128
MLA_NOPE = 64
MLA_ROPE = 32
MLA_QK = MLA_NOPE + MLA_ROPE
ROPE_BASE = 10000.0
N_GROUPS = 4
EPG = 8
N_EXPERTS = N_GROUPS * EPG
D_EXPERT = 512
D_IN = 3232
D_IN_PAD = 3328

T_LAT = B * L
T_CTX = B * CTX
T_ALL = T_LAT + T_CTX

TM = 256
N_TILES = T_ALL // TM
LAT_TILES = T_LAT // TM
TILES_PER_SEQ = L // TM
HCHUNK = 64
HSUB = 16
HGRN_SAFE_DECAY = 80.0
TMOE = 256
VMEM_LIMIT_BYTES = 56 * 1024 * 1024
NEG = -1e30

HI = lax.Precision.HIGHEST


def _cparams(sem, vmem=VMEM_LIMIT_BYTES):
    return pltpu.CompilerParams(dimension_semantics=sem, vmem_limit_bytes=vmem)


def _seg_of_tile(i):
    return jnp.where(i < LAT_TILES, i // TILES_PER_SEQ, B)


def _dot(a, b):
    return jnp.dot(a, b, preferred_element_type=F32)


def _dot_nt(a, b):
    return lax.dot_general(a, b, (((1,), (1,)), ((), ())), preferred_element_type=F32)


def _dot_tn(a, b):
    return lax.dot_general(a, b, (((0,), (0,)), ((), ())), preferred_element_type=F32)


def _split2(x):
    hi = x.astype(BF16)
    lo = (x - hi.astype(F32)).astype(BF16)
    return hi, lo


def _split3(x):
    h1 = x.astype(BF16)
    r1 = x - h1.astype(F32)
    h2 = r1.astype(BF16)
    h3 = (r1 - h2.astype(F32)).astype(BF16)
    return h1, h2, h3


LANE = 128
ROW_CH = D // LANE


def _store_rowtiles(ref, val):
    n = val.shape[0]
    for j in range(ROW_CH):
        ref[pl.ds(j, n, stride=ROW_CH), :] = val[:, j * LANE:(j + 1) * LANE]


def _load_rowtiles(ref, n):
    return jnp.concatenate([ref[pl.ds(j, n, stride=ROW_CH), :] for j in range(ROW_CH)], axis=-1)


def _group_sum(x, gm):
    hi, lo = _split2(x)
    return _dot(hi, gm) + _dot(lo, gm)


def _ada_kernel(c_ref, w_ref, b_ref, o_ref):
    cc = c_ref[...]
    sc = cc * jax.nn.sigmoid(cc)
    o_ref[0] = jnp.dot(sc, w_ref[0], preferred_element_type=F32, precision=HI) + b_ref[0]


def _adaln(cmat, w_ada, b_ada):
    tn = 1536
    return pl.pallas_call(
        _ada_kernel,
        grid=(DEPTH, 6 * D // tn),
        in_specs=[
            pl.BlockSpec((16, D), lambda l, j: (0, 0)),
            pl.BlockSpec((1, D, tn), lambda l, j: (l, 0, j)),
            pl.BlockSpec((1, 1, tn), lambda l, j: (l, 0, j)),
        ],
        out_specs=pl.BlockSpec((1, 16, tn), lambda l, j: (l, 0, j)),
        out_shape=jax.ShapeDtypeStruct((DEPTH, 16, 6 * D), F32),
        compiler_params=_cparams(("arbitrary", "arbitrary")),
        name="adaln",
    )(cmat, w_ada, b_ada.reshape(DEPTH, 1, 6 * D))


def _inproj_kernel(x_ref, mod_ref, g_ref, w_ref, o_hg, o_hy, o_na, o_mla):
    x = x_ref[...]
    ms = jnp.mean(x * x, axis=-1, keepdims=True)
    y = x * lax.rsqrt(ms + EPS) * g_ref[...]
    h = y * (1.0 + mod_ref[1:2, :]) + mod_ref[0:1, :]
    p = _dot(h.astype(BF16), w_ref[...])
    o_hg[...] = p[:, 0:1280]
    o_hy[...] = p[:, 1280:2048]
    o_na[...] = p[:, 2048:2816]
    o_mla[...] = p[:, 2816:3328]


def _inproj(X, mod_l, g, w_bf):
    return pl.pallas_call(
        _inproj_kernel,
        grid=(N_TILES,),
        in_specs=[
            pl.BlockSpec((TM, D), lambda i: (i, 0)),
            pl.BlockSpec((None, 6, D), lambda i: (_seg_of_tile(i), 0, 0)),
            pl.BlockSpec((1, D), lambda i: (0, 0)),
            pl.BlockSpec((D, D_IN_PAD), lambda i: (0, 0)),
        ],
        out_specs=[
            pl.BlockSpec((TM, 1280), lambda i: (i, 0)),
            pl.BlockSpec((TM, 768), lambda i: (i, 0)),
            pl.BlockSpec((TM, 768), lambda i: (i, 0)),
            pl.BlockSpec((TM, 512), lambda i: (i, 0)),
        ],
        out_shape=[
            jax.ShapeDtypeStruct((T_ALL, 1280), F32),
            jax.ShapeDtypeStruct((T_ALL, 768), F32),
            jax.ShapeDtypeStruct((T_ALL, 768), F32),
            jax.ShapeDtypeStruct((T_ALL, 512), F32),
        ],
        compiler_params=_cparams(("parallel",)),
        name="inproj",
    )(X, mod_l, g, w_bf)


HSTEP = 2 * HCHUNK


def _hgrn_prologue(p_ref, r0, zcol, c_ref, crow, tri_ref, reverse):
    q = p_ref[r0:r0 + HCHUNK, 0:GW]
    z = p_ref[r0:r0 + HCHUNK, zcol:zcol + GW]
    v = p_ref[r0:r0 + HCHUNK, 3 * GW:4 * GW]
    la = c_ref[crow:crow + 1, :]
    l1 = c_ref[crow + 1:crow + 2, :]
    oml = c_ref[crow + 2:crow + 3, :]
    e = jnp.exp(-jnp.abs(z))
    ope = 1.0 + e
    ls = jnp.minimum(z, 0.0) - jnp.log(ope)
    c2 = l1 + ls
    logf = jnp.maximum(la, c2) + jnp.log(1.0 + jnp.exp(-jnp.abs(la - c2)))
    kk = oml * (jnp.where(z >= 0.0, e, 1.0) / ope)
    h1, h2, h3 = _split3(logf)
    tri_full = tri_ref[1]
    bfull = _dot(tri_full, h1) + _dot(tri_full, h2) + _dot(tri_full, h3)
    half = HCHUNK // 2
    first, mid, last = (HCHUNK - 1, half, 0) if reverse else (0, half - 1, HCHUNK - 1)
    btot = bfull[last:last + 1]
    bmid = bfull[mid:mid + 1]
    worst = jnp.maximum(bfull[first:first + 1] - bmid, bmid - btot)
    return dict(r0=r0, q=q, kk=kk, v=v, splits=(h1, h2, h3), bfull=bfull, btot=btot, bmid=bmid, worst=worst)


def _hgrn_fast(c, st, gm, reverse):
    q, kk, v, bfull, btot, bmid = c["q"], c["kk"], c["v"], c["bfull"], c["btot"], c["bmid"]
    lane_head = lax.broadcasted_iota(jnp.int32, (HCHUNK, GW), 1) // 64
    qi = (q * jnp.exp(bfull - bmid)).astype(BF16)
    ke = (kk * jnp.exp(bmid - bfull)).astype(BF16)
    qx = jnp.concatenate([jnp.where(lane_head == h, qi, jnp.zeros_like(qi)) for h in range(NH)], axis=0)
    a = _dot_nt(qx, ke)
    t_idx = lax.broadcasted_iota(jnp.int32, a.shape, 0) % HCHUNK
    s_idx = lax.broadcasted_iota(jnp.int32, a.shape, 1)
    seen = (s_idx >= t_idx) if reverse else (s_idx <= t_idx)
    a = jnp.where(seen, a, 0.0).astype(BF16)
    vb = v.astype(BF16)
    o_all = _dot(a, vb)
    o = _dot_nt((q * jnp.exp(bfull)).astype(BF16), st.astype(BF16))
    for h in range(NH):
        o = o + jnp.where(lane_head == h, o_all[h * HCHUNK:(h + 1) * HCHUNK, :], 0.0)
    kd = (kk * jnp.exp(btot - bfull)).astype(BF16)
    return o, st * jnp.exp(btot) + _dot_tn(vb, kd) * gm


def _hgrn_slow(c, st, gm, gmb, tri_sub, reverse, o_ref):
    q, kk, v = c["q"], c["kk"], c["v"]
    h1, h2, h3 = c["splits"]
    bsub = _dot(tri_sub, h1) + _dot(tri_sub, h2) + _dot(tri_sub, h3)
    row = lax.broadcasted_iota(jnp.int32, (HSUB, GW), 0)
    order = range(HCHUNK // HSUB - 1, -1, -1) if reverse else range(HCHUNK // HSUB)
    for blk in order:
        r0 = blk * HSUB
        b_i = bsub[r0:r0 + HSUB]
        q_i = q[r0:r0 + HSUB]
        k_i = kk[r0:r0 + HSUB]
        v_i = v[r0:r0 + HSUB]
        bt_i = b_i[0:1] if reverse else b_i[HSUB - 1:HSUB]
        qe = (q_i * jnp.exp(b_i)).astype(BF16)
        o_inter = _dot_nt(qe, st.astype(BF16))
        parts = []
        for tl in range(HSUB):
            dlt = b_i[tl:tl + 1] - b_i
            valid = (row >= tl) if reverse else (row <= tl)
            w = jnp.exp(jnp.where(valid, dlt, NEG))
            parts.append((q_i[tl:tl + 1] * w) * k_i)
        pmat = jnp.concatenate(parts, axis=0).astype(BF16)
        abar = _dot(pmat, gmb)
        o_diag = jnp.sum(abar.reshape(HSUB, HSUB, GW) * v_i[None], axis=1)
        o_ref[c["r0"] + r0:c["r0"] + r0 + HSUB, :] = o_inter + o_diag
        kd = (k_i * jnp.exp(bt_i - b_i)).astype(BF16)
        upd = _dot_tn(v_i.astype(BF16), kd)
        st = st * jnp.exp(bt_i) + upd * gm
    return st


def _hgrn_kernel(pf_ref, pb_ref, c_ref, trif_ref, trib_ref, gm_ref, gmb_ref, of_ref, ob_ref, stf, stb):
    @pl.when(pl.program_id(1) == 0)
    def _():
        stf[...] = jnp.zeros_like(stf)
        stb[...] = jnp.zeros_like(stb)

    gm = gm_ref[...]
    dirs = []
    for p_ref, o_ref, st_ref, tri_ref, zcol, crow, reverse in (
            (pf_ref, of_ref, stf, trif_ref, GW, 0, False), (pb_ref, ob_ref, stb, trib_ref, 2 * GW, 3, True)):
        offs = (HCHUNK, 0) if reverse else (0, HCHUNK)
        chunks = [_hgrn_prologue(p_ref, r0, zcol, c_ref, crow, tri_ref, reverse) for r0 in offs]
        dirs.append((o_ref, st_ref, tri_ref, reverse, chunks))
    worst = functools.reduce(jnp.maximum, [c["worst"] for d in dirs for c in d[4]])
    safe = jnp.max(worst) < HGRN_SAFE_DECAY

    @pl.when(safe)
    def _():
        for o_ref, st_ref, tri_ref, reverse, chunks in dirs:
            st = st_ref[...]
            for c in chunks:
                o, st = _hgrn_fast(c, st, gm, reverse)
                o_ref[c["r0"]:c["r0"] + HCHUNK, :] = o
            st_ref[...] = st

    @pl.when(jnp.logical_not(safe))
    def _():
        gmb = gmb_ref[...]
        for o_ref, st_ref, tri_ref, reverse, chunks in dirs:
            st = st_ref[...]
            for c in chunks:
                st = _hgrn_slow(c, st, gm, gmb, tri_ref[0], reverse, o_ref)
            st_ref[...] = st


def _hgrn_block(b, n, reverse):
    nctx = CTX // HSTEP
    nlat = L // HSTEP
    jc = (nctx - 1 - n) if reverse else n
    jl = (nlat - 1 - (n - nctx)) if reverse else (n - nctx)
    return jnp.where(n < nctx, T_LAT // HSTEP + b * nctx + jc, b * nlat + jl)


def _hgrn(p_hg, consts, trif, trib, gm, gmb):
    nsteps = (CTX + L) // HSTEP
    full = lambda shape: pl.BlockSpec(shape, lambda b, n: (0,) * len(shape))
    return pl.pallas_call(
        _hgrn_kernel,
        grid=(B, nsteps),
        in_specs=[
            pl.BlockSpec((HSTEP, 1280), lambda b, n: (_hgrn_block(b, n, False), 0)),
            pl.BlockSpec((HSTEP, 1280), lambda b, n: (_hgrn_block(b, n, True), 0)),
            full((8, GW)),
            full((2, HCHUNK, HCHUNK)),
            full((2, HCHUNK, HCHUNK)),
            full((GW, GW)),
            full((GW, GW)),
        ],
        out_specs=[
            pl.BlockSpec((HSTEP, GW), lambda b, n: (_hgrn_block(b, n, False), 0)),
            pl.BlockSpec((HSTEP, GW), lambda b, n: (_hgrn_block(b, n, True), 0)),
        ],
        out_shape=[jax.ShapeDtypeStruct((T_ALL, GW), F32), jax.ShapeDtypeStruct((T_ALL, GW), F32)],
        scratch_shapes=[pltpu.VMEM((GW, GW), F32), pltpu.VMEM((GW, GW), F32)],
        compiler_params=_cparams(("arbitrary", "arbitrary")),
        name="hgrn",
    )(p_hg, p_hg, consts, trif, trib, gm, gmb)


def _alt_sum(x):
    n, c = x.shape
    sgn = jnp.where((lax.broadcasted_iota(jnp.int32, (n, c), 0) & 1) == 0, 1.0, -1.0)
    return jnp.sum(x * sgn, axis=0, keepdims=True)


def _hyfilt_kernel(feats_ref, w1_ref, b1_ref, fr_ref, w2_ref, b2_ref, w3_ref, b3_ref, dec_ref,
                   e_ref, o_ref, nq_ref):
    fr = fr_ref[...]
    feats = feats_ref[...]
    h = jnp.sin(fr * (jnp.dot(feats, w1_ref[...], preferred_element_type=F32, precision=HI) + b1_ref[...]))
    h = jnp.sin(fr * (jnp.dot(h, w2_ref[...], preferred_element_type=F32, precision=HI) + b2_ref[...]))
    filt = jnp.dot(h, w3_ref[...], preferred_element_type=F32, precision=HI) + b3_ref[...]
    filt = filt * jnp.exp(-feats[:, 0:1] * dec_ref[...])
    n = filt.shape[0]
    row = lax.broadcasted_iota(jnp.int32, (n, GW), 0)
    for o in range(2):
        fwd = filt[:, (2 * o) * GW:(2 * o + 1) * GW]
        bwd = jnp.where(row >= 1, filt[:, (2 * o + 1) * GW:(2 * o + 2) * GW], 0.0)
        ssq = jnp.sum(fwd * fwd + bwd * bwd, axis=0, keepdims=True)
        scale = lax.rsqrt(ssq + EPS)
        ev = (fwd + bwd) * scale
        e_ref[:, o * GW:(o + 1) * GW] = ev
        o_ref[:, o * GW:(o + 1) * GW] = (fwd - bwd) * scale
        nq_ref[:, o * GW:(o + 1) * GW] = _alt_sum(ev) * (0.5 / n)


def _hyfilt(feats, w1p, b1, fr, w2, b2, w3, b3, dec):
    n = feats.shape[0]
    return pl.pallas_call(
        _hyfilt_kernel,
        out_shape=[jax.ShapeDtypeStruct((n, 2 * GW), F32), jax.ShapeDtypeStruct((n, 2 * GW), F32),
                   jax.ShapeDtypeStruct((1, 2 * GW), F32)],
        compiler_params=_cparams(None),
        name="hyfilt",
    )(feats, w1p, b1, fr, w2, b2, w3, b3, dec)


def _hyspec_kernel(chi_ref, clo_ref, shi_ref, slo_ref, e_ref, o_ref, kre_ref, kim_ref, *, n):
    eh, el = _split2(e_ref[...])
    oh, ol = _split2(o_ref[...])
    kre = _dot(chi_ref[...], eh) + _dot(chi_ref[...], el) + _dot(clo_ref[...], eh)
    kim = _dot(shi_ref[...], oh) + _dot(shi_ref[...], ol) + _dot(slo_ref[...], oh)
    tr = kre.shape[0]
    grow = lax.broadcasted_iota(jnp.int32, kre.shape, 0) + pl.program_id(0) * tr
    s2 = 1.0 / n
    kre_ref[...] = kre * jnp.where(grow == 0, 0.5 * s2, s2)
    kim_ref[...] = kim * s2


def _hyspec(chi, clo, shi, slo, e, o):
    n = e.shape[0]
    tr = min(256, n)
    rows = pl.BlockSpec((tr, n), lambda i: (i, 0))
    full = pl.BlockSpec((n, 2 * GW), lambda i: (0, 0))
    outb = pl.BlockSpec((tr, 2 * GW), lambda i: (i, 0))
    return pl.pallas_call(
        functools.partial(_hyspec_kernel, n=n),
        grid=(n // tr,),
        in_specs=[rows, rows, rows, rows, full, full],
        out_specs=[outb, outb],
        out_shape=[jax.ShapeDtypeStruct((n, 2 * GW), F32)] * 2,
        compiler_params=_cparams(("parallel",)),
        name="hyspec",
    )(chi, clo, shi, slo, e, o)


def _hyena_kernel(u_ref, sw_ref, sb_ref, db_ref, c_ref, s_ref, kre_ref, kim_ref, knq_ref, o_ref,
                  z_scr, zb_scr, y_scr):
    n = u_ref.shape[0]
    ft = min(512, n)
    rc = min(256, n)
    nchunks = n // rc
    lrow = lax.broadcasted_iota(jnp.int32, (rc, GW), 0)
    sgn = jnp.where((lrow & 1) == 0, 1.0, -1.0)

    def short_conv(part, c):
        sl = slice(part * GW, (part + 1) * GW)
        r0 = c * rc
        u = u_ref[r0:r0 + rc, sl]
        prev = u_ref[r0 - 1:r0, sl] if c > 0 else jnp.zeros((1, GW), F32)
        nxt = u_ref[r0 + rc:r0 + rc + 1, sl] if c < nchunks - 1 else jnp.zeros((1, GW), F32)
        up = jnp.where(lrow == 0, prev, pltpu.roll(u, 1, 0))
        un = jnp.where(lrow == rc - 1, nxt, pltpu.roll(u, rc - 1, 0))
        return sw_ref[0:1, sl] * up + sw_ref[1:2, sl] * u + sw_ref[2:3, sl] * un + sb_ref[:, sl]

    for c in range(nchunks):
        z_scr[c * rc:(c + 1) * rc, :] = short_conv(0, c)
    for o in range(2):
        cols = slice(o * GW, (o + 1) * GW)
        znq = jnp.zeros((1, GW), F32)
        for c in range(nchunks):
            zc = z_scr[c * rc:(c + 1) * rc, :]
            zb_scr[c * rc:(c + 1) * rc, :] = zc.astype(BF16)
            znq = znq + jnp.sum(zc * sgn, axis=0, keepdims=True)
        ynq = znq * knq_ref[:, cols]
        for c in range(nchunks):
            y_scr[c * rc:(c + 1) * rc, :] = sgn * ynq
        for f in range(n // ft):
            rs = slice(f * ft, (f + 1) * ft)
            zre = _dot(c_ref[rs, :], zb_scr[...])
            zim = _dot(s_ref[rs, :], zb_scr[...])
            kre = kre_ref[rs, cols]
            kim = kim_ref[rs, cols]
            yre = (zre * kre - zim * kim).astype(BF16)
            yim = (zre * kim + zim * kre).astype(BF16)
            y_scr[...] += _dot(c_ref[:, rs], yre) + _dot(s_ref[:, rs], yim)
        dst = o_ref if o == 1 else z_scr
        for c in range(nchunks):
            rows = slice(c * rc, (c + 1) * rc)
            dst[rows, :] = short_conv(o + 1, c) * (y_scr[rows, :] + db_ref[o:o + 1, :] * z_scr[rows, :])


def _hyena(u, blk0, nb, n, sw, sb, db, cm, sm, kre, kim, knq):
    whole = pl.BlockSpec(memory_space=pltpu.VMEM)
    return pl.pallas_call(
        _hyena_kernel,
        grid=(nb,),
        in_specs=[
            pl.BlockSpec((n, 3 * GW), lambda b: (blk0 + b, 0)),
            pl.BlockSpec((3, 3 * GW), lambda b: (0, 0)),
            pl.BlockSpec((1, 3 * GW), lambda b: (0, 0)),
            pl.BlockSpec((2, GW), lambda b: (0, 0)),
            whole, whole, whole, whole, whole,
        ],
        out_specs=pl.BlockSpec((n, GW), lambda b: (b, 0)),
        out_shape=jax.ShapeDtypeStruct((nb * n, GW), F32),
        scratch_shapes=[pltpu.VMEM((n, GW), F32), pltpu.VMEM((n, GW), BF16), pltpu.VMEM((n, GW), F32)],
        compiler_params=_cparams(("arbitrary",)),
        name="hyena",
    )(u, sw, sb, db, cm, sm, kre, kim, knq)


def _dft_consts(n):
    kk = jnp.arange(n, dtype=jnp.int32)
    ph = (kk[:, None] * kk[None, :]) % (2 * n)
    ang = ph.astype(F32) * (math.pi / n)
    return jnp.cos(ang), -jnp.sin(ang)


def _hyena_feats(n):
    t = jnp.arange(n, dtype=F32)
    t_unit = jnp.linspace(0.0, 1.0, n, dtype=F32)
    bands = jnp.linspace(1e-4, HYENA_BANDS - 1, HYENA_BANDS, dtype=F32)
    ang = (2.0 * math.pi / n) * t[:, None] * bands[None, :]
    feats = jnp.concatenate([t_unit[:, None], jnp.cos(ang), -jnp.sin(ang)], axis=-1)
    return jnp.pad(feats, ((0, 0), (0, 128 - HYENA_EMB)))


def _naprep_kernel(p_ref, qg_ref, kg_ref, gm_ref, q_ref, k_ref, v_ref):
    p = p_ref[...]
    q = p[:, 0:GW]
    k = p[:, GW:2 * GW]
    gm = gm_ref[...]
    qn = q * lax.rsqrt(_group_sum(q * q, gm) * (1.0 / 64) + EPS) * qg_ref[...]
    kn = k * lax.rsqrt(_group_sum(k * k, gm) * (1.0 / 64) + EPS) * kg_ref[...]
    q_ref[...] = (qn * (64 ** -0.5)).astype(BF16)
    k_ref[...] = kn.astype(BF16)
    v_ref[...] = p[:, 2 * GW:3 * GW].astype(BF16)


def _naprep(p_na, qg, kg, gmb):
    tok = lambda w: pl.BlockSpec((TM, w), lambda i: (i, 0))
    full = lambda shape: pl.BlockSpec(shape, lambda i: (0,) * len(shape))
    return pl.pallas_call(
        _naprep_kernel,
        grid=(N_TILES,),
        in_specs=[tok(768), full((1, GW)), full((1, GW)), full((GW, GW))],
        out_specs=[tok(GW), tok(GW), tok(GW)],
        out_shape=[jax.ShapeDtypeStruct((T_ALL, GW), BF16)] * 3,
        compiler_params=_cparams(("parallel",)),
        name="naprep",
    )(p_na, qg, kg, gmb)


NA_RPS = 4


def _na_kernel(q_ref, k_ref, v_ref, kc_ref, vc_ref, bias_ref, o_ref):
    rows = L // GRID_W
    kc = kc_ref[...]
    vc = vc_ref[...]
    for j in range(NA_RPS):
        r = pl.program_id(1) * NA_RPS + j
        rs = jnp.clip(r - NA_ROWS // 2, 0, rows - NA_ROWS)
        variant = r - rs
        start = pl.multiple_of(rs * GRID_W, GRID_W)
        kw = k_ref[pl.ds(start, NA_ROWS * GRID_W), :]
        vw = v_ref[pl.ds(start, NA_ROWS * GRID_W), :]
        q = q_ref[j * GRID_W:(j + 1) * GRID_W, :]
        lane_head = lax.broadcasted_iota(jnp.int32, (GRID_W, GW), 1) // 64
        hmask = [lane_head == h for h in range(NH)]
        qx = jnp.concatenate([jnp.where(hmask[h], q, jnp.zeros_like(q)) for h in range(NH)], axis=0)
        s_loc = _dot_nt(qx, kw) + bias_ref[variant]
        s_ctx = _dot_nt(qx, kc)
        m = jnp.maximum(jnp.max(s_loc, axis=-1, keepdims=True), jnp.max(s_ctx, axis=-1, keepdims=True))
        p_loc = jnp.exp(s_loc - m)
        p_ctx = jnp.exp(s_ctx - m)
        den = jnp.sum(p_loc, axis=-1, keepdims=True) + jnp.sum(p_ctx, axis=-1, keepdims=True)
        o_all = (_dot(p_loc.astype(BF16), vw) + _dot(p_ctx.astype(BF16), vc)) / den
        o = jnp.zeros((GRID_W, GW), F32)
        for h in range(NH):
            o = o + jnp.where(hmask[h], o_all[h * GRID_W:(h + 1) * GRID_W, :], 0.0)
        o_ref[j * GRID_W:(j + 1) * GRID_W, :] = o


def _na(qn, kn, vn, bias_t):
    steps = L // GRID_W // NA_RPS
    tq = NA_RPS * GRID_W
    ctx_blk = T_LAT // CTX
    return pl.pallas_call(
        _na_kernel,
        grid=(B, steps),
        in_specs=[
            pl.BlockSpec((tq, GW), lambda b, r: (b * steps + r, 0)),
            pl.BlockSpec((L, GW), lambda b, r: (b, 0)),
            pl.BlockSpec((L, GW), lambda b, r: (b, 0)),
            pl.BlockSpec((CTX, GW), lambda b, r: (ctx_blk + b, 0)),
            pl.BlockSpec((CTX, GW), lambda b, r: (ctx_blk + b, 0)),
            pl.BlockSpec((NA_ROWS, NH * GRID_W, NA_ROWS * GRID_W), lambda b, r: (0, 0, 0)),
        ],
        out_specs=pl.BlockSpec((tq, GW), lambda b, r: (b * steps + r, 0)),
        out_shape=jax.ShapeDtypeStruct((T_LAT, GW), F32),
        compiler_params=_cparams(("arbitrary", "arbitrary")),
        name="na",
    )(qn, kn, vn, kn, vn, bias_t)


def _na_bias_table(rpb):
    cq = jnp.arange(GRID_W)
    cs = jnp.clip(cq - NA_COLS // 2, 0, GRID_W - NA_COLS)
    col_ok = (cq[None, :] >= cs[:, None]) & (cq[None, :] < cs[:, None] + NA_COLS)
    dc = jnp.clip(cq[None, :] - cq[:, None] + (NA_COLS - 1), 0, 2 * NA_COLS - 2)
    onehot = (dc[:, :, None] == jnp.arange(2 * NA_COLS - 1)[None, None, :]).astype(F32)
    full = jnp.einsum('qkc,hrc->hrqk', onehot, rpb.astype(F32), precision=HI)
    full = jnp.where(col_ok[None, None], full, NEG)
    tab = jnp.stack([full[:, NA_ROWS - 1 - a:2 * NA_ROWS - 1 - a] for a in range(NA_ROWS)], axis=0)
    return tab.transpose(0, 1, 3, 2, 4).reshape(NA_ROWS, NH * GRID_W, NA_ROWS * GRID_W)


def _attn_kernel(*refs, nkv, dq, dv):
    q = refs[0][...]
    ks = [refs[1 + 2 * j][...] for j in range(nkv)]
    vs = [refs[2 + 2 * j][...] for j in range(nkv)]
    o_ref = refs[1 + 2 * nkv]
    outs = []
    for h in range(NH):
        qh = q[:, h * dq:(h + 1) * dq]
        ss = [_dot_nt(qh, k[:, h * dq:(h + 1) * dq]) for k in ks]
        m = functools.reduce(jnp.maximum, [jnp.max(s, axis=-1, keepdims=True) for s in ss])
        ps = [jnp.exp(s - m) for s in ss]
        den = functools.reduce(lambda a, b2: a + b2, [jnp.sum(p, axis=-1, keepdims=True) for p in ps])
        o = functools.reduce(lambda a, b2: a + b2,
                             [_dot(p.astype(BF16), v[:, h * dv:(h + 1) * dv]) for p, v in zip(ps, vs)])
        outs.append(o / den)
    o_ref[...] = jnp.concatenate(outs, axis=-1)


def _attn_latent(q, k, v, dq, dv, tq):
    nq = L // tq
    ctx_blk = T_LAT // CTX
    return pl.pallas_call(
        functools.partial(_attn_kernel, nkv=2, dq=dq, dv=dv),
        grid=(B, nq),
        in_specs=[
            pl.BlockSpec((tq, NH * dq), lambda b, i: (b * nq + i, 0)),
            pl.BlockSpec((L, NH * dq), lambda b, i: (b, 0)),
            pl.BlockSpec((L, NH * dv), lambda b, i: (b, 0)),
            pl.BlockSpec((CTX, NH * dq), lambda b, i: (ctx_blk + b, 0)),
            pl.BlockSpec((CTX, NH * dv), lambda b, i: (ctx_blk + b, 0)),
        ],
        out_specs=pl.BlockSpec((tq, NH * dv), lambda b, i: (b * nq + i, 0)),
        out_shape=jax.ShapeDtypeStruct((T_LAT, NH * dv), F32),
        compiler_params=_cparams(("arbitrary", "arbitrary")),
        name="attn_latent",
    )(q, k, v, k, v)


def _attn_ctx(q, k, v, dq, dv):
    ctx_blk = T_LAT // CTX
    return pl.pallas_call(
        functools.partial(_attn_kernel, nkv=1, dq=dq, dv=dv),
        grid=(B,),
        in_specs=[
            pl.BlockSpec((CTX, NH * dq), lambda b: (ctx_blk + b, 0)),
            pl.BlockSpec((CTX, NH * dq), lambda b: (ctx_blk + b, 0)),
            pl.BlockSpec((CTX, NH * dv), lambda b: (ctx_blk + b, 0)),
        ],
        out_specs=pl.BlockSpec((CTX, NH * dv), lambda b: (b, 0)),
        out_shape=jax.ShapeDtypeStruct((T_CTX, NH * dv), F32),
        compiler_params=_cparams(("arbitrary",)),
        name="attn_ctx",
    )(q, k, v)


def _mlaprep_kernel(p_ref, qag_ref, kvag_ref, wq_ref, wk_ref, wv_ref, qg_ref, kg_ref, gm_ref, pm_ref,
                    cos_ref, sin_ref, q_ref, k_ref, v_ref):
    p = p_ref[...]
    cq = p[:, 0:MLA_Q_RANK]
    ckv = p[:, MLA_Q_RANK:MLA_Q_RANK + MLA_KV_RANK]
    krp = p[:, MLA_Q_RANK + MLA_KV_RANK:]
    cqn = cq * lax.rsqrt(jnp.mean(cq * cq, axis=-1, keepdims=True) + EPS) * qag_ref[...]
    ckvn = ckv * lax.rsqrt(jnp.mean(ckv * ckv, axis=-1, keepdims=True) + EPS) * kvag_ref[...]
    ckvb = ckvn.astype(BF16)
    q = _dot(cqn.astype(BF16), wq_ref[...])
    k = _dot(jnp.concatenate([ckvb, krp.astype(BF16)], axis=-1), wk_ref[...])
    v = _dot(ckvb, wv_ref[...])
    gm = gm_ref[...]
    q = q * lax.rsqrt(_group_sum(q * q, gm) * (1.0 / MLA_QK) + EPS) * qg_ref[...]
    k = k * lax.rsqrt(_group_sum(k * k, gm) * (1.0 / MLA_QK) + EPS) * kg_ref[...]
    cos = cos_ref[...]
    sin = sin_ref[...]
    pm = pm_ref[...]
    q = q * cos + _dot(q.astype(BF16), pm) * sin
    k = k * cos + _dot(k.astype(BF16), pm) * sin
    q_ref[...] = (q * (MLA_QK ** -0.5)).astype(BF16)
    k_ref[...] = k.astype(BF16)
    v_ref[...] = v.astype(BF16)


def _mlaprep(p_mla, qag, kvag, wq, wk, wv, qg, kg, gm, pm, cos_t, sin_t):
    tok = lambda w: pl.BlockSpec((TM, w), lambda i: (i, 0))
    full = lambda shape: pl.BlockSpec(shape, lambda i: (0,) * len(shape))
    pos = pl.BlockSpec((TM, 512), lambda i: (jnp.where(i < LAT_TILES, i % TILES_PER_SEQ, TILES_PER_SEQ), 0))
    return pl.pallas_call(
        _mlaprep_kernel,
        grid=(N_TILES,),
        in_specs=[tok(512), full((1, 256)), full((1, 128)), full((256, 512)), full((256, 512)),
                  full((128, 256)), full((1, 512)), full((1, 512)), full((512, 512)), full((512, 512)),
                  pos, pos],
        out_specs=[tok(512), tok(512), tok(GW)],
        out_shape=[jax.ShapeDtypeStruct((T_ALL, 512), BF16), jax.ShapeDtypeStruct((T_ALL, 512), BF16),
                   jax.ShapeDtypeStruct((T_ALL, GW), BF16)],
        compiler_params=_cparams(("parallel",)),
        name="mlaprep",
    )(p_mla, qag, kvag, wq, wk, wv, qg, kg, gm, pm, cos_t, sin_t)


def _rope_tables():
    t = jnp.arange(L)
    rowp = (t // GRID_W).astype(F32)
    colp = (t % GRID_W).astype(F32)
    half = MLA_ROPE // 2
    inv = ROPE_BASE ** (-jnp.arange(0, half, 2, dtype=F32) / half)
    j = jnp.arange(MLA_ROPE)
    pos = jnp.where(j[None, :] < half, rowp[:, None], colp[:, None])
    ang = pos * inv[j % (half // 2)][None, :]
    first = (j % half) < (half // 2)
    cos32 = jnp.cos(ang)
    sin32 = jnp.where(first[None, :], -jnp.sin(ang), jnp.sin(ang))
    cos_h = jnp.concatenate([jnp.ones((L, MLA_NOPE), F32), cos32, jnp.ones((L, 32), F32)], axis=-1)
    sin_h = jnp.concatenate([jnp.zeros((L, MLA_NOPE), F32), sin32, jnp.zeros((L, 32), F32)], axis=-1)
    cos_t = jnp.concatenate([jnp.tile(cos_h, (1, NH)), jnp.ones((TM, 512), F32)], axis=0)
    sin_t = jnp.concatenate([jnp.tile(sin_h, (1, NH)), jnp.zeros((TM, 512), F32)], axis=0)
    lane = jnp.arange(512)
    jj = lane % 128 - MLA_NOPE
    is_rope = (jj >= 0) & (jj < MLA_ROPE)
    partner = jnp.where(is_rope, jnp.where((jj % half) < (half // 2), lane + half // 2, lane - half // 2), lane)
    pm = (lane[:, None] == partner[None, :]).astype(BF16)
    return cos_t, sin_t, pm


def _outproj_kernel(x_ref, of_ref, ob_ref, g_ref, hyl_ref, hyc_ref, nal_ref, nac_ref, mll_ref, mlc_ref,
                    mod_ref, ng_ref, gm_ref, w_ref, n2_ref, wr_ref, x1_ref, h2_ref, lg_ref):
    oa = of_ref[...] + ob_ref[...]
    ms = _group_sum(oa * oa, gm_ref[...]) * (1.0 / 64)
    g = g_ref[...]
    oa = oa * lax.rsqrt(ms + EPS) * ng_ref[...] * (g * jax.nn.sigmoid(g))
    lat = pl.program_id(0) < LAT_TILES
    hy = jnp.where(lat, hyl_ref[...], hyc_ref[...])
    na = jnp.where(lat, nal_ref[...], nac_ref[...])
    mla = jnp.where(lat, mll_ref[...], mlc_ref[...])
    mix = jnp.concatenate([oa, hy, na, mla], axis=-1).astype(BF16)
    x1 = x_ref[...] + mod_ref[2:3, :] * _dot(mix, w_ref[...])
    x1_ref[...] = x1
    ms2 = jnp.mean(x1 * x1, axis=-1, keepdims=True)
    h2 = x1 * lax.rsqrt(ms2 + EPS) * n2_ref[...] * (1.0 + mod_ref[4:5, :]) + mod_ref[3:4, :]
    _store_rowtiles(h2_ref, h2)
    hh, hl = _split2(h2)
    wr = wr_ref[...]
    lg_ref[...] = _dot(hh, wr[:, 0:128]) + _dot(hl, wr[:, 0:128]) + _dot(hh, wr[:, 128:256])


def _outproj(n_tiles, X, o_f, o_b, p_hg, hy, na, mla, mod_l, ng, gmb, w_bf, n2g, wr):
    tok = lambda w: pl.BlockSpec((TM, w), lambda i: (i, 0))
    full = lambda shape: pl.BlockSpec(shape, lambda i: (0,) * len(shape))
    latb = pl.BlockSpec((TM, GW), lambda i: (jnp.minimum(i, LAT_TILES - 1), 0))
    ctxb = pl.BlockSpec((TM, GW), lambda i: (jnp.maximum(i - LAT_TILES, 0), 0))
    nt = n_tiles * TM
    return pl.pallas_call(
        _outproj_kernel,
        grid=(n_tiles,),
        in_specs=[tok(D), tok(GW), tok(GW), pl.BlockSpec((TM, GW), lambda i: (i, 4)),
                  latb, ctxb, latb, ctxb, latb, ctxb,
                  pl.BlockSpec((None, 6, D), lambda i: (_seg_of_tile(i), 0, 0)),
                  full((1, GW)), full((GW, GW)), full((D, D)), full((1, D)), full((D, 256))],
        out_specs=[tok(D), pl.BlockSpec((TM * ROW_CH, LANE), lambda i: (i, 0)), tok(128)],
        out_shape=[jax.ShapeDtypeStruct((nt, D), F32), jax.ShapeDtypeStruct((nt * ROW_CH, LANE), F32),
                   jax.ShapeDtypeStruct((nt, 128), F32)],
        compiler_params=_cparams(("parallel",)),
        name="outproj",
    )(X, o_f, o_b, p_hg, hy[0], hy[1], na[0], na[1], mla[0], mla[1], mod_l, ng, gmb, w_bf, n2g, wr)


def _route(logits, bg, be):
    t = logits.shape[0]
    lg = logits[:, 0:N_GROUPS] + bg
    le = (logits[:, N_GROUPS:N_GROUPS + N_EXPERTS] + be).reshape(t, N_GROUPS, EPG)
    pg = jax.nn.softmax(lg, axis=-1)
    g_sel = jnp.argmax(lg, axis=-1).astype(jnp.int32)
    p_sel = jnp.take_along_axis(pg, g_sel[:, None], axis=-1)[:, 0]
    le_sel = jnp.take_along_axis(le, g_sel[:, None, None], axis=1)[:, 0]
    top_v, top_i = lax.top_k(le_sel, 2)
    wts = jax.nn.softmax(top_v, axis=-1) * p_sel[:, None]
    eid = g_sel[:, None] * EPG + top_i.astype(jnp.int32)
    return eid, wts


def _dispatch_tables(eid):
    t = eid.shape[0]
    n = 2 * t
    nb = n // TMOE + N_EXPERTS
    flat_e = eid.reshape(n)
    onehot = (flat_e[:, None] == jnp.arange(N_EXPERTS, dtype=jnp.int32)[None, :]).astype(jnp.int32)
    csum = jnp.cumsum(onehot, axis=0)
    pos = jnp.sum(csum * onehot, axis=1) - 1
    counts = csum[-1]
    pcounts = ((counts + TMOE - 1) // TMOE) * TMOE
    pends = jnp.cumsum(pcounts)
    pstarts = pends - pcounts
    dest = (jnp.sum(jnp.where(onehot > 0, pstarts[None, :], 0), axis=1) + pos).astype(jnp.int32)
    blk_start = jnp.arange(nb, dtype=jnp.int32) * TMOE
    block_e = jnp.minimum(jnp.sum((pends[None, :] <= blk_start[:, None]).astype(jnp.int32), axis=1),
                          N_EXPERTS - 1).astype(jnp.int32)
    nblk = (pends[-1] // TMOE).astype(jnp.int32).reshape(1)
    pads = jnp.concatenate([pstarts + counts, pcounts - counts, nblk]).astype(jnp.int32)
    return block_e, nblk, dest, pads


PAD_PIECES = tuple(1 << b for b in range(TMOE.bit_length() - 1))


def _rows(tok, n):
    if isinstance(tok, int):
        return pl.ds(tok * ROW_CH, n * ROW_CH)
    return pl.ds(pl.multiple_of(tok * ROW_CH, ROW_CH), n * ROW_CH)


def _dispatch_kernel(dest_ref, pad_ref, h_ref, xs_out, zbuf, sem, zsem):
    base = pl.program_id(0) * (2 * TM)

    @pl.when(pl.program_id(0) == 0)
    def _():
        zbuf[...] = jnp.zeros_like(zbuf)
        ztok = zbuf.shape[0] // ROW_CH
        for phase in range(2):
            for e in range(N_EXPERTS):
                off = pad_ref[e]
                npad = pad_ref[N_EXPERTS + e]
                for piece in PAD_PIECES:
                    has = (npad & piece) != 0

                    @pl.when(has)
                    def _(off=off, piece=piece):
                        cp = pltpu.make_async_copy(zbuf.at[pl.ds(0, piece * ROW_CH)], xs_out.at[_rows(off, piece)], zsem)
                        cp.start() if phase == 0 else cp.wait()

                    off = off + jnp.where(has, piece, 0)

            first = pad_ref[2 * N_EXPERTS] * (TMOE // ztok)

            def tail(j, carry):
                cp = pltpu.make_async_copy(zbuf, xs_out.at[_rows(j * ztok, ztok)], zsem)
                cp.start() if phase == 0 else cp.wait()
                return carry

            lax.fori_loop(first, xs_out.shape[0] // zbuf.shape[0], tail, 0)

    def issue(r, carry):
        for k in range(2):
            pltpu.make_async_copy(h_ref.at[_rows(r, 1)], xs_out.at[_rows(dest_ref[base + 2 * r + k], 1)], sem).start()
        return carry

    lax.fori_loop(0, TM, issue, 0, unroll=8)
    pltpu.make_async_copy(xs_out.at[_rows(0, 2 * TM)], xs_out.at[_rows(0, 2 * TM)], sem).wait()


def _dispatch(n_tiles, n_slots, dest, pads, h2):
    grid_spec = pltpu.PrefetchScalarGridSpec(
        num_scalar_prefetch=2,
        grid=(n_tiles,),
        in_specs=[pl.BlockSpec((TM * ROW_CH, LANE), lambda i, dst, pd: (i, 0))],
        out_specs=pl.BlockSpec(memory_space=pl.ANY),
        scratch_shapes=[pltpu.VMEM((TMOE // 2 * ROW_CH, LANE), F32), pltpu.SemaphoreType.DMA(()),
                        pltpu.SemaphoreType.DMA(())],
    )
    return pl.pallas_call(
        _dispatch_kernel,
        grid_spec=grid_spec,
        out_shape=jax.ShapeDtypeStruct((n_slots * ROW_CH, LANE), F32),
        compiler_params=_cparams(("arbitrary",)),
        name="dispatch",
    )(dest, pads, h2)


def _experts_kernel(be_ref, nblk_ref, xs_ref, wg_ref, wu_ref, wd_ref, ys_ref, wgb, wub, wdb):
    i = pl.program_id(0)

    @pl.when((i == 0) | (be_ref[i] != be_ref[jnp.maximum(i - 1, 0)]))
    def _():
        wgb[...] = wg_ref[...].astype(BF16)
        wub[...] = wu_ref[...].astype(BF16)
        wdb[...] = wd_ref[...].astype(BF16)

    @pl.when(i < nblk_ref[0])
    def _():
        x = _load_rowtiles(xs_ref, TMOE).astype(BF16)
        gate = _dot(x, wgb[...])
        up = _dot(x, wub[...])
        act = (gate * jax.nn.sigmoid(gate)) * up
        _store_rowtiles(ys_ref, _dot(act.astype(BF16), wdb[...]))

    @pl.when(i >= nblk_ref[0])
    def _():
        ys_ref[...] = jnp.zeros_like(ys_ref)


def _experts(layer, block_e, nblk, xs, w_gate, w_up, w_down):
    nb = block_e.shape[0]
    used = lambda i, nk: jnp.minimum(i, nk[0] - 1)
    grid_spec = pltpu.PrefetchScalarGridSpec(
        num_scalar_prefetch=2,
        grid=(nb,),
        in_specs=[
            pl.BlockSpec((TMOE * ROW_CH, LANE), lambda i, be, nk: (used(i, nk), 0)),
            pl.BlockSpec((None, None, D, D_EXPERT), lambda i, be, nk: (layer, be[i], 0, 0)),
            pl.BlockSpec((None, None, D, D_EXPERT), lambda i, be, nk: (layer, be[i], 0, 0)),
            pl.BlockSpec((None, None, D_EXPERT, D), lambda i, be, nk: (layer, be[i], 0, 0)),
        ],
        out_specs=pl.BlockSpec((TMOE * ROW_CH, LANE), lambda i, be, nk: (i, 0)),
        scratch_shapes=[pltpu.VMEM((D, D_EXPERT), BF16), pltpu.VMEM((D, D_EXPERT), BF16),
                        pltpu.VMEM((D_EXPERT, D), BF16)],
    )
    return pl.pallas_call(
        _experts_kernel,
        grid_spec=grid_spec,
        out_shape=jax.ShapeDtypeStruct(xs.shape, F32),
        compiler_params=_cparams(("arbitrary",)),
        name="experts",
    )(block_e, nblk, xs, w_gate, w_up, w_down)


def _combine_kernel(dest_ref, x_ref, w_ref, mod_ref, ys_hbm, o_ref, ybuf, sem):
    i = pl.program_id(0)

    def fetch(tile, slot):
        base = tile * (2 * TM)

        def issue(r, carry):
            for k in range(2):
                pltpu.make_async_copy(ys_hbm.at[_rows(dest_ref[base + 2 * r + k], 1)],
                                      ybuf.at[slot, k, _rows(r, 1)], sem.at[slot]).start()
            return carry

        lax.fori_loop(0, TM, issue, 0, unroll=8)

    @pl.when(i == 0)
    def _():
        fetch(0, 0)

    @pl.when(i + 1 < pl.num_programs(0))
    def _():
        fetch(i + 1, (i + 1) % 2)

    slot = i % 2
    for k in range(2):
        pltpu.make_async_copy(ys_hbm.at[_rows(0, TM)], ybuf.at[slot, k], sem.at[slot]).wait()
    w = w_ref[...]
    y = w[:, 0:1] * _load_rowtiles(ybuf.at[slot, 0], TM) + w[:, 1:2] * _load_rowtiles(ybuf.at[slot, 1], TM)
    o_ref[...] = x_ref[...] + mod_ref[5:6, :] * y


def _combine(n_tiles, dest, X1, wts, mod_l, ys):
    grid_spec = pltpu.PrefetchScalarGridSpec(
        num_scalar_prefetch=1,
        grid=(n_tiles,),
        in_specs=[pl.BlockSpec((TM, D), lambda i, dst: (i, 0)),
                  pl.BlockSpec((TM, 2), lambda i, dst: (i, 0)),
                  pl.BlockSpec((None, 6, D), lambda i, dst: (_seg_of_tile(i), 0, 0)),
                  pl.BlockSpec(memory_space=pl.ANY)],
        out_specs=pl.BlockSpec((TM, D), lambda i, dst: (i, 0)),
        scratch_shapes=[pltpu.VMEM((2, 2, TM * ROW_CH, LANE), F32), pltpu.SemaphoreType.DMA((2,))],
    )
    return pl.pallas_call(
        _combine_kernel,
        grid_spec=grid_spec,
        out_shape=jax.ShapeDtypeStruct((n_tiles * TM, D), F32),
        compiler_params=_cparams(("arbitrary",)),
        name="combine",
    )(dest, X1, wts, mod_l, ys)


def _group_mask(width, group):
    lane = jnp.arange(width)
    return (lane[:, None] // group == lane[None, :] // group)


def _hgrn_tri(reverse):
    t = jnp.arange(HCHUNK)
    same = (t[:, None] // HSUB) == (t[None, :] // HSUB)
    order = (t[None, :] >= t[:, None]) if reverse else (t[None, :] <= t[:, None])
    return jnp.stack([same & order, order]).astype(BF16)


def _mla_weights(w_uq, w_ukv, q_g, k_g):
    wq = jnp.pad(w_uq.reshape(MLA_Q_RANK, NH, MLA_QK), ((0, 0), (0, 0), (0, 128 - MLA_QK))).reshape(MLA_Q_RANK, 512)
    kv = w_ukv.reshape(MLA_KV_RANK, NH, MLA_NOPE + 64)
    wk_top = jnp.pad(kv[:, :, :MLA_NOPE], ((0, 0), (0, 0), (0, 128 - MLA_NOPE))).reshape(MLA_KV_RANK, 512)
    lane = jnp.arange(512)
    src = jnp.arange(128)
    place = ((lane[None, :] % 128) == (src[:, None] + MLA_NOPE)) & (src[:, None] < MLA_ROPE)
    wk = jnp.concatenate([wk_top, place.astype(F32)], axis=0)
    wv = kv[:, :, MLA_NOPE:].reshape(MLA_KV_RANK, GW)
    pad_g = lambda g: jnp.tile(jnp.pad(g, (0, 128 - MLA_QK)), NH).reshape(1, 512)
    return wq.astype(BF16), wk.astype(BF16), wv.astype(BF16), pad_g(q_g), pad_g(k_g)


def kernel(x, c, ctx, c_ctx, w_ada, b_ada, norm1_g, norm2_g, w_in, w_out, hgrn_lb_logits, hgrn_norm_g,
           hy_short_w, hy_short_b, hy_w1, hy_b1, hy_freq, hy_w2, hy_b2, hy_w3, hy_b3, hy_decay, hy_bias,
           na_rpb, na_q_g, na_k_g, mla_q_a_g, mla_kv_a_g, mla_w_uq, mla_w_ukv, mla_q_g, mla_k_g,
           moe_wg, moe_bg, moe_we, moe_be, moe_w_gate, moe_w_up, moe_w_down):
    X = jnp.concatenate([x.reshape(T_LAT, D), ctx.reshape(T_CTX, D)], axis=0)
    cmat = jnp.concatenate([c, c_ctx[None, :], jnp.zeros((16 - B - 1, D), F32)], axis=0)
    mod = _adaln(cmat, w_ada, b_ada).reshape(DEPTH, 16, 6, D)

    lb_cum = jnp.cumsum(jax.nn.softmax(hgrn_lb_logits.astype(F32), axis=0), axis=0)
    lower = lb_cum - lb_cum[0:1]

    gm64 = _group_mask(GW, 64)
    gm64_f = gm64.astype(F32)
    gm64_b = gm64.astype(BF16)
    gm128_b = _group_mask(512, 128).astype(BF16)
    trif = _hgrn_tri(False)
    trib = _hgrn_tri(True)
    cos_t, sin_t, pm = _rope_tables()
    dft = {}
    for n in (L, CTX):
        cm, sm = _dft_consts(n)
        chi, clo = _split2(cm)
        shi, slo = _split2(sm)
        dft[n] = (chi, clo, shi, slo, _hyena_feats(n))

    for l in range(DEPTH):
        mod_l = mod[l]
        p_hg, p_hy, p_na, p_mla = _inproj(X, mod_l, norm1_g[l].reshape(1, D),
                                          jnp.pad(w_in[l], ((0, 0), (0, D_IN_PAD - D_IN))).astype(BF16))

        lb = lower[l]
        hconst = jnp.concatenate([
            jnp.stack([jnp.maximum(jnp.log(lb[d]), NEG), jnp.log1p(-lb[d]), 1.0 - lb[d]]) for d in range(2)
        ] + [jnp.zeros((2, GW), F32)], axis=0)
        o_f, o_b = _hgrn(p_hg, hconst, trif, trib, gm64_f, gm64_b)

        need_ctx = l < DEPTH - 1
        w1p = jnp.pad(hy_w1[l], ((0, 128 - HYENA_EMB), (0, 0)))
        o_hy = []
        for n, blk0 in ((L, 0), (CTX, T_LAT // CTX)):
            if n == CTX and not need_ctx:
                continue
            chi, clo, shi, slo, feats = dft[n]
            e, o, knq = _hyfilt(feats, w1p, hy_b1[l].reshape(1, -1), hy_freq[l].reshape(1, -1), hy_w2[l],
                                hy_b2[l].reshape(1, -1), hy_w3[l], hy_b3[l].reshape(1, -1),
                                hy_decay[l].reshape(1, 4 * GW))
            kre, kim = _hyspec(chi, clo, shi, slo, e, o)
            o_hy.append(_hyena(p_hy, blk0, B, n, hy_short_w[l], hy_short_b[l].reshape(1, -1), hy_bias[l],
                               chi, shi, kre, kim, knq))

        qn, kn, vn = _naprep(p_na, jnp.tile(na_q_g[l], NH).reshape(1, GW), jnp.tile(na_k_g[l], NH).reshape(1, GW),
                             gm64_b)
        o_na = [_na(qn, kn, vn, _na_bias_table(na_rpb[l]))]

        wq, wk, wv, qg, kg = _mla_weights(mla_w_uq[l], mla_w_ukv[l], mla_q_g[l], mla_k_g[l])
        mq, mk, mv = _mlaprep(p_mla, mla_q_a_g[l].reshape(1, -1), mla_kv_a_g[l].reshape(1, -1), wq, wk, wv,
                              qg, kg, gm128_b, pm, cos_t, sin_t)
        o_mla = [_attn_latent(mq, mk, mv, 128, 64, 256)]
        if need_ctx:
            o_na.append(_attn_ctx(qn, kn, vn, 64, 64))
            o_mla.append(_attn_ctx(mq, mk, mv, 128, 64))
        else:
            o_hy.append(o_hy[0])
            o_na.append(o_na[0])
            o_mla.append(o_mla[0])
        n_tiles = N_TILES if need_ctx else LAT_TILES

        wr = jnp.pad(jnp.concatenate([moe_wg[l], moe_we[l]], axis=1), ((0, 0), (0, 128 - N_GROUPS - N_EXPERTS)))
        wr_hi, wr_lo = _split2(wr)
        X1, h2, logits = _outproj(n_tiles, X, o_f, o_b, p_hg, o_hy, o_na, o_mla, mod_l,
                                  jnp.tile(hgrn_norm_g[l], NH).reshape(1, GW), gm64_b,
                                  w_out[l].astype(BF16), norm2_g[l].reshape(1, D),
                                  jnp.concatenate([wr_hi, wr_lo], axis=1))

        eid, wts = _route(logits, moe_bg[l], moe_be[l])
        block_e, nblk, dest, pads = _dispatch_tables(eid)
        xs = _dispatch(n_tiles, block_e.shape[0] * TMOE, dest, pads, h2)
        ys = _experts(l, block_e, nblk, xs, moe_w_gate, moe_w_up, moe_w_down)
        X = _combine(n_tiles, dest, X1, wts, mod_l, ys)

    return X.reshape(B, L, D)
```

```python
import functools
import math

import jax
import jax.numpy as jnp
from jax import lax
from jax.experimental import pallas as pl
from jax.experimental.pallas import tpu as pltpu

F32 = jnp.float32
BF16 = jnp.bfloat16

D = 1024
B = 8
L = 2048
CTX = 256
DEPTH = 4
GRID_W = 64
EPS = 1e-6
GW = 256
NH = 4
HYENA_BANDS = 16
HYENA_EMB = 1 + 2 * HYENA_BANDS
HYENA_FFN = 64
NA_ROWS = 8
NA_COLS = 16
MLA_Q_RANK = 256
MLA_KV_RANK = 128
MLA_NOPE = 64
MLA_ROPE = 32
MLA_QK = MLA_NOPE + MLA_ROPE
ROPE_BASE = 10000.0
N_GROUPS = 4
EPG = 8
N_EXPERTS = N_GROUPS * EPG
D_EXPERT = 512
D_IN = 3232
D_IN_PAD = 3328

T_LAT = B * L
T_CTX = B * CTX
T_ALL = T_LAT + T_CTX

TM = 256
N_TILES = T_ALL // TM
LAT_TILES = T_LAT // TM
TILES_PER_SEQ = L // TM
HCHUNK = 64
HSUB = 16
HGRN_SAFE_DECAY = 80.0
TMOE = 256
VMEM_LIMIT_BYTES = 56 * 1024 * 1024
NEG = -1e30

HI = lax.Precision.HIGHEST


def _cparams(sem, vmem=VMEM_LIMIT_BYTES):
    return pltpu.CompilerParams(dimension_semantics=sem, vmem_limit_bytes=vmem)


def _seg_of_tile(i):
    return jnp.where(i < LAT_TILES, i // TILES_PER_SEQ, B)


def _dot(a, b):
    return jnp.dot(a, b, preferred_element_type=F32)


def _dot_nt(a, b):
    return lax.dot_general(a, b, (((1,), (1,)), ((), ())), preferred_element_type=F32)


def _dot_tn(a, b):
    return lax.dot_general(a, b, (((0,), (0,)), ((), ())), preferred_element_type=F32)


def _split2(x):
    hi = x.astype(BF16)
    lo = (x - hi.astype(F32)).astype(BF16)
    return hi, lo


def _split3(x):
    h1 = x.astype(BF16)
    r1 = x - h1.astype(F32)
    h2 = r1.astype(BF16)
    h3 = (r1 - h2.astype(F32)).astype(BF16)
    return h1, h2, h3


LANE = 128
ROW_CH = D // LANE


def _store_rowtiles(ref, val):
    n = val.shape[0]
    for j in range(ROW_CH):
        ref[pl.ds(j, n, stride=ROW_CH), :] = val[:, j * LANE:(j + 1) * LANE]


def _load_rowtiles(ref, n):
    return jnp.concatenate([ref[pl.ds(j, n, stride=ROW_CH), :] for j in range(ROW_CH)], axis=-1)


def _group_sum(x, gm):
    hi, lo = _split2(x)
    return _dot(hi, gm) + _dot(lo, gm)


def _ada_kernel(c_ref, w_ref, b_ref, o_ref):
    cc = c_ref[...]
    sc = cc * jax.nn.sigmoid(cc)
    o_ref[0] = jnp.dot(sc, w_ref[0], preferred_element_type=F32, precision=HI) + b_ref[0]


def _adaln(cmat, w_ada, b_ada):
    tn = 1536
    return pl.pallas_call(
        _ada_kernel,
        grid=(DEPTH, 6 * D // tn),
        in_specs=[
            pl.BlockSpec((16, D), lambda l, j: (0, 0)),
            pl.BlockSpec((1, D, tn), lambda l, j: (l, 0, j)),
            pl.BlockSpec((1, 1, tn), lambda l, j: (l, 0, j)),
        ],
        out_specs=pl.BlockSpec((1, 16, tn), lambda l, j: (l, 0, j)),
        out_shape=jax.ShapeDtypeStruct((DEPTH, 16, 6 * D), F32),
        compiler_params=_cparams(("arbitrary", "arbitrary")),
        name="adaln",
    )(cmat, w_ada, b_ada.reshape(DEPTH, 1, 6 * D))


IN_WIDTHS = (1280, 768, 768, 512)


def _inproj_tile(x, mod_ref, g_ref, w_ref, outs):
    ms = jnp.mean(x * x, axis=-1, keepdims=True)
    y = x * lax.rsqrt(ms + EPS) * g_ref[...]
    h = y * (1.0 + mod_ref[1:2, :]) + mod_ref[0:1, :]
    p = _dot(h.astype(BF16), w_ref[...])
    c0 = 0
    for o_ref, w in zip(outs, IN_WIDTHS):
        o_ref[...] = p[:, c0:c0 + w]
        c0 += w


def _inproj_kernel(x_ref, mod_ref, g_ref, w_ref, o_hg, o_hy, o_na, o_mla):
    _inproj_tile(x_ref[...], mod_ref, g_ref, w_ref, (o_hg, o_hy, o_na, o_mla))


def _inproj_out_specs(index_map):
    specs = [pl.BlockSpec((TM, w), index_map) for w in IN_WIDTHS]
    shapes = [jax.ShapeDtypeStruct((T_ALL, w), F32) for w in IN_WIDTHS]
    return specs, shapes


def _inproj(X, mod_l, g, w_bf):
    out_specs, out_shape = _inproj_out_specs(lambda i: (i, 0))
    return pl.pallas_call(
        _inproj_kernel,
        grid=(N_TILES,),
        in_specs=[
            pl.BlockSpec((TM, D), lambda i: (i, 0)),
            pl.BlockSpec((None, 6, D), lambda i: (_seg_of_tile(i), 0, 0)),
            pl.BlockSpec((1, D), lambda i: (0, 0)),
            pl.BlockSpec((D, D_IN_PAD), lambda i: (0, 0)),
        ],
        out_specs=out_specs,
        out_shape=out_shape,
        compiler_params=_cparams(("parallel",)),
        name="inproj",
    )(X, mod_l, g, w_bf)


HSTEP = 2 * HCHUNK


def _hgrn_prologue(p_ref, r0, zcol, c_ref, crow, tri_ref, reverse):
    q = p_ref[r0:r0 + HCHUNK, 0:GW]
    z = p_ref[r0:r0 + HCHUNK, zcol:zcol + GW]
    v = p_ref[r0:r0 + HCHUNK, 3 * GW:4 * GW]
    la = c_ref[crow:crow + 1, :]
    l1 = c_ref[crow + 1:crow + 2, :]
    oml = c_ref[crow + 2:crow + 3, :]
    e = jnp.exp(-jnp.abs(z))
    ope = 1.0 + e
    ls = jnp.minimum(z, 0.0) - jnp.log(ope)
    c2 = l1 + ls
    logf = jnp.maximum(la, c2) + jnp.log(1.0 + jnp.exp(-jnp.abs(la - c2)))
    kk = oml * (jnp.where(z >= 0.0, e, 1.0) / ope)
    h1, h2, h3 = _split3(logf)
    tri_full = tri_ref[1]
    bfull = _dot(tri_full, h1) + _dot(tri_full, h2) + _dot(tri_full, h3)
    half = HCHUNK // 2
    first, mid, last = (HCHUNK - 1, half, 0) if reverse else (0, half - 1, HCHUNK - 1)
    btot = bfull[last:last + 1]
    bmid = bfull[mid:mid + 1]
    worst = jnp.maximum(bfull[first:first + 1] - bmid, bmid - btot)
    return dict(r0=r0, q=q, kk=kk, v=v, splits=(h1, h2, h3), bfull=bfull, btot=btot, bmid=bmid, worst=worst)


def _hgrn_fast(c, st, gm, reverse):
    q, kk, v, bfull, btot, bmid = c["q"], c["kk"], c["v"], c["bfull"], c["btot"], c["bmid"]
    lane_head = lax.broadcasted_iota(jnp.int32, (HCHUNK, GW), 1) // 64
    qi = (q * jnp.exp(bfull - bmid)).astype(BF16)
    ke = (kk * jnp.exp(bmid - bfull)).astype(BF16)
    qx = jnp.concatenate([jnp.where(lane_head == h, qi, jnp.zeros_like(qi)) for h in range(NH)], axis=0)
    a = _dot_nt(qx, ke)
    t_idx = lax.broadcasted_iota(jnp.int32, a.shape, 0) % HCHUNK
    s_idx = lax.broadcasted_iota(jnp.int32, a.shape, 1)
    seen = (s_idx >= t_idx) if reverse else (s_idx <= t_idx)
    a = jnp.where(seen, a, 0.0).astype(BF16)
    vb = v.astype(BF16)
    o_all = _dot(a, vb)
    o = _dot_nt((q * jnp.exp(bfull)).astype(BF16), st.astype(BF16))
    for h in range(NH):
        o = o + jnp.where(lane_head == h, o_all[h * HCHUNK:(h + 1) * HCHUNK, :], 0.0)
    kd = (kk * jnp.exp(btot - bfull)).astype(BF16)
    return o, st * jnp.exp(btot) + _dot_tn(vb, kd) * gm


def _hgrn_slow(c, st, gm, gmb, tri_sub, reverse, o_ref):
    q, kk, v = c["q"], c["kk"], c["v"]
    h1, h2, h3 = c["splits"]
    bsub = _dot(tri_sub, h1) + _dot(tri_sub, h2) + _dot(tri_sub, h3)
    row = lax.broadcasted_iota(jnp.int32, (HSUB, GW), 0)
    order = range(HCHUNK // HSUB - 1, -1, -1) if reverse else range(HCHUNK // HSUB)
    for blk in order:
        r0 = blk * HSUB
        b_i = bsub[r0:r0 + HSUB]
        q_i = q[r0:r0 + HSUB]
        k_i = kk[r0:r0 + HSUB]
        v_i = v[r0:r0 + HSUB]
        bt_i = b_i[0:1] if reverse else b_i[HSUB - 1:HSUB]
        qe = (q_i * jnp.exp(b_i)).astype(BF16)
        o_inter = _dot_nt(qe, st.astype(BF16))
        parts = []
        for tl in range(HSUB):
            dlt = b_i[tl:tl + 1] - b_i
            valid = (row >= tl) if reverse else (row <= tl)
            w = jnp.exp(jnp.where(valid, dlt, NEG))
            parts.append((q_i[tl:tl + 1] * w) * k_i)
        pmat = jnp.concatenate(parts, axis=0).astype(BF16)
        abar = _dot(pmat, gmb)
        o_diag = jnp.sum(abar.reshape(HSUB, HSUB, GW) * v_i[None], axis=1)
        o_ref[c["r0"] + r0:c["r0"] + r0 + HSUB, :] = o_inter + o_diag
        kd = (k_i * jnp.exp(bt_i - b_i)).astype(BF16)
        upd = _dot_tn(v_i.astype(BF16), kd)
        st = st * jnp.exp(bt_i) + upd * gm
    return st


def _hgrn_kernel(pf_ref, pb_ref, c_ref, trif_ref, trib_ref, gm_ref, gmb_ref, of_ref, ob_ref, stf, stb):
    @pl.when(pl.program_id(1) == 0)
    def _():
        stf[...] = jnp.zeros_like(stf)
        stb[...] = jnp.zeros_like(stb)

    gm = gm_ref[...]
    dirs = []
    for p_ref, o_ref, st_ref, tri_ref, zcol, crow, reverse in (
            (pf_ref, of_ref, stf, trif_ref, GW, 0, False), (pb_ref, ob_ref, stb, trib_ref, 2 * GW, 3, True)):
        offs = (HCHUNK, 0) if reverse else (0, HCHUNK)
        chunks = [_hgrn_prologue(p_ref, r0, zcol, c_ref, crow, tri_ref, reverse) for r0 in offs]
        dirs.append((o_ref, st_ref, tri_ref, reverse, chunks))
    worst = functools.reduce(jnp.maximum, [c["worst"] for d in dirs for c in d[4]])
    safe = jnp.max(worst) < HGRN_SAFE_DECAY

    @pl.when(safe)
    def _():
        for o_ref, st_ref, tri_ref, reverse, chunks in dirs:
            st = st_ref[...]
            for c in chunks:
                o, st = _hgrn_fast(c, st, gm, reverse)
                o_ref[c["r0"]:c["r0"] + HCHUNK, :] = o
            st_ref[...] = st

    @pl.when(jnp.logical_not(safe))
    def _():
        gmb = gmb_ref[...]
        for o_ref, st_ref, tri_ref, reverse, chunks in dirs:
            st = st_ref[...]
            for c in chunks:
                st = _hgrn_slow(c, st, gm, gmb, tri_ref[0], reverse, o_ref)
            st_ref[...] = st


def _hgrn_block(b, n, reverse):
    nctx = CTX // HSTEP
    nlat = L // HSTEP
    jc = (nctx - 1 - n) if reverse else n
    jl = (nlat - 1 - (n - nctx)) if reverse else (n - nctx)
    return jnp.where(n < nctx, T_LAT // HSTEP + b * nctx + jc, b * nlat + jl)


def _hgrn(p_hg, consts, trif, trib, gm, gmb):
    nsteps = (CTX + L) // HSTEP
    full = lambda shape: pl.BlockSpec(shape, lambda b, n: (0,) * len(shape))
    return pl.pallas_call(
        _hgrn_kernel,
        grid=(B, nsteps),
        in_specs=[
            pl.BlockSpec((HSTEP, 1280), lambda b, n: (_hgrn_block(b, n, False), 0)),
            pl.BlockSpec((HSTEP, 1280), lambda b, n: (_hgrn_block(b, n, True), 0)),
            full((8, GW)),
            full((2, HCHUNK, HCHUNK)),
            full((2, HCHUNK, HCHUNK)),
            full((GW, GW)),
            full((GW, GW)),
        ],
        out_specs=[
            pl.BlockSpec((HSTEP, GW), lambda b, n: (_hgrn_block(b, n, False), 0)),
            pl.BlockSpec((HSTEP, GW), lambda b, n: (_hgrn_block(b, n, True), 0)),
        ],
        out_shape=[jax.ShapeDtypeStruct((T_ALL, GW), F32), jax.ShapeDtypeStruct((T_ALL, GW), F32)],
        scratch_shapes=[pltpu.VMEM((GW, GW), F32), pltpu.VMEM((GW, GW), F32)],
        compiler_params=_cparams(("arbitrary", "arbitrary")),
        name="hgrn",
    )(p_hg, p_hg, consts, trif, trib, gm, gmb)


def _alt_sum(x):
    n, c = x.shape
    sgn = jnp.where((lax.broadcasted_iota(jnp.int32, (n, c), 0) & 1) == 0, 1.0, -1.0)
    return jnp.sum(x * sgn, axis=0, keepdims=True)


def _hyfilt_kernel(feats_ref, w1_ref, b1_ref, fr_ref, w2_ref, b2_ref, w3_ref, b3_ref, dec_ref,
                   e_ref, o_ref, nq_ref):
    fr = fr_ref[...]
    feats = feats_ref[...]
    h = jnp.sin(fr * (jnp.dot(feats, w1_ref[...], preferred_element_type=F32, precision=HI) + b1_ref[...]))
    h = jnp.sin(fr * (jnp.dot(h, w2_ref[...], preferred_element_type=F32, precision=HI) + b2_ref[...]))
    filt = jnp.dot(h, w3_ref[...], preferred_element_type=F32, precision=HI) + b3_ref[...]
    filt = filt * jnp.exp(-feats[:, 0:1] * dec_ref[...])
    n = filt.shape[0]
    row = lax.broadcasted_iota(jnp.int32, (n, GW), 0)
    for o in range(2):
        fwd = filt[:, (2 * o) * GW:(2 * o + 1) * GW]
        bwd = jnp.where(row >= 1, filt[:, (2 * o + 1) * GW:(2 * o + 2) * GW], 0.0)
        ssq = jnp.sum(fwd * fwd + bwd * bwd, axis=0, keepdims=True)
        scale = lax.rsqrt(ssq + EPS)
        ev = (fwd + bwd) * scale
        e_ref[:, o * GW:(o + 1) * GW] = ev
        o_ref[:, o * GW:(o + 1) * GW] = (fwd - bwd) * scale
        nq_ref[:, o * GW:(o + 1) * GW] = _alt_sum(ev) * (0.5 / n)


def _hyfilt(feats, w1p, b1, fr, w2, b2, w3, b3, dec):
    n = feats.shape[0]
    return pl.pallas_call(
        _hyfilt_kernel,
        out_shape=[jax.ShapeDtypeStruct((n, 2 * GW), F32), jax.ShapeDtypeStruct((n, 2 * GW), F32),
                   jax.ShapeDtypeStruct((1, 2 * GW), F32)],
        compiler_params=_cparams(None),
        name="hyfilt",
    )(feats, w1p, b1, fr, w2, b2, w3, b3, dec)


def _hyspec_kernel(chi_ref, clo_ref, shi_ref, slo_ref, e_ref, o_ref, kre_ref, kim_ref, *, n):
    eh, el = _split2(e_ref[...])
    oh, ol = _split2(o_ref[...])
    kre = _dot(chi_ref[...], eh) + _dot(chi_ref[...], el) + _dot(clo_ref[...], eh)
    kim = _dot(shi_ref[...], oh) + _dot(shi_ref[...], ol) + _dot(slo_ref[...], oh)
    tr = kre.shape[0]
    grow = lax.broadcasted_iota(jnp.int32, kre.shape, 0) + pl.program_id(0) * tr
    s2 = 1.0 / n
    kre_ref[...] = kre * jnp.where(grow == 0, 0.5 * s2, s2)
    kim_ref[...] = kim * s2


def _hyspec(chi, clo, shi, slo, e, o):
    n = e.shape[0]
    tr = min(256, n)
    rows = pl.BlockSpec((tr, n), lambda i: (i, 0))
    full = pl.BlockSpec((n, 2 * GW), lambda i: (0, 0))
    outb = pl.BlockSpec((tr, 2 * GW), lambda i: (i, 0))
    return pl.pallas_call(
        functools.partial(_hyspec_kernel, n=n),
        grid=(n // tr,),
        in_specs=[rows, rows, rows, rows, full, full],
        out_specs=[outb, outb],
        out_shape=[jax.ShapeDtypeStruct((n, 2 * GW), F32)] * 2,
        compiler_params=_cparams(("parallel",)),
        name="hyspec",
    )(chi, clo, shi, slo, e, o)


def _hyena_kernel(u_ref, sw_ref, sb_ref, db_ref, c_ref, s_ref, kre_ref, kim_ref, knq_ref, o_ref,
                  z_scr, zb_scr, y_scr):
    n = u_ref.shape[0]
    ft = min(512, n)
    rc = min(256, n)
    nchunks = n // rc
    lrow = lax.broadcasted_iota(jnp.int32, (rc, GW), 0)
    sgn = jnp.where((lrow & 1) == 0, 1.0, -1.0)

    def short_conv(part, c):
        sl = slice(part * GW, (part + 1) * GW)
        r0 = c * rc
        u = u_ref[r0:r0 + rc, sl]
        prev = u_ref[r0 - 1:r0, sl] if c > 0 else jnp.zeros((1, GW), F32)
        nxt = u_ref[r0 + rc:r0 + rc + 1, sl] if c < nchunks - 1 else jnp.zeros((1, GW), F32)
        up = jnp.where(lrow == 0, prev, pltpu.roll(u, 1, 0))
        un = jnp.where(lrow == rc - 1, nxt, pltpu.roll(u, rc - 1, 0))
        return sw_ref[0:1, sl] * up + sw_ref[1:2, sl] * u + sw_ref[2:3, sl] * un + sb_ref[:, sl]

    for c in range(nchunks):
        z_scr[c * rc:(c + 1) * rc, :] = short_conv(0, c)
    for o in range(2):
        cols = slice(o * GW, (o + 1) * GW)
        znq = jnp.zeros((1, GW), F32)
        for c in range(nchunks):
            zc = z_scr[c * rc:(c + 1) * rc, :]
            zb_scr[c * rc:(c + 1) * rc, :] = zc.astype(BF16)
            znq = znq + jnp.sum(zc * sgn, axis=0, keepdims=True)
        ynq = znq * knq_ref[:, cols]
        for c in range(nchunks):
            y_scr[c * rc:(c + 1) * rc, :] = sgn * ynq
        for f in range(n // ft):
            rs = slice(f * ft, (f + 1) * ft)
            zre = _dot(c_ref[rs, :], zb_scr[...])
            zim = _dot(s_ref[rs, :], zb_scr[...])
            kre = kre_ref[rs, cols]
            kim = kim_ref[rs, cols]
            yre = (zre * kre - zim * kim).astype(BF16)
            yim = (zre * kim + zim * kre).astype(BF16)
            y_scr[...] += _dot(c_ref[:, rs], yre) + _dot(s_ref[:, rs], yim)
        dst = o_ref if o == 1 else z_scr
        for c in range(nchunks):
            rows = slice(c * rc, (c + 1) * rc)
            dst[rows, :] = short_conv(o + 1, c) * (y_scr[rows, :] + db_ref[o:o + 1, :] * z_scr[rows, :])


def _hyena(u, blk0, nb, n, sw, sb, db, cm, sm, kre, kim, knq):
    whole = pl.BlockSpec(memory_space=pltpu.VMEM)
    return pl.pallas_call(
        _hyena_kernel,
        grid=(nb,),
        in_specs=[
            pl.BlockSpec((n, 3 * GW), lambda b: (blk0 + b, 0)),
            pl.BlockSpec((3, 3 * GW), lambda b: (0, 0)),
            pl.BlockSpec((1, 3 * GW), lambda b: (0, 0)),
            pl.BlockSpec((2, GW), lambda b: (0, 0)),
            whole, whole, whole, whole, whole,
        ],
        out_specs=pl.BlockSpec((n, GW), lambda b: (b, 0)),
        out_shape=jax.ShapeDtypeStruct((nb * n, GW), F32),
        scratch_shapes=[pltpu.VMEM((n, GW), F32), pltpu.VMEM((n, GW), BF16), pltpu.VMEM((n, GW), F32)],
        compiler_params=_cparams(("arbitrary",)),
        name="hyena",
    )(u, sw, sb, db, cm, sm, kre, kim, knq)


def _dft_consts(n):
    kk = jnp.arange(n, dtype=jnp.int32)
    ph = (kk[:, None] * kk[None, :]) % (2 * n)
    ang = ph.astype(F32) * (math.pi / n)
    return jnp.cos(ang), -jnp.sin(ang)


def _hyena_feats(n):
    t = jnp.arange(n, dtype=F32)
    t_unit = jnp.linspace(0.0, 1.0, n, dtype=F32)
    bands = jnp.linspace(1e-4, HYENA_BANDS - 1, HYENA_BANDS, dtype=F32)
    ang = (2.0 * math.pi / n) * t[:, None] * bands[None, :]
    feats = jnp.concatenate([t_unit[:, None], jnp.cos(ang), -jnp.sin(ang)], axis=-1)
    return jnp.pad(feats, ((0, 0), (0, 128 - HYENA_EMB)))


def _naprep_kernel(p_ref, qg_ref, kg_ref, gm_ref, q_ref, k_ref, v_ref):
    p = p_ref[...]
    q = p[:, 0:GW]
    k = p[:, GW:2 * GW]
    gm = gm_ref[...]
    qn = q * lax.rsqrt(_group_sum(q * q, gm) * (1.0 / 64) + EPS) * qg_ref[...]
    kn = k * lax.rsqrt(_group_sum(k * k, gm) * (1.0 / 64) + EPS) * kg_ref[...]
    q_ref[...] = (qn * (64 ** -0.5)).astype(BF16)
    k_ref[...] = kn.astype(BF16)
    v_ref[...] = p[:, 2 * GW:3 * GW].astype(BF16)


def _naprep(p_na, qg, kg, gmb):
    tok = lambda w: pl.BlockSpec((TM, w), lambda i: (i, 0))
    full = lambda shape: pl.BlockSpec(shape, lambda i: (0,) * len(shape))
    return pl.pallas_call(
        _naprep_kernel,
        grid=(N_TILES,),
        in_specs=[tok(768), full((1, GW)), full((1, GW)), full((GW, GW))],
        out_specs=[tok(GW), tok(GW), tok(GW)],
        out_shape=[jax.ShapeDtypeStruct((T_ALL, GW), BF16)] * 3,
        compiler_params=_cparams(("parallel",)),
        name="naprep",
    )(p_na, qg, kg, gmb)


NA_RPS = 4


def _na_kernel(q_ref, k_ref, v_ref, kc_ref, vc_ref, bias_ref, o_ref):
    rows = L // GRID_W
    kc = kc_ref[...]
    vc = vc_ref[...]
    for j in range(NA_RPS):
        r = pl.program_id(1) * NA_RPS + j
        rs = jnp.clip(r - NA_ROWS // 2, 0, rows - NA_ROWS)
        variant = r - rs
        start = pl.multiple_of(rs * GRID_W, GRID_W)
        kw = k_ref[pl.ds(start, NA_ROWS * GRID_W), :]
        vw = v_ref[pl.ds(start, NA_ROWS * GRID_W), :]
        q = q_ref[j * GRID_W:(j + 1) * GRID_W, :]
        lane_head = lax.broadcasted_iota(jnp.int32, (GRID_W, GW), 1) // 64
        hmask = [lane_head == h for h in range(NH)]
        qx = jnp.concatenate([jnp.where(hmask[h], q, jnp.zeros_like(q)) for h in range(NH)], axis=0)
        s_loc = _dot_nt(qx, kw) + bias_ref[variant]
        s_ctx = _dot_nt(qx, kc)
        m = jnp.maximum(jnp.max(s_loc, axis=-1, keepdims=True), jnp.max(s_ctx, axis=-1, keepdims=True))
        p_loc = jnp.exp(s_loc - m)
        p_ctx = jnp.exp(s_ctx - m)
        den = jnp.sum(p_loc, axis=-1, keepdims=True) + jnp.sum(p_ctx, axis=-1, keepdims=True)
        o_all = (_dot(p_loc.astype(BF16), vw) + _dot(p_ctx.astype(BF16), vc)) / den
        o = jnp.zeros((GRID_W, GW), F32)
        for h in range(NH):
            o = o + jnp.where(hmask[h], o_all[h * GRID_W:(h + 1) * GRID_W, :], 0.0)
        o_ref[j * GRID_W:(j + 1) * GRID_W, :] = o


def _na(qn, kn, vn, bias_t):
    steps = L // GRID_W // NA_RPS
    tq = NA_RPS * GRID_W
    ctx_blk = T_LAT // CTX
    return pl.pallas_call(
        _na_kernel,
        grid=(B, steps),
        in_specs=[
            pl.BlockSpec((tq, GW), lambda b, r: (b * steps + r, 0)),
            pl.BlockSpec((L, GW), lambda b, r: (b, 0)),
            pl.BlockSpec((L, GW), lambda b, r: (b, 0)),
            pl.BlockSpec((CTX, GW), lambda b, r: (ctx_blk + b, 0)),
            pl.BlockSpec((CTX, GW), lambda b, r: (ctx_blk + b, 0)),
            pl.BlockSpec((NA_ROWS, NH * GRID_W, NA_ROWS * GRID_W), lambda b, r: (0, 0, 0)),
        ],
        out_specs=pl.BlockSpec((tq, GW), lambda b, r: (b * steps + r, 0)),
        out_shape=jax.ShapeDtypeStruct((T_LAT, GW), F32),
        compiler_params=_cparams(("arbitrary", "arbitrary")),
        name="na",
    )(qn, kn, vn, kn, vn, bias_t)


def _na_bias_table(rpb):
    cq = jnp.arange(GRID_W)
    cs = jnp.clip(cq - NA_COLS // 2, 0, GRID_W - NA_COLS)
    col_ok = (cq[None, :] >= cs[:, None]) & (cq[None, :] < cs[:, None] + NA_COLS)
    dc = jnp.clip(cq[None, :] - cq[:, None] + (NA_COLS - 1), 0, 2 * NA_COLS - 2)
    onehot = (dc[:, :, None] == jnp.arange(2 * NA_COLS - 1)[None, None, :]).astype(F32)
    full = jnp.einsum('qkc,hrc->hrqk', onehot, rpb.astype(F32), precision=HI)
    full = jnp.where(col_ok[None, None], full, NEG)
    tab = jnp.stack([full[:, NA_ROWS - 1 - a:2 * NA_ROWS - 1 - a] for a in range(NA_ROWS)], axis=0)
    return tab.transpose(0, 1, 3, 2, 4).reshape(NA_ROWS, NH * GRID_W, NA_ROWS * GRID_W)


def _attn_kernel(*refs, nkv, dq, dv):
    q = refs[0][...]
    ks = [refs[1 + 2 * j][...] for j in range(nkv)]
    vs = [refs[2 + 2 * j][...] for j in range(nkv)]
    o_ref = refs[1 + 2 * nkv]
    outs = []
    for h in range(NH):
        qh = q[:, h * dq:(h + 1) * dq]
        ss = [_dot_nt(qh, k[:, h * dq:(h + 1) * dq]) for k in ks]
        m = functools.reduce(jnp.maximum, [jnp.max(s, axis=-1, keepdims=True) for s in ss])
        ps = [jnp.exp(s - m) for s in ss]
        den = functools.reduce(lambda a, b2: a + b2, [jnp.sum(p, axis=-1, keepdims=True) for p in ps])
        o = functools.reduce(lambda a, b2: a + b2,
                             [_dot(p.astype(BF16), v[:, h * dv:(h + 1) * dv]) for p, v in zip(ps, vs)])
        outs.append(o / den)
    o_ref[...] = jnp.concatenate(outs, axis=-1)


def _attn_latent(q, k, v, dq, dv, tq):
    nq = L // tq
    ctx_blk = T_LAT // CTX
    return pl.pallas_call(
        functools.partial(_attn_kernel, nkv=2, dq=dq, dv=dv),
        grid=(B, nq),
        in_specs=[
            pl.BlockSpec((tq, NH * dq), lambda b, i: (b * nq + i, 0)),
            pl.BlockSpec((L, NH * dq), lambda b, i: (b, 0)),
            pl.BlockSpec((L, NH * dv), lambda b, i: (b, 0)),
            pl.BlockSpec((CTX, NH * dq), lambda b, i: (ctx_blk + b, 0)),
            pl.BlockSpec((CTX, NH * dv), lambda b, i: (ctx_blk + b, 0)),
        ],
        out_specs=pl.BlockSpec((tq, NH * dv), lambda b, i: (b * nq + i, 0)),
        out_shape=jax.ShapeDtypeStruct((T_LAT, NH * dv), F32),
        compiler_params=_cparams(("arbitrary", "arbitrary")),
        name="attn_latent",
    )(q, k, v, k, v)


def _attn_ctx(q, k, v, dq, dv):
    ctx_blk = T_LAT // CTX
    return pl.pallas_call(
        functools.partial(_attn_kernel, nkv=1, dq=dq, dv=dv),
        grid=(B,),
        in_specs=[
            pl.BlockSpec((CTX, NH * dq), lambda b: (ctx_blk + b, 0)),
            pl.BlockSpec((CTX, NH * dq), lambda b: (ctx_blk + b, 0)),
            pl.BlockSpec((CTX, NH * dv), lambda b: (ctx_blk + b, 0)),
        ],
        out_specs=pl.BlockSpec((CTX, NH * dv), lambda b: (b, 0)),
        out_shape=jax.ShapeDtypeStruct((T_CTX, NH * dv), F32),
        compiler_params=_cparams(("arbitrary",)),
        name="attn_ctx",
    )(q, k, v)


def _mlaprep_kernel(p_ref, qag_ref, kvag_ref, wq_ref, wk_ref, wv_ref, qg_ref, kg_ref, gm_ref, pm_ref,
                    cos_ref, sin_ref, q_ref, k_ref, v_ref):
    p = p_ref[...]
    cq = p[:, 0:MLA_Q_RANK]
    ckv = p[:, MLA_Q_RANK:MLA_Q_RANK + MLA_KV_RANK]
    krp = p[:, MLA_Q_RANK + MLA_KV_RANK:]
    cqn = cq * lax.rsqrt(jnp.mean(cq * cq, axis=-1, keepdims=True) + EPS) * qag_ref[...]
    ckvn = ckv * lax.rsqrt(jnp.mean(ckv * ckv, axis=-1, keepdims=True) + EPS) * kvag_ref[...]
    ckvb = ckvn.astype(BF16)
    q = _dot(cqn.astype(BF16), wq_ref[...])
    k = _dot(jnp.concatenate([ckvb, krp.astype(BF16)], axis=-1), wk_ref[...])
    v = _dot(ckvb, wv_ref[...])
    gm = gm_ref[...]
    q = q * lax.rsqrt(_group_sum(q * q, gm) * (1.0 / MLA_QK) + EPS) * qg_ref[...]
    k = k * lax.rsqrt(_group_sum(k * k, gm) * (1.0 / MLA_QK) + EPS) * kg_ref[...]
    cos = cos_ref[...]
    sin = sin_ref[...]
    pm = pm_ref[...]
    q = q * cos + _dot(q.astype(BF16), pm) * sin
    k = k * cos + _dot(k.astype(BF16), pm) * sin
    q_ref[...] = (q * (MLA_QK ** -0.5)).astype(BF16)
    k_ref[...] = k.astype(BF16)
    v_ref[...] = v.astype(BF16)


def _mlaprep(p_mla, qag, kvag, wq, wk, wv, qg, kg, gm, pm, cos_t, sin_t):
    tok = lambda w: pl.BlockSpec((TM, w), lambda i: (i, 0))
    full = lambda shape: pl.BlockSpec(shape, lambda i: (0,) * len(shape))
    pos = pl.BlockSpec((TM, 512), lambda i: (jnp.where(i < LAT_TILES, i % TILES_PER_SEQ, TILES_PER_SEQ), 0))
    return pl.pallas_call(
        _mlaprep_kernel,
        grid=(N_TILES,),
        in_specs=[tok(512), full((1, 256)), full((1, 128)), full((256, 512)), full((256, 512)),
                  full((128, 256)), full((1, 512)), full((1, 512)), full((512, 512)), full((512, 512)),
                  pos, pos],
        out_specs=[tok(512), tok(512), tok(GW)],
        out_shape=[jax.ShapeDtypeStruct((T_ALL, 512), BF16), jax.ShapeDtypeStruct((T_ALL, 512), BF16),
                   jax.ShapeDtypeStruct((T_ALL, GW), BF16)],
        compiler_params=_cparams(("parallel",)),
        name="mlaprep",
    )(p_mla, qag, kvag, wq, wk, wv, qg, kg, gm, pm, cos_t, sin_t)


def _rope_tables():
    t = jnp.arange(L)
    rowp = (t // GRID_W).astype(F32)
    colp = (t % GRID_W).astype(F32)
    half = MLA_ROPE // 2
    inv = ROPE_BASE ** (-jnp.arange(0, half, 2, dtype=F32) / half)
    j = jnp.arange(MLA_ROPE)
    pos = jnp.where(j[None, :] < half, rowp[:, None], colp[:, None])
    ang = pos * inv[j % (half // 2)][None, :]
    first = (j % half) < (half // 2)
    cos32 = jnp.cos(ang)
    sin32 = jnp.where(first[None, :], -jnp.sin(ang), jnp.sin(ang))
    cos_h = jnp.concatenate([jnp.ones((L, MLA_NOPE), F32), cos32, jnp.ones((L, 32), F32)], axis=-1)
    sin_h = jnp.concatenate([jnp.zeros((L, MLA_NOPE), F32), sin32, jnp.zeros((L, 32), F32)], axis=-1)
    cos_t = jnp.concatenate([jnp.tile(cos_h, (1, NH)), jnp.ones((TM, 512), F32)], axis=0)
    sin_t = jnp.concatenate([jnp.tile(sin_h, (1, NH)), jnp.zeros((TM, 512), F32)], axis=0)
    lane = jnp.arange(512)
    jj = lane % 128 - MLA_NOPE
    is_rope = (jj >= 0) & (jj < MLA_ROPE)
    partner = jnp.where(is_rope, jnp.where((jj % half) < (half // 2), lane + half // 2, lane - half // 2), lane)
    pm = (lane[:, None] == partner[None, :]).astype(BF16)
    return cos_t, sin_t, pm


def _outproj_kernel(x_ref, of_ref, ob_ref, g_ref, hyl_ref, hyc_ref, nal_ref, nac_ref, mll_ref, mlc_ref,
                    mod_ref, ng_ref, gm_ref, w_ref, n2_ref, wr_ref, x1_ref, h2_ref, lg_ref):
    oa = of_ref[...] + ob_ref[...]
    ms = _group_sum(oa * oa, gm_ref[...]) * (1.0 / 64)
    g = g_ref[...]
    oa = oa * lax.rsqrt(ms + EPS) * ng_ref[...] * (g * jax.nn.sigmoid(g))
    lat = pl.program_id(0) < LAT_TILES
    hy = jnp.where(lat, hyl_ref[...], hyc_ref[...])
    na = jnp.where(lat, nal_ref[...], nac_ref[...])
    mla = jnp.where(lat, mll_ref[...], mlc_ref[...])
    mix = jnp.concatenate([oa, hy, na, mla], axis=-1).astype(BF16)
    x1 = x_ref[...] + mod_ref[2:3, :] * _dot(mix, w_ref[...])
    x1_ref[...] = x1
    ms2 = jnp.mean(x1 * x1, axis=-1, keepdims=True)
    h2 = x1 * lax.rsqrt(ms2 + EPS) * n2_ref[...] * (1.0 + mod_ref[4:5, :]) + mod_ref[3:4, :]
    _store_rowtiles(h2_ref, h2)
    hh, hl = _split2(h2)
    wr = wr_ref[...]
    lg_ref[...] = _dot(hh, wr[:, 0:128]) + _dot(hl, wr[:, 0:128]) + _dot(hh, wr[:, 128:256])


def _outproj(n_tiles, X, o_f, o_b, p_hg, hy, na, mla, mod_l, ng, gmb, w_bf, n2g, wr):
    tok = lambda w: pl.BlockSpec((TM, w), lambda i: (i, 0))
    full = lambda shape: pl.BlockSpec(shape, lambda i: (0,) * len(shape))
    latb = pl.BlockSpec((TM, GW), lambda i: (jnp.minimum(i, LAT_TILES - 1), 0))
    ctxb = pl.BlockSpec((TM, GW), lambda i: (jnp.maximum(i - LAT_TILES, 0), 0))
    nt = n_tiles * TM
    return pl.pallas_call(
        _outproj_kernel,
        grid=(n_tiles,),
        in_specs=[tok(D), tok(GW), tok(GW), pl.BlockSpec((TM, GW), lambda i: (i, 4)),
                  latb, ctxb, latb, ctxb, latb, ctxb,
                  pl.BlockSpec((None, 6, D), lambda i: (_seg_of_tile(i), 0, 0)),
                  full((1, GW)), full((GW, GW)), full((D, D)), full((1, D)), full((D, 256))],
        out_specs=[tok(D), pl.BlockSpec((TM * ROW_CH, LANE), lambda i: (i, 0)), tok(128)],
        out_shape=[jax.ShapeDtypeStruct((nt, D), F32), jax.ShapeDtypeStruct((nt * ROW_CH, LANE), F32),
                   jax.ShapeDtypeStruct((nt, 128), F32)],
        compiler_params=_cparams(("parallel",)),
        name="outproj",
    )(X, o_f, o_b, p_hg, hy[0], hy[1], na[0], na[1], mla[0], mla[1], mod_l, ng, gmb, w_bf, n2g, wr)


def _route_kernel(lg_ref, b_ref, ltri_ref, o_ref, cnt_ref, base):
    @pl.when(pl.program_id(0) == 0)
    def _():
        base[...] = jnp.zeros_like(base)

    l = lg_ref[...] + b_ref[...]
    lane_i = lax.broadcasted_iota(jnp.int32, l.shape, 1)
    lane = lane_i.astype(F32)

    def first_max(mask):
        v = jnp.max(jnp.where(mask, l, NEG), axis=-1, keepdims=True)
        i = jnp.min(jnp.where(mask & (l == v), lane, float(LANE)), axis=-1, keepdims=True)
        return v, i

    gmask = lane_i < N_GROUPS
    mg, g_sel = first_max(gmask)
    p_sel = 1.0 / jnp.sum(jnp.where(gmask, jnp.exp(jnp.where(gmask, l - mg, 0.0)), 0.0), axis=-1, keepdims=True)
    lane_group = jnp.right_shift(lane_i - N_GROUPS, EPG.bit_length() - 1).astype(F32)
    emask = (lane_i >= N_GROUPS) & (lane_i < N_GROUPS + N_EXPERTS) & (lane_group == g_sel)
    v1, i1 = first_max(emask)
    v2, i2 = first_max(emask & (lane != i1))
    r = jnp.exp(v2 - v1)
    w1 = p_sel / (1.0 + r)
    w2 = w1 * r
    hit1 = lane == i1
    hit2 = lane == i2
    cnt = jnp.where(hit1 | hit2, 1.0, 0.0)
    before = _dot(ltri_ref[...], cnt.astype(BF16)) + base[...]
    pos1 = jnp.sum(jnp.where(hit1, before, 0.0), axis=-1, keepdims=True)
    pos2 = jnp.sum(jnp.where(hit2, before, 0.0), axis=-1, keepdims=True)
    base[...] = base[...] + jnp.sum(cnt, axis=0, keepdims=True)
    cnt_ref[...] = jnp.broadcast_to(base[...], cnt_ref.shape)
    cols = [i1 - N_GROUPS, i2 - N_GROUPS, w1, w2, pos1, pos2]
    out = jnp.zeros(l.shape, F32)
    for j, c in enumerate(cols):
        out = jnp.where(lane_i == j, c, out)
    o_ref[...] = out


def _route(logits, bg, be):
    t = logits.shape[0]
    bias = jnp.pad(jnp.concatenate([bg, be]), (0, LANE - N_GROUPS - N_EXPERTS)).reshape(1, LANE)
    ltri = (jnp.arange(TM)[None, :] < jnp.arange(TM)[:, None]).astype(BF16)
    out, cnt = pl.pallas_call(
        _route_kernel,
        grid=(t // TM,),
        in_specs=[pl.BlockSpec((TM, LANE), lambda i: (i, 0)), pl.BlockSpec((1, LANE), lambda i: (0, 0)),
                  pl.BlockSpec((TM, TM), lambda i: (0, 0))],
        out_specs=[pl.BlockSpec((TM, LANE), lambda i: (i, 0)), pl.BlockSpec((8, LANE), lambda i: (0, 0))],
        out_shape=[jax.ShapeDtypeStruct((t, LANE), F32), jax.ShapeDtypeStruct((8, LANE), F32)],
        scratch_shapes=[pltpu.VMEM((1, LANE), F32)],
        compiler_params=_cparams(("arbitrary",)),
        name="route",
    )(logits, bias, ltri)
    eid = out[:, 0:2].astype(jnp.int32)
    pos = out[:, 4:6].astype(jnp.int32)
    counts = cnt[0, N_GROUPS:N_GROUPS + N_EXPERTS].astype(jnp.int32)
    return eid, out[:, 2:4], pos, counts


def _dispatch_tables(eid, pos, counts):
    t = eid.shape[0]
    n = 2 * t
    nb = n // TMOE + N_EXPERTS
    flat_e = eid.reshape(n)
    pos = pos.reshape(n)
    onehot = (flat_e[:, None] == jnp.arange(N_EXPERTS, dtype=jnp.int32)[None, :]).astype(jnp.int32)
    pcounts = ((counts + TMOE - 1) // TMOE) * TMOE
    pends = jnp.cumsum(pcounts)
    pstarts = pends - pcounts
    dest = (jnp.sum(jnp.where(onehot > 0, pstarts[None, :], 0), axis=1) + pos).astype(jnp.int32)
    blk_start = jnp.arange(nb, dtype=jnp.int32) * TMOE
    block_e = jnp.minimum(jnp.sum((pends[None, :] <= blk_start[:, None]).astype(jnp.int32), axis=1),
                          N_EXPERTS - 1).astype(jnp.int32)
    nblk = (pends[-1] // TMOE).astype(jnp.int32).reshape(1)
    pads = jnp.concatenate([pstarts + counts, pcounts - counts, nblk]).astype(jnp.int32)
    return block_e, nblk, dest, pads


PAD_PIECES = tuple(1 << b for b in range(TMOE.bit_length() - 1))


def _rows(tok, n):
    if isinstance(tok, int):
        return pl.ds(tok * ROW_CH, n * ROW_CH)
    return pl.ds(pl.multiple_of(tok * ROW_CH, ROW_CH), n * ROW_CH)


def _dispatch_kernel(dest_ref, pad_ref, h_ref, xs_out, zbuf, sem, zsem):
    base = pl.program_id(0) * (2 * TM)

    @pl.when(pl.program_id(0) == 0)
    def _():
        zbuf[...] = jnp.zeros_like(zbuf)
        ztok = zbuf.shape[0] // ROW_CH
        for phase in range(2):
            for e in range(N_EXPERTS):
                off = pad_ref[e]
                npad = pad_ref[N_EXPERTS + e]
                for piece in PAD_PIECES:
                    has = (npad & piece) != 0

                    @pl.when(has)
                    def _(off=off, piece=piece):
                        cp = pltpu.make_async_copy(zbuf.at[pl.ds(0, piece * ROW_CH)], xs_out.at[_rows(off, piece)], zsem)
                        cp.start() if phase == 0 else cp.wait()

                    off = off + jnp.where(has, piece, 0)

            first = pad_ref[2 * N_EXPERTS] * (TMOE // ztok)

            def tail(j, carry):
                cp = pltpu.make_async_copy(zbuf, xs_out.at[_rows(j * ztok, ztok)], zsem)
                cp.start() if phase == 0 else cp.wait()
                return carry

            lax.fori_loop(first, xs_out.shape[0] // zbuf.shape[0], tail, 0)

    def issue(r, carry):
        for k in range(2):
            pltpu.make_async_copy(h_ref.at[_rows(r, 1)], xs_out.at[_rows(dest_ref[base + 2 * r + k], 1)], sem).start()
        return carry

    lax.fori_loop(0, TM, issue, 0, unroll=8)
    pltpu.make_async_copy(xs_out.at[_rows(0, 2 * TM)], xs_out.at[_rows(0, 2 * TM)], sem).wait()


def _dispatch(n_tiles, n_slots, dest, pads, h2):
    grid_spec = pltpu.PrefetchScalarGridSpec(
        num_scalar_prefetch=2,
        grid=(n_tiles,),
        in_specs=[pl.BlockSpec((TM * ROW_CH, LANE), lambda i, dst, pd: (i, 0))],
        out_specs=pl.BlockSpec(memory_space=pl.ANY),
        scratch_shapes=[pltpu.VMEM((TMOE // 2 * ROW_CH, LANE), F32), pltpu.SemaphoreType.DMA(()),
                        pltpu.SemaphoreType.DMA(())],
    )
    return pl.pallas_call(
        _dispatch_kernel,
        grid_spec=grid_spec,
        out_shape=jax.ShapeDtypeStruct((n_slots * ROW_CH, LANE), F32),
        compiler_params=_cparams(("arbitrary",)),
        name="dispatch",
    )(dest, pads, h2)


def _experts_kernel(be_ref, nblk_ref, xs_ref, wg_ref, wu_ref, wd_ref, ys_ref, wgb, wub, wdb):
    i = pl.program_id(0)

    @pl.when((i == 0) | (be_ref[i] != be_ref[jnp.maximum(i - 1, 0)]))
    def _():
        wgb[...] = wg_ref[...].astype(BF16)
        wub[...] = wu_ref[...].astype(BF16)
        wdb[...] = wd_ref[...].astype(BF16)

    @pl.when(i < nblk_ref[0])
    def _():
        x = _load_rowtiles(xs_ref, TMOE).astype(BF16)
        gate = _dot(x, wgb[...])
        up = _dot(x, wub[...])
        act = (gate * jax.nn.sigmoid(gate)) * up
        _store_rowtiles(ys_ref, _dot(act.astype(BF16), wdb[...]))

    @pl.when(i >= nblk_ref[0])
    def _():
        ys_ref[...] = jnp.zeros_like(ys_ref)


def _experts(layer, block_e, nblk, xs, w_gate, w_up, w_down):
    nb = block_e.shape[0]
    used = lambda i, nk: jnp.minimum(i, nk[0] - 1)
    grid_spec = pltpu.PrefetchScalarGridSpec(
        num_scalar_prefetch=2,
        grid=(nb,),
        in_specs=[
            pl.BlockSpec((TMOE * ROW_CH, LANE), lambda i, be, nk: (used(i, nk), 0)),
            pl.BlockSpec((None, None, D, D_EXPERT), lambda i, be, nk: (layer, be[i], 0, 0)),
            pl.BlockSpec((None, None, D, D_EXPERT), lambda i, be, nk: (layer, be[i], 0, 0)),
            pl.BlockSpec((None, None, D_EXPERT, D), lambda i, be, nk: (layer, be[i], 0, 0)),
        ],
        out_specs=pl.BlockSpec((TMOE * ROW_CH, LANE), lambda i, be, nk: (i, 0)),
        scratch_shapes=[pltpu.VMEM((D, D_EXPERT), BF16), pltpu.VMEM((D, D_EXPERT), BF16),
                        pltpu.VMEM((D_EXPERT, D), BF16)],
    )
    return pl.pallas_call(
        _experts_kernel,
        grid_spec=grid_spec,
        out_shape=jax.ShapeDtypeStruct(xs.shape, F32),
        compiler_params=_cparams(("arbitrary",)),
        name="experts",
    )(block_e, nblk, xs, w_gate, w_up, w_down)


def _combine_tile(dest_ref, x_ref, w_ref, mod_ref, ys_hbm, ybuf, sem):
    i = pl.program_id(0)

    def fetch(tile, slot):
        base = tile * (2 * TM)

        def issue(r, carry):
            for k in range(2):
                pltpu.make_async_copy(ys_hbm.at[_rows(dest_ref[base + 2 * r + k], 1)],
                                      ybuf.at[slot, k, _rows(r, 1)], sem.at[slot]).start()
            return carry

        lax.fori_loop(0, TM, issue, 0, unroll=8)

    @pl.when(i == 0)
    def _():
        fetch(0, 0)

    @pl.when(i + 1 < pl.num_programs(0))
    def _():
        fetch(i + 1, (i + 1) % 2)

    slot = i % 2
    for k in range(2):
        pltpu.make_async_copy(ys_hbm.at[_rows(0, TM)], ybuf.at[slot, k], sem.at[slot]).wait()
    w = w_ref[...]
    y = w[:, 0:1] * _load_rowtiles(ybuf.at[slot, 0], TM) + w[:, 1:2] * _load_rowtiles(ybuf.at[slot, 1], TM)
    return x_ref[...] + mod_ref[5:6, :] * y


def _combine_kernel(dest_ref, x_ref, w_ref, mod_ref, ys_hbm, o_ref, ybuf, sem):
    o_ref[...] = _combine_tile(dest_ref, x_ref, w_ref, mod_ref, ys_hbm, ybuf, sem)


def _combine_in_specs():
    return [pl.BlockSpec((TM, D), lambda i, dst: (i, 0)),
            pl.BlockSpec((TM, 2), lambda i, dst: (i, 0)),
            pl.BlockSpec((None, 6, D), lambda i, dst: (_seg_of_tile(i), 0, 0)),
            pl.BlockSpec(memory_space=pl.ANY)]


COMBINE_SCRATCH = [pltpu.VMEM((2, 2, TM * ROW_CH, LANE), F32), pltpu.SemaphoreType.DMA((2,))]


def _combine(n_tiles, dest, X1, wts, mod_l, ys):
    grid_spec = pltpu.PrefetchScalarGridSpec(
        num_scalar_prefetch=1,
        grid=(n_tiles,),
        in_specs=_combine_in_specs(),
        out_specs=pl.BlockSpec((TM, D), lambda i, dst: (i, 0)),
        scratch_shapes=COMBINE_SCRATCH,
    )
    return pl.pallas_call(
        _combine_kernel,
        grid_spec=grid_spec,
        out_shape=jax.ShapeDtypeStruct((n_tiles * TM, D), F32),
        compiler_params=_cparams(("arbitrary",)),
        name="combine",
    )(dest, X1, wts, mod_l, ys)


def _combine_inproj_kernel(dest_ref, x_ref, w_ref, modp_ref, ys_hbm, mod_ref, g_ref, win_ref,
                           x_out, o_hg, o_hy, o_na, o_mla, ybuf, sem):
    x = _combine_tile(dest_ref, x_ref, w_ref, modp_ref, ys_hbm, ybuf, sem)
    x_out[...] = x
    _inproj_tile(x, mod_ref, g_ref, win_ref, (o_hg, o_hy, o_na, o_mla))


def _combine_inproj(dest, X1, wts, mod_prev, ys, mod_l, g, w_bf):
    p_specs, p_shapes = _inproj_out_specs(lambda i, dst: (i, 0))
    grid_spec = pltpu.PrefetchScalarGridSpec(
        num_scalar_prefetch=1,
        grid=(N_TILES,),
        in_specs=_combine_in_specs() + [
            pl.BlockSpec((None, 6, D), lambda i, dst: (_seg_of_tile(i), 0, 0)),
            pl.BlockSpec((1, D), lambda i, dst: (0, 0)),
            pl.BlockSpec((D, D_IN_PAD), lambda i, dst: (0, 0)),
        ],
        out_specs=[pl.BlockSpec((TM, D), lambda i, dst: (i, 0))] + p_specs,
        scratch_shapes=COMBINE_SCRATCH,
    )
    return pl.pallas_call(
        _combine_inproj_kernel,
        grid_spec=grid_spec,
        out_shape=[jax.ShapeDtypeStruct((T_ALL, D), F32)] + p_shapes,
        compiler_params=_cparams(("arbitrary",)),
        name="combine_inproj",
    )(dest, X1, wts, mod_prev, ys, mod_l, g, w_bf)


def _group_mask(width, group):
    lane = jnp.arange(width)
    return (lane[:, None] // group == lane[None, :] // group)


def _hgrn_tri(reverse):
    t = jnp.arange(HCHUNK)
    same = (t[:, None] // HSUB) == (t[None, :] // HSUB)
    order = (t[None, :] >= t[:, None]) if reverse else (t[None, :] <= t[:, None])
    return jnp.stack([same & order, order]).astype(BF16)


def _mla_weights(w_uq, w_ukv, q_g, k_g):
    wq = jnp.pad(w_uq.reshape(MLA_Q_RANK, NH, MLA_QK), ((0, 0), (0, 0), (0, 128 - MLA_QK))).reshape(MLA_Q_RANK, 512)
    kv = w_ukv.reshape(MLA_KV_RANK, NH, MLA_NOPE + 64)
    wk_top = jnp.pad(kv[:, :, :MLA_NOPE], ((0, 0), (0, 0), (0, 128 - MLA_NOPE))).reshape(MLA_KV_RANK, 512)
    lane = jnp.arange(512)
    src = jnp.arange(128)
    place = ((lane[None, :] % 128) == (src[:, None] + MLA_NOPE)) & (src[:, None] < MLA_ROPE)
    wk = jnp.concatenate([wk_top, place.astype(F32)], axis=0)
    wv = kv[:, :, MLA_NOPE:].reshape(MLA_KV_RANK, GW)
    pad_g = lambda g: jnp.tile(jnp.pad(g, (0, 128 - MLA_QK)), NH).reshape(1, 512)
    return wq.astype(BF16), wk.astype(BF16), wv.astype(BF16), pad_g(q_g), pad_g(k_g)


def kernel(x, c, ctx, c_ctx, w_ada, b_ada, norm1_g, norm2_g, w_in, w_out, hgrn_lb_logits, hgrn_norm_g,
           hy_short_w, hy_short_b, hy_w1, hy_b1, hy_freq, hy_w2, hy_b2, hy_w3, hy_b3, hy_decay, hy_bias,
           na_rpb, na_q_g, na_k_g, mla_q_a_g, mla_kv_a_g, mla_w_uq, mla_w_ukv, mla_q_g, mla_k_g,
           moe_wg, moe_bg, moe_we, moe_be, moe_w_gate, moe_w_up, moe_w_down):
    X = jnp.concatenate([x.reshape(T_LAT, D), ctx.reshape(T_CTX, D)], axis=0)
    cmat = jnp.concatenate([c, c_ctx[None, :], jnp.zeros((16 - B - 1, D), F32)], axis=0)
    mod = _adaln(cmat, w_ada, b_ada).reshape(DEPTH, 16, 6, D)

    lb_cum = jnp.cumsum(jax.nn.softmax(hgrn_lb_logits.astype(F32), axis=0), axis=0)
    lower = lb_cum - lb_cum[0:1]

    gm64 = _group_mask(GW, 64)
    gm64_f = gm64.astype(F32)
    gm64_b = gm64.astype(BF16)
    gm128_b = _group_mask(512, 128).astype(BF16)
    trif = _hgrn_tri(False)
    trib = _hgrn_tri(True)
    cos_t, sin_t, pm = _rope_tables()
    dft = {}
    for n in (L, CTX):
        cm, sm = _dft_consts(n)
        chi, clo = _split2(cm)
        shi, slo = _split2(sm)
        dft[n] = (chi, clo, shi, slo, _hyena_feats(n))

    pending = None
    for l in range(DEPTH):
        mod_l = mod[l]
        w_in_l = jnp.pad(w_in[l], ((0, 0), (0, D_IN_PAD - D_IN))).astype(BF16)
        if pending is None:
            p_hg, p_hy, p_na, p_mla = _inproj(X, mod_l, norm1_g[l].reshape(1, D), w_in_l)
        else:
            X, p_hg, p_hy, p_na, p_mla = _combine_inproj(*pending, mod_l, norm1_g[l].reshape(1, D), w_in_l)

        lb = lower[l]
        hconst = jnp.concatenate([
            jnp.stack([jnp.maximum(jnp.log(lb[d]), NEG), jnp.log1p(-lb[d]), 1.0 - lb[d]]) for d in range(2)
        ] + [jnp.zeros((2, GW), F32)], axis=0)
        o_f, o_b = _hgrn(p_hg, hconst, trif, trib, gm64_f, gm64_b)

        need_ctx = l < DEPTH - 1
        w1p = jnp.pad(hy_w1[l], ((0, 128 - HYENA_EMB), (0, 0)))
        o_hy = []
        for n, blk0 in ((L, 0), (CTX, T_LAT // CTX)):
            if n == CTX and not need_ctx:
                continue
            chi, clo, shi, slo, feats = dft[n]
            e, o, knq = _hyfilt(feats, w1p, hy_b1[l].reshape(1, -1), hy_freq[l].reshape(1, -1), hy_w2[l],
                                hy_b2[l].reshape(1, -1), hy_w3[l], hy_b3[l].reshape(1, -1),
                                hy_decay[l].reshape(1, 4 * GW))
            kre, kim = _hyspec(chi, clo, shi, slo, e, o)
            o_hy.append(_hyena(p_hy, blk0, B, n, hy_short_w[l], hy_short_b[l].reshape(1, -1), hy_bias[l],
                               chi, shi, kre, kim, knq))

        qn, kn, vn = _naprep(p_na, jnp.tile(na_q_g[l], NH).reshape(1, GW), jnp.tile(na_k_g[l], NH).reshape(1, GW),
                             gm64_b)
        o_na = [_na(qn, kn, vn, _na_bias_table(na_rpb[l]))]

        wq, wk, wv, qg, kg = _mla_weights(mla_w_uq[l], mla_w_ukv[l], mla_q_g[l], mla_k_g[l])
        mq, mk, mv = _mlaprep(p_mla, mla_q_a_g[l].reshape(1, -1), mla_kv_a_g[l].reshape(1, -1), wq, wk, wv,
                              qg, kg, gm128_b, pm, cos_t, sin_t)
        o_mla = [_attn_latent(mq, mk, mv, 128, 64, 256)]
        if need_ctx:
            o_na.append(_attn_ctx(qn, kn, vn, 64, 64))
            o_mla.append(_attn_ctx(mq, mk, mv, 128, 64))
        else:
            o_hy.append(o_hy[0])
            o_na.append(o_na[0])
            o_mla.append(o_mla[0])
        n_tiles = N_TILES if need_ctx else LAT_TILES

        wr = jnp.pad(jnp.concatenate([moe_wg[l], moe_we[l]], axis=1), ((0, 0), (0, 128 - N_GROUPS - N_EXPERTS)))
        wr_hi, wr_lo = _split2(wr)
        X1, h2, logits = _outproj(n_tiles, X, o_f, o_b, p_hg, o_hy, o_na, o_mla, mod_l,
                                  jnp.tile(hgrn_norm_g[l], NH).reshape(1, GW), gm64_b,
                                  w_out[l].astype(BF16), norm2_g[l].reshape(1, D),
                                  jnp.concatenate([wr_hi, wr_lo], axis=1))

        eid, wts, pos, counts = _route(logits, moe_bg[l], moe_be[l])
        block_e, nblk, dest, pads = _dispatch_tables(eid, pos, counts)
        xs = _dispatch(n_tiles, block_e.shape[0] * TMOE, dest, pads, h2)
        ys = _experts(l, block_e, nblk, xs, moe_w_gate, moe_w_up, moe_w_down)
        pending = (dest, X1, wts, mod_l, ys)

    return _combine(LAT_TILES, *pending).reshape(B, L, D)
```

```python
import functools
import math

import jax
import jax.numpy as jnp
from jax import lax
from jax.experimental import pallas as pl
from jax.experimental.pallas import tpu as pltpu

F32 = jnp.float32
BF16 = jnp.bfloat16

D = 1024
B = 8
L = 2048
CTX = 256
DEPTH = 4
GRID_W = 64
EPS = 1e-6
GW = 256
NH = 4
HYENA_BANDS = 16
HYENA_EMB = 1 + 2 * HYENA_BANDS
HYENA_FFN = 64
NA_ROWS = 8
NA_COLS = 16
MLA_Q_RANK = 256
MLA_KV_RANK = 128
MLA_NOPE = 64
MLA_ROPE = 32
MLA_QK = MLA_NOPE + MLA_ROPE
ROPE_BASE = 10000.0
N_GROUPS = 4
EPG = 8
N_EXPERTS = N_GROUPS * EPG
D_EXPERT = 512
D_IN = 3232
D_IN_PAD = 3328

T_LAT = B * L
T_CTX = B * CTX
T_ALL = T_LAT + T_CTX

TM = 256
N_TILES = T_ALL // TM
LAT_TILES = T_LAT // TM
TILES_PER_SEQ = L // TM
HCHUNK = 64
HSUB = 16
HGRN_SAFE_DECAY = 80.0
TMOE = 256
VMEM_LIMIT_BYTES = 56 * 1024 * 1024
NEG = -1e30

HI = lax.Precision.HIGHEST


def _cparams(sem, vmem=VMEM_LIMIT_BYTES):
    return pltpu.CompilerParams(dimension_semantics=sem, vmem_limit_bytes=vmem)


def _seg_of_tile(i):
    return jnp.where(i < LAT_TILES, i // TILES_PER_SEQ, B)


def _dot(a, b):
    return jnp.dot(a, b, preferred_element_type=F32)


def _dot_nt(a, b):
    return lax.dot_general(a, b, (((1,), (1,)), ((), ())), preferred_element_type=F32)


def _dot_tn(a, b):
    return lax.dot_general(a, b, (((0,), (0,)), ((), ())), preferred_element_type=F32)


def _split2(x):
    hi = x.astype(BF16)
    lo = (x - hi.astype(F32)).astype(BF16)
    return hi, lo


def _split3(x):
    h1 = x.astype(BF16)
    r1 = x - h1.astype(F32)
    h2 = r1.astype(BF16)
    h3 = (r1 - h2.astype(F32)).astype(BF16)
    return h1, h2, h3


LANE = 128
ROW_CH = D // LANE


def _store_rowtiles(ref, val):
    n = val.shape[0]
    for j in range(ROW_CH):
        ref[pl.ds(j, n, stride=ROW_CH), :] = val[:, j * LANE:(j + 1) * LANE]


def _load_rowtiles(ref, n):
    return jnp.concatenate([ref[pl.ds(j, n, stride=ROW_CH), :] for j in range(ROW_CH)], axis=-1)


def _group_sum(x, gm):
    hi, lo = _split2(x)
    return _dot(hi, gm) + _dot(lo, gm)


def _ada_kernel(c_ref, w_ref, b_ref, o_ref):
    cc = c_ref[...]
    sc = cc * jax.nn.sigmoid(cc)
    o_ref[0] = jnp.dot(sc, w_ref[0], preferred_element_type=F32, precision=HI) + b_ref[0]


def _adaln(cmat, w_ada, b_ada):
    tn = 1536
    return pl.pallas_call(
        _ada_kernel,
        grid=(DEPTH, 6 * D // tn),
        in_specs=[
            pl.BlockSpec((16, D), lambda l, j: (0, 0)),
            pl.BlockSpec((1, D, tn), lambda l, j: (l, 0, j)),
            pl.BlockSpec((1, 1, tn), lambda l, j: (l, 0, j)),
        ],
        out_specs=pl.BlockSpec((1, 16, tn), lambda l, j: (l, 0, j)),
        out_shape=jax.ShapeDtypeStruct((DEPTH, 16, 6 * D), F32),
        compiler_params=_cparams(("arbitrary", "arbitrary")),
        name="adaln",
    )(cmat, w_ada, b_ada.reshape(DEPTH, 1, 6 * D))


IN_WIDTHS = (1280, 768, 768, 512)


def _inproj_tile(x, mod_ref, g_ref, w_ref, outs):
    ms = jnp.mean(x * x, axis=-1, keepdims=True)
    y = x * lax.rsqrt(ms + EPS) * g_ref[...]
    h = y * (1.0 + mod_ref[1:2, :]) + mod_ref[0:1, :]
    p = _dot(h.astype(BF16), w_ref[...])
    c0 = 0
    for o_ref, w in zip(outs, IN_WIDTHS):
        o_ref[...] = p[:, c0:c0 + w]
        c0 += w


def _inproj_kernel(x_ref, mod_ref, g_ref, w_ref, o_hg, o_hy, o_na, o_mla):
    _inproj_tile(x_ref[...], mod_ref, g_ref, w_ref, (o_hg, o_hy, o_na, o_mla))


def _inproj_out_specs(index_map):
    specs = [pl.BlockSpec((TM, w), index_map) for w in IN_WIDTHS]
    shapes = [jax.ShapeDtypeStruct((T_ALL, w), F32) for w in IN_WIDTHS]
    return specs, shapes


def _inproj(X, mod_l, g, w_bf):
    out_specs, out_shape = _inproj_out_specs(lambda i: (i, 0))
    return pl.pallas_call(
        _inproj_kernel,
        grid=(N_TILES,),
        in_specs=[
            pl.BlockSpec((TM, D), lambda i: (i, 0)),
            pl.BlockSpec((None, 6, D), lambda i: (_seg_of_tile(i), 0, 0)),
            pl.BlockSpec((1, D), lambda i: (0, 0)),
            pl.BlockSpec((D, D_IN_PAD), lambda i: (0, 0)),
        ],
        out_specs=out_specs,
        out_shape=out_shape,
        compiler_params=_cparams(("parallel",)),
        name="inproj",
    )(X, mod_l, g, w_bf)


HSTEP = 4 * HCHUNK


def _hgrn_prologue(p_ref, r0, zcol, c_ref, crow, tri_ref, reverse):
    q = p_ref[r0:r0 + HCHUNK, 0:GW]
    z = p_ref[r0:r0 + HCHUNK, zcol:zcol + GW]
    v = p_ref[r0:r0 + HCHUNK, 3 * GW:4 * GW]
    la = c_ref[crow:crow + 1, :]
    l1 = c_ref[crow + 1:crow + 2, :]
    oml = c_ref[crow + 2:crow + 3, :]
    e = jnp.exp(-jnp.abs(z))
    ope = 1.0 + e
    ls = jnp.minimum(z, 0.0) - jnp.log(ope)
    c2 = l1 + ls
    logf = jnp.maximum(la, c2) + jnp.log(1.0 + jnp.exp(-jnp.abs(la - c2)))
    kk = oml * (jnp.where(z >= 0.0, e, 1.0) / ope)
    h1, h2, h3 = _split3(logf)
    tri_full = tri_ref[1]
    bfull = _dot(tri_full, h1) + _dot(tri_full, h2) + _dot(tri_full, h3)
    half = HCHUNK // 2
    first, mid, last = (HCHUNK - 1, half, 0) if reverse else (0, half - 1, HCHUNK - 1)
    btot = bfull[last:last + 1]
    bmid = bfull[mid:mid + 1]
    worst = jnp.maximum(bfull[first:first + 1] - bmid, bmid - btot)
    return dict(r0=r0, q=q, kk=kk, v=v, splits=(h1, h2, h3), bfull=bfull, btot=btot, bmid=bmid, worst=worst)


def _hgrn_fast(c, st, gm, reverse):
    q, kk, v, bfull, btot, bmid = c["q"], c["kk"], c["v"], c["bfull"], c["btot"], c["bmid"]
    lane_head = lax.broadcasted_iota(jnp.int32, (HCHUNK, GW), 1) // 64
    qi = (q * jnp.exp(bfull - bmid)).astype(BF16)
    ke = (kk * jnp.exp(bmid - bfull)).astype(BF16)
    qx = jnp.concatenate([jnp.where(lane_head == h, qi, jnp.zeros_like(qi)) for h in range(NH)], axis=0)
    a = _dot_nt(qx, ke)
    t_idx = lax.broadcasted_iota(jnp.int32, a.shape, 0) % HCHUNK
    s_idx = lax.broadcasted_iota(jnp.int32, a.shape, 1)
    seen = (s_idx >= t_idx) if reverse else (s_idx <= t_idx)
    a = jnp.where(seen, a, 0.0).astype(BF16)
    vb = v.astype(BF16)
    o_all = _dot(a, vb)
    o = _dot_nt((q * jnp.exp(bfull)).astype(BF16), st.astype(BF16))
    for h in range(NH):
        o = o + jnp.where(lane_head == h, o_all[h * HCHUNK:(h + 1) * HCHUNK, :], 0.0)
    kd = (kk * jnp.exp(btot - bfull)).astype(BF16)
    return o, st * jnp.exp(btot) + _dot_tn(vb, kd) * gm


def _hgrn_slow(c, st, gm, gmb, tri_sub, reverse, o_ref):
    q, kk, v = c["q"], c["kk"], c["v"]
    h1, h2, h3 = c["splits"]
    bsub = _dot(tri_sub, h1) + _dot(tri_sub, h2) + _dot(tri_sub, h3)
    row = lax.broadcasted_iota(jnp.int32, (HSUB, GW), 0)
    order = range(HCHUNK // HSUB - 1, -1, -1) if reverse else range(HCHUNK // HSUB)
    for blk in order:
        r0 = blk * HSUB
        b_i = bsub[r0:r0 + HSUB]
        q_i = q[r0:r0 + HSUB]
        k_i = kk[r0:r0 + HSUB]
        v_i = v[r0:r0 + HSUB]
        bt_i = b_i[0:1] if reverse else b_i[HSUB - 1:HSUB]
        qe = (q_i * jnp.exp(b_i)).astype(BF16)
        o_inter = _dot_nt(qe, st.astype(BF16))
        parts = []
        for tl in range(HSUB):
            dlt = b_i[tl:tl + 1] - b_i
            valid = (row >= tl) if reverse else (row <= tl)
            w = jnp.exp(jnp.where(valid, dlt, NEG))
            parts.append((q_i[tl:tl + 1] * w) * k_i)
        pmat = jnp.concatenate(parts, axis=0).astype(BF16)
        abar = _dot(pmat, gmb)
        o_diag = jnp.sum(abar.reshape(HSUB, HSUB, GW) * v_i[None], axis=1)
        o_ref[c["r0"] + r0:c["r0"] + r0 + HSUB, :] = o_inter + o_diag
        kd = (k_i * jnp.exp(bt_i - b_i)).astype(BF16)
        upd = _dot_tn(v_i.astype(BF16), kd)
        st = st * jnp.exp(bt_i) + upd * gm
    return st


def _hgrn_kernel(pf_ref, pb_ref, c_ref, trif_ref, trib_ref, gm_ref, gmb_ref, of_ref, ob_ref, stf, stb):
    @pl.when(pl.program_id(1) == 0)
    def _():
        stf[...] = jnp.zeros_like(stf)
        stb[...] = jnp.zeros_like(stb)

    gm = gm_ref[...]
    dirs = []
    for p_ref, o_ref, st_ref, tri_ref, zcol, crow, reverse in (
            (pf_ref, of_ref, stf, trif_ref, GW, 0, False), (pb_ref, ob_ref, stb, trib_ref, 2 * GW, 3, True)):
        offs = tuple(range(0, HSTEP, HCHUNK))
        offs = offs[::-1] if reverse else offs
        chunks = [_hgrn_prologue(p_ref, r0, zcol, c_ref, crow, tri_ref, reverse) for r0 in offs]
        dirs.append((o_ref, st_ref, tri_ref, reverse, chunks))
    worst = functools.reduce(jnp.maximum, [c["worst"] for d in dirs for c in d[4]])
    safe = jnp.max(worst) < HGRN_SAFE_DECAY

    @pl.when(safe)
    def _():
        for o_ref, st_ref, tri_ref, reverse, chunks in dirs:
            st = st_ref[...]
            for c in chunks:
                o, st = _hgrn_fast(c, st, gm, reverse)
                o_ref[c["r0"]:c["r0"] + HCHUNK, :] = o
            st_ref[...] = st

    @pl.when(jnp.logical_not(safe))
    def _():
        gmb = gmb_ref[...]
        for o_ref, st_ref, tri_ref, reverse, chunks in dirs:
            st = st_ref[...]
            for c in chunks:
                st = _hgrn_slow(c, st, gm, gmb, tri_ref[0], reverse, o_ref)
            st_ref[...] = st


def _hgrn_block(b, n, reverse):
    nctx = CTX // HSTEP
    nlat = L // HSTEP
    jc = (nctx - 1 - n) if reverse else n
    jl = (nlat - 1 - (n - nctx)) if reverse else (n - nctx)
    return jnp.where(n < nctx, T_LAT // HSTEP + b * nctx + jc, b * nlat + jl)


def _hgrn(p_hg, consts, trif, trib, gm, gmb):
    nsteps = (CTX + L) // HSTEP
    full = lambda shape: pl.BlockSpec(shape, lambda b, n: (0,) * len(shape))
    return pl.pallas_call(
        _hgrn_kernel,
        grid=(B, nsteps),
        in_specs=[
            pl.BlockSpec((HSTEP, 1280), lambda b, n: (_hgrn_block(b, n, False), 0)),
            pl.BlockSpec((HSTEP, 1280), lambda b, n: (_hgrn_block(b, n, True), 0)),
            full((8, GW)),
            full((2, HCHUNK, HCHUNK)),
            full((2, HCHUNK, HCHUNK)),
            full((GW, GW)),
            full((GW, GW)),
        ],
        out_specs=[
            pl.BlockSpec((HSTEP, GW), lambda b, n: (_hgrn_block(b, n, False), 0)),
            pl.BlockSpec((HSTEP, GW), lambda b, n: (_hgrn_block(b, n, True), 0)),
        ],
        out_shape=[jax.ShapeDtypeStruct((T_ALL, GW), F32), jax.ShapeDtypeStruct((T_ALL, GW), F32)],
        scratch_shapes=[pltpu.VMEM((GW, GW), F32), pltpu.VMEM((GW, GW), F32)],
        compiler_params=_cparams(("arbitrary", "arbitrary")),
        name="hgrn",
    )(p_hg, p_hg, consts, trif, trib, gm, gmb)


def _alt_sum(x):
    n, c = x.shape
    sgn = jnp.where((lax.broadcasted_iota(jnp.int32, (n, c), 0) & 1) == 0, 1.0, -1.0)
    return jnp.sum(x * sgn, axis=0, keepdims=True)


def _hyfilt_kernel(feats_ref, w1_ref, b1_ref, fr_ref, w2_ref, b2_ref, w3_ref, b3_ref, dec_ref,
                   e_ref, o_ref, nq_ref):
    fr = fr_ref[...]
    feats = feats_ref[...]
    h = jnp.sin(fr * (jnp.dot(feats, w1_ref[...], preferred_element_type=F32, precision=HI) + b1_ref[...]))
    h = jnp.sin(fr * (jnp.dot(h, w2_ref[...], preferred_element_type=F32, precision=HI) + b2_ref[...]))
    filt = jnp.dot(h, w3_ref[...], preferred_element_type=F32, precision=HI) + b3_ref[...]
    filt = filt * jnp.exp(-feats[:, 0:1] * dec_ref[...])
    n = filt.shape[0]
    row = lax.broadcasted_iota(jnp.int32, (n, GW), 0)
    for o in range(2):
        fwd = filt[:, (2 * o) * GW:(2 * o + 1) * GW]
        bwd = jnp.where(row >= 1, filt[:, (2 * o + 1) * GW:(2 * o + 2) * GW], 0.0)
        ssq = jnp.sum(fwd * fwd + bwd * bwd, axis=0, keepdims=True)
        scale = lax.rsqrt(ssq + EPS)
        ev = (fwd + bwd) * scale
        e_ref[:, o * GW:(o + 1) * GW] = ev
        o_ref[:, o * GW:(o + 1) * GW] = (fwd - bwd) * scale
        nq_ref[:, o * GW:(o + 1) * GW] = _alt_sum(ev) * (0.5 / n)


def _hyfilt(feats, w1p, b1, fr, w2, b2, w3, b3, dec):
    n = feats.shape[0]
    return pl.pallas_call(
        _hyfilt_kernel,
        out_shape=[jax.ShapeDtypeStruct((n, 2 * GW), F32), jax.ShapeDtypeStruct((n, 2 * GW), F32),
                   jax.ShapeDtypeStruct((1, 2 * GW), F32)],
        compiler_params=_cparams(None),
        name="hyfilt",
    )(feats, w1p, b1, fr, w2, b2, w3, b3, dec)


def _hyspec_kernel(chi_ref, clo_ref, shi_ref, slo_ref, e_ref, o_ref, kre_ref, kim_ref, *, n):
    eh, el = _split2(e_ref[...])
    oh, ol = _split2(o_ref[...])
    kre = _dot(chi_ref[...], eh) + _dot(chi_ref[...], el) + _dot(clo_ref[...], eh)
    kim = _dot(shi_ref[...], oh) + _dot(shi_ref[...], ol) + _dot(slo_ref[...], oh)
    tr = kre.shape[0]
    grow = lax.broadcasted_iota(jnp.int32, kre.shape, 0) + pl.program_id(0) * tr
    s2 = 1.0 / n
    kre_ref[...] = kre * jnp.where(grow == 0, 0.5 * s2, s2)
    kim_ref[...] = kim * s2


def _hyspec(chi, clo, shi, slo, e, o):
    n = e.shape[0]
    tr = min(256, n)
    rows = pl.BlockSpec((tr, n), lambda i: (i, 0))
    full = pl.BlockSpec((n, 2 * GW), lambda i: (0, 0))
    outb = pl.BlockSpec((tr, 2 * GW), lambda i: (i, 0))
    return pl.pallas_call(
        functools.partial(_hyspec_kernel, n=n),
        grid=(n // tr,),
        in_specs=[rows, rows, rows, rows, full, full],
        out_specs=[outb, outb],
        out_shape=[jax.ShapeDtypeStruct((n, 2 * GW), F32)] * 2,
        compiler_params=_cparams(("parallel",)),
        name="hyspec",
    )(chi, clo, shi, slo, e, o)


def _hyena_kernel(u_ref, sw_ref, sb_ref, db_ref, c_ref, s_ref, kre_ref, kim_ref, knq_ref, o_ref,
                  z_scr, zb_scr, y_scr):
    n = u_ref.shape[0]
    ft = min(512, n)
    rc = min(256, n)
    nchunks = n // rc
    lrow = lax.broadcasted_iota(jnp.int32, (rc, GW), 0)
    sgn = jnp.where((lrow & 1) == 0, 1.0, -1.0)

    def short_conv(part, c):
        sl = slice(part * GW, (part + 1) * GW)
        r0 = c * rc
        u = u_ref[r0:r0 + rc, sl]
        prev = u_ref[r0 - 1:r0, sl] if c > 0 else jnp.zeros((1, GW), F32)
        nxt = u_ref[r0 + rc:r0 + rc + 1, sl] if c < nchunks - 1 else jnp.zeros((1, GW), F32)
        up = jnp.where(lrow == 0, prev, pltpu.roll(u, 1, 0))
        un = jnp.where(lrow == rc - 1, nxt, pltpu.roll(u, rc - 1, 0))
        return sw_ref[0:1, sl] * up + sw_ref[1:2, sl] * u + sw_ref[2:3, sl] * un + sb_ref[:, sl]

    for c in range(nchunks):
        z_scr[c * rc:(c + 1) * rc, :] = short_conv(0, c)
    for o in range(2):
        cols = slice(o * GW, (o + 1) * GW)
        znq = jnp.zeros((1, GW), F32)
        for c in range(nchunks):
            zc = z_scr[c * rc:(c + 1) * rc, :]
            zb_scr[c * rc:(c + 1) * rc, :] = zc.astype(BF16)
            znq = znq + jnp.sum(zc * sgn, axis=0, keepdims=True)
        ynq = znq * knq_ref[:, cols]
        for c in range(nchunks):
            y_scr[c * rc:(c + 1) * rc, :] = sgn * ynq
        for f in range(n // ft):
            rs = slice(f * ft, (f + 1) * ft)
            zre = _dot(c_ref[rs, :], zb_scr[...])
            zim = _dot(s_ref[rs, :], zb_scr[...])
            kre = kre_ref[rs, cols]
            kim = kim_ref[rs, cols]
            yre = (zre * kre - zim * kim).astype(BF16)
            yim = (zre * kim + zim * kre).astype(BF16)
            y_scr[...] += _dot(c_ref[:, rs], yre) + _dot(s_ref[:, rs], yim)
        dst = o_ref if o == 1 else z_scr
        for c in range(nchunks):
            rows = slice(c * rc, (c + 1) * rc)
            dst[rows, :] = short_conv(o + 1, c) * (y_scr[rows, :] + db_ref[o:o + 1, :] * z_scr[rows, :])


def _hyena(u, blk0, nb, n, sw, sb, db, cm, sm, kre, kim, knq):
    whole = pl.BlockSpec(memory_space=pltpu.VMEM)
    return pl.pallas_call(
        _hyena_kernel,
        grid=(nb,),
        in_specs=[
            pl.BlockSpec((n, 3 * GW), lambda b: (blk0 + b, 0)),
            pl.BlockSpec((3, 3 * GW), lambda b: (0, 0)),
            pl.BlockSpec((1, 3 * GW), lambda b: (0, 0)),
            pl.BlockSpec((2, GW), lambda b: (0, 0)),
            whole, whole, whole, whole, whole,
        ],
        out_specs=pl.BlockSpec((n, GW), lambda b: (b, 0)),
        out_shape=jax.ShapeDtypeStruct((nb * n, GW), F32),
        scratch_shapes=[pltpu.VMEM((n, GW), F32), pltpu.VMEM((n, GW), BF16), pltpu.VMEM((n, GW), F32)],
        compiler_params=_cparams(("arbitrary",)),
        name="hyena",
    )(u, sw, sb, db, cm, sm, kre, kim, knq)


def _dft_consts(n):
    kk = jnp.arange(n, dtype=jnp.int32)
    ph = (kk[:, None] * kk[None, :]) % (2 * n)
    ang = ph.astype(F32) * (math.pi / n)
    return jnp.cos(ang), -jnp.sin(ang)


def _hyena_feats(n):
    t = jnp.arange(n, dtype=F32)
    t_unit = jnp.linspace(0.0, 1.0, n, dtype=F32)
    bands = jnp.linspace(1e-4, HYENA_BANDS - 1, HYENA_BANDS, dtype=F32)
    ang = (2.0 * math.pi / n) * t[:, None] * bands[None, :]
    feats = jnp.concatenate([t_unit[:, None], jnp.cos(ang), -jnp.sin(ang)], axis=-1)
    return jnp.pad(feats, ((0, 0), (0, 128 - HYENA_EMB)))


def _naprep_kernel(p_ref, qg_ref, kg_ref, gm_ref, q_ref, k_ref, v_ref):
    p = p_ref[...]
    q = p[:, 0:GW]
    k = p[:, GW:2 * GW]
    gm = gm_ref[...]
    qn = q * lax.rsqrt(_group_sum(q * q, gm) * (1.0 / 64) + EPS) * qg_ref[...]
    kn = k * lax.rsqrt(_group_sum(k * k, gm) * (1.0 / 64) + EPS) * kg_ref[...]
    q_ref[...] = (qn * (64 ** -0.5)).astype(BF16)
    k_ref[...] = kn.astype(BF16)
    v_ref[...] = p[:, 2 * GW:3 * GW].astype(BF16)


def _naprep(p_na, qg, kg, gmb):
    tok = lambda w: pl.BlockSpec((TM, w), lambda i: (i, 0))
    full = lambda shape: pl.BlockSpec(shape, lambda i: (0,) * len(shape))
    return pl.pallas_call(
        _naprep_kernel,
        grid=(N_TILES,),
        in_specs=[tok(768), full((1, GW)), full((1, GW)), full((GW, GW))],
        out_specs=[tok(GW), tok(GW), tok(GW)],
        out_shape=[jax.ShapeDtypeStruct((T_ALL, GW), BF16)] * 3,
        compiler_params=_cparams(("parallel",)),
        name="naprep",
    )(p_na, qg, kg, gmb)


NA_RPS = 4


def _na_kernel(q_ref, k_ref, v_ref, kc_ref, vc_ref, bias_ref, o_ref):
    rows = L // GRID_W
    kc = kc_ref[...]
    vc = vc_ref[...]
    for j in range(NA_RPS):
        r = pl.program_id(1) * NA_RPS + j
        rs = jnp.clip(r - NA_ROWS // 2, 0, rows - NA_ROWS)
        variant = r - rs
        start = pl.multiple_of(rs * GRID_W, GRID_W)
        kw = k_ref[pl.ds(start, NA_ROWS * GRID_W), :]
        vw = v_ref[pl.ds(start, NA_ROWS * GRID_W), :]
        q = q_ref[j * GRID_W:(j + 1) * GRID_W, :]
        lane_head = lax.broadcasted_iota(jnp.int32, (GRID_W, GW), 1) // 64
        hmask = [lane_head == h for h in range(NH)]
        qx = jnp.concatenate([jnp.where(hmask[h], q, jnp.zeros_like(q)) for h in range(NH)], axis=0)
        s_loc = _dot_nt(qx, kw) + bias_ref[variant]
        s_ctx = _dot_nt(qx, kc)
        m = jnp.maximum(jnp.max(s_loc, axis=-1, keepdims=True), jnp.max(s_ctx, axis=-1, keepdims=True))
        p_loc = jnp.exp(s_loc - m)
        p_ctx = jnp.exp(s_ctx - m)
        den = jnp.sum(p_loc, axis=-1, keepdims=True) + jnp.sum(p_ctx, axis=-1, keepdims=True)
        o_all = (_dot(p_loc.astype(BF16), vw) + _dot(p_ctx.astype(BF16), vc)) / den
        o = jnp.zeros((GRID_W, GW), F32)
        for h in range(NH):
            o = o + jnp.where(hmask[h], o_all[h * GRID_W:(h + 1) * GRID_W, :], 0.0)
        o_ref[j * GRID_W:(j + 1) * GRID_W, :] = o


def _na(qn, kn, vn, bias_t):
    steps = L // GRID_W // NA_RPS
    tq = NA_RPS * GRID_W
    ctx_blk = T_LAT // CTX
    return pl.pallas_call(
        _na_kernel,
        grid=(B, steps),
        in_specs=[
            pl.BlockSpec((tq, GW), lambda b, r: (b * steps + r, 0)),
            pl.BlockSpec((L, GW), lambda b, r: (b, 0)),
            pl.BlockSpec((L, GW), lambda b, r: (b, 0)),
            pl.BlockSpec((CTX, GW), lambda b, r: (ctx_blk + b, 0)),
            pl.BlockSpec((CTX, GW), lambda b, r: (ctx_blk + b, 0)),
            pl.BlockSpec((NA_ROWS, NH * GRID_W, NA_ROWS * GRID_W), lambda b, r: (0, 0, 0)),
        ],
        out_specs=pl.BlockSpec((tq, GW), lambda b, r: (b * steps + r, 0)),
        out_shape=jax.ShapeDtypeStruct((T_LAT, GW), F32),
        compiler_params=_cparams(("arbitrary", "arbitrary")),
        name="na",
    )(qn, kn, vn, kn, vn, bias_t)


def _na_bias_table(rpb):
    cq = jnp.arange(GRID_W)
    cs = jnp.clip(cq - NA_COLS // 2, 0, GRID_W - NA_COLS)
    col_ok = (cq[None, :] >= cs[:, None]) & (cq[None, :] < cs[:, None] + NA_COLS)
    dc = jnp.clip(cq[None, :] - cq[:, None] + (NA_COLS - 1), 0, 2 * NA_COLS - 2)
    onehot = (dc[:, :, None] == jnp.arange(2 * NA_COLS - 1)[None, None, :]).astype(F32)
    full = jnp.einsum('qkc,hrc->hrqk', onehot, rpb.astype(F32), precision=HI)
    full = jnp.where(col_ok[None, None], full, NEG)
    tab = jnp.stack([full[:, NA_ROWS - 1 - a:2 * NA_ROWS - 1 - a] for a in range(NA_ROWS)], axis=0)
    return tab.transpose(0, 1, 3, 2, 4).reshape(NA_ROWS, NH * GRID_W, NA_ROWS * GRID_W)


def _attn_kernel(*refs, nkv, dq, dv):
    q = refs[0][...]
    ks = [refs[1 + 2 * j][...] for j in range(nkv)]
    vs = [refs[2 + 2 * j][...] for j in range(nkv)]
    o_ref = refs[1 + 2 * nkv]
    outs = []
    for h in range(NH):
        qh = q[:, h * dq:(h + 1) * dq]
        ss = [_dot_nt(qh, k[:, h * dq:(h + 1) * dq]) for k in ks]
        m = functools.reduce(jnp.maximum, [jnp.max(s, axis=-1, keepdims=True) for s in ss])
        ps = [jnp.exp(s - m) for s in ss]
        den = functools.reduce(lambda a, b2: a + b2, [jnp.sum(p, axis=-1, keepdims=True) for p in ps])
        o = functools.reduce(lambda a, b2: a + b2,
                             [_dot(p.astype(BF16), v[:, h * dv:(h + 1) * dv]) for p, v in zip(ps, vs)])
        outs.append(o / den)
    o_ref[...] = jnp.concatenate(outs, axis=-1)


def _attn_latent(q, k, v, dq, dv, tq):
    nq = L // tq
    ctx_blk = T_LAT // CTX
    return pl.pallas_call(
        functools.partial(_attn_kernel, nkv=2, dq=dq, dv=dv),
        grid=(B, nq),
        in_specs=[
            pl.BlockSpec((tq, NH * dq), lambda b, i: (b * nq + i, 0)),
            pl.BlockSpec((L, NH * dq), lambda b, i: (b, 0)),
            pl.BlockSpec((L, NH * dv), lambda b, i: (b, 0)),
            pl.BlockSpec((CTX, NH * dq), lambda b, i: (ctx_blk + b, 0)),
            pl.BlockSpec((CTX, NH * dv), lambda b, i: (ctx_blk + b, 0)),
        ],
        out_specs=pl.BlockSpec((tq, NH * dv), lambda b, i: (b * nq + i, 0)),
        out_shape=jax.ShapeDtypeStruct((T_LAT, NH * dv), F32),
        compiler_params=_cparams(("arbitrary", "arbitrary")),
        name="attn_latent",
    )(q, k, v, k, v)


def _attn_ctx(q, k, v, dq, dv):
    ctx_blk = T_LAT // CTX
    return pl.pallas_call(
        functools.partial(_attn_kernel, nkv=1, dq=dq, dv=dv),
        grid=(B,),
        in_specs=[
            pl.BlockSpec((CTX, NH * dq), lambda b: (ctx_blk + b, 0)),
            pl.BlockSpec((CTX, NH * dq), lambda b: (ctx_blk + b, 0)),
            pl.BlockSpec((CTX, NH * dv), lambda b: (ctx_blk + b, 0)),
        ],
        out_specs=pl.BlockSpec((CTX, NH * dv), lambda b: (b, 0)),
        out_shape=jax.ShapeDtypeStruct((T_CTX, NH * dv), F32),
        compiler_params=_cparams(("arbitrary",)),
        name="attn_ctx",
    )(q, k, v)


def _mlaprep_kernel(p_ref, qag_ref, kvag_ref, wq_ref, wk_ref, wv_ref, qg_ref, kg_ref,
                    cos_ref, sin_ref, q_ref, k_ref, v_ref):
    p = p_ref[...]
    cq = p[:, 0:MLA_Q_RANK]
    ckv = p[:, MLA_Q_RANK:MLA_Q_RANK + MLA_KV_RANK]
    krp = p[:, MLA_Q_RANK + MLA_KV_RANK:]
    cqn = cq * lax.rsqrt(jnp.mean(cq * cq, axis=-1, keepdims=True) + EPS) * qag_ref[...]
    ckvn = ckv * lax.rsqrt(jnp.mean(ckv * ckv, axis=-1, keepdims=True) + EPS) * kvag_ref[...]
    ckvb = ckvn.astype(BF16)
    q = _dot(cqn.astype(BF16), wq_ref[...])
    k = _dot(jnp.concatenate([ckvb, krp.astype(BF16)], axis=-1), wk_ref[...])
    v = _dot(ckvb, wv_ref[...])

    def head_norm(x, g):
        outs = []
        for h in range(NH):
            xh = x[:, h * LANE:(h + 1) * LANE]
            ms = jnp.sum(xh * xh, axis=-1, keepdims=True) * (1.0 / MLA_QK)
            outs.append(xh * lax.rsqrt(ms + EPS))
        return jnp.concatenate(outs, axis=-1) * g

    cos = cos_ref[...]
    sin = sin_ref[...]
    half = MLA_ROPE // 2
    lane = lax.broadcasted_iota(jnp.int32, cos.shape, 1)
    first = ((lane % LANE - MLA_NOPE) & (half - 1)) < half // 2

    def rope(x):
        width = x.shape[1]
        partner = jnp.where(first, pltpu.roll(x, width - half // 2, 1), pltpu.roll(x, half // 2, 1))
        return x * cos + partner * sin

    q = rope(head_norm(q, qg_ref[...]))
    k = rope(head_norm(k, kg_ref[...]))
    q_ref[...] = (q * (MLA_QK ** -0.5)).astype(BF16)
    k_ref[...] = k.astype(BF16)
    v_ref[...] = v.astype(BF16)


def _mlaprep(p_mla, qag, kvag, wq, wk, wv, qg, kg, cos_t, sin_t):
    tok = lambda w: pl.BlockSpec((TM, w), lambda i: (i, 0))
    full = lambda shape: pl.BlockSpec(shape, lambda i: (0,) * len(shape))
    pos = pl.BlockSpec((TM, 512), lambda i: (jnp.where(i < LAT_TILES, i % TILES_PER_SEQ, TILES_PER_SEQ), 0))
    return pl.pallas_call(
        _mlaprep_kernel,
        grid=(N_TILES,),
        in_specs=[tok(512), full((1, 256)), full((1, 128)), full((256, 512)), full((256, 512)),
                  full((128, 256)), full((1, 512)), full((1, 512)), pos, pos],
        out_specs=[tok(512), tok(512), tok(GW)],
        out_shape=[jax.ShapeDtypeStruct((T_ALL, 512), BF16), jax.ShapeDtypeStruct((T_ALL, 512), BF16),
                   jax.ShapeDtypeStruct((T_ALL, GW), BF16)],
        compiler_params=_cparams(("parallel",)),
        name="mlaprep",
    )(p_mla, qag, kvag, wq, wk, wv, qg, kg, cos_t, sin_t)


def _rope_tables():
    t = jnp.arange(L)
    rowp = (t // GRID_W).astype(F32)
    colp = (t % GRID_W).astype(F32)
    half = MLA_ROPE // 2
    inv = ROPE_BASE ** (-jnp.arange(0, half, 2, dtype=F32) / half)
    j = jnp.arange(MLA_ROPE)
    pos = jnp.where(j[None, :] < half, rowp[:, None], colp[:, None])
    ang = pos * inv[j % (half // 2)][None, :]
    first = (j % half) < (half // 2)
    cos32 = jnp.cos(ang)
    sin32 = jnp.where(first[None, :], -jnp.sin(ang), jnp.sin(ang))
    cos_h = jnp.concatenate([jnp.ones((L, MLA_NOPE), F32), cos32, jnp.ones((L, 32), F32)], axis=-1)
    sin_h = jnp.concatenate([jnp.zeros((L, MLA_NOPE), F32), sin32, jnp.zeros((L, 32), F32)], axis=-1)
    cos_t = jnp.concatenate([jnp.tile(cos_h, (1, NH)), jnp.ones((TM, 512), F32)], axis=0)
    sin_t = jnp.concatenate([jnp.tile(sin_h, (1, NH)), jnp.zeros((TM, 512), F32)], axis=0)
    return cos_t, sin_t


def _outproj_kernel(x_ref, of_ref, ob_ref, g_ref, hyl_ref, hyc_ref, nal_ref, nac_ref, mll_ref, mlc_ref,
                    mod_ref, ng_ref, gm_ref, w_ref, n2_ref, wr_ref, x1_ref, h2_ref, lg_ref):
    oa = of_ref[...] + ob_ref[...]
    ms = _group_sum(oa * oa, gm_ref[...]) * (1.0 / 64)
    g = g_ref[...]
    oa = oa * lax.rsqrt(ms + EPS) * ng_ref[...] * (g * jax.nn.sigmoid(g))
    lat = pl.program_id(0) < LAT_TILES
    hy = jnp.where(lat, hyl_ref[...], hyc_ref[...])
    na = jnp.where(lat, nal_ref[...], nac_ref[...])
    mla = jnp.where(lat, mll_ref[...], mlc_ref[...])
    mix = jnp.concatenate([oa, hy, na, mla], axis=-1).astype(BF16)
    x1 = x_ref[...] + mod_ref[2:3, :] * _dot(mix, w_ref[...])
    x1_ref[...] = x1
    ms2 = jnp.mean(x1 * x1, axis=-1, keepdims=True)
    h2 = x1 * lax.rsqrt(ms2 + EPS) * n2_ref[...] * (1.0 + mod_ref[4:5, :]) + mod_ref[3:4, :]
    _store_rowtiles(h2_ref, h2)
    hh, hl = _split2(h2)
    wr = wr_ref[...]
    lg_ref[...] = _dot(hh, wr[:, 0:128]) + _dot(hl, wr[:, 0:128]) + _dot(hh, wr[:, 128:256])


def _outproj(n_tiles, X, o_f, o_b, p_hg, hy, na, mla, mod_l, ng, gmb, w_bf, n2g, wr):
    tok = lambda w: pl.BlockSpec((TM, w), lambda i: (i, 0))
    full = lambda shape: pl.BlockSpec(shape, lambda i: (0,) * len(shape))
    latb = pl.BlockSpec((TM, GW), lambda i: (jnp.minimum(i, LAT_TILES - 1), 0))
    ctxb = pl.BlockSpec((TM, GW), lambda i: (jnp.maximum(i - LAT_TILES, 0), 0))
    nt = n_tiles * TM
    return pl.pallas_call(
        _outproj_kernel,
        grid=(n_tiles,),
        in_specs=[tok(D), tok(GW), tok(GW), pl.BlockSpec((TM, GW), lambda i: (i, 4)),
                  latb, ctxb, latb, ctxb, latb, ctxb,
                  pl.BlockSpec((None, 6, D), lambda i: (_seg_of_tile(i), 0, 0)),
                  full((1, GW)), full((GW, GW)), full((D, D)), full((1, D)), full((D, 256))],
        out_specs=[tok(D), pl.BlockSpec((TM * ROW_CH, LANE), lambda i: (i, 0)), tok(128)],
        out_shape=[jax.ShapeDtypeStruct((nt, D), F32), jax.ShapeDtypeStruct((nt * ROW_CH, LANE), F32),
                   jax.ShapeDtypeStruct((nt, 128), F32)],
        compiler_params=_cparams(("parallel",)),
        name="outproj",
    )(X, o_f, o_b, p_hg, hy[0], hy[1], na[0], na[1], mla[0], mla[1], mod_l, ng, gmb, w_bf, n2g, wr)


def _route_kernel(lg_ref, b_ref, ltri_ref, o_ref, cnt_ref, base):
    @pl.when(pl.program_id(0) == 0)
    def _():
        base[...] = jnp.zeros_like(base)

    l = lg_ref[...] + b_ref[...]
    lane_i = lax.broadcasted_iota(jnp.int32, l.shape, 1)
    lane = lane_i.astype(F32)

    def first_max(mask):
        v = jnp.max(jnp.where(mask, l, NEG), axis=-1, keepdims=True)
        i = jnp.min(jnp.where(mask & (l == v), lane, float(LANE)), axis=-1, keepdims=True)
        return v, i

    gmask = lane_i < N_GROUPS
    mg, g_sel = first_max(gmask)
    p_sel = 1.0 / jnp.sum(jnp.where(gmask, jnp.exp(jnp.where(gmask, l - mg, 0.0)), 0.0), axis=-1, keepdims=True)
    lane_group = jnp.right_shift(lane_i - N_GROUPS, EPG.bit_length() - 1).astype(F32)
    emask = (lane_i >= N_GROUPS) & (lane_i < N_GROUPS + N_EXPERTS) & (lane_group == g_sel)
    v1, i1 = first_max(emask)
    v2, i2 = first_max(emask & (lane != i1))
    r = jnp.exp(v2 - v1)
    w1 = p_sel / (1.0 + r)
    w2 = w1 * r
    hit1 = lane == i1
    hit2 = lane == i2
    cnt = jnp.where(hit1 | hit2, 1.0, 0.0)
    before = _dot(ltri_ref[...], cnt.astype(BF16)) + base[...]
    pos1 = jnp.sum(jnp.where(hit1, before, 0.0), axis=-1, keepdims=True)
    pos2 = jnp.sum(jnp.where(hit2, before, 0.0), axis=-1, keepdims=True)
    base[...] = base[...] + jnp.sum(cnt, axis=0, keepdims=True)
    cnt_ref[...] = jnp.broadcast_to(base[...], cnt_ref.shape)
    cols = [i1 - N_GROUPS, i2 - N_GROUPS, w1, w2, pos1, pos2]
    out = jnp.zeros(l.shape, F32)
    for j, c in enumerate(cols):
        out = jnp.where(lane_i == j, c, out)
    o_ref[...] = out


def _route(logits, bg, be):
    t = logits.shape[0]
    bias = jnp.pad(jnp.concatenate([bg, be]), (0, LANE - N_GROUPS - N_EXPERTS)).reshape(1, LANE)
    ltri = (jnp.arange(TM)[None, :] < jnp.arange(TM)[:, None]).astype(BF16)
    out, cnt = pl.pallas_call(
        _route_kernel,
        grid=(t // TM,),
        in_specs=[pl.BlockSpec((TM, LANE), lambda i: (i, 0)), pl.BlockSpec((1, LANE), lambda i: (0, 0)),
                  pl.BlockSpec((TM, TM), lambda i: (0, 0))],
        out_specs=[pl.BlockSpec((TM, LANE), lambda i: (i, 0)), pl.BlockSpec((8, LANE), lambda i: (0, 0))],
        out_shape=[jax.ShapeDtypeStruct((t, LANE), F32), jax.ShapeDtypeStruct((8, LANE), F32)],
        scratch_shapes=[pltpu.VMEM((1, LANE), F32)],
        compiler_params=_cparams(("arbitrary",)),
        name="route",
    )(logits, bias, ltri)
    eid = out[:, 0:2].astype(jnp.int32)
    pos = out[:, 4:6].astype(jnp.int32)
    counts = cnt[0, N_GROUPS:N_GROUPS + N_EXPERTS].astype(jnp.int32)
    return eid, out[:, 2:4], pos, counts


def _dispatch_tables(eid, pos, counts):
    t = eid.shape[0]
    n = 2 * t
    nb = n // TMOE + N_EXPERTS
    flat_e = eid.reshape(n)
    pos = pos.reshape(n)
    onehot = (flat_e[:, None] == jnp.arange(N_EXPERTS, dtype=jnp.int32)[None, :]).astype(jnp.int32)
    pcounts = ((counts + TMOE - 1) // TMOE) * TMOE
    pends = jnp.cumsum(pcounts)
    pstarts = pends - pcounts
    dest = (jnp.sum(jnp.where(onehot > 0, pstarts[None, :], 0), axis=1) + pos).astype(jnp.int32)
    blk_start = jnp.arange(nb, dtype=jnp.int32) * TMOE
    block_e = jnp.minimum(jnp.sum((pends[None, :] <= blk_start[:, None]).astype(jnp.int32), axis=1),
                          N_EXPERTS - 1).astype(jnp.int32)
    nblk = (pends[-1] // TMOE).astype(jnp.int32).reshape(1)
    pads = jnp.concatenate([pstarts + counts, pcounts - counts, nblk]).astype(jnp.int32)
    return block_e, nblk, dest, pads


PAD_PIECES = tuple(1 << b for b in range(TMOE.bit_length() - 1))


def _rows(tok, n):
    if isinstance(tok, int):
        return pl.ds(tok * ROW_CH, n * ROW_CH)
    return pl.ds(pl.multiple_of(tok * ROW_CH, ROW_CH), n * ROW_CH)


def _dispatch_kernel(dest_ref, pad_ref, h_ref, xs_out, zbuf, sem, zsem):
    base = pl.program_id(0) * (2 * TM)

    @pl.when(pl.program_id(0) == 0)
    def _():
        zbuf[...] = jnp.zeros_like(zbuf)
        ztok = zbuf.shape[0] // ROW_CH
        for phase in range(2):
            for e in range(N_EXPERTS):
                off = pad_ref[e]
                npad = pad_ref[N_EXPERTS + e]
                for piece in PAD_PIECES:
                    has = (npad & piece) != 0

                    @pl.when(has)
                    def _(off=off, piece=piece):
                        cp = pltpu.make_async_copy(zbuf.at[pl.ds(0, piece * ROW_CH)], xs_out.at[_rows(off, piece)], zsem)
                        cp.start() if phase == 0 else cp.wait()

                    off = off + jnp.where(has, piece, 0)

            first = pad_ref[2 * N_EXPERTS] * (TMOE // ztok)

            def tail(j, carry):
                cp = pltpu.make_async_copy(zbuf, xs_out.at[_rows(j * ztok, ztok)], zsem)
                cp.start() if phase == 0 else cp.wait()
                return carry

            lax.fori_loop(first, xs_out.shape[0] // zbuf.shape[0], tail, 0)

    def issue(r, carry):
        for k in range(2):
            pltpu.make_async_copy(h_ref.at[_rows(r, 1)], xs_out.at[_rows(dest_ref[base + 2 * r + k], 1)], sem).start()
        return carry

    lax.fori_loop(0, TM, issue, 0, unroll=8)
    pltpu.make_async_copy(xs_out.at[_rows(0, 2 * TM)], xs_out.at[_rows(0, 2 * TM)], sem).wait()


def _dispatch(n_tiles, n_slots, dest, pads, h2):
    grid_spec = pltpu.PrefetchScalarGridSpec(
        num_scalar_prefetch=2,
        grid=(n_tiles,),
        in_specs=[pl.BlockSpec((TM * ROW_CH, LANE), lambda i, dst, pd: (i, 0))],
        out_specs=pl.BlockSpec(memory_space=pl.ANY),
        scratch_shapes=[pltpu.VMEM((TMOE // 2 * ROW_CH, LANE), F32), pltpu.SemaphoreType.DMA(()),
                        pltpu.SemaphoreType.DMA(())],
    )
    return pl.pallas_call(
        _dispatch_kernel,
        grid_spec=grid_spec,
        out_shape=jax.ShapeDtypeStruct((n_slots * ROW_CH, LANE), F32),
        compiler_params=_cparams(("arbitrary",)),
        name="dispatch",
    )(dest, pads, h2)


def _experts_kernel(be_ref, nblk_ref, xs_ref, wg_ref, wu_ref, wd_ref, ys_ref, wgb, wub, wdb):
    i = pl.program_id(0)

    @pl.when((i == 0) | (be_ref[i] != be_ref[jnp.maximum(i - 1, 0)]))
    def _():
        wgb[...] = wg_ref[...].astype(BF16)
        wub[...] = wu_ref[...].astype(BF16)
        wdb[...] = wd_ref[...].astype(BF16)

    @pl.when(i < nblk_ref[0])
    def _():
        x = _load_rowtiles(xs_ref, TMOE).astype(BF16)
        gate = _dot(x, wgb[...])
        up = _dot(x, wub[...])
        act = (gate * jax.nn.sigmoid(gate)) * up
        _store_rowtiles(ys_ref, _dot(act.astype(BF16), wdb[...]))

    @pl.when(i >= nblk_ref[0])
    def _():
        ys_ref[...] = jnp.zeros_like(ys_ref)


def _experts(layer, block_e, nblk, xs, w_gate, w_up, w_down):
    nb = block_e.shape[0]
    used = lambda i, nk: jnp.minimum(i, nk[0] - 1)
    grid_spec = pltpu.PrefetchScalarGridSpec(
        num_scalar_prefetch=2,
        grid=(nb,),
        in_specs=[
            pl.BlockSpec((TMOE * ROW_CH, LANE), lambda i, be, nk: (used(i, nk), 0)),
            pl.BlockSpec((None, None, D, D_EXPERT), lambda i, be, nk: (layer, be[i], 0, 0)),
            pl.BlockSpec((None, None, D, D_EXPERT), lambda i, be, nk: (layer, be[i], 0, 0)),
            pl.BlockSpec((None, None, D_EXPERT, D), lambda i, be, nk: (layer, be[i], 0, 0)),
        ],
        out_specs=pl.BlockSpec((TMOE * ROW_CH, LANE), lambda i, be, nk: (i, 0)),
        scratch_shapes=[pltpu.VMEM((D, D_EXPERT), BF16), pltpu.VMEM((D, D_EXPERT), BF16),
                        pltpu.VMEM((D_EXPERT, D), BF16)],
    )
    return pl.pallas_call(
        _experts_kernel,
        grid_spec=grid_spec,
        out_shape=jax.ShapeDtypeStruct(xs.shape, F32),
        compiler_params=_cparams(("arbitrary",)),
        name="experts",
    )(block_e, nblk, xs, w_gate, w_up, w_down)


def _combine_tile(dest_ref, x_ref, w_ref, mod_ref, ys_hbm, ybuf, sem):
    i = pl.program_id(0)

    def fetch(tile, slot):
        base = tile * (2 * TM)

        def issue(r, carry):
            for k in range(2):
                pltpu.make_async_copy(ys_hbm.at[_rows(dest_ref[base + 2 * r + k], 1)],
                                      ybuf.at[slot, k, _rows(r, 1)], sem.at[slot]).start()
            return carry

        lax.fori_loop(0, TM, issue, 0, unroll=8)

    @pl.when(i == 0)
    def _():
        fetch(0, 0)

    @pl.when(i + 1 < pl.num_programs(0))
    def _():
        fetch(i + 1, (i + 1) % 2)

    slot = i % 2
    for k in range(2):
        pltpu.make_async_copy(ys_hbm.at[_rows(0, TM)], ybuf.at[slot, k], sem.at[slot]).wait()
    w = w_ref[...]
    y = w[:, 0:1] * _load_rowtiles(ybuf.at[slot, 0], TM) + w[:, 1:2] * _load_rowtiles(ybuf.at[slot, 1], TM)
    return x_ref[...] + mod_ref[5:6, :] * y


def _combine_kernel(dest_ref, x_ref, w_ref, mod_ref, ys_hbm, o_ref, ybuf, sem):
    o_ref[...] = _combine_tile(dest_ref, x_ref, w_ref, mod_ref, ys_hbm, ybuf, sem)


def _combine_in_specs():
    return [pl.BlockSpec((TM, D), lambda i, dst: (i, 0)),
            pl.BlockSpec((TM, 2), lambda i, dst: (i, 0)),
            pl.BlockSpec((None, 6, D), lambda i, dst: (_seg_of_tile(i), 0, 0)),
            pl.BlockSpec(memory_space=pl.ANY)]


COMBINE_SCRATCH = [pltpu.VMEM((2, 2, TM * ROW_CH, LANE), F32), pltpu.SemaphoreType.DMA((2,))]


def _combine(n_tiles, dest, X1, wts, mod_l, ys):
    grid_spec = pltpu.PrefetchScalarGridSpec(
        num_scalar_prefetch=1,
        grid=(n_tiles,),
        in_specs=_combine_in_specs(),
        out_specs=pl.BlockSpec((TM, D), lambda i, dst: (i, 0)),
        scratch_shapes=COMBINE_SCRATCH,
    )
    return pl.pallas_call(
        _combine_kernel,
        grid_spec=grid_spec,
        out_shape=jax.ShapeDtypeStruct((n_tiles * TM, D), F32),
        compiler_params=_cparams(("arbitrary",)),
        name="combine",
    )(dest, X1, wts, mod_l, ys)


def _combine_inproj_kernel(dest_ref, x_ref, w_ref, modp_ref, ys_hbm, mod_ref, g_ref, win_ref,
                           x_out, o_hg, o_hy, o_na, o_mla, ybuf, sem):
    x = _combine_tile(dest_ref, x_ref, w_ref, modp_ref, ys_hbm, ybuf, sem)
    x_out[...] = x
    _inproj_tile(x, mod_ref, g_ref, win_ref, (o_hg, o_hy, o_na, o_mla))


def _combine_inproj(dest, X1, wts, mod_prev, ys, mod_l, g, w_bf):
    p_specs, p_shapes = _inproj_out_specs(lambda i, dst: (i, 0))
    grid_spec = pltpu.PrefetchScalarGridSpec(
        num_scalar_prefetch=1,
        grid=(N_TILES,),
        in_specs=_combine_in_specs() + [
            pl.BlockSpec((None, 6, D), lambda i, dst: (_seg_of_tile(i), 0, 0)),
            pl.BlockSpec((1, D), lambda i, dst: (0, 0)),
            pl.BlockSpec((D, D_IN_PAD), lambda i, dst: (0, 0)),
        ],
        out_specs=[pl.BlockSpec((TM, D), lambda i, dst: (i, 0))] + p_specs,
        scratch_shapes=COMBINE_SCRATCH,
    )
    return pl.pallas_call(
        _combine_inproj_kernel,
        grid_spec=grid_spec,
        out_shape=[jax.ShapeDtypeStruct((T_ALL, D), F32)] + p_shapes,
        compiler_params=_cparams(("arbitrary",)),
        name="combine_inproj",
    )(dest, X1, wts, mod_prev, ys, mod_l, g, w_bf)


def _group_mask(width, group):
    lane = jnp.arange(width)
    return (lane[:, None] // group == lane[None, :] // group)


def _hgrn_tri(reverse):
    t = jnp.arange(HCHUNK)
    same = (t[:, None] // HSUB) == (t[None, :] // HSUB)
    order = (t[None, :] >= t[:, None]) if reverse else (t[None, :] <= t[:, None])
    return jnp.stack([same & order, order]).astype(BF16)


def _mla_weights(w_uq, w_ukv, q_g, k_g):
    wq = jnp.pad(w_uq.reshape(MLA_Q_RANK, NH, MLA_QK), ((0, 0), (0, 0), (0, 128 - MLA_QK))).reshape(MLA_Q_RANK, 512)
    kv = w_ukv.reshape(MLA_KV_RANK, NH, MLA_NOPE + 64)
    wk_top = jnp.pad(kv[:, :, :MLA_NOPE], ((0, 0), (0, 0), (0, 128 - MLA_NOPE))).reshape(MLA_KV_RANK, 512)
    lane = jnp.arange(512)
    src = jnp.arange(128)
    place = ((lane[None, :] % 128) == (src[:, None] + MLA_NOPE)) & (src[:, None] < MLA_ROPE)
    wk = jnp.concatenate([wk_top, place.astype(F32)], axis=0)
    wv = kv[:, :, MLA_NOPE:].reshape(MLA_KV_RANK, GW)
    pad_g = lambda g: jnp.tile(jnp.pad(g, (0, 128 - MLA_QK)), NH).reshape(1, 512)
    return wq.astype(BF16), wk.astype(BF16), wv.astype(BF16), pad_g(q_g), pad_g(k_g)


def kernel(x, c, ctx, c_ctx, w_ada, b_ada, norm1_g, norm2_g, w_in, w_out, hgrn_lb_logits, hgrn_norm_g,
           hy_short_w, hy_short_b, hy_w1, hy_b1, hy_freq, hy_w2, hy_b2, hy_w3, hy_b3, hy_decay, hy_bias,
           na_rpb, na_q_g, na_k_g, mla_q_a_g, mla_kv_a_g, mla_w_uq, mla_w_ukv, mla_q_g, mla_k_g,
           moe_wg, moe_bg, moe_we, moe_be, moe_w_gate, moe_w_up, moe_w_down):
    X = jnp.concatenate([x.reshape(T_LAT, D), ctx.reshape(T_CTX, D)], axis=0)
    cmat = jnp.concatenate([c, c_ctx[None, :], jnp.zeros((16 - B - 1, D), F32)], axis=0)
    mod = _adaln(cmat, w_ada, b_ada).reshape(DEPTH, 16, 6, D)

    lb_cum = jnp.cumsum(jax.nn.softmax(hgrn_lb_logits.astype(F32), axis=0), axis=0)
    lower = lb_cum - lb_cum[0:1]

    gm64 = _group_mask(GW, 64)
    gm64_f = gm64.astype(F32)
    gm64_b = gm64.astype(BF16)
    trif = _hgrn_tri(False)
    trib = _hgrn_tri(True)
    cos_t, sin_t = _rope_tables()
    dft = {}
    for n in (L, CTX):
        cm, sm = _dft_consts(n)
        chi, clo = _split2(cm)
        shi, slo = _split2(sm)
        dft[n] = (chi, clo, shi, slo, _hyena_feats(n))

    pending = None
    for l in range(DEPTH):
        mod_l = mod[l]
        w_in_l = jnp.pad(w_in[l], ((0, 0), (0, D_IN_PAD - D_IN))).astype(BF16)
        if pending is None:
            p_hg, p_hy, p_na, p_mla = _inproj(X, mod_l, norm1_g[l].reshape(1, D), w_in_l)
        else:
            X, p_hg, p_hy, p_na, p_mla = _combine_inproj(*pending, mod_l, norm1_g[l].reshape(1, D), w_in_l)

        lb = lower[l]
        hconst = jnp.concatenate([
            jnp.stack([jnp.maximum(jnp.log(lb[d]), NEG), jnp.log1p(-lb[d]), 1.0 - lb[d]]) for d in range(2)
        ] + [jnp.zeros((2, GW), F32)], axis=0)
        o_f, o_b = _hgrn(p_hg, hconst, trif, trib, gm64_f, gm64_b)

        need_ctx = l < DEPTH - 1
        w1p = jnp.pad(hy_w1[l], ((0, 128 - HYENA_EMB), (0, 0)))
        o_hy = []
        for n, blk0 in ((L, 0), (CTX, T_LAT // CTX)):
            if n == CTX and not need_ctx:
                continue
            chi, clo, shi, slo, feats = dft[n]
            e, o, knq = _hyfilt(feats, w1p, hy_b1[l].reshape(1, -1), hy_freq[l].reshape(1, -1), hy_w2[l],
                                hy_b2[l].reshape(1, -1), hy_w3[l], hy_b3[l].reshape(1, -1),
                                hy_decay[l].reshape(1, 4 * GW))
            kre, kim = _hyspec(chi, clo, shi, slo, e, o)
            o_hy.append(_hyena(p_hy, blk0, B, n, hy_short_w[l], hy_short_b[l].reshape(1, -1), hy_bias[l],
                               chi, shi, kre, kim, knq))

        qn, kn, vn = _naprep(p_na, jnp.tile(na_q_g[l], NH).reshape(1, GW), jnp.tile(na_k_g[l], NH).reshape(1, GW),
                             gm64_b)
        o_na = [_na(qn, kn, vn, _na_bias_table(na_rpb[l]))]

        wq, wk, wv, qg, kg = _mla_weights(mla_w_uq[l], mla_w_ukv[l], mla_q_g[l], mla_k_g[l])
        mq, mk, mv = _mlaprep(p_mla, mla_q_a_g[l].reshape(1, -1), mla_kv_a_g[l].reshape(1, -1), wq, wk, wv,
                              qg, kg, cos_t, sin_t)
        o_mla = [_attn_latent(mq, mk, mv, 128, 64, 256)]
        if need_ctx:
            o_na.append(_attn_ctx(qn, kn, vn, 64, 64))
            o_mla.append(_attn_ctx(mq, mk, mv, 128, 64))
        else:
            o_hy.append(o_hy[0])
            o_na.append(o_na[0])
            o_mla.append(o_mla[0])
        n_tiles = N_TILES if need_ctx else LAT_TILES

        wr = jnp.pad(jnp.concatenate([moe_wg[l], moe_we[l]], axis=1), ((0, 0), (0, 128 - N_GROUPS - N_EXPERTS)))
        wr_hi, wr_lo = _split2(wr)
        X1, h2, logits = _outproj(n_tiles, X, o_f, o_b, p_hg, o_hy, o_na, o_mla, mod_l,
                                  jnp.tile(hgrn_norm_g[l], NH).reshape(1, GW), gm64_b,
                                  w_out[l].astype(BF16), norm2_g[l].reshape(1, D),
                                  jnp.concatenate([wr_hi, wr_lo], axis=1))

        eid, wts, pos, counts = _route(logits, moe_bg[l], moe_be[l])
        block_e, nblk, dest, pads = _dispatch_tables(eid, pos, counts)
        xs = _dispatch(n_tiles, block_e.shape[0] * TMOE, dest, pads, h2)
        ys = _experts(l, block_e, nblk, xs, moe_w_gate, moe_w_up, moe_w_down)
        pending = (dest, X1, wts, mod_l, ys)

    return _combine(LAT_TILES, *pending).reshape(B, L, D)
```

```python
import functools
import math

import jax
import jax.numpy as jnp
from jax import lax
from jax.experimental import pallas as pl
from jax.experimental.pallas import tpu as pltpu

F32 = jnp.float32
BF16 = jnp.bfloat16

D = 1024
B = 8
L = 2048
CTX = 256
DEPTH = 4
GRID_W = 64
EPS = 1e-6
GW = 256
NH = 4
HYENA_BANDS = 16
HYENA_EMB = 1 + 2 * HYENA_BANDS
HYENA_FFN = 64
NA_ROWS = 8
NA_COLS = 16
MLA_Q_RANK = 256
MLA_KV_RANK = 128
MLA_NOPE = 64
MLA_ROPE = 32
MLA_QK = MLA_NOPE + MLA_ROPE
ROPE_BASE = 10000.0
N_GROUPS = 4
EPG = 8
N_EXPERTS = N_GROUPS * EPG
D_EXPERT = 512
D_IN = 3232
D_IN_PAD = 3328

T_LAT = B * L
T_CTX = B * CTX
T_ALL = T_LAT + T_CTX

TM = 256
N_TILES = T_ALL // TM
LAT_TILES = T_LAT // TM
TILES_PER_SEQ = L // TM
HCHUNK = 64
HSUB = 16
HGRN_SAFE_DECAY = 80.0
TMOE = 256
VMEM_LIMIT_BYTES = 56 * 1024 * 1024
NEG = -1e30

HI = lax.Precision.HIGHEST


def _cparams(sem, vmem=VMEM_LIMIT_BYTES):
    return pltpu.CompilerParams(dimension_semantics=sem, vmem_limit_bytes=vmem)


def _seg_of_tile(i):
    return jnp.where(i < LAT_TILES, i // TILES_PER_SEQ, B)


def _dot(a, b):
    return jnp.dot(a, b, preferred_element_type=F32)


def _dot_nt(a, b):
    return lax.dot_general(a, b, (((1,), (1,)), ((), ())), preferred_element_type=F32)


def _dot_tn(a, b):
    return lax.dot_general(a, b, (((0,), (0,)), ((), ())), preferred_element_type=F32)


def _split2(x):
    hi = x.astype(BF16)
    lo = (x - hi.astype(F32)).astype(BF16)
    return hi, lo


def _split3(x):
    h1 = x.astype(BF16)
    r1 = x - h1.astype(F32)
    h2 = r1.astype(BF16)
    h3 = (r1 - h2.astype(F32)).astype(BF16)
    return h1, h2, h3


LANE = 128
ROW_CH = D // LANE


def _store_rowtiles(ref, val):
    n = val.shape[0]
    for j in range(ROW_CH):
        ref[pl.ds(j, n, stride=ROW_CH), :] = val[:, j * LANE:(j + 1) * LANE]


def _load_rowtiles(ref, n):
    return jnp.concatenate([ref[pl.ds(j, n, stride=ROW_CH), :] for j in range(ROW_CH)], axis=-1)


def _group_sum(x, gm):
    hi, lo = _split2(x)
    return _dot(hi, gm) + _dot(lo, gm)


def _ada_kernel(c_ref, w_ref, b_ref, o_ref):
    cc = c_ref[...]
    sc = cc * jax.nn.sigmoid(cc)
    o_ref[0] = jnp.dot(sc, w_ref[0], preferred_element_type=F32, precision=HI) + b_ref[0]


def _adaln(cmat, w_ada, b_ada):
    tn = 1536
    return pl.pallas_call(
        _ada_kernel,
        grid=(DEPTH, 6 * D // tn),
        in_specs=[
            pl.BlockSpec((16, D), lambda l, j: (0, 0)),
            pl.BlockSpec((1, D, tn), lambda l, j: (l, 0, j)),
            pl.BlockSpec((1, 1, tn), lambda l, j: (l, 0, j)),
        ],
        out_specs=pl.BlockSpec((1, 16, tn), lambda l, j: (l, 0, j)),
        out_shape=jax.ShapeDtypeStruct((DEPTH, 16, 6 * D), F32),
        compiler_params=_cparams(("arbitrary", "arbitrary")),
        name="adaln",
    )(cmat, w_ada, b_ada.reshape(DEPTH, 1, 6 * D))


IN_WIDTHS = (1280, 768, 768, 512)


def _inproj_tile(x, mod_ref, g_ref, w_ref, outs):
    ms = jnp.mean(x * x, axis=-1, keepdims=True)
    y = x * lax.rsqrt(ms + EPS) * g_ref[...]
    h = y * (1.0 + mod_ref[1:2, :]) + mod_ref[0:1, :]
    p = _dot(h.astype(BF16), w_ref[...])
    c0 = 0
    for o_ref, w in zip(outs, IN_WIDTHS):
        o_ref[...] = p[:, c0:c0 + w]
        c0 += w


def _inproj_kernel(x_ref, mod_ref, g_ref, w_ref, o_hg, o_hy, o_na, o_mla):
    _inproj_tile(x_ref[...], mod_ref, g_ref, w_ref, (o_hg, o_hy, o_na, o_mla))


def _inproj_out_specs(index_map):
    specs = [pl.BlockSpec((TM, w), index_map) for w in IN_WIDTHS]
    shapes = [jax.ShapeDtypeStruct((T_ALL, w), F32) for w in IN_WIDTHS]
    return specs, shapes


def _inproj(X, mod_l, g, w_bf):
    out_specs, out_shape = _inproj_out_specs(lambda i: (i, 0))
    return pl.pallas_call(
        _inproj_kernel,
        grid=(N_TILES,),
        in_specs=[
            pl.BlockSpec((TM, D), lambda i: (i, 0)),
            pl.BlockSpec((None, 6, D), lambda i: (_seg_of_tile(i), 0, 0)),
            pl.BlockSpec((1, D), lambda i: (0, 0)),
            pl.BlockSpec((D, D_IN_PAD), lambda i: (0, 0)),
        ],
        out_specs=out_specs,
        out_shape=out_shape,
        compiler_params=_cparams(("parallel",)),
        name="inproj",
    )(X, mod_l, g, w_bf)


HSTEP = 4 * HCHUNK


def _hgrn_prologue(p_ref, r0, zcol, c_ref, crow, tri_ref, reverse):
    q = p_ref[r0:r0 + HCHUNK, 0:GW]
    z = p_ref[r0:r0 + HCHUNK, zcol:zcol + GW]
    v = p_ref[r0:r0 + HCHUNK, 3 * GW:4 * GW]
    la = c_ref[crow:crow + 1, :]
    l1 = c_ref[crow + 1:crow + 2, :]
    oml = c_ref[crow + 2:crow + 3, :]
    e = jnp.exp(-jnp.abs(z))
    ope = 1.0 + e
    ls = jnp.minimum(z, 0.0) - jnp.log(ope)
    c2 = l1 + ls
    logf = jnp.maximum(la, c2) + jnp.log(1.0 + jnp.exp(-jnp.abs(la - c2)))
    kk = oml * (jnp.where(z >= 0.0, e, 1.0) / ope)
    h1, h2, h3 = _split3(logf)
    tri_full = tri_ref[1]
    bfull = _dot(tri_full, h1) + _dot(tri_full, h2) + _dot(tri_full, h3)
    half = HCHUNK // 2
    first, mid, last = (HCHUNK - 1, half, 0) if reverse else (0, half - 1, HCHUNK - 1)
    btot = bfull[last:last + 1]
    bmid = bfull[mid:mid + 1]
    worst = jnp.maximum(bfull[first:first + 1] - bmid, bmid - btot)
    return dict(r0=r0, q=q, kk=kk, v=v, splits=(h1, h2, h3), bfull=bfull, btot=btot, bmid=bmid, worst=worst)


def _hgrn_fast(c, st, gm, reverse):
    q, kk, v, bfull, btot, bmid = c["q"], c["kk"], c["v"], c["bfull"], c["btot"], c["bmid"]
    lane_head = lax.broadcasted_iota(jnp.int32, (HCHUNK, GW), 1) // 64
    qi = (q * jnp.exp(bfull - bmid)).astype(BF16)
    ke = (kk * jnp.exp(bmid - bfull)).astype(BF16)
    qx = jnp.concatenate([jnp.where(lane_head == h, qi, jnp.zeros_like(qi)) for h in range(NH)], axis=0)
    a = _dot_nt(qx, ke)
    t_idx = lax.broadcasted_iota(jnp.int32, a.shape, 0) % HCHUNK
    s_idx = lax.broadcasted_iota(jnp.int32, a.shape, 1)
    seen = (s_idx >= t_idx) if reverse else (s_idx <= t_idx)
    a = jnp.where(seen, a, 0.0).astype(BF16)
    vb = v.astype(BF16)
    o_all = _dot(a, vb)
    o = _dot_nt((q * jnp.exp(bfull)).astype(BF16), st.astype(BF16))
    for h in range(NH):
        o = o + jnp.where(lane_head == h, o_all[h * HCHUNK:(h + 1) * HCHUNK, :], 0.0)
    kd = (kk * jnp.exp(btot - bfull)).astype(BF16)
    return o, st * jnp.exp(btot) + _dot_tn(vb, kd) * gm


def _hgrn_slow(c, st, gm, gmb, tri_sub, reverse, o_ref):
    q, kk, v = c["q"], c["kk"], c["v"]
    h1, h2, h3 = c["splits"]
    bsub = _dot(tri_sub, h1) + _dot(tri_sub, h2) + _dot(tri_sub, h3)
    row = lax.broadcasted_iota(jnp.int32, (HSUB, GW), 0)
    order = range(HCHUNK // HSUB - 1, -1, -1) if reverse else range(HCHUNK // HSUB)
    for blk in order:
        r0 = blk * HSUB
        b_i = bsub[r0:r0 + HSUB]
        q_i = q[r0:r0 + HSUB]
        k_i = kk[r0:r0 + HSUB]
        v_i = v[r0:r0 + HSUB]
        bt_i = b_i[0:1] if reverse else b_i[HSUB - 1:HSUB]
        qe = (q_i * jnp.exp(b_i)).astype(BF16)
        o_inter = _dot_nt(qe, st.astype(BF16))
        parts = []
        for tl in range(HSUB):
            dlt = b_i[tl:tl + 1] - b_i
            valid = (row >= tl) if reverse else (row <= tl)
            w = jnp.exp(jnp.where(valid, dlt, NEG))
            parts.append((q_i[tl:tl + 1] * w) * k_i)
        pmat = jnp.concatenate(parts, axis=0).astype(BF16)
        abar = _dot(pmat, gmb)
        o_diag = jnp.sum(abar.reshape(HSUB, HSUB, GW) * v_i[None], axis=1)
        o_ref[c["r0"] + r0:c["r0"] + r0 + HSUB, :] = o_inter + o_diag
        kd = (k_i * jnp.exp(bt_i - b_i)).astype(BF16)
        upd = _dot_tn(v_i.astype(BF16), kd)
        st = st * jnp.exp(bt_i) + upd * gm
    return st


def _hgrn_kernel(pf_ref, pb_ref, c_ref, trif_ref, trib_ref, gm_ref, gmb_ref, of_ref, ob_ref, stf, stb):
    @pl.when(pl.program_id(1) == 0)
    def _():
        stf[...] = jnp.zeros_like(stf)
        stb[...] = jnp.zeros_like(stb)

    gm = gm_ref[...]
    dirs = []
    for p_ref, o_ref, st_ref, tri_ref, zcol, crow, reverse in (
            (pf_ref, of_ref, stf, trif_ref, GW, 0, False), (pb_ref, ob_ref, stb, trib_ref, 2 * GW, 3, True)):
        offs = tuple(range(0, HSTEP, HCHUNK))
        offs = offs[::-1] if reverse else offs
        chunks = [_hgrn_prologue(p_ref, r0, zcol, c_ref, crow, tri_ref, reverse) for r0 in offs]
        dirs.append((o_ref, st_ref, tri_ref, reverse, chunks))
    worst = functools.reduce(jnp.maximum, [c["worst"] for d in dirs for c in d[4]])
    safe = jnp.max(worst) < HGRN_SAFE_DECAY

    @pl.when(safe)
    def _():
        for o_ref, st_ref, tri_ref, reverse, chunks in dirs:
            st = st_ref[...]
            for c in chunks:
                o, st = _hgrn_fast(c, st, gm, reverse)
                o_ref[c["r0"]:c["r0"] + HCHUNK, :] = o
            st_ref[...] = st

    @pl.when(jnp.logical_not(safe))
    def _():
        gmb = gmb_ref[...]
        for o_ref, st_ref, tri_ref, reverse, chunks in dirs:
            st = st_ref[...]
            for c in chunks:
                st = _hgrn_slow(c, st, gm, gmb, tri_ref[0], reverse, o_ref)
            st_ref[...] = st


def _hgrn_block(b, n, reverse):
    nctx = CTX // HSTEP
    nlat = L // HSTEP
    jc = (nctx - 1 - n) if reverse else n
    jl = (nlat - 1 - (n - nctx)) if reverse else (n - nctx)
    return jnp.where(n < nctx, T_LAT // HSTEP + b * nctx + jc, b * nlat + jl)


def _hgrn(p_hg, consts, trif, trib, gm, gmb):
    nsteps = (CTX + L) // HSTEP
    full = lambda shape: pl.BlockSpec(shape, lambda b, n: (0,) * len(shape))
    return pl.pallas_call(
        _hgrn_kernel,
        grid=(B, nsteps),
        in_specs=[
            pl.BlockSpec((HSTEP, 1280), lambda b, n: (_hgrn_block(b, n, False), 0)),
            pl.BlockSpec((HSTEP, 1280), lambda b, n: (_hgrn_block(b, n, True), 0)),
            full((8, GW)),
            full((2, HCHUNK, HCHUNK)),
            full((2, HCHUNK, HCHUNK)),
            full((GW, GW)),
            full((GW, GW)),
        ],
        out_specs=[
            pl.BlockSpec((HSTEP, GW), lambda b, n: (_hgrn_block(b, n, False), 0)),
            pl.BlockSpec((HSTEP, GW), lambda b, n: (_hgrn_block(b, n, True), 0)),
        ],
        out_shape=[jax.ShapeDtypeStruct((T_ALL, GW), F32), jax.ShapeDtypeStruct((T_ALL, GW), F32)],
        scratch_shapes=[pltpu.VMEM((GW, GW), F32), pltpu.VMEM((GW, GW), F32)],
        compiler_params=_cparams(("arbitrary", "arbitrary")),
        name="hgrn",
    )(p_hg, p_hg, consts, trif, trib, gm, gmb)


def _alt_sum(x):
    n, c = x.shape
    sgn = jnp.where((lax.broadcasted_iota(jnp.int32, (n, c), 0) & 1) == 0, 1.0, -1.0)
    return jnp.sum(x * sgn, axis=0, keepdims=True)


def _hyfilt_kernel(feats_ref, w1_ref, b1_ref, fr_ref, w2_ref, b2_ref, w3_ref, b3_ref, dec_ref,
                   e_ref, o_ref, nq_ref):
    fr = fr_ref[...]
    feats = feats_ref[...]
    h = jnp.sin(fr * (jnp.dot(feats, w1_ref[...], preferred_element_type=F32, precision=HI) + b1_ref[...]))
    h = jnp.sin(fr * (jnp.dot(h, w2_ref[...], preferred_element_type=F32, precision=HI) + b2_ref[...]))
    filt = jnp.dot(h, w3_ref[...], preferred_element_type=F32, precision=HI) + b3_ref[...]
    filt = filt * jnp.exp(-feats[:, 0:1] * dec_ref[...])
    n = filt.shape[0]
    row = lax.broadcasted_iota(jnp.int32, (n, GW), 0)
    for o in range(2):
        fwd = filt[:, (2 * o) * GW:(2 * o + 1) * GW]
        bwd = jnp.where(row >= 1, filt[:, (2 * o + 1) * GW:(2 * o + 2) * GW], 0.0)
        ssq = jnp.sum(fwd * fwd + bwd * bwd, axis=0, keepdims=True)
        scale = lax.rsqrt(ssq + EPS)
        ev = (fwd + bwd) * scale
        e_ref[:, o * GW:(o + 1) * GW] = ev
        o_ref[:, o * GW:(o + 1) * GW] = (fwd - bwd) * scale
        nq_ref[:, o * GW:(o + 1) * GW] = _alt_sum(ev) * (0.5 / n)


def _hyfilt(feats, w1p, b1, fr, w2, b2, w3, b3, dec):
    n = feats.shape[0]
    return pl.pallas_call(
        _hyfilt_kernel,
        out_shape=[jax.ShapeDtypeStruct((n, 2 * GW), F32), jax.ShapeDtypeStruct((n, 2 * GW), F32),
                   jax.ShapeDtypeStruct((1, 2 * GW), F32)],
        compiler_params=_cparams(None),
        name="hyfilt",
    )(feats, w1p, b1, fr, w2, b2, w3, b3, dec)


def _hyspec_kernel(chi_ref, clo_ref, shi_ref, slo_ref, e_ref, o_ref, kre_ref, kim_ref, *, n):
    eh, el = _split2(e_ref[...])
    oh, ol = _split2(o_ref[...])
    kre = _dot(chi_ref[...], eh) + _dot(chi_ref[...], el) + _dot(clo_ref[...], eh)
    kim = _dot(shi_ref[...], oh) + _dot(shi_ref[...], ol) + _dot(slo_ref[...], oh)
    tr = kre.shape[0]
    grow = lax.broadcasted_iota(jnp.int32, kre.shape, 0) + pl.program_id(0) * tr
    s2 = 1.0 / n
    kre_ref[...] = kre * jnp.where(grow == 0, 0.5 * s2, s2)
    kim_ref[...] = kim * s2


def _hyspec(chi, clo, shi, slo, e, o):
    n = e.shape[0]
    tr = min(256, n)
    rows = pl.BlockSpec((tr, n), lambda i: (i, 0))
    full = pl.BlockSpec((n, 2 * GW), lambda i: (0, 0))
    outb = pl.BlockSpec((tr, 2 * GW), lambda i: (i, 0))
    return pl.pallas_call(
        functools.partial(_hyspec_kernel, n=n),
        grid=(n // tr,),
        in_specs=[rows, rows, rows, rows, full, full],
        out_specs=[outb, outb],
        out_shape=[jax.ShapeDtypeStruct((n, 2 * GW), F32)] * 2,
        compiler_params=_cparams(("parallel",)),
        name="hyspec",
    )(chi, clo, shi, slo, e, o)


def _hyena_kernel(u_ref, sw_ref, sb_ref, db_ref, c_ref, s_ref, kre_ref, kim_ref, knq_ref, o_ref,
                  z_scr, zb_scr, y_scr):
    n = u_ref.shape[0]
    ft = min(512, n)
    rc = min(256, n)
    nchunks = n // rc
    lrow = lax.broadcasted_iota(jnp.int32, (rc, GW), 0)
    sgn = jnp.where((lrow & 1) == 0, 1.0, -1.0)

    def short_conv(part, c):
        sl = slice(part * GW, (part + 1) * GW)
        r0 = c * rc
        u = u_ref[r0:r0 + rc, sl]
        prev = u_ref[r0 - 1:r0, sl] if c > 0 else jnp.zeros((1, GW), F32)
        nxt = u_ref[r0 + rc:r0 + rc + 1, sl] if c < nchunks - 1 else jnp.zeros((1, GW), F32)
        up = jnp.where(lrow == 0, prev, pltpu.roll(u, 1, 0))
        un = jnp.where(lrow == rc - 1, nxt, pltpu.roll(u, rc - 1, 0))
        return sw_ref[0:1, sl] * up + sw_ref[1:2, sl] * u + sw_ref[2:3, sl] * un + sb_ref[:, sl]

    for c in range(nchunks):
        z_scr[c * rc:(c + 1) * rc, :] = short_conv(0, c)
    for o in range(2):
        cols = slice(o * GW, (o + 1) * GW)
        znq = jnp.zeros((1, GW), F32)
        for c in range(nchunks):
            zc = z_scr[c * rc:(c + 1) * rc, :]
            zb_scr[c * rc:(c + 1) * rc, :] = zc.astype(BF16)
            znq = znq + jnp.sum(zc * sgn, axis=0, keepdims=True)
        ynq = znq * knq_ref[:, cols]
        for c in range(nchunks):
            y_scr[c * rc:(c + 1) * rc, :] = sgn * ynq
        for f in range(n // ft):
            rs = slice(f * ft, (f + 1) * ft)
            zre = _dot(c_ref[rs, :], zb_scr[...])
            zim = _dot(s_ref[rs, :], zb_scr[...])
            kre = kre_ref[rs, cols]
            kim = kim_ref[rs, cols]
            yre = (zre * kre - zim * kim).astype(BF16)
            yim = (zre * kim + zim * kre).astype(BF16)
            y_scr[...] += _dot(c_ref[:, rs], yre) + _dot(s_ref[:, rs], yim)
        dst = o_ref if o == 1 else z_scr
        for c in range(nchunks):
            rows = slice(c * rc, (c + 1) * rc)
            dst[rows, :] = short_conv(o + 1, c) * (y_scr[rows, :] + db_ref[o:o + 1, :] * z_scr[rows, :])


def _hyena(u, blk0, nb, n, sw, sb, db, cm, sm, kre, kim, knq):
    whole = pl.BlockSpec(memory_space=pltpu.VMEM)
    return pl.pallas_call(
        _hyena_kernel,
        grid=(nb,),
        in_specs=[
            pl.BlockSpec((n, 3 * GW), lambda b: (blk0 + b, 0)),
            pl.BlockSpec((3, 3 * GW), lambda b: (0, 0)),
            pl.BlockSpec((1, 3 * GW), lambda b: (0, 0)),
            pl.BlockSpec((2, GW), lambda b: (0, 0)),
            whole, whole, whole, whole, whole,
        ],
        out_specs=pl.BlockSpec((n, GW), lambda b: (b, 0)),
        out_shape=jax.ShapeDtypeStruct((nb * n, GW), F32),
        scratch_shapes=[pltpu.VMEM((n, GW), F32), pltpu.VMEM((n, GW), BF16), pltpu.VMEM((n, GW), F32)],
        compiler_params=_cparams(("arbitrary",)),
        name="hyena",
    )(u, sw, sb, db, cm, sm, kre, kim, knq)


def _dft_consts(n):
    kk = jnp.arange(n, dtype=jnp.int32)
    ph = (kk[:, None] * kk[None, :]) % (2 * n)
    ang = ph.astype(F32) * (math.pi / n)
    return jnp.cos(ang), -jnp.sin(ang)


def _hyena_feats(n):
    t = jnp.arange(n, dtype=F32)
    t_unit = jnp.linspace(0.0, 1.0, n, dtype=F32)
    bands = jnp.linspace(1e-4, HYENA_BANDS - 1, HYENA_BANDS, dtype=F32)
    ang = (2.0 * math.pi / n) * t[:, None] * bands[None, :]
    feats = jnp.concatenate([t_unit[:, None], jnp.cos(ang), -jnp.sin(ang)], axis=-1)
    return jnp.pad(feats, ((0, 0), (0, 128 - HYENA_EMB)))


def _naprep_kernel(p_ref, qg_ref, kg_ref, gm_ref, q_ref, k_ref, v_ref):
    p = p_ref[...]
    q = p[:, 0:GW]
    k = p[:, GW:2 * GW]
    gm = gm_ref[...]
    qn = q * lax.rsqrt(_group_sum(q * q, gm) * (1.0 / 64) + EPS) * qg_ref[...]
    kn = k * lax.rsqrt(_group_sum(k * k, gm) * (1.0 / 64) + EPS) * kg_ref[...]
    q_ref[...] = (qn * (64 ** -0.5)).astype(BF16)
    k_ref[...] = kn.astype(BF16)
    v_ref[...] = p[:, 2 * GW:3 * GW].astype(BF16)


def _naprep(p_na, qg, kg, gmb):
    tok = lambda w: pl.BlockSpec((TM, w), lambda i: (i, 0))
    full = lambda shape: pl.BlockSpec(shape, lambda i: (0,) * len(shape))
    return pl.pallas_call(
        _naprep_kernel,
        grid=(N_TILES,),
        in_specs=[tok(768), full((1, GW)), full((1, GW)), full((GW, GW))],
        out_specs=[tok(GW), tok(GW), tok(GW)],
        out_shape=[jax.ShapeDtypeStruct((T_ALL, GW), BF16)] * 3,
        compiler_params=_cparams(("parallel",)),
        name="naprep",
    )(p_na, qg, kg, gmb)


NA_RPS = 4


def _na_kernel(q_ref, k_ref, v_ref, kc_ref, vc_ref, bias_ref, o_ref):
    rows = L // GRID_W
    kc = kc_ref[...]
    vc = vc_ref[...]
    for j in range(NA_RPS):
        r = pl.program_id(1) * NA_RPS + j
        rs = jnp.clip(r - NA_ROWS // 2, 0, rows - NA_ROWS)
        variant = r - rs
        start = pl.multiple_of(rs * GRID_W, GRID_W)
        kw = k_ref[pl.ds(start, NA_ROWS * GRID_W), :]
        vw = v_ref[pl.ds(start, NA_ROWS * GRID_W), :]
        q = q_ref[j * GRID_W:(j + 1) * GRID_W, :]
        lane_head = lax.broadcasted_iota(jnp.int32, (GRID_W, GW), 1) // 64
        hmask = [lane_head == h for h in range(NH)]
        qx = jnp.concatenate([jnp.where(hmask[h], q, jnp.zeros_like(q)) for h in range(NH)], axis=0)
        s_loc = _dot_nt(qx, kw) + bias_ref[variant]
        s_ctx = _dot_nt(qx, kc)
        m = jnp.maximum(jnp.max(s_loc, axis=-1, keepdims=True), jnp.max(s_ctx, axis=-1, keepdims=True))
        p_loc = jnp.exp(s_loc - m)
        p_ctx = jnp.exp(s_ctx - m)
        den = jnp.sum(p_loc, axis=-1, keepdims=True) + jnp.sum(p_ctx, axis=-1, keepdims=True)
        o_all = (_dot(p_loc.astype(BF16), vw) + _dot(p_ctx.astype(BF16), vc)) / den
        o = jnp.zeros((GRID_W, GW), F32)
        for h in range(NH):
            o = o + jnp.where(hmask[h], o_all[h * GRID_W:(h + 1) * GRID_W, :], 0.0)
        o_ref[j * GRID_W:(j + 1) * GRID_W, :] = o


def _na(qn, kn, vn, bias_t):
    steps = L // GRID_W // NA_RPS
    tq = NA_RPS * GRID_W
    ctx_blk = T_LAT // CTX
    return pl.pallas_call(
        _na_kernel,
        grid=(B, steps),
        in_specs=[
            pl.BlockSpec((tq, GW), lambda b, r: (b * steps + r, 0)),
            pl.BlockSpec((L, GW), lambda b, r: (b, 0)),
            pl.BlockSpec((L, GW), lambda b, r: (b, 0)),
            pl.BlockSpec((CTX, GW), lambda b, r: (ctx_blk + b, 0)),
            pl.BlockSpec((CTX, GW), lambda b, r: (ctx_blk + b, 0)),
            pl.BlockSpec((NA_ROWS, NH * GRID_W, NA_ROWS * GRID_W), lambda b, r: (0, 0, 0)),
        ],
        out_specs=pl.BlockSpec((tq, GW), lambda b, r: (b * steps + r, 0)),
        out_shape=jax.ShapeDtypeStruct((T_LAT, GW), F32),
        compiler_params=_cparams(("arbitrary", "arbitrary")),
        name="na",
    )(qn, kn, vn, kn, vn, bias_t)


def _na_bias_table(rpb):
    cq = jnp.arange(GRID_W)
    cs = jnp.clip(cq - NA_COLS // 2, 0, GRID_W - NA_COLS)
    col_ok = (cq[None, :] >= cs[:, None]) & (cq[None, :] < cs[:, None] + NA_COLS)
    dc = jnp.clip(cq[None, :] - cq[:, None] + (NA_COLS - 1), 0, 2 * NA_COLS - 2)
    onehot = (dc[:, :, None] == jnp.arange(2 * NA_COLS - 1)[None, None, :]).astype(F32)
    full = jnp.einsum('qkc,hrc->hrqk', onehot, rpb.astype(F32), precision=HI)
    full = jnp.where(col_ok[None, None], full, NEG)
    tab = jnp.stack([full[:, NA_ROWS - 1 - a:2 * NA_ROWS - 1 - a] for a in range(NA_ROWS)], axis=0)
    return tab.transpose(0, 1, 3, 2, 4).reshape(NA_ROWS, NH * GRID_W, NA_ROWS * GRID_W)


def _attn_kernel(*refs, nkv, dq, dv):
    q = refs[0][...]
    ks = [refs[1 + 2 * j][...] for j in range(nkv)]
    vs = [refs[2 + 2 * j][...] for j in range(nkv)]
    o_ref = refs[1 + 2 * nkv]
    outs = []
    for h in range(NH):
        qh = q[:, h * dq:(h + 1) * dq]
        ss = [_dot_nt(qh, k[:, h * dq:(h + 1) * dq]) for k in ks]
        m = functools.reduce(jnp.maximum, [jnp.max(s, axis=-1, keepdims=True) for s in ss])
        ps = [jnp.exp(s - m) for s in ss]
        den = functools.reduce(lambda a, b2: a + b2, [jnp.sum(p, axis=-1, keepdims=True) for p in ps])
        o = functools.reduce(lambda a, b2: a + b2,
                             [_dot(p.astype(BF16), v[:, h * dv:(h + 1) * dv]) for p, v in zip(ps, vs)])
        outs.append(o / den)
    o_ref[...] = jnp.concatenate(outs, axis=-1)


def _attn_latent(q, k, v, dq, dv, tq):
    nq = L // tq
    ctx_blk = T_LAT // CTX
    return pl.pallas_call(
        functools.partial(_attn_kernel, nkv=2, dq=dq, dv=dv),
        grid=(B, nq),
        in_specs=[
            pl.BlockSpec((tq, NH * dq), lambda b, i: (b * nq + i, 0)),
            pl.BlockSpec((L, NH * dq), lambda b, i: (b, 0)),
            pl.BlockSpec((L, NH * dv), lambda b, i: (b, 0)),
            pl.BlockSpec((CTX, NH * dq), lambda b, i: (ctx_blk + b, 0)),
            pl.BlockSpec((CTX, NH * dv), lambda b, i: (ctx_blk + b, 0)),
        ],
        out_specs=pl.BlockSpec((tq, NH * dv), lambda b, i: (b * nq + i, 0)),
        out_shape=jax.ShapeDtypeStruct((T_LAT, NH * dv), F32),
        compiler_params=_cparams(("arbitrary", "arbitrary")),
        name="attn_latent",
    )(q, k, v, k, v)


def _attn_ctx(q, k, v, dq, dv):
    ctx_blk = T_LAT // CTX
    return pl.pallas_call(
        functools.partial(_attn_kernel, nkv=1, dq=dq, dv=dv),
        grid=(B,),
        in_specs=[
            pl.BlockSpec((CTX, NH * dq), lambda b: (ctx_blk + b, 0)),
            pl.BlockSpec((CTX, NH * dq), lambda b: (ctx_blk + b, 0)),
            pl.BlockSpec((CTX, NH * dv), lambda b: (ctx_blk + b, 0)),
        ],
        out_specs=pl.BlockSpec((CTX, NH * dv), lambda b: (b, 0)),
        out_shape=jax.ShapeDtypeStruct((T_CTX, NH * dv), F32),
        compiler_params=_cparams(("arbitrary",)),
        name="attn_ctx",
    )(q, k, v)


def _mlaprep_kernel(p_ref, qag_ref, kvag_ref, wq_ref, wk_ref, wv_ref, qg_ref, kg_ref,
                    cos_ref, sin_ref, q_ref, k_ref, v_ref):
    p = p_ref[...]
    cq = p[:, 0:MLA_Q_RANK]
    ckv = p[:, MLA_Q_RANK:MLA_Q_RANK + MLA_KV_RANK]
    krp = p[:, MLA_Q_RANK + MLA_KV_RANK:]
    cqn = cq * lax.rsqrt(jnp.mean(cq * cq, axis=-1, keepdims=True) + EPS) * qag_ref[...]
    ckvn = ckv * lax.rsqrt(jnp.mean(ckv * ckv, axis=-1, keepdims=True) + EPS) * kvag_ref[...]
    ckvb = ckvn.astype(BF16)
    q = _dot(cqn.astype(BF16), wq_ref[...])
    k = _dot(jnp.concatenate([ckvb, krp.astype(BF16)], axis=-1), wk_ref[...])
    v = _dot(ckvb, wv_ref[...])

    def head_norm(x, g):
        outs = []
        for h in range(NH):
            xh = x[:, h * LANE:(h + 1) * LANE]
            ms = jnp.sum(xh * xh, axis=-1, keepdims=True) * (1.0 / MLA_QK)
            outs.append(xh * lax.rsqrt(ms + EPS))
        return jnp.concatenate(outs, axis=-1) * g

    cos = cos_ref[...]
    sin = sin_ref[...]
    half = MLA_ROPE // 2
    lane = lax.broadcasted_iota(jnp.int32, cos.shape, 1)
    first = ((lane % LANE - MLA_NOPE) & (half - 1)) < half // 2

    def rope(x):
        width = x.shape[1]
        partner = jnp.where(first, pltpu.roll(x, width - half // 2, 1), pltpu.roll(x, half // 2, 1))
        return x * cos + partner * sin

    q = rope(head_norm(q, qg_ref[...]))
    k = rope(head_norm(k, kg_ref[...]))
    q_ref[...] = (q * (MLA_QK ** -0.5)).astype(BF16)
    k_ref[...] = k.astype(BF16)
    v_ref[...] = v.astype(BF16)


def _mlaprep(p_mla, qag, kvag, wq, wk, wv, qg, kg, cos_t, sin_t):
    tok = lambda w: pl.BlockSpec((TM, w), lambda i: (i, 0))
    full = lambda shape: pl.BlockSpec(shape, lambda i: (0,) * len(shape))
    pos = pl.BlockSpec((TM, 512), lambda i: (jnp.where(i < LAT_TILES, i % TILES_PER_SEQ, TILES_PER_SEQ), 0))
    return pl.pallas_call(
        _mlaprep_kernel,
        grid=(N_TILES,),
        in_specs=[tok(512), full((1, 256)), full((1, 128)), full((256, 512)), full((256, 512)),
                  full((128, 256)), full((1, 512)), full((1, 512)), pos, pos],
        out_specs=[tok(512), tok(512), tok(GW)],
        out_shape=[jax.ShapeDtypeStruct((T_ALL, 512), BF16), jax.ShapeDtypeStruct((T_ALL, 512), BF16),
                   jax.ShapeDtypeStruct((T_ALL, GW), BF16)],
        compiler_params=_cparams(("parallel",)),
        name="mlaprep",
    )(p_mla, qag, kvag, wq, wk, wv, qg, kg, cos_t, sin_t)


def _rope_tables():
    t = jnp.arange(L)
    rowp = (t // GRID_W).astype(F32)
    colp = (t % GRID_W).astype(F32)
    half = MLA_ROPE // 2
    inv = ROPE_BASE ** (-jnp.arange(0, half, 2, dtype=F32) / half)
    j = jnp.arange(MLA_ROPE)
    pos = jnp.where(j[None, :] < half, rowp[:, None], colp[:, None])
    ang = pos * inv[j % (half // 2)][None, :]
    first = (j % half) < (half // 2)
    cos32 = jnp.cos(ang)
    sin32 = jnp.where(first[None, :], -jnp.sin(ang), jnp.sin(ang))
    cos_h = jnp.concatenate([jnp.ones((L, MLA_NOPE), F32), cos32, jnp.ones((L, 32), F32)], axis=-1)
    sin_h = jnp.concatenate([jnp.zeros((L, MLA_NOPE), F32), sin32, jnp.zeros((L, 32), F32)], axis=-1)
    cos_t = jnp.concatenate([jnp.tile(cos_h, (1, NH)), jnp.ones((TM, 512), F32)], axis=0)
    sin_t = jnp.concatenate([jnp.tile(sin_h, (1, NH)), jnp.zeros((TM, 512), F32)], axis=0)
    return cos_t, sin_t


def _outproj_kernel(x_ref, of_ref, ob_ref, g_ref, hyl_ref, hyc_ref, nal_ref, nac_ref, mll_ref, mlc_ref,
                    mod_ref, ng_ref, gm_ref, w_ref, n2_ref, wr_ref, x1_ref, h2_ref, lg_ref):
    oa = of_ref[...] + ob_ref[...]
    ms = _group_sum(oa * oa, gm_ref[...]) * (1.0 / 64)
    g = g_ref[...]
    oa = oa * lax.rsqrt(ms + EPS) * ng_ref[...] * (g * jax.nn.sigmoid(g))
    lat = pl.program_id(0) < LAT_TILES
    hy = jnp.where(lat, hyl_ref[...], hyc_ref[...])
    na = jnp.where(lat, nal_ref[...], nac_ref[...])
    mla = jnp.where(lat, mll_ref[...], mlc_ref[...])
    mix = jnp.concatenate([oa, hy, na, mla], axis=-1).astype(BF16)
    x1 = x_ref[...] + mod_ref[2:3, :] * _dot(mix, w_ref[...])
    x1_ref[...] = x1
    ms2 = jnp.mean(x1 * x1, axis=-1, keepdims=True)
    h2 = x1 * lax.rsqrt(ms2 + EPS) * n2_ref[...] * (1.0 + mod_ref[4:5, :]) + mod_ref[3:4, :]
    _store_rowtiles(h2_ref, h2)
    hh, hl = _split2(h2)
    wr = wr_ref[...]
    lg_ref[...] = _dot(hh, wr[:, 0:128]) + _dot(hl, wr[:, 0:128]) + _dot(hh, wr[:, 128:256])


def _outproj(n_tiles, X, o_f, o_b, p_hg, hy, na, mla, mod_l, ng, gmb, w_bf, n2g, wr):
    tok = lambda w: pl.BlockSpec((TM, w), lambda i: (i, 0))
    full = lambda shape: pl.BlockSpec(shape, lambda i: (0,) * len(shape))
    latb = pl.BlockSpec((TM, GW), lambda i: (jnp.minimum(i, LAT_TILES - 1), 0))
    ctxb = pl.BlockSpec((TM, GW), lambda i: (jnp.maximum(i - LAT_TILES, 0), 0))
    nt = n_tiles * TM
    return pl.pallas_call(
        _outproj_kernel,
        grid=(n_tiles,),
        in_specs=[tok(D), tok(GW), tok(GW), pl.BlockSpec((TM, GW), lambda i: (i, 4)),
                  latb, ctxb, latb, ctxb, latb, ctxb,
                  pl.BlockSpec((None, 6, D), lambda i: (_seg_of_tile(i), 0, 0)),
                  full((1, GW)), full((GW, GW)), full((D, D)), full((1, D)), full((D, 256))],
        out_specs=[tok(D), pl.BlockSpec((TM * ROW_CH, LANE), lambda i: (i, 0)), tok(128)],
        out_shape=[jax.ShapeDtypeStruct((nt, D), F32), jax.ShapeDtypeStruct((nt * ROW_CH, LANE), F32),
                   jax.ShapeDtypeStruct((nt, 128), F32)],
        compiler_params=_cparams(("parallel",)),
        name="outproj",
    )(X, o_f, o_b, p_hg, hy[0], hy[1], na[0], na[1], mla[0], mla[1], mod_l, ng, gmb, w_bf, n2g, wr)


def _route_kernel(lg_ref, b_ref, ltri_ref, o_ref, cnt_ref, base):
    @pl.when(pl.program_id(0) == 0)
    def _():
        base[...] = jnp.zeros_like(base)

    l = lg_ref[...] + b_ref[...]
    lane_i = lax.broadcasted_iota(jnp.int32, l.shape, 1)
    lane = lane_i.astype(F32)

    def first_max(mask):
        v = jnp.max(jnp.where(mask, l, NEG), axis=-1, keepdims=True)
        i = jnp.min(jnp.where(mask & (l == v), lane, float(LANE)), axis=-1, keepdims=True)
        return v, i

    gmask = lane_i < N_GROUPS
    mg, g_sel = first_max(gmask)
    p_sel = 1.0 / jnp.sum(jnp.where(gmask, jnp.exp(jnp.where(gmask, l - mg, 0.0)), 0.0), axis=-1, keepdims=True)
    lane_group = jnp.right_shift(lane_i - N_GROUPS, EPG.bit_length() - 1).astype(F32)
    emask = (lane_i >= N_GROUPS) & (lane_i < N_GROUPS + N_EXPERTS) & (lane_group == g_sel)
    v1, i1 = first_max(emask)
    v2, i2 = first_max(emask & (lane != i1))
    r = jnp.exp(v2 - v1)
    w1 = p_sel / (1.0 + r)
    w2 = w1 * r
    hit1 = lane == i1
    hit2 = lane == i2
    cnt = jnp.where(hit1 | hit2, 1.0, 0.0)
    before = _dot(ltri_ref[...], cnt.astype(BF16)) + base[...]
    pos1 = jnp.sum(jnp.where(hit1, before, 0.0), axis=-1, keepdims=True)
    pos2 = jnp.sum(jnp.where(hit2, before, 0.0), axis=-1, keepdims=True)
    base[...] = base[...] + jnp.sum(cnt, axis=0, keepdims=True)
    cnt_ref[...] = jnp.broadcast_to(base[...], cnt_ref.shape)
    cols = [i1 - N_GROUPS, i2 - N_GROUPS, w1, w2, pos1, pos2]
    out = jnp.zeros(l.shape, F32)
    for j, c in enumerate(cols):
        out = jnp.where(lane_i == j, c, out)
    o_ref[...] = out


def _route(logits, bg, be):
    t = logits.shape[0]
    bias = jnp.pad(jnp.concatenate([bg, be]), (0, LANE - N_GROUPS - N_EXPERTS)).reshape(1, LANE)
    ltri = (jnp.arange(TM)[None, :] < jnp.arange(TM)[:, None]).astype(BF16)
    out, cnt = pl.pallas_call(
        _route_kernel,
        grid=(t // TM,),
        in_specs=[pl.BlockSpec((TM, LANE), lambda i: (i, 0)), pl.BlockSpec((1, LANE), lambda i: (0, 0)),
                  pl.BlockSpec((TM, TM), lambda i: (0, 0))],
        out_specs=[pl.BlockSpec((TM, LANE), lambda i: (i, 0)), pl.BlockSpec((8, LANE), lambda i: (0, 0))],
        out_shape=[jax.ShapeDtypeStruct((t, LANE), F32), jax.ShapeDtypeStruct((8, LANE), F32)],
        scratch_shapes=[pltpu.VMEM((1, LANE), F32)],
        compiler_params=_cparams(("arbitrary",)),
        name="route",
    )(logits, bias, ltri)
    eid = out[:, 0:2].astype(jnp.int32)
    pos = out[:, 4:6].astype(jnp.int32)
    counts = cnt[0, N_GROUPS:N_GROUPS + N_EXPERTS].astype(jnp.int32)
    return eid, out[:, 2:4], pos, counts


def _dispatch_tables(eid, pos, counts):
    t = eid.shape[0]
    n = 2 * t
    nb = n // TMOE + N_EXPERTS
    flat_e = eid.reshape(n)
    pos = pos.reshape(n)
    onehot = (flat_e[:, None] == jnp.arange(N_EXPERTS, dtype=jnp.int32)[None, :]).astype(jnp.int32)
    pcounts = ((counts + TMOE - 1) // TMOE) * TMOE
    pends = jnp.cumsum(pcounts)
    pstarts = pends - pcounts
    dest = (jnp.sum(jnp.where(onehot > 0, pstarts[None, :], 0), axis=1) + pos).astype(jnp.int32)
    blk_start = jnp.arange(nb, dtype=jnp.int32) * TMOE
    block_e = jnp.minimum(jnp.sum((pends[None, :] <= blk_start[:, None]).astype(jnp.int32), axis=1),
                          N_EXPERTS - 1).astype(jnp.int32)
    nblk = (pends[-1] // TMOE).astype(jnp.int32).reshape(1)
    return block_e, nblk, dest


def _rows(tok, n):
    if isinstance(tok, int):
        return pl.ds(tok * ROW_CH, n * ROW_CH)
    return pl.ds(pl.multiple_of(tok * ROW_CH, ROW_CH), n * ROW_CH)


def _slotmap_kernel(dest_ref, tok_ref):
    def clear(s, carry):
        tok_ref[s] = 0
        return carry

    lax.fori_loop(0, tok_ref.shape[0], clear, 0, unroll=16)

    def body(p, carry):
        tok_ref[dest_ref[p]] = lax.shift_right_logical(p, 1)
        return carry

    lax.fori_loop(0, dest_ref.shape[0], body, 0, unroll=8)


def _slotmap(dest, n_slots):
    smem = pl.BlockSpec(memory_space=pltpu.SMEM)
    return pl.pallas_call(
        _slotmap_kernel,
        in_specs=[smem],
        out_specs=smem,
        out_shape=jax.ShapeDtypeStruct((n_slots,), jnp.int32),
        name="slotmap",
    )(dest)


def _experts_kernel(be_ref, nblk_ref, stok_ref, h_hbm, wg_ref, wu_ref, wd_ref, ys_ref, xbuf, gsem, wgb, wub, wdb):
    i = pl.program_id(0)
    nb = nblk_ref[0]

    def fetch(blk, slot):
        base = blk * TMOE

        def issue(r, carry):
            pltpu.make_async_copy(h_hbm.at[_rows(stok_ref[base + r], 1)], xbuf.at[slot, _rows(r, 1)],
                                  gsem.at[slot]).start()
            return carry

        lax.fori_loop(0, TMOE, issue, 0, unroll=8)

    @pl.when(i == 0)
    def _():
        fetch(0, 0)

    @pl.when(i + 1 < nb)
    def _():
        fetch(i + 1, (i + 1) % 2)

    @pl.when((i == 0) | (be_ref[i] != be_ref[jnp.maximum(i - 1, 0)]))
    def _():
        wgb[...] = wg_ref[...].astype(BF16)
        wub[...] = wu_ref[...].astype(BF16)
        wdb[...] = wd_ref[...].astype(BF16)

    @pl.when(i < nb)
    def _():
        slot = i % 2
        pltpu.make_async_copy(h_hbm.at[_rows(0, TMOE)], xbuf.at[slot], gsem.at[slot]).wait()
        x = _load_rowtiles(xbuf.at[slot], TMOE).astype(BF16)
        gate = _dot(x, wgb[...])
        up = _dot(x, wub[...])
        act = (gate * jax.nn.sigmoid(gate)) * up
        _store_rowtiles(ys_ref, _dot(act.astype(BF16), wdb[...]))

    @pl.when(i >= nb)
    def _():
        ys_ref[...] = jnp.zeros_like(ys_ref)


def _experts(layer, block_e, nblk, slot_tok, h2, w_gate, w_up, w_down):
    nb = block_e.shape[0]
    wspec = lambda shape: pl.BlockSpec((None, None) + shape, lambda i, be, nk, st: (layer, be[i], 0, 0))
    grid_spec = pltpu.PrefetchScalarGridSpec(
        num_scalar_prefetch=3,
        grid=(nb,),
        in_specs=[pl.BlockSpec(memory_space=pl.ANY), wspec((D, D_EXPERT)), wspec((D, D_EXPERT)), wspec((D_EXPERT, D))],
        out_specs=pl.BlockSpec((TMOE * ROW_CH, LANE), lambda i, be, nk, st: (i, 0)),
        scratch_shapes=[pltpu.VMEM((2, TMOE * ROW_CH, LANE), F32), pltpu.SemaphoreType.DMA((2,)),
                        pltpu.VMEM((D, D_EXPERT), BF16), pltpu.VMEM((D, D_EXPERT), BF16),
                        pltpu.VMEM((D_EXPERT, D), BF16)],
    )
    return pl.pallas_call(
        _experts_kernel,
        grid_spec=grid_spec,
        out_shape=jax.ShapeDtypeStruct((nb * TMOE * ROW_CH, LANE), F32),
        compiler_params=_cparams(("arbitrary",)),
        name="experts",
    )(block_e, nblk, slot_tok, h2, w_gate, w_up, w_down)


def _combine_tile(dest_ref, x_ref, w_ref, mod_ref, ys_hbm, ybuf, sem):
    i = pl.program_id(0)

    def fetch(tile, slot):
        base = tile * (2 * TM)

        def issue(r, carry):
            for k in range(2):
                pltpu.make_async_copy(ys_hbm.at[_rows(dest_ref[base + 2 * r + k], 1)],
                                      ybuf.at[slot, k, _rows(r, 1)], sem.at[slot]).start()
            return carry

        lax.fori_loop(0, TM, issue, 0, unroll=8)

    @pl.when(i == 0)
    def _():
        fetch(0, 0)

    @pl.when(i + 1 < pl.num_programs(0))
    def _():
        fetch(i + 1, (i + 1) % 2)

    slot = i % 2
    for k in range(2):
        pltpu.make_async_copy(ys_hbm.at[_rows(0, TM)], ybuf.at[slot, k], sem.at[slot]).wait()
    w = w_ref[...]
    y = w[:, 0:1] * _load_rowtiles(ybuf.at[slot, 0], TM) + w[:, 1:2] * _load_rowtiles(ybuf.at[slot, 1], TM)
    return x_ref[...] + mod_ref[5:6, :] * y


def _combine_kernel(dest_ref, x_ref, w_ref, mod_ref, ys_hbm, o_ref, ybuf, sem):
    o_ref[...] = _combine_tile(dest_ref, x_ref, w_ref, mod_ref, ys_hbm, ybuf, sem)


def _combine_in_specs():
    return [pl.BlockSpec((TM, D), lambda i, dst: (i, 0)),
            pl.BlockSpec((TM, 2), lambda i, dst: (i, 0)),
            pl.BlockSpec((None, 6, D), lambda i, dst: (_seg_of_tile(i), 0, 0)),
            pl.BlockSpec(memory_space=pl.ANY)]


COMBINE_SCRATCH = [pltpu.VMEM((2, 2, TM * ROW_CH, LANE), F32), pltpu.SemaphoreType.DMA((2,))]


def _combine(n_tiles, dest, X1, wts, mod_l, ys):
    grid_spec = pltpu.PrefetchScalarGridSpec(
        num_scalar_prefetch=1,
        grid=(n_tiles,),
        in_specs=_combine_in_specs(),
        out_specs=pl.BlockSpec((TM, D), lambda i, dst: (i, 0)),
        scratch_shapes=COMBINE_SCRATCH,
    )
    return pl.pallas_call(
        _combine_kernel,
        grid_spec=grid_spec,
        out_shape=jax.ShapeDtypeStruct((n_tiles * TM, D), F32),
        compiler_params=_cparams(("arbitrary",)),
        name="combine",
    )(dest, X1, wts, mod_l, ys)


def _combine_inproj_kernel(dest_ref, x_ref, w_ref, modp_ref, ys_hbm, mod_ref, g_ref, win_ref,
                           x_out, o_hg, o_hy, o_na, o_mla, ybuf, sem):
    x = _combine_tile(dest_ref, x_ref, w_ref, modp_ref, ys_hbm, ybuf, sem)
    x_out[...] = x
    _inproj_tile(x, mod_ref, g_ref, win_ref, (o_hg, o_hy, o_na, o_mla))


def _combine_inproj(dest, X1, wts, mod_prev, ys, mod_l, g, w_bf):
    p_specs, p_shapes = _inproj_out_specs(lambda i, dst: (i, 0))
    grid_spec = pltpu.PrefetchScalarGridSpec(
        num_scalar_prefetch=1,
        grid=(N_TILES,),
        in_specs=_combine_in_specs() + [
            pl.BlockSpec((None, 6, D), lambda i, dst: (_seg_of_tile(i), 0, 0)),
            pl.BlockSpec((1, D), lambda i, dst: (0, 0)),
            pl.BlockSpec((D, D_IN_PAD), lambda i, dst: (0, 0)),
        ],
        out_specs=[pl.BlockSpec((TM, D), lambda i, dst: (i, 0))] + p_specs,
        scratch_shapes=COMBINE_SCRATCH,
    )
    return pl.pallas_call(
        _combine_inproj_kernel,
        grid_spec=grid_spec,
        out_shape=[jax.ShapeDtypeStruct((T_ALL, D), F32)] + p_shapes,
        compiler_params=_cparams(("arbitrary",)),
        name="combine_inproj",
    )(dest, X1, wts, mod_prev, ys, mod_l, g, w_bf)


def _group_mask(width, group):
    lane = jnp.arange(width)
    return (lane[:, None] // group == lane[None, :] // group)


def _hgrn_tri(reverse):
    t = jnp.arange(HCHUNK)
    same = (t[:, None] // HSUB) == (t[None, :] // HSUB)
    order = (t[None, :] >= t[:, None]) if reverse else (t[None, :] <= t[:, None])
    return jnp.stack([same & order, order]).astype(BF16)


def _mla_weights(w_uq, w_ukv, q_g, k_g):
    wq = jnp.pad(w_uq.reshape(MLA_Q_RANK, NH, MLA_QK), ((0, 0), (0, 0), (0, 128 - MLA_QK))).reshape(MLA_Q_RANK, 512)
    kv = w_ukv.reshape(MLA_KV_RANK, NH, MLA_NOPE + 64)
    wk_top = jnp.pad(kv[:, :, :MLA_NOPE], ((0, 0), (0, 0), (0, 128 - MLA_NOPE))).reshape(MLA_KV_RANK, 512)
    lane = jnp.arange(512)
    src = jnp.arange(128)
    place = ((lane[None, :] % 128) == (src[:, None] + MLA_NOPE)) & (src[:, None] < MLA_ROPE)
    wk = jnp.concatenate([wk_top, place.astype(F32)], axis=0)
    wv = kv[:, :, MLA_NOPE:].reshape(MLA_KV_RANK, GW)
    pad_g = lambda g: jnp.tile(jnp.pad(g, (0, 128 - MLA_QK)), NH).reshape(1, 512)
    return wq.astype(BF16), wk.astype(BF16), wv.astype(BF16), pad_g(q_g), pad_g(k_g)


def kernel(x, c, ctx, c_ctx, w_ada, b_ada, norm1_g, norm2_g, w_in, w_out, hgrn_lb_logits, hgrn_norm_g,
           hy_short_w, hy_short_b, hy_w1, hy_b1, hy_freq, hy_w2, hy_b2, hy_w3, hy_b3, hy_decay, hy_bias,
           na_rpb, na_q_g, na_k_g, mla_q_a_g, mla_kv_a_g, mla_w_uq, mla_w_ukv, mla_q_g, mla_k_g,
           moe_wg, moe_bg, moe_we, moe_be, moe_w_gate, moe_w_up, moe_w_down):
    X = jnp.concatenate([x.reshape(T_LAT, D), ctx.reshape(T_CTX, D)], axis=0)
    cmat = jnp.concatenate([c, c_ctx[None, :], jnp.zeros((16 - B - 1, D), F32)], axis=0)
    mod = _adaln(cmat, w_ada, b_ada).reshape(DEPTH, 16, 6, D)

    lb_cum = jnp.cumsum(jax.nn.softmax(hgrn_lb_logits.astype(F32), axis=0), axis=0)
    lower = lb_cum - lb_cum[0:1]

    gm64 = _group_mask(GW, 64)
    gm64_f = gm64.astype(F32)
    gm64_b = gm64.astype(BF16)
    trif = _hgrn_tri(False)
    trib = _hgrn_tri(True)
    cos_t, sin_t = _rope_tables()
    dft = {}
    for n in (L, CTX):
        cm, sm = _dft_consts(n)
        chi, clo = _split2(cm)
        shi, slo = _split2(sm)
        dft[n] = (chi, clo, shi, slo, _hyena_feats(n))

    pending = None
    for l in range(DEPTH):
        mod_l = mod[l]
        w_in_l = jnp.pad(w_in[l], ((0, 0), (0, D_IN_PAD - D_IN))).astype(BF16)
        if pending is None:
            p_hg, p_hy, p_na, p_mla = _inproj(X, mod_l, norm1_g[l].reshape(1, D), w_in_l)
        else:
            X, p_hg, p_hy, p_na, p_mla = _combine_inproj(*pending, mod_l, norm1_g[l].reshape(1, D), w_in_l)

        lb = lower[l]
        hconst = jnp.concatenate([
            jnp.stack([jnp.maximum(jnp.log(lb[d]), NEG), jnp.log1p(-lb[d]), 1.0 - lb[d]]) for d in range(2)
        ] + [jnp.zeros((2, GW), F32)], axis=0)
        o_f, o_b = _hgrn(p_hg, hconst, trif, trib, gm64_f, gm64_b)

        need_ctx = l < DEPTH - 1
        w1p = jnp.pad(hy_w1[l], ((0, 128 - HYENA_EMB), (0, 0)))
        o_hy = []
        for n, blk0 in ((L, 0), (CTX, T_LAT // CTX)):
            if n == CTX and not need_ctx:
                continue
            chi, clo, shi, slo, feats = dft[n]
            e, o, knq = _hyfilt(feats, w1p, hy_b1[l].reshape(1, -1), hy_freq[l].reshape(1, -1), hy_w2[l],
                                hy_b2[l].reshape(1, -1), hy_w3[l], hy_b3[l].reshape(1, -1),
                                hy_decay[l].reshape(1, 4 * GW))
            kre, kim = _hyspec(chi, clo, shi, slo, e, o)
            o_hy.append(_hyena(p_hy, blk0, B, n, hy_short_w[l], hy_short_b[l].reshape(1, -1), hy_bias[l],
                               chi, shi, kre, kim, knq))

        qn, kn, vn = _naprep(p_na, jnp.tile(na_q_g[l], NH).reshape(1, GW), jnp.tile(na_k_g[l], NH).reshape(1, GW),
                             gm64_b)
        o_na = [_na(qn, kn, vn, _na_bias_table(na_rpb[l]))]

        wq, wk, wv, qg, kg = _mla_weights(mla_w_uq[l], mla_w_ukv[l], mla_q_g[l], mla_k_g[l])
        mq, mk, mv = _mlaprep(p_mla, mla_q_a_g[l].reshape(1, -1), mla_kv_a_g[l].reshape(1, -1), wq, wk, wv,
                              qg, kg, cos_t, sin_t)
        o_mla = [_attn_latent(mq, mk, mv, 128, 64, 256)]
        if need_ctx:
            o_na.append(_attn_ctx(qn, kn, vn, 64, 64))
            o_mla.append(_attn_ctx(mq, mk, mv, 128, 64))
        else:
            o_hy.append(o_hy[0])
            o_na.append(o_na[0])
            o_mla.append(o_mla[0])
        n_tiles = N_TILES if need_ctx else LAT_TILES

        wr = jnp.pad(jnp.concatenate([moe_wg[l], moe_we[l]], axis=1), ((0, 0), (0, 128 - N_GROUPS - N_EXPERTS)))
        wr_hi, wr_lo = _split2(wr)
        X1, h2, logits = _outproj(n_tiles, X, o_f, o_b, p_hg, o_hy, o_na, o_mla, mod_l,
                                  jnp.tile(hgrn_norm_g[l], NH).reshape(1, GW), gm64_b,
                                  w_out[l].astype(BF16), norm2_g[l].reshape(1, D),
                                  jnp.concatenate([wr_hi, wr_lo], axis=1))

        eid, wts, pos, counts = _route(logits, moe_bg[l], moe_be[l])
        block_e, nblk, dest = _dispatch_tables(eid, pos, counts)
        slot_tok = _slotmap(dest, block_e.shape[0] * TMOE)
        ys = _experts(l, block_e, nblk, slot_tok, h2, moe_w_gate, moe_w_up, moe_w_down)
        pending = (dest, X1, wts, mod_l, ys)

    return _combine(LAT_TILES, *pending).reshape(B, L, D)
```

```python
import functools
import math

import jax
import jax.numpy as jnp
from jax import lax
from jax.experimental import pallas as pl
from jax.experimental.pallas import tpu as pltpu

F32 = jnp.float32
BF16 = jnp.bfloat16

D = 1024
B = 8
L = 2048
CTX = 256
DEPTH = 4
GRID_W = 64
EPS = 1e-6
GW = 256
NH = 4
HYENA_BANDS = 16
HYENA_EMB = 1 + 2 * HYENA_BANDS
HYENA_FFN = 64
NA_ROWS = 8
NA_COLS = 16
MLA_Q_RANK = 256
MLA_KV_RANK = 128
MLA_NOPE = 64
MLA_ROPE = 32
MLA_QK = MLA_NOPE + MLA_ROPE
ROPE_BASE = 10000.0
N_GROUPS = 4
EPG = 8
N_EXPERTS = N_GROUPS * EPG
D_EXPERT = 512
D_IN = 3232
D_IN_PAD = 3328

T_LAT = B * L
T_CTX = B * CTX
T_ALL = T_LAT + T_CTX

TM = 512
N_TILES = T_ALL // TM
LAT_TILES = T_LAT // TM
TILES_PER_SEQ = L // TM
HCHUNK = 64
HSUB = 16
HGRN_SAFE_DECAY = 80.0
TMOE = 256
VMEM_LIMIT_BYTES = 56 * 1024 * 1024
NEG = -1e30

HI = lax.Precision.HIGHEST


def _cparams(sem, vmem=VMEM_LIMIT_BYTES):
    return pltpu.CompilerParams(dimension_semantics=sem, vmem_limit_bytes=vmem)


def _seg_of_tile(i):
    return jnp.where(i < LAT_TILES, i // TILES_PER_SEQ, B)


def _dot(a, b):
    return jnp.dot(a, b, preferred_element_type=F32)


def _dot_nt(a, b):
    return lax.dot_general(a, b, (((1,), (1,)), ((), ())), preferred_element_type=F32)


def _dot_tn(a, b):
    return lax.dot_general(a, b, (((0,), (0,)), ((), ())), preferred_element_type=F32)


def _split2(x):
    hi = x.astype(BF16)
    lo = (x - hi.astype(F32)).astype(BF16)
    return hi, lo


def _split3(x):
    h1 = x.astype(BF16)
    r1 = x - h1.astype(F32)
    h2 = r1.astype(BF16)
    h3 = (r1 - h2.astype(F32)).astype(BF16)
    return h1, h2, h3


LANE = 128
ROW_CH = D // LANE


def _store_rowtiles(ref, val):
    n = val.shape[0]
    for j in range(ROW_CH):
        ref[pl.ds(j, n, stride=ROW_CH), :] = val[:, j * LANE:(j + 1) * LANE]


def _load_rowtiles(ref, n):
    return jnp.concatenate([ref[pl.ds(j, n, stride=ROW_CH), :] for j in range(ROW_CH)], axis=-1)


def _group_sum(x, gm):
    hi, lo = _split2(x)
    return _dot(hi, gm) + _dot(lo, gm)


def _ada_kernel(c_ref, w_ref, b_ref, o_ref):
    cc = c_ref[...]
    sc = cc * jax.nn.sigmoid(cc)
    o_ref[0] = jnp.dot(sc, w_ref[0], preferred_element_type=F32, precision=HI) + b_ref[0]


def _adaln(cmat, w_ada, b_ada):
    tn = 1536
    return pl.pallas_call(
        _ada_kernel,
        grid=(DEPTH, 6 * D // tn),
        in_specs=[
            pl.BlockSpec((16, D), lambda l, j: (0, 0)),
            pl.BlockSpec((1, D, tn), lambda l, j: (l, 0, j)),
            pl.BlockSpec((1, 1, tn), lambda l, j: (l, 0, j)),
        ],
        out_specs=pl.BlockSpec((1, 16, tn), lambda l, j: (l, 0, j)),
        out_shape=jax.ShapeDtypeStruct((DEPTH, 16, 6 * D), F32),
        compiler_params=_cparams(("arbitrary", "arbitrary")),
        name="adaln",
    )(cmat, w_ada, b_ada.reshape(DEPTH, 1, 6 * D))


IN_WIDTHS = (1280, 768, 768, 512)


def _inproj_tile(x, mod_ref, g_ref, w_ref, outs):
    ms = jnp.mean(x * x, axis=-1, keepdims=True)
    y = x * lax.rsqrt(ms + EPS) * g_ref[...]
    h = y * (1.0 + mod_ref[1:2, :]) + mod_ref[0:1, :]
    p = _dot(h.astype(BF16), w_ref[...])
    c0 = 0
    for o_ref, w in zip(outs, IN_WIDTHS):
        o_ref[...] = p[:, c0:c0 + w]
        c0 += w


def _inproj_kernel(x_ref, mod_ref, g_ref, w_ref, o_hg, o_hy, o_na, o_mla):
    _inproj_tile(x_ref[...], mod_ref, g_ref, w_ref, (o_hg, o_hy, o_na, o_mla))


def _inproj_out_specs(index_map):
    specs = [pl.BlockSpec((TM, w), index_map) for w in IN_WIDTHS]
    shapes = [jax.ShapeDtypeStruct((T_ALL, w), F32) for w in IN_WIDTHS]
    return specs, shapes


def _inproj(X, mod_l, g, w_bf):
    out_specs, out_shape = _inproj_out_specs(lambda i: (i, 0))
    return pl.pallas_call(
        _inproj_kernel,
        grid=(N_TILES,),
        in_specs=[
            pl.BlockSpec((TM, D), lambda i: (i, 0)),
            pl.BlockSpec((None, 6, D), lambda i: (_seg_of_tile(i), 0, 0)),
            pl.BlockSpec((1, D), lambda i: (0, 0)),
            pl.BlockSpec((D, D_IN_PAD), lambda i: (0, 0)),
        ],
        out_specs=out_specs,
        out_shape=out_shape,
        compiler_params=_cparams(("parallel",)),
        name="inproj",
    )(X, mod_l, g, w_bf)


HSTEP = 4 * HCHUNK


def _hgrn_prologue(p_ref, r0, zcol, c_ref, crow, tri_ref, reverse):
    q = p_ref[r0:r0 + HCHUNK, 0:GW]
    z = p_ref[r0:r0 + HCHUNK, zcol:zcol + GW]
    v = p_ref[r0:r0 + HCHUNK, 3 * GW:4 * GW]
    la = c_ref[crow:crow + 1, :]
    l1 = c_ref[crow + 1:crow + 2, :]
    oml = c_ref[crow + 2:crow + 3, :]
    e = jnp.exp(-jnp.abs(z))
    ope = 1.0 + e
    ls = jnp.minimum(z, 0.0) - jnp.log(ope)
    c2 = l1 + ls
    logf = jnp.maximum(la, c2) + jnp.log(1.0 + jnp.exp(-jnp.abs(la - c2)))
    kk = oml * (jnp.where(z >= 0.0, e, 1.0) / ope)
    h1, h2, h3 = _split3(logf)
    tri_full = tri_ref[1]
    bfull = _dot(tri_full, h1) + _dot(tri_full, h2) + _dot(tri_full, h3)
    half = HCHUNK // 2
    first, mid, last = (HCHUNK - 1, half, 0) if reverse else (0, half - 1, HCHUNK - 1)
    btot = bfull[last:last + 1]
    bmid = bfull[mid:mid + 1]
    worst = jnp.maximum(bfull[first:first + 1] - bmid, bmid - btot)
    return dict(r0=r0, q=q, kk=kk, v=v, splits=(h1, h2, h3), bfull=bfull, btot=btot, bmid=bmid, worst=worst)


def _hgrn_fast(c, st, gm, reverse):
    q, kk, v, bfull, btot, bmid = c["q"], c["kk"], c["v"], c["bfull"], c["btot"], c["bmid"]
    lane_head = lax.broadcasted_iota(jnp.int32, (HCHUNK, GW), 1) // 64
    qi = (q * jnp.exp(bfull - bmid)).astype(BF16)
    ke = (kk * jnp.exp(bmid - bfull)).astype(BF16)
    qx = jnp.concatenate([jnp.where(lane_head == h, qi, jnp.zeros_like(qi)) for h in range(NH)], axis=0)
    a = _dot_nt(qx, ke)
    t_idx = lax.broadcasted_iota(jnp.int32, a.shape, 0) % HCHUNK
    s_idx = lax.broadcasted_iota(jnp.int32, a.shape, 1)
    seen = (s_idx >= t_idx) if reverse else (s_idx <= t_idx)
    a = jnp.where(seen, a, 0.0).astype(BF16)
    vb = v.astype(BF16)
    o_all = _dot(a, vb)
    o = _dot_nt((q * jnp.exp(bfull)).astype(BF16), st.astype(BF16))
    for h in range(NH):
        o = o + jnp.where(lane_head == h, o_all[h * HCHUNK:(h + 1) * HCHUNK, :], 0.0)
    kd = (kk * jnp.exp(btot - bfull)).astype(BF16)
    return o, st * jnp.exp(btot) + _dot_tn(vb, kd) * gm


def _hgrn_slow(c, st, gm, gmb, tri_sub, reverse, o_ref):
    q, kk, v = c["q"], c["kk"], c["v"]
    h1, h2, h3 = c["splits"]
    bsub = _dot(tri_sub, h1) + _dot(tri_sub, h2) + _dot(tri_sub, h3)
    row = lax.broadcasted_iota(jnp.int32, (HSUB, GW), 0)
    order = range(HCHUNK // HSUB - 1, -1, -1) if reverse else range(HCHUNK // HSUB)
    for blk in order:
        r0 = blk * HSUB
        b_i = bsub[r0:r0 + HSUB]
        q_i = q[r0:r0 + HSUB]
        k_i = kk[r0:r0 + HSUB]
        v_i = v[r0:r0 + HSUB]
        bt_i = b_i[0:1] if reverse else b_i[HSUB - 1:HSUB]
        qe = (q_i * jnp.exp(b_i)).astype(BF16)
        o_inter = _dot_nt(qe, st.astype(BF16))
        parts = []
        for tl in range(HSUB):
            dlt = b_i[tl:tl + 1] - b_i
            valid = (row >= tl) if reverse else (row <= tl)
            w = jnp.exp(jnp.where(valid, dlt, NEG))
            parts.append((q_i[tl:tl + 1] * w) * k_i)
        pmat = jnp.concatenate(parts, axis=0).astype(BF16)
        abar = _dot(pmat, gmb)
        o_diag = jnp.sum(abar.reshape(HSUB, HSUB, GW) * v_i[None], axis=1)
        o_ref[c["r0"] + r0:c["r0"] + r0 + HSUB, :] = o_inter + o_diag
        kd = (k_i * jnp.exp(bt_i - b_i)).astype(BF16)
        upd = _dot_tn(v_i.astype(BF16), kd)
        st = st * jnp.exp(bt_i) + upd * gm
    return st


def _hgrn_kernel(pf_ref, pb_ref, c_ref, trif_ref, trib_ref, gm_ref, gmb_ref, of_ref, ob_ref, stf, stb):
    @pl.when(pl.program_id(1) == 0)
    def _():
        stf[...] = jnp.zeros_like(stf)
        stb[...] = jnp.zeros_like(stb)

    gm = gm_ref[...]
    dirs = []
    for p_ref, o_ref, st_ref, tri_ref, zcol, crow, reverse in (
            (pf_ref, of_ref, stf, trif_ref, GW, 0, False), (pb_ref, ob_ref, stb, trib_ref, 2 * GW, 3, True)):
        offs = tuple(range(0, HSTEP, HCHUNK))
        offs = offs[::-1] if reverse else offs
        chunks = [_hgrn_prologue(p_ref, r0, zcol, c_ref, crow, tri_ref, reverse) for r0 in offs]
        dirs.append((o_ref, st_ref, tri_ref, reverse, chunks))
    worst = functools.reduce(jnp.maximum, [c["worst"] for d in dirs for c in d[4]])
    safe = jnp.max(worst) < HGRN_SAFE_DECAY

    @pl.when(safe)
    def _():
        for o_ref, st_ref, tri_ref, reverse, chunks in dirs:
            st = st_ref[...]
            for c in chunks:
                o, st = _hgrn_fast(c, st, gm, reverse)
                o_ref[c["r0"]:c["r0"] + HCHUNK, :] = o
            st_ref[...] = st

    @pl.when(jnp.logical_not(safe))
    def _():
        gmb = gmb_ref[...]
        for o_ref, st_ref, tri_ref, reverse, chunks in dirs:
            st = st_ref[...]
            for c in chunks:
                st = _hgrn_slow(c, st, gm, gmb, tri_ref[0], reverse, o_ref)
            st_ref[...] = st


def _hgrn_block(b, n, reverse):
    nctx = CTX // HSTEP
    nlat = L // HSTEP
    jc = (nctx - 1 - n) if reverse else n
    jl = (nlat - 1 - (n - nctx)) if reverse else (n - nctx)
    return jnp.where(n < nctx, T_LAT // HSTEP + b * nctx + jc, b * nlat + jl)


def _hgrn(p_hg, consts, trif, trib, gm, gmb):
    nsteps = (CTX + L) // HSTEP
    full = lambda shape: pl.BlockSpec(shape, lambda b, n: (0,) * len(shape))
    return pl.pallas_call(
        _hgrn_kernel,
        grid=(B, nsteps),
        in_specs=[
            pl.BlockSpec((HSTEP, 1280), lambda b, n: (_hgrn_block(b, n, False), 0)),
            pl.BlockSpec((HSTEP, 1280), lambda b, n: (_hgrn_block(b, n, True), 0)),
            full((8, GW)),
            full((2, HCHUNK, HCHUNK)),
            full((2, HCHUNK, HCHUNK)),
            full((GW, GW)),
            full((GW, GW)),
        ],
        out_specs=[
            pl.BlockSpec((HSTEP, GW), lambda b, n: (_hgrn_block(b, n, False), 0)),
            pl.BlockSpec((HSTEP, GW), lambda b, n: (_hgrn_block(b, n, True), 0)),
        ],
        out_shape=[jax.ShapeDtypeStruct((T_ALL, GW), F32), jax.ShapeDtypeStruct((T_ALL, GW), F32)],
        scratch_shapes=[pltpu.VMEM((GW, GW), F32), pltpu.VMEM((GW, GW), F32)],
        compiler_params=_cparams(("arbitrary", "arbitrary")),
        name="hgrn",
    )(p_hg, p_hg, consts, trif, trib, gm, gmb)


def _alt_sum(x):
    n, c = x.shape
    sgn = jnp.where((lax.broadcasted_iota(jnp.int32, (n, c), 0) & 1) == 0, 1.0, -1.0)
    return jnp.sum(x * sgn, axis=0, keepdims=True)


def _hyfilt_kernel(feats_ref, w1_ref, b1_ref, fr_ref, w2_ref, b2_ref, w3_ref, b3_ref, dec_ref,
                   e_ref, o_ref, nq_ref):
    fr = fr_ref[...]
    feats = feats_ref[...]
    h = jnp.sin(fr * (jnp.dot(feats, w1_ref[...], preferred_element_type=F32, precision=HI) + b1_ref[...]))
    h = jnp.sin(fr * (jnp.dot(h, w2_ref[...], preferred_element_type=F32, precision=HI) + b2_ref[...]))
    filt = jnp.dot(h, w3_ref[...], preferred_element_type=F32, precision=HI) + b3_ref[...]
    filt = filt * jnp.exp(-feats[:, 0:1] * dec_ref[...])
    n = filt.shape[0]
    row = lax.broadcasted_iota(jnp.int32, (n, GW), 0)
    for o in range(2):
        fwd = filt[:, (2 * o) * GW:(2 * o + 1) * GW]
        bwd = jnp.where(row >= 1, filt[:, (2 * o + 1) * GW:(2 * o + 2) * GW], 0.0)
        ssq = jnp.sum(fwd * fwd + bwd * bwd, axis=0, keepdims=True)
        scale = lax.rsqrt(ssq + EPS)
        ev = (fwd + bwd) * scale
        e_ref[:, o * GW:(o + 1) * GW] = ev
        o_ref[:, o * GW:(o + 1) * GW] = (fwd - bwd) * scale
        nq_ref[:, o * GW:(o + 1) * GW] = _alt_sum(ev) * (0.5 / n)


def _hyfilt(feats, w1p, b1, fr, w2, b2, w3, b3, dec):
    n = feats.shape[0]
    return pl.pallas_call(
        _hyfilt_kernel,
        out_shape=[jax.ShapeDtypeStruct((n, 2 * GW), F32), jax.ShapeDtypeStruct((n, 2 * GW), F32),
                   jax.ShapeDtypeStruct((1, 2 * GW), F32)],
        compiler_params=_cparams(None),
        name="hyfilt",
    )(feats, w1p, b1, fr, w2, b2, w3, b3, dec)


def _hyspec_kernel(chi_ref, clo_ref, shi_ref, slo_ref, e_ref, o_ref, kre_ref, kim_ref, *, n):
    eh, el = _split2(e_ref[...])
    oh, ol = _split2(o_ref[...])
    kre = _dot(chi_ref[...], eh) + _dot(chi_ref[...], el) + _dot(clo_ref[...], eh)
    kim = _dot(shi_ref[...], oh) + _dot(shi_ref[...], ol) + _dot(slo_ref[...], oh)
    tr = kre.shape[0]
    grow = lax.broadcasted_iota(jnp.int32, kre.shape, 0) + pl.program_id(0) * tr
    s2 = 1.0 / n
    kre_ref[...] = kre * jnp.where(grow == 0, 0.5 * s2, s2)
    kim_ref[...] = kim * s2


def _hyspec(chi, clo, shi, slo, e, o):
    n = e.shape[0]
    tr = min(256, n)
    rows = pl.BlockSpec((tr, n), lambda i: (i, 0))
    full = pl.BlockSpec((n, 2 * GW), lambda i: (0, 0))
    outb = pl.BlockSpec((tr, 2 * GW), lambda i: (i, 0))
    return pl.pallas_call(
        functools.partial(_hyspec_kernel, n=n),
        grid=(n // tr,),
        in_specs=[rows, rows, rows, rows, full, full],
        out_specs=[outb, outb],
        out_shape=[jax.ShapeDtypeStruct((n, 2 * GW), F32)] * 2,
        compiler_params=_cparams(("parallel",)),
        name="hyspec",
    )(chi, clo, shi, slo, e, o)


def _hyena_kernel(u_ref, sw_ref, sb_ref, db_ref, c_ref, s_ref, kre_ref, kim_ref, knq_ref, o_ref,
                  z_scr, zb_scr, y_scr):
    n = u_ref.shape[0]
    ft = min(512, n)
    rc = min(256, n)
    nchunks = n // rc
    lrow = lax.broadcasted_iota(jnp.int32, (rc, GW), 0)
    sgn = jnp.where((lrow & 1) == 0, 1.0, -1.0)

    def short_conv(part, c):
        sl = slice(part * GW, (part + 1) * GW)
        r0 = c * rc
        u = u_ref[r0:r0 + rc, sl]
        prev = u_ref[r0 - 1:r0, sl] if c > 0 else jnp.zeros((1, GW), F32)
        nxt = u_ref[r0 + rc:r0 + rc + 1, sl] if c < nchunks - 1 else jnp.zeros((1, GW), F32)
        up = jnp.where(lrow == 0, prev, pltpu.roll(u, 1, 0))
        un = jnp.where(lrow == rc - 1, nxt, pltpu.roll(u, rc - 1, 0))
        return sw_ref[0:1, sl] * up + sw_ref[1:2, sl] * u + sw_ref[2:3, sl] * un + sb_ref[:, sl]

    for c in range(nchunks):
        z_scr[c * rc:(c + 1) * rc, :] = short_conv(0, c)
    for o in range(2):
        cols = slice(o * GW, (o + 1) * GW)
        znq = jnp.zeros((1, GW), F32)
        for c in range(nchunks):
            zc = z_scr[c * rc:(c + 1) * rc, :]
            zb_scr[c * rc:(c + 1) * rc, :] = zc.astype(BF16)
            znq = znq + jnp.sum(zc * sgn, axis=0, keepdims=True)
        ynq = znq * knq_ref[:, cols]
        for c in range(nchunks):
            y_scr[c * rc:(c + 1) * rc, :] = sgn * ynq
        for f in range(n // ft):
            rs = slice(f * ft, (f + 1) * ft)
            zre = _dot(c_ref[rs, :], zb_scr[...])
            zim = _dot(s_ref[rs, :], zb_scr[...])
            kre = kre_ref[rs, cols]
            kim = kim_ref[rs, cols]
            yre = (zre * kre - zim * kim).astype(BF16)
            yim = (zre * kim + zim * kre).astype(BF16)
            y_scr[...] += _dot(c_ref[:, rs], yre) + _dot(s_ref[:, rs], yim)
        dst = o_ref if o == 1 else z_scr
        for c in range(nchunks):
            rows = slice(c * rc, (c + 1) * rc)
            dst[rows, :] = short_conv(o + 1, c) * (y_scr[rows, :] + db_ref[o:o + 1, :] * z_scr[rows, :])


def _hyena(u, blk0, nb, n, sw, sb, db, cm, sm, kre, kim, knq):
    whole = pl.BlockSpec(memory_space=pltpu.VMEM)
    return pl.pallas_call(
        _hyena_kernel,
        grid=(nb,),
        in_specs=[
            pl.BlockSpec((n, 3 * GW), lambda b: (blk0 + b, 0)),
            pl.BlockSpec((3, 3 * GW), lambda b: (0, 0)),
            pl.BlockSpec((1, 3 * GW), lambda b: (0, 0)),
            pl.BlockSpec((2, GW), lambda b: (0, 0)),
            whole, whole, whole, whole, whole,
        ],
        out_specs=pl.BlockSpec((n, GW), lambda b: (b, 0)),
        out_shape=jax.ShapeDtypeStruct((nb * n, GW), F32),
        scratch_shapes=[pltpu.VMEM((n, GW), F32), pltpu.VMEM((n, GW), BF16), pltpu.VMEM((n, GW), F32)],
        compiler_params=_cparams(("arbitrary",)),
        name="hyena",
    )(u, sw, sb, db, cm, sm, kre, kim, knq)


def _dft_consts(n):
    kk = jnp.arange(n, dtype=jnp.int32)
    ph = (kk[:, None] * kk[None, :]) % (2 * n)
    ang = ph.astype(F32) * (math.pi / n)
    return jnp.cos(ang), -jnp.sin(ang)


def _hyena_feats(n):
    t = jnp.arange(n, dtype=F32)
    t_unit = jnp.linspace(0.0, 1.0, n, dtype=F32)
    bands = jnp.linspace(1e-4, HYENA_BANDS - 1, HYENA_BANDS, dtype=F32)
    ang = (2.0 * math.pi / n) * t[:, None] * bands[None, :]
    feats = jnp.concatenate([t_unit[:, None], jnp.cos(ang), -jnp.sin(ang)], axis=-1)
    return jnp.pad(feats, ((0, 0), (0, 128 - HYENA_EMB)))


def _naprep_kernel(p_ref, qg_ref, kg_ref, gm_ref, q_ref, k_ref, v_ref):
    p = p_ref[...]
    q = p[:, 0:GW]
    k = p[:, GW:2 * GW]
    gm = gm_ref[...]
    qn = q * lax.rsqrt(_group_sum(q * q, gm) * (1.0 / 64) + EPS) * qg_ref[...]
    kn = k * lax.rsqrt(_group_sum(k * k, gm) * (1.0 / 64) + EPS) * kg_ref[...]
    q_ref[...] = (qn * (64 ** -0.5)).astype(BF16)
    k_ref[...] = kn.astype(BF16)
    v_ref[...] = p[:, 2 * GW:3 * GW].astype(BF16)


def _naprep(p_na, qg, kg, gmb):
    tok = lambda w: pl.BlockSpec((TM, w), lambda i: (i, 0))
    full = lambda shape: pl.BlockSpec(shape, lambda i: (0,) * len(shape))
    return pl.pallas_call(
        _naprep_kernel,
        grid=(N_TILES,),
        in_specs=[tok(768), full((1, GW)), full((1, GW)), full((GW, GW))],
        out_specs=[tok(GW), tok(GW), tok(GW)],
        out_shape=[jax.ShapeDtypeStruct((T_ALL, GW), BF16)] * 3,
        compiler_params=_cparams(("parallel",)),
        name="naprep",
    )(p_na, qg, kg, gmb)


NA_RPS = 4


def _na_kernel(q_ref, k_ref, v_ref, kc_ref, vc_ref, bias_ref, o_ref):
    rows = L // GRID_W
    kc = kc_ref[...]
    vc = vc_ref[...]
    for j in range(NA_RPS):
        r = pl.program_id(1) * NA_RPS + j
        rs = jnp.clip(r - NA_ROWS // 2, 0, rows - NA_ROWS)
        variant = r - rs
        start = pl.multiple_of(rs * GRID_W, GRID_W)
        kw = k_ref[pl.ds(start, NA_ROWS * GRID_W), :]
        vw = v_ref[pl.ds(start, NA_ROWS * GRID_W), :]
        q = q_ref[j * GRID_W:(j + 1) * GRID_W, :]
        lane_head = lax.broadcasted_iota(jnp.int32, (GRID_W, GW), 1) // 64
        hmask = [lane_head == h for h in range(NH)]
        qx = jnp.concatenate([jnp.where(hmask[h], q, jnp.zeros_like(q)) for h in range(NH)], axis=0)
        s_loc = _dot_nt(qx, kw) + bias_ref[variant]
        s_ctx = _dot_nt(qx, kc)
        m = jnp.maximum(jnp.max(s_loc, axis=-1, keepdims=True), jnp.max(s_ctx, axis=-1, keepdims=True))
        p_loc = jnp.exp(s_loc - m)
        p_ctx = jnp.exp(s_ctx - m)
        den = jnp.sum(p_loc, axis=-1, keepdims=True) + jnp.sum(p_ctx, axis=-1, keepdims=True)
        o_all = (_dot(p_loc.astype(BF16), vw) + _dot(p_ctx.astype(BF16), vc)) / den
        o = jnp.zeros((GRID_W, GW), F32)
        for h in range(NH):
            o = o + jnp.where(hmask[h], o_all[h * GRID_W:(h + 1) * GRID_W, :], 0.0)
        o_ref[j * GRID_W:(j + 1) * GRID_W, :] = o


def _na(qn, kn, vn, bias_t):
    steps = L // GRID_W // NA_RPS
    tq = NA_RPS * GRID_W
    ctx_blk = T_LAT // CTX
    return pl.pallas_call(
        _na_kernel,
        grid=(B, steps),
        in_specs=[
            pl.BlockSpec((tq, GW), lambda b, r: (b * steps + r, 0)),
            pl.BlockSpec((L, GW), lambda b, r: (b, 0)),
            pl.BlockSpec((L, GW), lambda b, r: (b, 0)),
            pl.BlockSpec((CTX, GW), lambda b, r: (ctx_blk + b, 0)),
            pl.BlockSpec((CTX, GW), lambda b, r: (ctx_blk + b, 0)),
            pl.BlockSpec((NA_ROWS, NH * GRID_W, NA_ROWS * GRID_W), lambda b, r: (0, 0, 0)),
        ],
        out_specs=pl.BlockSpec((tq, GW), lambda b, r: (b * steps + r, 0)),
        out_shape=jax.ShapeDtypeStruct((T_LAT, GW), F32),
        compiler_params=_cparams(("arbitrary", "arbitrary")),
        name="na",
    )(qn, kn, vn, kn, vn, bias_t)


def _na_bias_table(rpb):
    cq = jnp.arange(GRID_W)
    cs = jnp.clip(cq - NA_COLS // 2, 0, GRID_W - NA_COLS)
    col_ok = (cq[None, :] >= cs[:, None]) & (cq[None, :] < cs[:, None] + NA_COLS)
    dc = jnp.clip(cq[None, :] - cq[:, None] + (NA_COLS - 1), 0, 2 * NA_COLS - 2)
    onehot = (dc[:, :, None] == jnp.arange(2 * NA_COLS - 1)[None, None, :]).astype(F32)
    full = jnp.einsum('qkc,hrc->hrqk', onehot, rpb.astype(F32), precision=HI)
    full = jnp.where(col_ok[None, None], full, NEG)
    tab = jnp.stack([full[:, NA_ROWS - 1 - a:2 * NA_ROWS - 1 - a] for a in range(NA_ROWS)], axis=0)
    return tab.transpose(0, 1, 3, 2, 4).reshape(NA_ROWS, NH * GRID_W, NA_ROWS * GRID_W)


def _attn_kernel(*refs, nkv, dq, dv):
    q = refs[0][...]
    ks = [refs[1 + 2 * j][...] for j in range(nkv)]
    vs = [refs[2 + 2 * j][...] for j in range(nkv)]
    o_ref = refs[1 + 2 * nkv]
    outs = []
    for h in range(NH):
        qh = q[:, h * dq:(h + 1) * dq]
        ss = [_dot_nt(qh, k[:, h * dq:(h + 1) * dq]) for k in ks]
        m = functools.reduce(jnp.maximum, [jnp.max(s, axis=-1, keepdims=True) for s in ss])
        ps = [jnp.exp(s - m) for s in ss]
        den = functools.reduce(lambda a, b2: a + b2, [jnp.sum(p, axis=-1, keepdims=True) for p in ps])
        o = functools.reduce(lambda a, b2: a + b2,
                             [_dot(p.astype(BF16), v[:, h * dv:(h + 1) * dv]) for p, v in zip(ps, vs)])
        outs.append(o / den)
    o_ref[...] = jnp.concatenate(outs, axis=-1)


def _attn_latent(q, k, v, dq, dv, tq):
    nq = L // tq
    ctx_blk = T_LAT // CTX
    return pl.pallas_call(
        functools.partial(_attn_kernel, nkv=2, dq=dq, dv=dv),
        grid=(B, nq),
        in_specs=[
            pl.BlockSpec((tq, NH * dq), lambda b, i: (b * nq + i, 0)),
            pl.BlockSpec((L, NH * dq), lambda b, i: (b, 0)),
            pl.BlockSpec((L, NH * dv), lambda b, i: (b, 0)),
            pl.BlockSpec((CTX, NH * dq), lambda b, i: (ctx_blk + b, 0)),
            pl.BlockSpec((CTX, NH * dv), lambda b, i: (ctx_blk + b, 0)),
        ],
        out_specs=pl.BlockSpec((tq, NH * dv), lambda b, i: (b * nq + i, 0)),
        out_shape=jax.ShapeDtypeStruct((T_LAT, NH * dv), F32),
        compiler_params=_cparams(("arbitrary", "arbitrary")),
        name="attn_latent",
    )(q, k, v, k, v)


def _attn_ctx(q, k, v, dq, dv):
    ctx_blk = T_LAT // CTX
    return pl.pallas_call(
        functools.partial(_attn_kernel, nkv=1, dq=dq, dv=dv),
        grid=(B,),
        in_specs=[
            pl.BlockSpec((CTX, NH * dq), lambda b: (ctx_blk + b, 0)),
            pl.BlockSpec((CTX, NH * dq), lambda b: (ctx_blk + b, 0)),
            pl.BlockSpec((CTX, NH * dv), lambda b: (ctx_blk + b, 0)),
        ],
        out_specs=pl.BlockSpec((CTX, NH * dv), lambda b: (b, 0)),
        out_shape=jax.ShapeDtypeStruct((T_CTX, NH * dv), F32),
        compiler_params=_cparams(("arbitrary",)),
        name="attn_ctx",
    )(q, k, v)


def _mlaprep_kernel(p_ref, qag_ref, kvag_ref, wq_ref, wk_ref, wv_ref, qg_ref, kg_ref,
                    cos_ref, sin_ref, q_ref, k_ref, v_ref):
    p = p_ref[...]
    cq = p[:, 0:MLA_Q_RANK]
    ckv = p[:, MLA_Q_RANK:MLA_Q_RANK + MLA_KV_RANK]
    krp = p[:, MLA_Q_RANK + MLA_KV_RANK:]
    cqn = cq * lax.rsqrt(jnp.mean(cq * cq, axis=-1, keepdims=True) + EPS) * qag_ref[...]
    ckvn = ckv * lax.rsqrt(jnp.mean(ckv * ckv, axis=-1, keepdims=True) + EPS) * kvag_ref[...]
    ckvb = ckvn.astype(BF16)
    q = _dot(cqn.astype(BF16), wq_ref[...])
    k = _dot(jnp.concatenate([ckvb, krp.astype(BF16)], axis=-1), wk_ref[...])
    v = _dot(ckvb, wv_ref[...])

    def head_norm(x, g):
        outs = []
        for h in range(NH):
            xh = x[:, h * LANE:(h + 1) * LANE]
            ms = jnp.sum(xh * xh, axis=-1, keepdims=True) * (1.0 / MLA_QK)
            outs.append(xh * lax.rsqrt(ms + EPS))
        return jnp.concatenate(outs, axis=-1) * g

    cos = cos_ref[...]
    sin = sin_ref[...]
    half = MLA_ROPE // 2
    lane = lax.broadcasted_iota(jnp.int32, cos.shape, 1)
    first = ((lane % LANE - MLA_NOPE) & (half - 1)) < half // 2

    def rope(x):
        width = x.shape[1]
        partner = jnp.where(first, pltpu.roll(x, width - half // 2, 1), pltpu.roll(x, half // 2, 1))
        return x * cos + partner * sin

    q = rope(head_norm(q, qg_ref[...]))
    k = rope(head_norm(k, kg_ref[...]))
    q_ref[...] = (q * (MLA_QK ** -0.5)).astype(BF16)
    k_ref[...] = k.astype(BF16)
    v_ref[...] = v.astype(BF16)


def _mlaprep(p_mla, qag, kvag, wq, wk, wv, qg, kg, cos_t, sin_t):
    tok = lambda w: pl.BlockSpec((TM, w), lambda i: (i, 0))
    full = lambda shape: pl.BlockSpec(shape, lambda i: (0,) * len(shape))
    pos = pl.BlockSpec((TM, 512), lambda i: (jnp.where(i < LAT_TILES, i % TILES_PER_SEQ, TILES_PER_SEQ), 0))
    return pl.pallas_call(
        _mlaprep_kernel,
        grid=(N_TILES,),
        in_specs=[tok(512), full((1, 256)), full((1, 128)), full((256, 512)), full((256, 512)),
                  full((128, 256)), full((1, 512)), full((1, 512)), pos, pos],
        out_specs=[tok(512), tok(512), tok(GW)],
        out_shape=[jax.ShapeDtypeStruct((T_ALL, 512), BF16), jax.ShapeDtypeStruct((T_ALL, 512), BF16),
                   jax.ShapeDtypeStruct((T_ALL, GW), BF16)],
        compiler_params=_cparams(("parallel",)),
        name="mlaprep",
    )(p_mla, qag, kvag, wq, wk, wv, qg, kg, cos_t, sin_t)


def _rope_tables():
    t = jnp.arange(L)
    rowp = (t // GRID_W).astype(F32)
    colp = (t % GRID_W).astype(F32)
    half = MLA_ROPE // 2
    inv = ROPE_BASE ** (-jnp.arange(0, half, 2, dtype=F32) / half)
    j = jnp.arange(MLA_ROPE)
    pos = jnp.where(j[None, :] < half, rowp[:, None], colp[:, None])
    ang = pos * inv[j % (half // 2)][None, :]
    first = (j % half) < (half // 2)
    cos32 = jnp.cos(ang)
    sin32 = jnp.where(first[None, :], -jnp.sin(ang), jnp.sin(ang))
    cos_h = jnp.concatenate([jnp.ones((L, MLA_NOPE), F32), cos32, jnp.ones((L, 32), F32)], axis=-1)
    sin_h = jnp.concatenate([jnp.zeros((L, MLA_NOPE), F32), sin32, jnp.zeros((L, 32), F32)], axis=-1)
    cos_t = jnp.concatenate([jnp.tile(cos_h, (1, NH)), jnp.ones((TM, 512), F32)], axis=0)
    sin_t = jnp.concatenate([jnp.tile(sin_h, (1, NH)), jnp.zeros((TM, 512), F32)], axis=0)
    return cos_t, sin_t


def _outproj_kernel(x_ref, of_ref, ob_ref, g_ref, hyl_ref, hyc_ref, nal_ref, nac_ref, mll_ref, mlc_ref,
                    mod_ref, ng_ref, gm_ref, w_ref, n2_ref, wr_ref, x1_ref, h2_ref, lg_ref):
    oa = of_ref[...] + ob_ref[...]
    ms = _group_sum(oa * oa, gm_ref[...]) * (1.0 / 64)
    g = g_ref[...]
    oa = oa * lax.rsqrt(ms + EPS) * ng_ref[...] * (g * jax.nn.sigmoid(g))
    lat = pl.program_id(0) < LAT_TILES
    hy = jnp.where(lat, hyl_ref[...], hyc_ref[...])
    na = jnp.where(lat, nal_ref[...], nac_ref[...])
    mla = jnp.where(lat, mll_ref[...], mlc_ref[...])
    mix = jnp.concatenate([oa, hy, na, mla], axis=-1).astype(BF16)
    x1 = x_ref[...] + mod_ref[2:3, :] * _dot(mix, w_ref[...])
    x1_ref[...] = x1
    ms2 = jnp.mean(x1 * x1, axis=-1, keepdims=True)
    h2 = x1 * lax.rsqrt(ms2 + EPS) * n2_ref[...] * (1.0 + mod_ref[4:5, :]) + mod_ref[3:4, :]
    _store_rowtiles(h2_ref, h2)
    hh, hl = _split2(h2)
    wr = wr_ref[...]
    lg_ref[...] = _dot(hh, wr[:, 0:128]) + _dot(hl, wr[:, 0:128]) + _dot(hh, wr[:, 128:256])


def _outproj(n_tiles, X, o_f, o_b, p_hg, hy, na, mla, mod_l, ng, gmb, w_bf, n2g, wr):
    tok = lambda w: pl.BlockSpec((TM, w), lambda i: (i, 0))
    full = lambda shape: pl.BlockSpec(shape, lambda i: (0,) * len(shape))
    latb = pl.BlockSpec((TM, GW), lambda i: (jnp.minimum(i, LAT_TILES - 1), 0))
    ctxb = pl.BlockSpec((TM, GW), lambda i: (jnp.maximum(i - LAT_TILES, 0), 0))
    nt = n_tiles * TM
    return pl.pallas_call(
        _outproj_kernel,
        grid=(n_tiles,),
        in_specs=[tok(D), tok(GW), tok(GW), pl.BlockSpec((TM, GW), lambda i: (i, 4)),
                  latb, ctxb, latb, ctxb, latb, ctxb,
                  pl.BlockSpec((None, 6, D), lambda i: (_seg_of_tile(i), 0, 0)),
                  full((1, GW)), full((GW, GW)), full((D, D)), full((1, D)), full((D, 256))],
        out_specs=[tok(D), pl.BlockSpec((TM * ROW_CH, LANE), lambda i: (i, 0)), tok(128)],
        out_shape=[jax.ShapeDtypeStruct((nt, D), F32), jax.ShapeDtypeStruct((nt * ROW_CH, LANE), F32),
                   jax.ShapeDtypeStruct((nt, 128), F32)],
        compiler_params=_cparams(("parallel",)),
        name="outproj",
    )(X, o_f, o_b, p_hg, hy[0], hy[1], na[0], na[1], mla[0], mla[1], mod_l, ng, gmb, w_bf, n2g, wr)


def _route_kernel(lg_ref, b_ref, ltri_ref, o_ref, cnt_ref, base):
    @pl.when(pl.program_id(0) == 0)
    def _():
        base[...] = jnp.zeros_like(base)

    l = lg_ref[...] + b_ref[...]
    lane_i = lax.broadcasted_iota(jnp.int32, l.shape, 1)
    lane = lane_i.astype(F32)

    def first_max(mask):
        v = jnp.max(jnp.where(mask, l, NEG), axis=-1, keepdims=True)
        i = jnp.min(jnp.where(mask & (l == v), lane, float(LANE)), axis=-1, keepdims=True)
        return v, i

    gmask = lane_i < N_GROUPS
    mg, g_sel = first_max(gmask)
    p_sel = 1.0 / jnp.sum(jnp.where(gmask, jnp.exp(jnp.where(gmask, l - mg, 0.0)), 0.0), axis=-1, keepdims=True)
    lane_group = jnp.right_shift(lane_i - N_GROUPS, EPG.bit_length() - 1).astype(F32)
    emask = (lane_i >= N_GROUPS) & (lane_i < N_GROUPS + N_EXPERTS) & (lane_group == g_sel)
    v1, i1 = first_max(emask)
    v2, i2 = first_max(emask & (lane != i1))
    r = jnp.exp(v2 - v1)
    w1 = p_sel / (1.0 + r)
    w2 = w1 * r
    hit1 = lane == i1
    hit2 = lane == i2
    cnt = jnp.where(hit1 | hit2, 1.0, 0.0)
    before = _dot(ltri_ref[...], cnt.astype(BF16)) + base[...]
    pos1 = jnp.sum(jnp.where(hit1, before, 0.0), axis=-1, keepdims=True)
    pos2 = jnp.sum(jnp.where(hit2, before, 0.0), axis=-1, keepdims=True)
    base[...] = base[...] + jnp.sum(cnt, axis=0, keepdims=True)
    cnt_ref[...] = jnp.broadcast_to(base[...], cnt_ref.shape)
    cols = [i1 - N_GROUPS, i2 - N_GROUPS, w1, w2, pos1, pos2]
    out = jnp.zeros(l.shape, F32)
    for j, c in enumerate(cols):
        out = jnp.where(lane_i == j, c, out)
    o_ref[...] = out


def _route(logits, bg, be):
    t = logits.shape[0]
    bias = jnp.pad(jnp.concatenate([bg, be]), (0, LANE - N_GROUPS - N_EXPERTS)).reshape(1, LANE)
    ltri = (jnp.arange(TM)[None, :] < jnp.arange(TM)[:, None]).astype(BF16)
    out, cnt = pl.pallas_call(
        _route_kernel,
        grid=(t // TM,),
        in_specs=[pl.BlockSpec((TM, LANE), lambda i: (i, 0)), pl.BlockSpec((1, LANE), lambda i: (0, 0)),
                  pl.BlockSpec((TM, TM), lambda i: (0, 0))],
        out_specs=[pl.BlockSpec((TM, LANE), lambda i: (i, 0)), pl.BlockSpec((8, LANE), lambda i: (0, 0))],
        out_shape=[jax.ShapeDtypeStruct((t, LANE), F32), jax.ShapeDtypeStruct((8, LANE), F32)],
        scratch_shapes=[pltpu.VMEM((1, LANE), F32)],
        compiler_params=_cparams(("arbitrary",)),
        name="route",
    )(logits, bias, ltri)
    eid = out[:, 0:2].astype(jnp.int32)
    pos = out[:, 4:6].astype(jnp.int32)
    counts = cnt[0, N_GROUPS:N_GROUPS + N_EXPERTS].astype(jnp.int32)
    return eid, out[:, 2:4], pos, counts


def _dispatch_tables(eid, pos, counts):
    t = eid.shape[0]
    n = 2 * t
    nb = n // TMOE + N_EXPERTS
    flat_e = eid.reshape(n)
    pos = pos.reshape(n)
    onehot = (flat_e[:, None] == jnp.arange(N_EXPERTS, dtype=jnp.int32)[None, :]).astype(jnp.int32)
    pcounts = ((counts + TMOE - 1) // TMOE) * TMOE
    pends = jnp.cumsum(pcounts)
    pstarts = pends - pcounts
    dest = (jnp.sum(jnp.where(onehot > 0, pstarts[None, :], 0), axis=1) + pos).astype(jnp.int32)
    blk_start = jnp.arange(nb, dtype=jnp.int32) * TMOE
    block_e = jnp.minimum(jnp.sum((pends[None, :] <= blk_start[:, None]).astype(jnp.int32), axis=1),
                          N_EXPERTS - 1).astype(jnp.int32)
    nblk = (pends[-1] // TMOE).astype(jnp.int32).reshape(1)
    pads = jnp.concatenate([pstarts + counts, pcounts - counts, nblk]).astype(jnp.int32)
    return block_e, nblk, dest, pads


PAD_PIECES = tuple(1 << b for b in range(TMOE.bit_length() - 1))


def _rows(tok, n):
    if isinstance(tok, int):
        return pl.ds(tok * ROW_CH, n * ROW_CH)
    return pl.ds(pl.multiple_of(tok * ROW_CH, ROW_CH), n * ROW_CH)


def _dispatch_kernel(dest_ref, pad_ref, h_ref, xs_out, zbuf, sem, zsem):
    base = pl.program_id(0) * (2 * TM)

    @pl.when(pl.program_id(0) == 0)
    def _():
        zbuf[...] = jnp.zeros_like(zbuf)
        ztok = zbuf.shape[0] // ROW_CH
        for phase in range(2):
            for e in range(N_EXPERTS):
                off = pad_ref[e]
                npad = pad_ref[N_EXPERTS + e]
                for piece in PAD_PIECES:
                    has = (npad & piece) != 0

                    @pl.when(has)
                    def _(off=off, piece=piece):
                        cp = pltpu.make_async_copy(zbuf.at[pl.ds(0, piece * ROW_CH)], xs_out.at[_rows(off, piece)], zsem)
                        cp.start() if phase == 0 else cp.wait()

                    off = off + jnp.where(has, piece, 0)

            first = pad_ref[2 * N_EXPERTS] * (TMOE // ztok)

            def tail(j, carry):
                cp = pltpu.make_async_copy(zbuf, xs_out.at[_rows(j * ztok, ztok)], zsem)
                cp.start() if phase == 0 else cp.wait()
                return carry

            lax.fori_loop(first, xs_out.shape[0] // zbuf.shape[0], tail, 0)

    def issue(r, carry):
        for k in range(2):
            pltpu.make_async_copy(h_ref.at[_rows(r, 1)], xs_out.at[_rows(dest_ref[base + 2 * r + k], 1)], sem).start()
        return carry

    lax.fori_loop(0, TM, issue, 0, unroll=8)
    pltpu.make_async_copy(xs_out.at[_rows(0, 2 * TM)], xs_out.at[_rows(0, 2 * TM)], sem).wait()


def _dispatch(n_tiles, n_slots, dest, pads, h2):
    grid_spec = pltpu.PrefetchScalarGridSpec(
        num_scalar_prefetch=2,
        grid=(n_tiles,),
        in_specs=[pl.BlockSpec((TM * ROW_CH, LANE), lambda i, dst, pd: (i, 0))],
        out_specs=pl.BlockSpec(memory_space=pl.ANY),
        scratch_shapes=[pltpu.VMEM((TMOE // 2 * ROW_CH, LANE), F32), pltpu.SemaphoreType.DMA(()),
                        pltpu.SemaphoreType.DMA(())],
    )
    return pl.pallas_call(
        _dispatch_kernel,
        grid_spec=grid_spec,
        out_shape=jax.ShapeDtypeStruct((n_slots * ROW_CH, LANE), F32),
        compiler_params=_cparams(("arbitrary",)),
        name="dispatch",
    )(dest, pads, h2)


def _experts_kernel(be_ref, nblk_ref, xs_ref, wg_ref, wu_ref, wd_ref, ys_ref, wgb, wub, wdb):
    i = pl.program_id(0)

    @pl.when((i == 0) | (be_ref[i] != be_ref[jnp.maximum(i - 1, 0)]))
    def _():
        wgb[...] = wg_ref[...].astype(BF16)
        wub[...] = wu_ref[...].astype(BF16)
        wdb[...] = wd_ref[...].astype(BF16)

    @pl.when(i < nblk_ref[0])
    def _():
        x = _load_rowtiles(xs_ref, TMOE).astype(BF16)
        gate = _dot(x, wgb[...])
        up = _dot(x, wub[...])
        act = (gate * jax.nn.sigmoid(gate)) * up
        _store_rowtiles(ys_ref, _dot(act.astype(BF16), wdb[...]))

    @pl.when(i >= nblk_ref[0])
    def _():
        ys_ref[...] = jnp.zeros_like(ys_ref)


def _experts(layer, block_e, nblk, xs, w_gate, w_up, w_down):
    nb = block_e.shape[0]
    used = lambda i, nk: jnp.minimum(i, nk[0] - 1)
    grid_spec = pltpu.PrefetchScalarGridSpec(
        num_scalar_prefetch=2,
        grid=(nb,),
        in_specs=[
            pl.BlockSpec((TMOE * ROW_CH, LANE), lambda i, be, nk: (used(i, nk), 0)),
            pl.BlockSpec((None, None, D, D_EXPERT), lambda i, be, nk: (layer, be[i], 0, 0)),
            pl.BlockSpec((None, None, D, D_EXPERT), lambda i, be, nk: (layer, be[i], 0, 0)),
            pl.BlockSpec((None, None, D_EXPERT, D), lambda i, be, nk: (layer, be[i], 0, 0)),
        ],
        out_specs=pl.BlockSpec((TMOE * ROW_CH, LANE), lambda i, be, nk: (i, 0)),
        scratch_shapes=[pltpu.VMEM((D, D_EXPERT), BF16), pltpu.VMEM((D, D_EXPERT), BF16),
                        pltpu.VMEM((D_EXPERT, D), BF16)],
    )
    return pl.pallas_call(
        _experts_kernel,
        grid_spec=grid_spec,
        out_shape=jax.ShapeDtypeStruct(xs.shape, F32),
        compiler_params=_cparams(("arbitrary",)),
        name="experts",
    )(block_e, nblk, xs, w_gate, w_up, w_down)


def _combine_tile(dest_ref, x_ref, w_ref, mod_ref, ys_hbm, ybuf, sem):
    i = pl.program_id(0)

    def fetch(tile, slot):
        base = tile * (2 * TM)

        def issue(r, carry):
            for k in range(2):
                pltpu.make_async_copy(ys_hbm.at[_rows(dest_ref[base + 2 * r + k], 1)],
                                      ybuf.at[slot, k, _rows(r, 1)], sem.at[slot]).start()
            return carry

        lax.fori_loop(0, TM, issue, 0, unroll=8)

    @pl.when(i == 0)
    def _():
        fetch(0, 0)

    @pl.when(i + 1 < pl.num_programs(0))
    def _():
        fetch(i + 1, (i + 1) % 2)

    slot = i % 2
    for k in range(2):
        pltpu.make_async_copy(ys_hbm.at[_rows(0, TM)], ybuf.at[slot, k], sem.at[slot]).wait()
    w = w_ref[...]
    y = w[:, 0:1] * _load_rowtiles(ybuf.at[slot, 0], TM) + w[:, 1:2] * _load_rowtiles(ybuf.at[slot, 1], TM)
    return x_ref[...] + mod_ref[5:6, :] * y


def _combine_kernel(dest_ref, x_ref, w_ref, mod_ref, ys_hbm, o_ref, ybuf, sem):
    o_ref[...] = _combine_tile(dest_ref, x_ref, w_ref, mod_ref, ys_hbm, ybuf, sem)


def _combine_in_specs():
    return [pl.BlockSpec((TM, D), lambda i, dst: (i, 0)),
            pl.BlockSpec((TM, 2), lambda i, dst: (i, 0)),
            pl.BlockSpec((None, 6, D), lambda i, dst: (_seg_of_tile(i), 0, 0)),
            pl.BlockSpec(memory_space=pl.ANY)]


COMBINE_SCRATCH = [pltpu.VMEM((2, 2, TM * ROW_CH, LANE), F32), pltpu.SemaphoreType.DMA((2,))]


def _combine(n_tiles, dest, X1, wts, mod_l, ys):
    grid_spec = pltpu.PrefetchScalarGridSpec(
        num_scalar_prefetch=1,
        grid=(n_tiles,),
        in_specs=_combine_in_specs(),
        out_specs=pl.BlockSpec((TM, D), lambda i, dst: (i, 0)),
        scratch_shapes=COMBINE_SCRATCH,
    )
    return pl.pallas_call(
        _combine_kernel,
        grid_spec=grid_spec,
        out_shape=jax.ShapeDtypeStruct((n_tiles * TM, D), F32),
        compiler_params=_cparams(("arbitrary",)),
        name="combine",
    )(dest, X1, wts, mod_l, ys)


def _combine_inproj_kernel(dest_ref, x_ref, w_ref, modp_ref, ys_hbm, mod_ref, g_ref, win_ref,
                           x_out, o_hg, o_hy, o_na, o_mla, ybuf, sem):
    x = _combine_tile(dest_ref, x_ref, w_ref, modp_ref, ys_hbm, ybuf, sem)
    x_out[...] = x
    _inproj_tile(x, mod_ref, g_ref, win_ref, (o_hg, o_hy, o_na, o_mla))


def _combine_inproj(dest, X1, wts, mod_prev, ys, mod_l, g, w_bf):
    p_specs, p_shapes = _inproj_out_specs(lambda i, dst: (i, 0))
    grid_spec = pltpu.PrefetchScalarGridSpec(
        num_scalar_prefetch=1,
        grid=(N_TILES,),
        in_specs=_combine_in_specs() + [
            pl.BlockSpec((None, 6, D), lambda i, dst: (_seg_of_tile(i), 0, 0)),
            pl.BlockSpec((1, D), lambda i, dst: (0, 0)),
            pl.BlockSpec((D, D_IN_PAD), lambda i, dst: (0, 0)),
        ],
        out_specs=[pl.BlockSpec((TM, D), lambda i, dst: (i, 0))] + p_specs,
        scratch_shapes=COMBINE_SCRATCH,
    )
    return pl.pallas_call(
        _combine_inproj_kernel,
        grid_spec=grid_spec,
        out_shape=[jax.ShapeDtypeStruct((T_ALL, D), F32)] + p_shapes,
        compiler_params=_cparams(("arbitrary",)),
        name="combine_inproj",
    )(dest, X1, wts, mod_prev, ys, mod_l, g, w_bf)


def _group_mask(width, group):
    lane = jnp.arange(width)
    return (lane[:, None] // group == lane[None, :] // group)


def _hgrn_tri(reverse):
    t = jnp.arange(HCHUNK)
    same = (t[:, None] // HSUB) == (t[None, :] // HSUB)
    order = (t[None, :] >= t[:, None]) if reverse else (t[None, :] <= t[:, None])
    return jnp.stack([same & order, order]).astype(BF16)


def _mla_weights(w_uq, w_ukv, q_g, k_g):
    wq = jnp.pad(w_uq.reshape(MLA_Q_RANK, NH, MLA_QK), ((0, 0), (0, 0), (0, 128 - MLA_QK))).reshape(MLA_Q_RANK, 512)
    kv = w_ukv.reshape(MLA_KV_RANK, NH, MLA_NOPE + 64)
    wk_top = jnp.pad(kv[:, :, :MLA_NOPE], ((0, 0), (0, 0), (0, 128 - MLA_NOPE))).reshape(MLA_KV_RANK, 512)
    lane = jnp.arange(512)
    src = jnp.arange(128)
    place = ((lane[None, :] % 128) == (src[:, None] + MLA_NOPE)) & (src[:, None] < MLA_ROPE)
    wk = jnp.concatenate([wk_top, place.astype(F32)], axis=0)
    wv = kv[:, :, MLA_NOPE:].reshape(MLA_KV_RANK, GW)
    pad_g = lambda g: jnp.tile(jnp.pad(g, (0, 128 - MLA_QK)), NH).reshape(1, 512)
    return wq.astype(BF16), wk.astype(BF16), wv.astype(BF16), pad_g(q_g), pad_g(k_g)


def kernel(x, c, ctx, c_ctx, w_ada, b_ada, norm1_g, norm2_g, w_in, w_out, hgrn_lb_logits, hgrn_norm_g,
           hy_short_w, hy_short_b, hy_w1, hy_b1, hy_freq, hy_w2, hy_b2, hy_w3, hy_b3, hy_decay, hy_bias,
           na_rpb, na_q_g, na_k_g, mla_q_a_g, mla_kv_a_g, mla_w_uq, mla_w_ukv, mla_q_g, mla_k_g,
           moe_wg, moe_bg, moe_we, moe_be, moe_w_gate, moe_w_up, moe_w_down):
    X = jnp.concatenate([x.reshape(T_LAT, D), ctx.reshape(T_CTX, D)], axis=0)
    cmat = jnp.concatenate([c, c_ctx[None, :], jnp.zeros((16 - B - 1, D), F32)], axis=0)
    mod = _adaln(cmat, w_ada, b_ada).reshape(DEPTH, 16, 6, D)

    lb_cum = jnp.cumsum(jax.nn.softmax(hgrn_lb_logits.astype(F32), axis=0), axis=0)
    lower = lb_cum - lb_cum[0:1]

    gm64 = _group_mask(GW, 64)
    gm64_f = gm64.astype(F32)
    gm64_b = gm64.astype(BF16)
    trif = _hgrn_tri(False)
    trib = _hgrn_tri(True)
    cos_t, sin_t = _rope_tables()
    dft = {}
    for n in (L, CTX):
        cm, sm = _dft_consts(n)
        chi, clo = _split2(cm)
        shi, slo = _split2(sm)
        dft[n] = (chi, clo, shi, slo, _hyena_feats(n))

    pending = None
    for l in range(DEPTH):
        mod_l = mod[l]
        w_in_l = jnp.pad(w_in[l], ((0, 0), (0, D_IN_PAD - D_IN))).astype(BF16)
        if pending is None:
            p_hg, p_hy, p_na, p_mla = _inproj(X, mod_l, norm1_g[l].reshape(1, D), w_in_l)
        else:
            X, p_hg, p_hy, p_na, p_mla = _combine_inproj(*pending, mod_l, norm1_g[l].reshape(1, D), w_in_l)

        lb = lower[l]
        hconst = jnp.concatenate([
            jnp.stack([jnp.maximum(jnp.log(lb[d]), NEG), jnp.log1p(-lb[d]), 1.0 - lb[d]]) for d in range(2)
        ] + [jnp.zeros((2, GW), F32)], axis=0)
        o_f, o_b = _hgrn(p_hg, hconst, trif, trib, gm64_f, gm64_b)

        need_ctx = l < DEPTH - 1
        w1p = jnp.pad(hy_w1[l], ((0, 128 - HYENA_EMB), (0, 0)))
        o_hy = []
        for n, blk0 in ((L, 0), (CTX, T_LAT // CTX)):
            if n == CTX and not need_ctx:
                continue
            chi, clo, shi, slo, feats = dft[n]
            e, o, knq = _hyfilt(feats, w1p, hy_b1[l].reshape(1, -1), hy_freq[l].reshape(1, -1), hy_w2[l],
                                hy_b2[l].reshape(1, -1), hy_w3[l], hy_b3[l].reshape(1, -1),
                                hy_decay[l].reshape(1, 4 * GW))
            kre, kim = _hyspec(chi, clo, shi, slo, e, o)
            o_hy.append(_hyena(p_hy, blk0, B, n, hy_short_w[l], hy_short_b[l].reshape(1, -1), hy_bias[l],
                               chi, shi, kre, kim, knq))

        qn, kn, vn = _naprep(p_na, jnp.tile(na_q_g[l], NH).reshape(1, GW), jnp.tile(na_k_g[l], NH).reshape(1, GW),
                             gm64_b)
        o_na = [_na(qn, kn, vn, _na_bias_table(na_rpb[l]))]

        wq, wk, wv, qg, kg = _mla_weights(mla_w_uq[l], mla_w_ukv[l], mla_q_g[l], mla_k_g[l])
        mq, mk, mv = _mlaprep(p_mla, mla_q_a_g[l].reshape(1, -1), mla_kv_a_g[l].reshape(1, -1), wq, wk, wv,
                              qg, kg, cos_t, sin_t)
        o_mla = [_attn_latent(mq, mk, mv, 128, 64, 256)]
        if need_ctx:
            o_na.append(_attn_ctx(qn, kn, vn, 64, 64))
            o_mla.append(_attn_ctx(mq, mk, mv, 128, 64))
        else:
            o_hy.append(o_hy[0])
            o_na.append(o_na[0])
            o_mla.append(o_mla[0])
        n_tiles = N_TILES if need_ctx else LAT_TILES

        wr = jnp.pad(jnp.concatenate([moe_wg[l], moe_we[l]], axis=1), ((0, 0), (0, 128 - N_GROUPS - N_EXPERTS)))
        wr_hi, wr_lo = _split2(wr)
        X1, h2, logits = _outproj(n_tiles, X, o_f, o_b, p_hg, o_hy, o_na, o_mla, mod_l,
                                  jnp.tile(hgrn_norm_g[l], NH).reshape(1, GW), gm64_b,
                                  w_out[l].astype(BF16), norm2_g[l].reshape(1, D),
                                  jnp.concatenate([wr_hi, wr_lo], axis=1))

        eid, wts, pos, counts = _route(logits, moe_bg[l], moe_be[l])
        block_e, nblk, dest, pads = _dispatch_tables(eid, pos, counts)
        xs = _dispatch(n_tiles, block_e.shape[0] * TMOE, dest, pads, h2)
        ys = _experts(l, block_e, nblk, xs, moe_w_gate, moe_w_up, moe_w_down)
        pending = (dest, X1, wts, mod_l, ys)

    return _combine(LAT_TILES, *pending).reshape(B, L, D)
```

```python
import functools
import math

import jax
import jax.numpy as jnp
from jax import lax
from jax.experimental import pallas as pl
from jax.experimental.pallas import tpu as pltpu

F32 = jnp.float32
BF16 = jnp.bfloat16

D = 1024
B = 8
L = 2048
CTX = 256
DEPTH = 4
GRID_W = 64
EPS = 1e-6
GW = 256
NH = 4
HYENA_BANDS = 16
HYENA_EMB = 1 + 2 * HYENA_BANDS
HYENA_FFN = 64
NA_ROWS = 8
NA_COLS = 16
MLA_Q_RANK = 256
MLA_KV_RANK = 128
MLA_NOPE = 64
MLA_ROPE = 32
MLA_QK = MLA_NOPE + MLA_ROPE
ROPE_BASE = 10000.0
N_GROUPS = 4
EPG = 8
N_EXPERTS = N_GROUPS * EPG
D_EXPERT = 512
D_IN = 3232
D_IN_PAD = 3328

T_LAT = B * L
T_CTX = B * CTX
T_ALL = T_LAT + T_CTX

TM = 512
N_TILES = T_ALL // TM
LAT_TILES = T_LAT // TM
TILES_PER_SEQ = L // TM
HCHUNK = 64
HSUB = 16
HGRN_SAFE_DECAY = 80.0
TMOE = 256
VMEM_LIMIT_BYTES = 56 * 1024 * 1024
NEG = -1e30

HI = lax.Precision.HIGHEST


def _cparams(sem, vmem=VMEM_LIMIT_BYTES):
    return pltpu.CompilerParams(dimension_semantics=sem, vmem_limit_bytes=vmem)


def _seg_of_tile(i):
    return jnp.where(i < LAT_TILES, i // TILES_PER_SEQ, B)


def _dot(a, b):
    return jnp.dot(a, b, preferred_element_type=F32)


def _dot_nt(a, b):
    return lax.dot_general(a, b, (((1,), (1,)), ((), ())), preferred_element_type=F32)


def _dot_tn(a, b):
    return lax.dot_general(a, b, (((0,), (0,)), ((), ())), preferred_element_type=F32)


def _split2(x):
    hi = x.astype(BF16)
    lo = (x - hi.astype(F32)).astype(BF16)
    return hi, lo


def _split3(x):
    h1 = x.astype(BF16)
    r1 = x - h1.astype(F32)
    h2 = r1.astype(BF16)
    h3 = (r1 - h2.astype(F32)).astype(BF16)
    return h1, h2, h3


LANE = 128
ROW_CH = D // LANE


def _store_rowtiles(ref, val):
    n = val.shape[0]
    for j in range(ROW_CH):
        ref[pl.ds(j, n, stride=ROW_CH), :] = val[:, j * LANE:(j + 1) * LANE]


def _load_rowtiles(ref, n):
    return jnp.concatenate([ref[pl.ds(j, n, stride=ROW_CH), :] for j in range(ROW_CH)], axis=-1)


def _group_sum(x, gm):
    hi, lo = _split2(x)
    return _dot(hi, gm) + _dot(lo, gm)


def _ada_kernel(c_ref, w_ref, b_ref, o_ref):
    cc = c_ref[...]
    sc = cc * jax.nn.sigmoid(cc)
    o_ref[0] = jnp.dot(sc, w_ref[0], preferred_element_type=F32, precision=HI) + b_ref[0]


def _adaln(cmat, w_ada, b_ada):
    tn = 1536
    return pl.pallas_call(
        _ada_kernel,
        grid=(DEPTH, 6 * D // tn),
        in_specs=[
            pl.BlockSpec((16, D), lambda l, j: (0, 0)),
            pl.BlockSpec((1, D, tn), lambda l, j: (l, 0, j)),
            pl.BlockSpec((1, 1, tn), lambda l, j: (l, 0, j)),
        ],
        out_specs=pl.BlockSpec((1, 16, tn), lambda l, j: (l, 0, j)),
        out_shape=jax.ShapeDtypeStruct((DEPTH, 16, 6 * D), F32),
        compiler_params=_cparams(("arbitrary", "arbitrary")),
        name="adaln",
    )(cmat, w_ada, b_ada.reshape(DEPTH, 1, 6 * D))


IN_WIDTHS = (1280, 768, 768, 512)


def _inproj_tile(x, mod_ref, g_ref, w_ref, outs):
    ms = jnp.mean(x * x, axis=-1, keepdims=True)
    y = x * lax.rsqrt(ms + EPS) * g_ref[...]
    h = y * (1.0 + mod_ref[1:2, :]) + mod_ref[0:1, :]
    p = _dot(h.astype(BF16), w_ref[...])
    c0 = 0
    for o_ref, w in zip(outs, IN_WIDTHS):
        o_ref[...] = p[:, c0:c0 + w]
        c0 += w


def _inproj_kernel(x_ref, mod_ref, g_ref, w_ref, o_hg, o_hy, o_na, o_mla):
    _inproj_tile(x_ref[...], mod_ref, g_ref, w_ref, (o_hg, o_hy, o_na, o_mla))


def _inproj_out_specs(index_map):
    specs = [pl.BlockSpec((TM, w), index_map) for w in IN_WIDTHS]
    shapes = [jax.ShapeDtypeStruct((T_ALL, w), F32) for w in IN_WIDTHS]
    return specs, shapes


def _inproj(X, mod_l, g, w_bf):
    out_specs, out_shape = _inproj_out_specs(lambda i: (i, 0))
    return pl.pallas_call(
        _inproj_kernel,
        grid=(N_TILES,),
        in_specs=[
            pl.BlockSpec((TM, D), lambda i: (i, 0)),
            pl.BlockSpec((None, 6, D), lambda i: (_seg_of_tile(i), 0, 0)),
            pl.BlockSpec((1, D), lambda i: (0, 0)),
            pl.BlockSpec((D, D_IN_PAD), lambda i: (0, 0)),
        ],
        out_specs=out_specs,
        out_shape=out_shape,
        compiler_params=_cparams(("parallel",)),
        name="inproj",
    )(X, mod_l, g, w_bf)


HSTEP = 4 * HCHUNK


def _hgrn_prologue(p_ref, r0, zcol, c_ref, crow, tri_ref, reverse):
    q = p_ref[r0:r0 + HCHUNK, 0:GW]
    z = p_ref[r0:r0 + HCHUNK, zcol:zcol + GW]
    v = p_ref[r0:r0 + HCHUNK, 3 * GW:4 * GW]
    la = c_ref[crow:crow + 1, :]
    l1 = c_ref[crow + 1:crow + 2, :]
    oml = c_ref[crow + 2:crow + 3, :]
    e = jnp.exp(-jnp.abs(z))
    ope = 1.0 + e
    ls = jnp.minimum(z, 0.0) - jnp.log(ope)
    c2 = l1 + ls
    logf = jnp.maximum(la, c2) + jnp.log(1.0 + jnp.exp(-jnp.abs(la - c2)))
    kk = oml * (jnp.where(z >= 0.0, e, 1.0) / ope)
    h1, h2, h3 = _split3(logf)
    tri_full = tri_ref[1]
    bfull = _dot(tri_full, h1) + _dot(tri_full, h2) + _dot(tri_full, h3)
    half = HCHUNK // 2
    first, mid, last = (HCHUNK - 1, half, 0) if reverse else (0, half - 1, HCHUNK - 1)
    btot = bfull[last:last + 1]
    bmid = bfull[mid:mid + 1]
    worst = jnp.maximum(bfull[first:first + 1] - bmid, bmid - btot)
    return dict(r0=r0, q=q, kk=kk, v=v, splits=(h1, h2, h3), bfull=bfull, btot=btot, bmid=bmid, worst=worst)


def _hgrn_fast(c, st, gm, reverse):
    q, kk, v, bfull, btot, bmid = c["q"], c["kk"], c["v"], c["bfull"], c["btot"], c["bmid"]
    lane_head = lax.broadcasted_iota(jnp.int32, (HCHUNK, GW), 1) // 64
    qi = (q * jnp.exp(bfull - bmid)).astype(BF16)
    ke = (kk * jnp.exp(bmid - bfull)).astype(BF16)
    qx = jnp.concatenate([jnp.where(lane_head == h, qi, jnp.zeros_like(qi)) for h in range(NH)], axis=0)
    a = _dot_nt(qx, ke)
    t_idx = lax.broadcasted_iota(jnp.int32, a.shape, 0) % HCHUNK
    s_idx = lax.broadcasted_iota(jnp.int32, a.shape, 1)
    seen = (s_idx >= t_idx) if reverse else (s_idx <= t_idx)
    a = jnp.where(seen, a, 0.0).astype(BF16)
    vb = v.astype(BF16)
    o_all = _dot(a, vb)
    o = _dot_nt((q * jnp.exp(bfull)).astype(BF16), st.astype(BF16))
    for h in range(NH):
        o = o + jnp.where(lane_head == h, o_all[h * HCHUNK:(h + 1) * HCHUNK, :], 0.0)
    kd = (kk * jnp.exp(btot - bfull)).astype(BF16)
    return o, st * jnp.exp(btot) + _dot_tn(vb, kd) * gm


def _hgrn_slow(c, st, gm, gmb, tri_sub, reverse, o_ref):
    q, kk, v = c["q"], c["kk"], c["v"]
    h1, h2, h3 = c["splits"]
    bsub = _dot(tri_sub, h1) + _dot(tri_sub, h2) + _dot(tri_sub, h3)
    row = lax.broadcasted_iota(jnp.int32, (HSUB, GW), 0)
    order = range(HCHUNK // HSUB - 1, -1, -1) if reverse else range(HCHUNK // HSUB)
    for blk in order:
        r0 = blk * HSUB
        b_i = bsub[r0:r0 + HSUB]
        q_i = q[r0:r0 + HSUB]
        k_i = kk[r0:r0 + HSUB]
        v_i = v[r0:r0 + HSUB]
        bt_i = b_i[0:1] if reverse else b_i[HSUB - 1:HSUB]
        qe = (q_i * jnp.exp(b_i)).astype(BF16)
        o_inter = _dot_nt(qe, st.astype(BF16))
        parts = []
        for tl in range(HSUB):
            dlt = b_i[tl:tl + 1] - b_i
            valid = (row >= tl) if reverse else (row <= tl)
            w = jnp.exp(jnp.where(valid, dlt, NEG))
            parts.append((q_i[tl:tl + 1] * w) * k_i)
        pmat = jnp.concatenate(parts, axis=0).astype(BF16)
        abar = _dot(pmat, gmb)
        o_diag = jnp.sum(abar.reshape(HSUB, HSUB, GW) * v_i[None], axis=1)
        o_ref[c["r0"] + r0:c["r0"] + r0 + HSUB, :] = o_inter + o_diag
        kd = (k_i * jnp.exp(bt_i - b_i)).astype(BF16)
        upd = _dot_tn(v_i.astype(BF16), kd)
        st = st * jnp.exp(bt_i) + upd * gm
    return st


def _hgrn_kernel(pf_ref, pb_ref, c_ref, trif_ref, trib_ref, gm_ref, gmb_ref, of_ref, ob_ref, stf, stb):
    @pl.when(pl.program_id(1) == 0)
    def _():
        stf[...] = jnp.zeros_like(stf)
        stb[...] = jnp.zeros_like(stb)

    gm = gm_ref[...]
    dirs = []
    for p_ref, o_ref, st_ref, tri_ref, zcol, crow, reverse in (
            (pf_ref, of_ref, stf, trif_ref, GW, 0, False), (pb_ref, ob_ref, stb, trib_ref, 2 * GW, 3, True)):
        offs = tuple(range(0, HSTEP, HCHUNK))
        offs = offs[::-1] if reverse else offs
        chunks = [_hgrn_prologue(p_ref, r0, zcol, c_ref, crow, tri_ref, reverse) for r0 in offs]
        dirs.append((o_ref, st_ref, tri_ref, reverse, chunks))
    worst = functools.reduce(jnp.maximum, [c["worst"] for d in dirs for c in d[4]])
    safe = jnp.max(worst) < HGRN_SAFE_DECAY

    @pl.when(safe)
    def _():
        for o_ref, st_ref, tri_ref, reverse, chunks in dirs:
            st = st_ref[...]
            for c in chunks:
                o, st = _hgrn_fast(c, st, gm, reverse)
                o_ref[c["r0"]:c["r0"] + HCHUNK, :] = o
            st_ref[...] = st

    @pl.when(jnp.logical_not(safe))
    def _():
        gmb = gmb_ref[...]
        for o_ref, st_ref, tri_ref, reverse, chunks in dirs:
            st = st_ref[...]
            for c in chunks:
                st = _hgrn_slow(c, st, gm, gmb, tri_ref[0], reverse, o_ref)
            st_ref[...] = st


def _hgrn_block(b, n, reverse):
    nctx = CTX // HSTEP
    nlat = L // HSTEP
    jc = (nctx - 1 - n) if reverse else n
    jl = (nlat - 1 - (n - nctx)) if reverse else (n - nctx)
    return jnp.where(n < nctx, T_LAT // HSTEP + b * nctx + jc, b * nlat + jl)


def _hgrn(p_hg, consts, trif, trib, gm, gmb):
    nsteps = (CTX + L) // HSTEP
    full = lambda shape: pl.BlockSpec(shape, lambda b, n: (0,) * len(shape))
    return pl.pallas_call(
        _hgrn_kernel,
        grid=(B, nsteps),
        in_specs=[
            pl.BlockSpec((HSTEP, 1280), lambda b, n: (_hgrn_block(b, n, False), 0)),
            pl.BlockSpec((HSTEP, 1280), lambda b, n: (_hgrn_block(b, n, True), 0)),
            full((8, GW)),
            full((2, HCHUNK, HCHUNK)),
            full((2, HCHUNK, HCHUNK)),
            full((GW, GW)),
            full((GW, GW)),
        ],
        out_specs=[
            pl.BlockSpec((HSTEP, GW), lambda b, n: (_hgrn_block(b, n, False), 0)),
            pl.BlockSpec((HSTEP, GW), lambda b, n: (_hgrn_block(b, n, True), 0)),
        ],
        out_shape=[jax.ShapeDtypeStruct((T_ALL, GW), F32), jax.ShapeDtypeStruct((T_ALL, GW), F32)],
        scratch_shapes=[pltpu.VMEM((GW, GW), F32), pltpu.VMEM((GW, GW), F32)],
        compiler_params=_cparams(("arbitrary", "arbitrary")),
        name="hgrn",
    )(p_hg, p_hg, consts, trif, trib, gm, gmb)


def _alt_sum(x):
    n, c = x.shape
    sgn = jnp.where((lax.broadcasted_iota(jnp.int32, (n, c), 0) & 1) == 0, 1.0, -1.0)
    return jnp.sum(x * sgn, axis=0, keepdims=True)


def _hyfilt_kernel(feats_ref, w1_ref, b1_ref, fr_ref, w2_ref, b2_ref, w3_ref, b3_ref, dec_ref,
                   e_ref, o_ref, nq_ref):
    fr = fr_ref[...]
    feats = feats_ref[...]
    h = jnp.sin(fr * (jnp.dot(feats, w1_ref[...], preferred_element_type=F32, precision=HI) + b1_ref[...]))
    h = jnp.sin(fr * (jnp.dot(h, w2_ref[...], preferred_element_type=F32, precision=HI) + b2_ref[...]))
    filt = jnp.dot(h, w3_ref[...], preferred_element_type=F32, precision=HI) + b3_ref[...]
    filt = filt * jnp.exp(-feats[:, 0:1] * dec_ref[...])
    n = filt.shape[0]
    row = lax.broadcasted_iota(jnp.int32, (n, GW), 0)
    for o in range(2):
        fwd = filt[:, (2 * o) * GW:(2 * o + 1) * GW]
        bwd = jnp.where(row >= 1, filt[:, (2 * o + 1) * GW:(2 * o + 2) * GW], 0.0)
        ssq = jnp.sum(fwd * fwd + bwd * bwd, axis=0, keepdims=True)
        scale = lax.rsqrt(ssq + EPS)
        ev = (fwd + bwd) * scale
        e_ref[:, o * GW:(o + 1) * GW] = ev
        o_ref[:, o * GW:(o + 1) * GW] = (fwd - bwd) * scale
        nq_ref[:, o * GW:(o + 1) * GW] = _alt_sum(ev) * (0.5 / n)


def _hyfilt(feats, w1p, b1, fr, w2, b2, w3, b3, dec):
    n = feats.shape[0]
    return pl.pallas_call(
        _hyfilt_kernel,
        out_shape=[jax.ShapeDtypeStruct((n, 2 * GW), F32), jax.ShapeDtypeStruct((n, 2 * GW), F32),
                   jax.ShapeDtypeStruct((1, 2 * GW), F32)],
        compiler_params=_cparams(None),
        name="hyfilt",
    )(feats, w1p, b1, fr, w2, b2, w3, b3, dec)


def _hyspec_kernel(chi_ref, clo_ref, shi_ref, slo_ref, e_ref, o_ref, kre_ref, kim_ref, *, n):
    eh, el = _split2(e_ref[...])
    oh, ol = _split2(o_ref[...])
    kre = _dot(chi_ref[...], eh) + _dot(chi_ref[...], el) + _dot(clo_ref[...], eh)
    kim = _dot(shi_ref[...], oh) + _dot(shi_ref[...], ol) + _dot(slo_ref[...], oh)
    tr = kre.shape[0]
    grow = lax.broadcasted_iota(jnp.int32, kre.shape, 0) + pl.program_id(0) * tr
    s2 = 1.0 / n
    kre_ref[...] = kre * jnp.where(grow == 0, 0.5 * s2, s2)
    kim_ref[...] = kim * s2


def _hyspec(chi, clo, shi, slo, e, o):
    n = e.shape[0]
    tr = min(256, n)
    rows = pl.BlockSpec((tr, n), lambda i: (i, 0))
    full = pl.BlockSpec((n, 2 * GW), lambda i: (0, 0))
    outb = pl.BlockSpec((tr, 2 * GW), lambda i: (i, 0))
    return pl.pallas_call(
        functools.partial(_hyspec_kernel, n=n),
        grid=(n // tr,),
        in_specs=[rows, rows, rows, rows, full, full],
        out_specs=[outb, outb],
        out_shape=[jax.ShapeDtypeStruct((n, 2 * GW), F32)] * 2,
        compiler_params=_cparams(("parallel",)),
        name="hyspec",
    )(chi, clo, shi, slo, e, o)


def _hyena_kernel(u_ref, sw_ref, sb_ref, db_ref, c_ref, s_ref, kre_ref, kim_ref, knq_ref, o_ref,
                  z_scr, zb_scr, y_scr):
    n = u_ref.shape[0]
    ft = min(512, n)
    rc = min(256, n)
    nchunks = n // rc
    lrow = lax.broadcasted_iota(jnp.int32, (rc, GW), 0)
    sgn = jnp.where((lrow & 1) == 0, 1.0, -1.0)

    def short_conv(part, c):
        sl = slice(part * GW, (part + 1) * GW)
        r0 = c * rc
        u = u_ref[r0:r0 + rc, sl]
        prev = u_ref[r0 - 1:r0, sl] if c > 0 else jnp.zeros((1, GW), F32)
        nxt = u_ref[r0 + rc:r0 + rc + 1, sl] if c < nchunks - 1 else jnp.zeros((1, GW), F32)
        up = jnp.where(lrow == 0, prev, pltpu.roll(u, 1, 0))
        un = jnp.where(lrow == rc - 1, nxt, pltpu.roll(u, rc - 1, 0))
        return sw_ref[0:1, sl] * up + sw_ref[1:2, sl] * u + sw_ref[2:3, sl] * un + sb_ref[:, sl]

    for c in range(nchunks):
        z_scr[c * rc:(c + 1) * rc, :] = short_conv(0, c)
    for o in range(2):
        cols = slice(o * GW, (o + 1) * GW)
        znq = jnp.zeros((1, GW), F32)
        for c in range(nchunks):
            zc = z_scr[c * rc:(c + 1) * rc, :]
            zb_scr[c * rc:(c + 1) * rc, :] = zc.astype(BF16)
            znq = znq + jnp.sum(zc * sgn, axis=0, keepdims=True)
        ynq = znq * knq_ref[:, cols]
        for c in range(nchunks):
            y_scr[c * rc:(c + 1) * rc, :] = sgn * ynq
        for f in range(n // ft):
            rs = slice(f * ft, (f + 1) * ft)
            zre = _dot(c_ref[rs, :], zb_scr[...])
            zim = _dot(s_ref[rs, :], zb_scr[...])
            kre = kre_ref[rs, cols]
            kim = kim_ref[rs, cols]
            yre = (zre * kre - zim * kim).astype(BF16)
            yim = (zre * kim + zim * kre).astype(BF16)
            y_scr[...] += _dot(c_ref[:, rs], yre) + _dot(s_ref[:, rs], yim)
        dst = o_ref if o == 1 else z_scr
        for c in range(nchunks):
            rows = slice(c * rc, (c + 1) * rc)
            zn = short_conv(o + 1, c) * (y_scr[rows, :] + db_ref[o:o + 1, :] * z_scr[rows, :])
            dst[rows, :] = zn.astype(dst.dtype)


def _hyena(u, blk0, nb, n, sw, sb, db, cm, sm, kre, kim, knq):
    whole = pl.BlockSpec(memory_space=pltpu.VMEM)
    return pl.pallas_call(
        _hyena_kernel,
        grid=(nb,),
        in_specs=[
            pl.BlockSpec((n, 3 * GW), lambda b: (blk0 + b, 0)),
            pl.BlockSpec((3, 3 * GW), lambda b: (0, 0)),
            pl.BlockSpec((1, 3 * GW), lambda b: (0, 0)),
            pl.BlockSpec((2, GW), lambda b: (0, 0)),
            whole, whole, whole, whole, whole,
        ],
        out_specs=pl.BlockSpec((n, GW), lambda b: (b, 0)),
        out_shape=jax.ShapeDtypeStruct((nb * n, GW), BF16),
        scratch_shapes=[pltpu.VMEM((n, GW), F32), pltpu.VMEM((n, GW), BF16), pltpu.VMEM((n, GW), F32)],
        compiler_params=_cparams(("arbitrary",)),
        name="hyena",
    )(u, sw, sb, db, cm, sm, kre, kim, knq)


def _dft_consts(n):
    kk = jnp.arange(n, dtype=jnp.int32)
    ph = (kk[:, None] * kk[None, :]) % (2 * n)
    ang = ph.astype(F32) * (math.pi / n)
    return jnp.cos(ang), -jnp.sin(ang)


def _hyena_feats(n):
    t = jnp.arange(n, dtype=F32)
    t_unit = jnp.linspace(0.0, 1.0, n, dtype=F32)
    bands = jnp.linspace(1e-4, HYENA_BANDS - 1, HYENA_BANDS, dtype=F32)
    ang = (2.0 * math.pi / n) * t[:, None] * bands[None, :]
    feats = jnp.concatenate([t_unit[:, None], jnp.cos(ang), -jnp.sin(ang)], axis=-1)
    return jnp.pad(feats, ((0, 0), (0, 128 - HYENA_EMB)))


def _naprep_kernel(p_ref, qg_ref, kg_ref, gm_ref, q_ref, k_ref, v_ref):
    p = p_ref[...]
    q = p[:, 0:GW]
    k = p[:, GW:2 * GW]
    gm = gm_ref[...]
    qn = q * lax.rsqrt(_group_sum(q * q, gm) * (1.0 / 64) + EPS) * qg_ref[...]
    kn = k * lax.rsqrt(_group_sum(k * k, gm) * (1.0 / 64) + EPS) * kg_ref[...]
    q_ref[...] = (qn * (64 ** -0.5)).astype(BF16)
    k_ref[...] = kn.astype(BF16)
    v_ref[...] = p[:, 2 * GW:3 * GW].astype(BF16)


def _naprep(p_na, qg, kg, gmb):
    tok = lambda w: pl.BlockSpec((TM, w), lambda i: (i, 0))
    full = lambda shape: pl.BlockSpec(shape, lambda i: (0,) * len(shape))
    return pl.pallas_call(
        _naprep_kernel,
        grid=(N_TILES,),
        in_specs=[tok(768), full((1, GW)), full((1, GW)), full((GW, GW))],
        out_specs=[tok(GW), tok(GW), tok(GW)],
        out_shape=[jax.ShapeDtypeStruct((T_ALL, GW), BF16)] * 3,
        compiler_params=_cparams(("parallel",)),
        name="naprep",
    )(p_na, qg, kg, gmb)


NA_RPS = 8


def _na_kernel(q_ref, k_ref, v_ref, kc_ref, vc_ref, bias_ref, o_ref):
    rows = L // GRID_W
    kc = kc_ref[...]
    vc = vc_ref[...]
    for j in range(NA_RPS):
        r = pl.program_id(1) * NA_RPS + j
        rs = jnp.clip(r - NA_ROWS // 2, 0, rows - NA_ROWS)
        variant = r - rs
        start = pl.multiple_of(rs * GRID_W, GRID_W)
        kw = k_ref[pl.ds(start, NA_ROWS * GRID_W), :]
        vw = v_ref[pl.ds(start, NA_ROWS * GRID_W), :]
        q = q_ref[j * GRID_W:(j + 1) * GRID_W, :]
        lane_head = lax.broadcasted_iota(jnp.int32, (GRID_W, GW), 1) // 64
        hmask = [lane_head == h for h in range(NH)]
        qx = jnp.concatenate([jnp.where(hmask[h], q, jnp.zeros_like(q)) for h in range(NH)], axis=0)
        s_loc = _dot_nt(qx, kw) + bias_ref[variant]
        s_ctx = _dot_nt(qx, kc)
        m = jnp.maximum(jnp.max(s_loc, axis=-1, keepdims=True), jnp.max(s_ctx, axis=-1, keepdims=True))
        p_loc = jnp.exp(s_loc - m)
        p_ctx = jnp.exp(s_ctx - m)
        den = jnp.sum(p_loc, axis=-1, keepdims=True) + jnp.sum(p_ctx, axis=-1, keepdims=True)
        o_all = (_dot(p_loc.astype(BF16), vw) + _dot(p_ctx.astype(BF16), vc)) / den
        o = jnp.zeros((GRID_W, GW), F32)
        for h in range(NH):
            o = o + jnp.where(hmask[h], o_all[h * GRID_W:(h + 1) * GRID_W, :], 0.0)
        o_ref[j * GRID_W:(j + 1) * GRID_W, :] = o.astype(o_ref.dtype)


def _na(qn, kn, vn, bias_t):
    steps = L // GRID_W // NA_RPS
    tq = NA_RPS * GRID_W
    ctx_blk = T_LAT // CTX
    return pl.pallas_call(
        _na_kernel,
        grid=(B, steps),
        in_specs=[
            pl.BlockSpec((tq, GW), lambda b, r: (b * steps + r, 0)),
            pl.BlockSpec((L, GW), lambda b, r: (b, 0)),
            pl.BlockSpec((L, GW), lambda b, r: (b, 0)),
            pl.BlockSpec((CTX, GW), lambda b, r: (ctx_blk + b, 0)),
            pl.BlockSpec((CTX, GW), lambda b, r: (ctx_blk + b, 0)),
            pl.BlockSpec((NA_ROWS, NH * GRID_W, NA_ROWS * GRID_W), lambda b, r: (0, 0, 0)),
        ],
        out_specs=pl.BlockSpec((tq, GW), lambda b, r: (b * steps + r, 0)),
        out_shape=jax.ShapeDtypeStruct((T_LAT, GW), BF16),
        compiler_params=_cparams(("arbitrary", "arbitrary")),
        name="na",
    )(qn, kn, vn, kn, vn, bias_t)


def _na_bias_table(rpb):
    cq = jnp.arange(GRID_W)
    cs = jnp.clip(cq - NA_COLS // 2, 0, GRID_W - NA_COLS)
    col_ok = (cq[None, :] >= cs[:, None]) & (cq[None, :] < cs[:, None] + NA_COLS)
    dc = jnp.clip(cq[None, :] - cq[:, None] + (NA_COLS - 1), 0, 2 * NA_COLS - 2)
    onehot = (dc[:, :, None] == jnp.arange(2 * NA_COLS - 1)[None, None, :]).astype(F32)
    full = jnp.einsum('qkc,hrc->hrqk', onehot, rpb.astype(F32), precision=HI)
    full = jnp.where(col_ok[None, None], full, NEG)
    tab = jnp.stack([full[:, NA_ROWS - 1 - a:2 * NA_ROWS - 1 - a] for a in range(NA_ROWS)], axis=0)
    return tab.transpose(0, 1, 3, 2, 4).reshape(NA_ROWS, NH * GRID_W, NA_ROWS * GRID_W)


def _attn_kernel(*refs, nkv, dq, dv):
    q = refs[0][...]
    ks = [refs[1 + 2 * j][...] for j in range(nkv)]
    vs = [refs[2 + 2 * j][...] for j in range(nkv)]
    o_ref = refs[1 + 2 * nkv]
    outs = []
    for h in range(NH):
        qh = q[:, h * dq:(h + 1) * dq]
        ss = [_dot_nt(qh, k[:, h * dq:(h + 1) * dq]) for k in ks]
        m = functools.reduce(jnp.maximum, [jnp.max(s, axis=-1, keepdims=True) for s in ss])
        ps = [jnp.exp(s - m) for s in ss]
        den = functools.reduce(lambda a, b2: a + b2, [jnp.sum(p, axis=-1, keepdims=True) for p in ps])
        o = functools.reduce(lambda a, b2: a + b2,
                             [_dot(p.astype(BF16), v[:, h * dv:(h + 1) * dv]) for p, v in zip(ps, vs)])
        outs.append(o / den)
    o_ref[...] = jnp.concatenate(outs, axis=-1).astype(o_ref.dtype)


def _attn_latent(q, k, v, dq, dv, tq):
    nq = L // tq
    ctx_blk = T_LAT // CTX
    return pl.pallas_call(
        functools.partial(_attn_kernel, nkv=2, dq=dq, dv=dv),
        grid=(B, nq),
        in_specs=[
            pl.BlockSpec((tq, NH * dq), lambda b, i: (b * nq + i, 0)),
            pl.BlockSpec((L, NH * dq), lambda b, i: (b, 0)),
            pl.BlockSpec((L, NH * dv), lambda b, i: (b, 0)),
            pl.BlockSpec((CTX, NH * dq), lambda b, i: (ctx_blk + b, 0)),
            pl.BlockSpec((CTX, NH * dv), lambda b, i: (ctx_blk + b, 0)),
        ],
        out_specs=pl.BlockSpec((tq, NH * dv), lambda b, i: (b * nq + i, 0)),
        out_shape=jax.ShapeDtypeStruct((T_LAT, NH * dv), BF16),
        compiler_params=_cparams(("arbitrary", "arbitrary")),
        name="attn_latent",
    )(q, k, v, k, v)


def _attn_ctx(q, k, v, dq, dv):
    ctx_blk = T_LAT // CTX
    return pl.pallas_call(
        functools.partial(_attn_kernel, nkv=1, dq=dq, dv=dv),
        grid=(B,),
        in_specs=[
            pl.BlockSpec((CTX, NH * dq), lambda b: (ctx_blk + b, 0)),
            pl.BlockSpec((CTX, NH * dq), lambda b: (ctx_blk + b, 0)),
            pl.BlockSpec((CTX, NH * dv), lambda b: (ctx_blk + b, 0)),
        ],
        out_specs=pl.BlockSpec((CTX, NH * dv), lambda b: (b, 0)),
        out_shape=jax.ShapeDtypeStruct((T_CTX, NH * dv), BF16),
        compiler_params=_cparams(("arbitrary",)),
        name="attn_ctx",
    )(q, k, v)


def _mlaprep_kernel(p_ref, qag_ref, kvag_ref, wq_ref, wk_ref, wv_ref, qg_ref, kg_ref,
                    cos_ref, sin_ref, q_ref, k_ref, v_ref):
    p = p_ref[...]
    cq = p[:, 0:MLA_Q_RANK]
    ckv = p[:, MLA_Q_RANK:MLA_Q_RANK + MLA_KV_RANK]
    krp = p[:, MLA_Q_RANK + MLA_KV_RANK:]
    cqn = cq * lax.rsqrt(jnp.mean(cq * cq, axis=-1, keepdims=True) + EPS) * qag_ref[...]
    ckvn = ckv * lax.rsqrt(jnp.mean(ckv * ckv, axis=-1, keepdims=True) + EPS) * kvag_ref[...]
    ckvb = ckvn.astype(BF16)
    q = _dot(cqn.astype(BF16), wq_ref[...])
    k = _dot(jnp.concatenate([ckvb, krp.astype(BF16)], axis=-1), wk_ref[...])
    v = _dot(ckvb, wv_ref[...])

    def head_norm(x, g):
        outs = []
        for h in range(NH):
            xh = x[:, h * LANE:(h + 1) * LANE]
            ms = jnp.sum(xh * xh, axis=-1, keepdims=True) * (1.0 / MLA_QK)
            outs.append(xh * lax.rsqrt(ms + EPS))
        return jnp.concatenate(outs, axis=-1) * g

    cos = cos_ref[...]
    sin = sin_ref[...]
    half = MLA_ROPE // 2
    lane = lax.broadcasted_iota(jnp.int32, cos.shape, 1)
    first = ((lane % LANE - MLA_NOPE) & (half - 1)) < half // 2

    def rope(x):
        width = x.shape[1]
        partner = jnp.where(first, pltpu.roll(x, width - half // 2, 1), pltpu.roll(x, half // 2, 1))
        return x * cos + partner * sin

    q = rope(head_norm(q, qg_ref[...]))
    k = rope(head_norm(k, kg_ref[...]))
    q_ref[...] = (q * (MLA_QK ** -0.5)).astype(BF16)
    k_ref[...] = k.astype(BF16)
    v_ref[...] = v.astype(BF16)


def _mlaprep(p_mla, qag, kvag, wq, wk, wv, qg, kg, cos_t, sin_t):
    tok = lambda w: pl.BlockSpec((TM, w), lambda i: (i, 0))
    full = lambda shape: pl.BlockSpec(shape, lambda i: (0,) * len(shape))
    pos = pl.BlockSpec((TM, 512), lambda i: (jnp.where(i < LAT_TILES, i % TILES_PER_SEQ, TILES_PER_SEQ), 0))
    return pl.pallas_call(
        _mlaprep_kernel,
        grid=(N_TILES,),
        in_specs=[tok(512), full((1, 256)), full((1, 128)), full((256, 512)), full((256, 512)),
                  full((128, 256)), full((1, 512)), full((1, 512)), pos, pos],
        out_specs=[tok(512), tok(512), tok(GW)],
        out_shape=[jax.ShapeDtypeStruct((T_ALL, 512), BF16), jax.ShapeDtypeStruct((T_ALL, 512), BF16),
                   jax.ShapeDtypeStruct((T_ALL, GW), BF16)],
        compiler_params=_cparams(("parallel",)),
        name="mlaprep",
    )(p_mla, qag, kvag, wq, wk, wv, qg, kg, cos_t, sin_t)


def _rope_tables():
    t = jnp.arange(L)
    rowp = (t // GRID_W).astype(F32)
    colp = (t % GRID_W).astype(F32)
    half = MLA_ROPE // 2
    inv = ROPE_BASE ** (-jnp.arange(0, half, 2, dtype=F32) / half)
    j = jnp.arange(MLA_ROPE)
    pos = jnp.where(j[None, :] < half, rowp[:, None], colp[:, None])
    ang = pos * inv[j % (half // 2)][None, :]
    first = (j % half) < (half // 2)
    cos32 = jnp.cos(ang)
    sin32 = jnp.where(first[None, :], -jnp.sin(ang), jnp.sin(ang))
    cos_h = jnp.concatenate([jnp.ones((L, MLA_NOPE), F32), cos32, jnp.ones((L, 32), F32)], axis=-1)
    sin_h = jnp.concatenate([jnp.zeros((L, MLA_NOPE), F32), sin32, jnp.zeros((L, 32), F32)], axis=-1)
    cos_t = jnp.concatenate([jnp.tile(cos_h, (1, NH)), jnp.ones((TM, 512), F32)], axis=0)
    sin_t = jnp.concatenate([jnp.tile(sin_h, (1, NH)), jnp.zeros((TM, 512), F32)], axis=0)
    return cos_t, sin_t


def _outproj_kernel(x_ref, of_ref, ob_ref, g_ref, hyl_ref, hyc_ref, nal_ref, nac_ref, mll_ref, mlc_ref,
                    mod_ref, ng_ref, gm_ref, w_ref, n2_ref, wr_ref, x1_ref, h2_ref, lg_ref):
    oa = of_ref[...] + ob_ref[...]
    ms = _group_sum(oa * oa, gm_ref[...]) * (1.0 / 64)
    g = g_ref[...]
    oa = oa * lax.rsqrt(ms + EPS) * ng_ref[...] * (g * jax.nn.sigmoid(g))
    lat = pl.program_id(0) < LAT_TILES
    hy = jnp.where(lat, hyl_ref[...], hyc_ref[...])
    na = jnp.where(lat, nal_ref[...], nac_ref[...])
    mla = jnp.where(lat, mll_ref[...], mlc_ref[...])
    mix = jnp.concatenate([oa.astype(BF16), hy, na, mla], axis=-1)
    x1 = x_ref[...] + mod_ref[2:3, :] * _dot(mix, w_ref[...])
    x1_ref[...] = x1
    ms2 = jnp.mean(x1 * x1, axis=-1, keepdims=True)
    h2 = x1 * lax.rsqrt(ms2 + EPS) * n2_ref[...] * (1.0 + mod_ref[4:5, :]) + mod_ref[3:4, :]
    _store_rowtiles(h2_ref, h2)
    hh, hl = _split2(h2)
    wr = wr_ref[...]
    lg_ref[...] = _dot(hh, wr[:, 0:128]) + _dot(hl, wr[:, 0:128]) + _dot(hh, wr[:, 128:256])


def _outproj(n_tiles, X, o_f, o_b, p_hg, hy, na, mla, mod_l, ng, gmb, w_bf, n2g, wr):
    tok = lambda w: pl.BlockSpec((TM, w), lambda i: (i, 0))
    full = lambda shape: pl.BlockSpec(shape, lambda i: (0,) * len(shape))
    latb = pl.BlockSpec((TM, GW), lambda i: (jnp.minimum(i, LAT_TILES - 1), 0))
    ctxb = pl.BlockSpec((TM, GW), lambda i: (jnp.maximum(i - LAT_TILES, 0), 0))
    nt = n_tiles * TM
    return pl.pallas_call(
        _outproj_kernel,
        grid=(n_tiles,),
        in_specs=[tok(D), tok(GW), tok(GW), pl.BlockSpec((TM, GW), lambda i: (i, 4)),
                  latb, ctxb, latb, ctxb, latb, ctxb,
                  pl.BlockSpec((None, 6, D), lambda i: (_seg_of_tile(i), 0, 0)),
                  full((1, GW)), full((GW, GW)), full((D, D)), full((1, D)), full((D, 256))],
        out_specs=[tok(D), pl.BlockSpec((TM * ROW_CH, LANE), lambda i: (i, 0)), tok(128)],
        out_shape=[jax.ShapeDtypeStruct((nt, D), F32), jax.ShapeDtypeStruct((nt * ROW_CH, LANE), F32),
                   jax.ShapeDtypeStruct((nt, 128), F32)],
        compiler_params=_cparams(("parallel",)),
        name="outproj",
    )(X, o_f, o_b, p_hg, hy[0], hy[1], na[0], na[1], mla[0], mla[1], mod_l, ng, gmb, w_bf, n2g, wr)


def _route_kernel(lg_ref, b_ref, ltri_ref, o_ref, cnt_ref, base):
    @pl.when(pl.program_id(0) == 0)
    def _():
        base[...] = jnp.zeros_like(base)

    l = lg_ref[...] + b_ref[...]
    lane_i = lax.broadcasted_iota(jnp.int32, l.shape, 1)
    lane = lane_i.astype(F32)

    def first_max(mask):
        v = jnp.max(jnp.where(mask, l, NEG), axis=-1, keepdims=True)
        i = jnp.min(jnp.where(mask & (l == v), lane, float(LANE)), axis=-1, keepdims=True)
        return v, i

    gmask = lane_i < N_GROUPS
    mg, g_sel = first_max(gmask)
    p_sel = 1.0 / jnp.sum(jnp.where(gmask, jnp.exp(jnp.where(gmask, l - mg, 0.0)), 0.0), axis=-1, keepdims=True)
    lane_group = jnp.right_shift(lane_i - N_GROUPS, EPG.bit_length() - 1).astype(F32)
    emask = (lane_i >= N_GROUPS) & (lane_i < N_GROUPS + N_EXPERTS) & (lane_group == g_sel)
    v1, i1 = first_max(emask)
    v2, i2 = first_max(emask & (lane != i1))
    r = jnp.exp(v2 - v1)
    w1 = p_sel / (1.0 + r)
    w2 = w1 * r
    hit1 = lane == i1
    hit2 = lane == i2
    cnt = jnp.where(hit1 | hit2, 1.0, 0.0)
    before = _dot(ltri_ref[...], cnt.astype(BF16)) + base[...]
    pos1 = jnp.sum(jnp.where(hit1, before, 0.0), axis=-1, keepdims=True)
    pos2 = jnp.sum(jnp.where(hit2, before, 0.0), axis=-1, keepdims=True)
    base[...] = base[...] + jnp.sum(cnt, axis=0, keepdims=True)
    cnt_ref[...] = jnp.broadcast_to(base[...], cnt_ref.shape)
    cols = [i1 - N_GROUPS, i2 - N_GROUPS, w1, w2, pos1, pos2]
    out = jnp.zeros(l.shape, F32)
    for j, c in enumerate(cols):
        out = jnp.where(lane_i == j, c, out)
    o_ref[...] = out


def _route(logits, bg, be):
    t = logits.shape[0]
    bias = jnp.pad(jnp.concatenate([bg, be]), (0, LANE - N_GROUPS - N_EXPERTS)).reshape(1, LANE)
    ltri = (jnp.arange(TM)[None, :] < jnp.arange(TM)[:, None]).astype(BF16)
    out, cnt = pl.pallas_call(
        _route_kernel,
        grid=(t // TM,),
        in_specs=[pl.BlockSpec((TM, LANE), lambda i: (i, 0)), pl.BlockSpec((1, LANE), lambda i: (0, 0)),
                  pl.BlockSpec((TM, TM), lambda i: (0, 0))],
        out_specs=[pl.BlockSpec((TM, LANE), lambda i: (i, 0)), pl.BlockSpec((8, LANE), lambda i: (0, 0))],
        out_shape=[jax.ShapeDtypeStruct((t, LANE), F32), jax.ShapeDtypeStruct((8, LANE), F32)],
        scratch_shapes=[pltpu.VMEM((1, LANE), F32)],
        compiler_params=_cparams(("arbitrary",)),
        name="route",
    )(logits, bias, ltri)
    eid = out[:, 0:2].astype(jnp.int32)
    pos = out[:, 4:6].astype(jnp.int32)
    counts = cnt[0, N_GROUPS:N_GROUPS + N_EXPERTS].astype(jnp.int32)
    return eid, out[:, 2:4], pos, counts


def _dispatch_tables(eid, pos, counts):
    t = eid.shape[0]
    n = 2 * t
    nb = n // TMOE + N_EXPERTS
    flat_e = eid.reshape(n)
    pos = pos.reshape(n)
    onehot = (flat_e[:, None] == jnp.arange(N_EXPERTS, dtype=jnp.int32)[None, :]).astype(jnp.int32)
    pcounts = ((counts + TMOE - 1) // TMOE) * TMOE
    pends = jnp.cumsum(pcounts)
    pstarts = pends - pcounts
    dest = (jnp.sum(jnp.where(onehot > 0, pstarts[None, :], 0), axis=1) + pos).astype(jnp.int32)
    blk_start = jnp.arange(nb, dtype=jnp.int32) * TMOE
    block_e = jnp.minimum(jnp.sum((pends[None, :] <= blk_start[:, None]).astype(jnp.int32), axis=1),
                          N_EXPERTS - 1).astype(jnp.int32)
    nblk = (pends[-1] // TMOE).astype(jnp.int32).reshape(1)
    pads = jnp.concatenate([pstarts + counts, pcounts - counts, nblk]).astype(jnp.int32)
    return block_e, nblk, dest, pads


PAD_PIECES = tuple(1 << b for b in range(TMOE.bit_length() - 1))


def _rows(tok, n):
    if isinstance(tok, int):
        return pl.ds(tok * ROW_CH, n * ROW_CH)
    return pl.ds(pl.multiple_of(tok * ROW_CH, ROW_CH), n * ROW_CH)


def _dispatch_kernel(dest_ref, pad_ref, h_ref, xs_out, zbuf, sem, zsem):
    base = pl.program_id(0) * (2 * TM)

    @pl.when(pl.program_id(0) == 0)
    def _():
        zbuf[...] = jnp.zeros_like(zbuf)
        ztok = zbuf.shape[0] // ROW_CH
        for phase in range(2):
            for e in range(N_EXPERTS):
                off = pad_ref[e]
                npad = pad_ref[N_EXPERTS + e]
                for piece in PAD_PIECES:
                    has = (npad & piece) != 0

                    @pl.when(has)
                    def _(off=off, piece=piece):
                        cp = pltpu.make_async_copy(zbuf.at[pl.ds(0, piece * ROW_CH)], xs_out.at[_rows(off, piece)], zsem)
                        cp.start() if phase == 0 else cp.wait()

                    off = off + jnp.where(has, piece, 0)

            first = pad_ref[2 * N_EXPERTS] * (TMOE // ztok)

            def tail(j, carry):
                cp = pltpu.make_async_copy(zbuf, xs_out.at[_rows(j * ztok, ztok)], zsem)
                cp.start() if phase == 0 else cp.wait()
                return carry

            lax.fori_loop(first, xs_out.shape[0] // zbuf.shape[0], tail, 0)

    def issue(r, carry):
        for k in range(2):
            pltpu.make_async_copy(h_ref.at[_rows(r, 1)], xs_out.at[_rows(dest_ref[base + 2 * r + k], 1)], sem).start()
        return carry

    lax.fori_loop(0, TM, issue, 0, unroll=8)
    pltpu.make_async_copy(xs_out.at[_rows(0, 2 * TM)], xs_out.at[_rows(0, 2 * TM)], sem).wait()


def _dispatch(n_tiles, n_slots, dest, pads, h2):
    grid_spec = pltpu.PrefetchScalarGridSpec(
        num_scalar_prefetch=2,
        grid=(n_tiles,),
        in_specs=[pl.BlockSpec((TM * ROW_CH, LANE), lambda i, dst, pd: (i, 0))],
        out_specs=pl.BlockSpec(memory_space=pl.ANY),
        scratch_shapes=[pltpu.VMEM((TMOE // 2 * ROW_CH, LANE), F32), pltpu.SemaphoreType.DMA(()),
                        pltpu.SemaphoreType.DMA(())],
    )
    return pl.pallas_call(
        _dispatch_kernel,
        grid_spec=grid_spec,
        out_shape=jax.ShapeDtypeStruct((n_slots * ROW_CH, LANE), F32),
        compiler_params=_cparams(("arbitrary",)),
        name="dispatch",
    )(dest, pads, h2)


def _experts_kernel(be_ref, nblk_ref, xs_ref, wg_ref, wu_ref, wd_ref, ys_ref, wgb, wub, wdb):
    i = pl.program_id(0)

    @pl.when((i == 0) | (be_ref[i] != be_ref[jnp.maximum(i - 1, 0)]))
    def _():
        wgb[...] = wg_ref[...].astype(BF16)
        wub[...] = wu_ref[...].astype(BF16)
        wdb[...] = wd_ref[...].astype(BF16)

    @pl.when(i < nblk_ref[0])
    def _():
        x = _load_rowtiles(xs_ref, TMOE).astype(BF16)
        gate = _dot(x, wgb[...])
        up = _dot(x, wub[...])
        act = (gate * jax.nn.sigmoid(gate)) * up
        _store_rowtiles(ys_ref, _dot(act.astype(BF16), wdb[...]))

    @pl.when(i >= nblk_ref[0])
    def _():
        ys_ref[...] = jnp.zeros_like(ys_ref)


def _experts(layer, block_e, nblk, xs, w_gate, w_up, w_down):
    nb = block_e.shape[0]
    used = lambda i, nk: jnp.minimum(i, nk[0] - 1)
    grid_spec = pltpu.PrefetchScalarGridSpec(
        num_scalar_prefetch=2,
        grid=(nb,),
        in_specs=[
            pl.BlockSpec((TMOE * ROW_CH, LANE), lambda i, be, nk: (used(i, nk), 0)),
            pl.BlockSpec((None, None, D, D_EXPERT), lambda i, be, nk: (layer, be[i], 0, 0)),
            pl.BlockSpec((None, None, D, D_EXPERT), lambda i, be, nk: (layer, be[i], 0, 0)),
            pl.BlockSpec((None, None, D_EXPERT, D), lambda i, be, nk: (layer, be[i], 0, 0)),
        ],
        out_specs=pl.BlockSpec((TMOE * ROW_CH, LANE), lambda i, be, nk: (i, 0)),
        scratch_shapes=[pltpu.VMEM((D, D_EXPERT), BF16), pltpu.VMEM((D, D_EXPERT), BF16),
                        pltpu.VMEM((D_EXPERT, D), BF16)],
    )
    return pl.pallas_call(
        _experts_kernel,
        grid_spec=grid_spec,
        out_shape=jax.ShapeDtypeStruct(xs.shape, F32),
        compiler_params=_cparams(("arbitrary",)),
        name="experts",
    )(block_e, nblk, xs, w_gate, w_up, w_down)


def _combine_tile(dest_ref, x_ref, w_ref, mod_ref, ys_hbm, ybuf, sem):
    i = pl.program_id(0)

    def fetch(tile, slot):
        base = tile * (2 * TM)

        def issue(r, carry):
            for k in range(2):
                pltpu.make_async_copy(ys_hbm.at[_rows(dest_ref[base + 2 * r + k], 1)],
                                      ybuf.at[slot, k, _rows(r, 1)], sem.at[slot]).start()
            return carry

        lax.fori_loop(0, TM, issue, 0, unroll=8)

    @pl.when(i == 0)
    def _():
        fetch(0, 0)

    @pl.when(i + 1 < pl.num_programs(0))
    def _():
        fetch(i + 1, (i + 1) % 2)

    slot = i % 2
    for k in range(2):
        pltpu.make_async_copy(ys_hbm.at[_rows(0, TM)], ybuf.at[slot, k], sem.at[slot]).wait()
    w = w_ref[...]
    y = w[:, 0:1] * _load_rowtiles(ybuf.at[slot, 0], TM) + w[:, 1:2] * _load_rowtiles(ybuf.at[slot, 1], TM)
    return x_ref[...] + mod_ref[5:6, :] * y


def _combine_kernel(dest_ref, x_ref, w_ref, mod_ref, ys_hbm, o_ref, ybuf, sem):
    o_ref[...] = _combine_tile(dest_ref, x_ref, w_ref, mod_ref, ys_hbm, ybuf, sem)


def _combine_in_specs():
    return [pl.BlockSpec((TM, D), lambda i, dst: (i, 0)),
            pl.BlockSpec((TM, 2), lambda i, dst: (i, 0)),
            pl.BlockSpec((None, 6, D), lambda i, dst: (_seg_of_tile(i), 0, 0)),
            pl.BlockSpec(memory_space=pl.ANY)]


COMBINE_SCRATCH = [pltpu.VMEM((2, 2, TM * ROW_CH, LANE), F32), pltpu.SemaphoreType.DMA((2,))]


def _combine(n_tiles, dest, X1, wts, mod_l, ys):
    grid_spec = pltpu.PrefetchScalarGridSpec(
        num_scalar_prefetch=1,
        grid=(n_tiles,),
        in_specs=_combine_in_specs(),
        out_specs=pl.BlockSpec((TM, D), lambda i, dst: (i, 0)),
        scratch_shapes=COMBINE_SCRATCH,
    )
    return pl.pallas_call(
        _combine_kernel,
        grid_spec=grid_spec,
        out_shape=jax.ShapeDtypeStruct((n_tiles * TM, D), F32),
        compiler_params=_cparams(("arbitrary",)),
        name="combine",
    )(dest, X1, wts, mod_l, ys)


def _combine_inproj_kernel(dest_ref, x_ref, w_ref, modp_ref, ys_hbm, mod_ref, g_ref, win_ref,
                           x_out, o_hg, o_hy, o_na, o_mla, ybuf, sem):
    x = _combine_tile(dest_ref, x_ref, w_ref, modp_ref, ys_hbm, ybuf, sem)
    x_out[...] = x
    _inproj_tile(x, mod_ref, g_ref, win_ref, (o_hg, o_hy, o_na, o_mla))


def _combine_inproj(dest, X1, wts, mod_prev, ys, mod_l, g, w_bf):
    p_specs, p_shapes = _inproj_out_specs(lambda i, dst: (i, 0))
    grid_spec = pltpu.PrefetchScalarGridSpec(
        num_scalar_prefetch=1,
        grid=(N_TILES,),
        in_specs=_combine_in_specs() + [
            pl.BlockSpec((None, 6, D), lambda i, dst: (_seg_of_tile(i), 0, 0)),
            pl.BlockSpec((1, D), lambda i, dst: (0, 0)),
            pl.BlockSpec((D, D_IN_PAD), lambda i, dst: (0, 0)),
        ],
        out_specs=[pl.BlockSpec((TM, D), lambda i, dst: (i, 0))] + p_specs,
        scratch_shapes=COMBINE_SCRATCH,
    )
    return pl.pallas_call(
        _combine_inproj_kernel,
        grid_spec=grid_spec,
        out_shape=[jax.ShapeDtypeStruct((T_ALL, D), F32)] + p_shapes,
        compiler_params=_cparams(("arbitrary",)),
        name="combine_inproj",
    )(dest, X1, wts, mod_prev, ys, mod_l, g, w_bf)


def _group_mask(width, group):
    lane = jnp.arange(width)
    return (lane[:, None] // group == lane[None, :] // group)


def _hgrn_tri(reverse):
    t = jnp.arange(HCHUNK)
    same = (t[:, None] // HSUB) == (t[None, :] // HSUB)
    order = (t[None, :] >= t[:, None]) if reverse else (t[None, :] <= t[:, None])
    return jnp.stack([same & order, order]).astype(BF16)


def _mla_weights(w_uq, w_ukv, q_g, k_g):
    wq = jnp.pad(w_uq.reshape(MLA_Q_RANK, NH, MLA_QK), ((0, 0), (0, 0), (0, 128 - MLA_QK))).reshape(MLA_Q_RANK, 512)
    kv = w_ukv.reshape(MLA_KV_RANK, NH, MLA_NOPE + 64)
    wk_top = jnp.pad(kv[:, :, :MLA_NOPE], ((0, 0), (0, 0), (0, 128 - MLA_NOPE))).reshape(MLA_KV_RANK, 512)
    lane = jnp.arange(512)
    src = jnp.arange(128)
    place = ((lane[None, :] % 128) == (src[:, None] + MLA_NOPE)) & (src[:, None] < MLA_ROPE)
    wk = jnp.concatenate([wk_top, place.astype(F32)], axis=0)
    wv = kv[:, :, MLA_NOPE:].reshape(MLA_KV_RANK, GW)
    pad_g = lambda g: jnp.tile(jnp.pad(g, (0, 128 - MLA_QK)), NH).reshape(1, 512)
    return wq.astype(BF16), wk.astype(BF16), wv.astype(BF16), pad_g(q_g), pad_g(k_g)


def kernel(x, c, ctx, c_ctx, w_ada, b_ada, norm1_g, norm2_g, w_in, w_out, hgrn_lb_logits, hgrn_norm_g,
           hy_short_w, hy_short_b, hy_w1, hy_b1, hy_freq, hy_w2, hy_b2, hy_w3, hy_b3, hy_decay, hy_bias,
           na_rpb, na_q_g, na_k_g, mla_q_a_g, mla_kv_a_g, mla_w_uq, mla_w_ukv, mla_q_g, mla_k_g,
           moe_wg, moe_bg, moe_we, moe_be, moe_w_gate, moe_w_up, moe_w_down):
    X = jnp.concatenate([x.reshape(T_LAT, D), ctx.reshape(T_CTX, D)], axis=0)
    cmat = jnp.concatenate([c, c_ctx[None, :], jnp.zeros((16 - B - 1, D), F32)], axis=0)
    mod = _adaln(cmat, w_ada, b_ada).reshape(DEPTH, 16, 6, D)

    lb_cum = jnp.cumsum(jax.nn.softmax(hgrn_lb_logits.astype(F32), axis=0), axis=0)
    lower = lb_cum - lb_cum[0:1]

    gm64 = _group_mask(GW, 64)
    gm64_f = gm64.astype(F32)
    gm64_b = gm64.astype(BF16)
    trif = _hgrn_tri(False)
    trib = _hgrn_tri(True)
    cos_t, sin_t = _rope_tables()
    dft = {}
    for n in (L, CTX):
        cm, sm = _dft_consts(n)
        chi, clo = _split2(cm)
        shi, slo = _split2(sm)
        dft[n] = (chi, clo, shi, slo, _hyena_feats(n))

    pending = None
    for l in range(DEPTH):
        mod_l = mod[l]
        w_in_l = jnp.pad(w_in[l], ((0, 0), (0, D_IN_PAD - D_IN))).astype(BF16)
        if pending is None:
            p_hg, p_hy, p_na, p_mla = _inproj(X, mod_l, norm1_g[l].reshape(1, D), w_in_l)
        else:
            X, p_hg, p_hy, p_na, p_mla = _combine_inproj(*pending, mod_l, norm1_g[l].reshape(1, D), w_in_l)

        lb = lower[l]
        hconst = jnp.concatenate([
            jnp.stack([jnp.maximum(jnp.log(lb[d]), NEG), jnp.log1p(-lb[d]), 1.0 - lb[d]]) for d in range(2)
        ] + [jnp.zeros((2, GW), F32)], axis=0)
        o_f, o_b = _hgrn(p_hg, hconst, trif, trib, gm64_f, gm64_b)

        need_ctx = l < DEPTH - 1
        w1p = jnp.pad(hy_w1[l], ((0, 128 - HYENA_EMB), (0, 0)))
        o_hy = []
        for n, blk0 in ((L, 0), (CTX, T_LAT // CTX)):
            if n == CTX and not need_ctx:
                continue
            chi, clo, shi, slo, feats = dft[n]
            e, o, knq = _hyfilt(feats, w1p, hy_b1[l].reshape(1, -1), hy_freq[l].reshape(1, -1), hy_w2[l],
                                hy_b2[l].reshape(1, -1), hy_w3[l], hy_b3[l].reshape(1, -1),
                                hy_decay[l].reshape(1, 4 * GW))
            kre, kim = _hyspec(chi, clo, shi, slo, e, o)
            o_hy.append(_hyena(p_hy, blk0, B, n, hy_short_w[l], hy_short_b[l].reshape(1, -1), hy_bias[l],
                               chi, shi, kre, kim, knq))

        qn, kn, vn = _naprep(p_na, jnp.tile(na_q_g[l], NH).reshape(1, GW), jnp.tile(na_k_g[l], NH).reshape(1, GW),
                             gm64_b)
        o_na = [_na(qn, kn, vn, _na_bias_table(na_rpb[l]))]

        wq, wk, wv, qg, kg = _mla_weights(mla_w_uq[l], mla_w_ukv[l], mla_q_g[l], mla_k_g[l])
        mq, mk, mv = _mlaprep(p_mla, mla_q_a_g[l].reshape(1, -1), mla_kv_a_g[l].reshape(1, -1), wq, wk, wv,
                              qg, kg, cos_t, sin_t)
        o_mla = [_attn_latent(mq, mk, mv, 128, 64, 512)]
        if need_ctx:
            o_na.append(_attn_ctx(qn, kn, vn, 64, 64))
            o_mla.append(_attn_ctx(mq, mk, mv, 128, 64))
        else:
            o_hy.append(o_hy[0])
            o_na.append(o_na[0])
            o_mla.append(o_mla[0])
        n_tiles = N_TILES if need_ctx else LAT_TILES

        wr = jnp.pad(jnp.concatenate([moe_wg[l], moe_we[l]], axis=1), ((0, 0), (0, 128 - N_GROUPS - N_EXPERTS)))
        wr_hi, wr_lo = _split2(wr)
        X1, h2, logits = _outproj(n_tiles, X, o_f, o_b, p_hg, o_hy, o_na, o_mla, mod_l,
                                  jnp.tile(hgrn_norm_g[l], NH).reshape(1, GW), gm64_b,
                                  w_out[l].astype(BF16), norm2_g[l].reshape(1, D),
                                  jnp.concatenate([wr_hi, wr_lo], axis=1))

        eid, wts, pos, counts = _route(logits, moe_bg[l], moe_be[l])
        block_e, nblk, dest, pads = _dispatch_tables(eid, pos, counts)
        xs = _dispatch(n_tiles, block_e.shape[0] * TMOE, dest, pads, h2)
        ys = _experts(l, block_e, nblk, xs, moe_w_gate, moe_w_up, moe_w_down)
        pending = (dest, X1, wts, mod_l, ys)

    return _combine(LAT_TILES, *pending).reshape(B, L, D)
```

```python
import functools
import math

import jax
import jax.numpy as jnp
from jax import lax
from jax.experimental import pallas as pl
from jax.experimental.pallas import tpu as pltpu

F32 = jnp.float32
BF16 = jnp.bfloat16

D = 1024
B = 8
L = 2048
CTX = 256
DEPTH = 4
GRID_W = 64
EPS = 1e-6
GW = 256
NH = 4
HYENA_BANDS = 16
HYENA_EMB = 1 + 2 * HYENA_BANDS
HYENA_FFN = 64
NA_ROWS = 8
NA_COLS = 16
MLA_Q_RANK = 256
MLA_KV_RANK = 128
MLA_NOPE = 64
MLA_ROPE = 32
MLA_QK = MLA_NOPE + MLA_ROPE
ROPE_BASE = 10000.0
N_GROUPS = 4
EPG = 8
N_EXPERTS = N_GROUPS * EPG
D_EXPERT = 512
D_IN = 3232
D_IN_PAD = 3328

T_LAT = B * L
T_CTX = B * CTX
T_ALL = T_LAT + T_CTX

TM = 512
N_TILES = T_ALL // TM
LAT_TILES = T_LAT // TM
TILES_PER_SEQ = L // TM
HCHUNK = 64
HSUB = 16
HGRN_SAFE_DECAY = 80.0
TMOE = 256
VMEM_LIMIT_BYTES = 56 * 1024 * 1024
NEG = -1e30

HI = lax.Precision.HIGHEST


def _cparams(sem, vmem=VMEM_LIMIT_BYTES):
    return pltpu.CompilerParams(dimension_semantics=sem, vmem_limit_bytes=vmem)


def _seg_of_tile(i):
    return jnp.where(i < LAT_TILES, i // TILES_PER_SEQ, B)


def _dot(a, b):
    return jnp.dot(a, b, preferred_element_type=F32)


def _dot_nt(a, b):
    return lax.dot_general(a, b, (((1,), (1,)), ((), ())), preferred_element_type=F32)


def _dot_tn(a, b):
    return lax.dot_general(a, b, (((0,), (0,)), ((), ())), preferred_element_type=F32)


def _split2(x):
    hi = x.astype(BF16)
    lo = (x - hi.astype(F32)).astype(BF16)
    return hi, lo


def _split3(x):
    h1 = x.astype(BF16)
    r1 = x - h1.astype(F32)
    h2 = r1.astype(BF16)
    h3 = (r1 - h2.astype(F32)).astype(BF16)
    return h1, h2, h3


LANE = 128
ROW_CH = D // LANE


def _store_rowtiles(ref, val):
    n = val.shape[0]
    for j in range(ROW_CH):
        ref[pl.ds(j, n, stride=ROW_CH), :] = val[:, j * LANE:(j + 1) * LANE]


def _load_rowtiles(ref, n):
    return jnp.concatenate([ref[pl.ds(j, n, stride=ROW_CH), :] for j in range(ROW_CH)], axis=-1)


def _group_sum(x, gm):
    hi, lo = _split2(x)
    return _dot(hi, gm) + _dot(lo, gm)


def _ada_kernel(c_ref, w_ref, b_ref, o_ref):
    cc = c_ref[...]
    sc = cc * jax.nn.sigmoid(cc)
    o_ref[0] = jnp.dot(sc, w_ref[0], preferred_element_type=F32, precision=HI) + b_ref[0]


def _adaln(cmat, w_ada, b_ada):
    tn = 1536
    return pl.pallas_call(
        _ada_kernel,
        grid=(DEPTH, 6 * D // tn),
        in_specs=[
            pl.BlockSpec((16, D), lambda l, j: (0, 0)),
            pl.BlockSpec((1, D, tn), lambda l, j: (l, 0, j)),
            pl.BlockSpec((1, 1, tn), lambda l, j: (l, 0, j)),
        ],
        out_specs=pl.BlockSpec((1, 16, tn), lambda l, j: (l, 0, j)),
        out_shape=jax.ShapeDtypeStruct((DEPTH, 16, 6 * D), F32),
        compiler_params=_cparams(("arbitrary", "arbitrary")),
        name="adaln",
    )(cmat, w_ada, b_ada.reshape(DEPTH, 1, 6 * D))


IN_WIDTHS = (1280, 768, 768, 512)


def _inproj_tile(x, mod_ref, g_ref, w_ref, outs):
    ms = jnp.mean(x * x, axis=-1, keepdims=True)
    y = x * lax.rsqrt(ms + EPS) * g_ref[...]
    h = y * (1.0 + mod_ref[1:2, :]) + mod_ref[0:1, :]
    p = _dot(h.astype(BF16), w_ref[...])
    c0 = 0
    for o_ref, w in zip(outs, IN_WIDTHS):
        o_ref[...] = p[:, c0:c0 + w]
        c0 += w


def _inproj_kernel(x_ref, mod_ref, g_ref, w_ref, o_hg, o_hy, o_na, o_mla):
    _inproj_tile(x_ref[...], mod_ref, g_ref, w_ref, (o_hg, o_hy, o_na, o_mla))


def _inproj_out_specs(index_map):
    specs = [pl.BlockSpec((TM, w), index_map) for w in IN_WIDTHS]
    shapes = [jax.ShapeDtypeStruct((T_ALL, w), F32) for w in IN_WIDTHS]
    return specs, shapes


def _inproj(X, mod_l, g, w_bf):
    out_specs, out_shape = _inproj_out_specs(lambda i: (i, 0))
    return pl.pallas_call(
        _inproj_kernel,
        grid=(N_TILES,),
        in_specs=[
            pl.BlockSpec((TM, D), lambda i: (i, 0)),
            pl.BlockSpec((None, 6, D), lambda i: (_seg_of_tile(i), 0, 0)),
            pl.BlockSpec((1, D), lambda i: (0, 0)),
            pl.BlockSpec((D, D_IN_PAD), lambda i: (0, 0)),
        ],
        out_specs=out_specs,
        out_shape=out_shape,
        compiler_params=_cparams(("parallel",)),
        name="inproj",
    )(X, mod_l, g, w_bf)


HSTEP = 4 * HCHUNK


def _hgrn_prologue(p_ref, r0, zcol, c_ref, crow, tri_ref, reverse):
    q = p_ref[r0:r0 + HCHUNK, 0:GW]
    z = p_ref[r0:r0 + HCHUNK, zcol:zcol + GW]
    v = p_ref[r0:r0 + HCHUNK, 3 * GW:4 * GW]
    la = c_ref[crow:crow + 1, :]
    l1 = c_ref[crow + 1:crow + 2, :]
    oml = c_ref[crow + 2:crow + 3, :]
    e = jnp.exp(-jnp.abs(z))
    ope = 1.0 + e
    ls = jnp.minimum(z, 0.0) - jnp.log(ope)
    c2 = l1 + ls
    logf = jnp.maximum(la, c2) + jnp.log(1.0 + jnp.exp(-jnp.abs(la - c2)))
    kk = oml * (jnp.where(z >= 0.0, e, 1.0) / ope)
    h1, h2, h3 = _split3(logf)
    tri_full = tri_ref[1]
    bfull = _dot(tri_full, h1) + _dot(tri_full, h2) + _dot(tri_full, h3)
    half = HCHUNK // 2
    first, mid, last = (HCHUNK - 1, half, 0) if reverse else (0, half - 1, HCHUNK - 1)
    btot = bfull[last:last + 1]
    bmid = bfull[mid:mid + 1]
    worst = jnp.maximum(bfull[first:first + 1] - bmid, bmid - btot)
    return dict(r0=r0, q=q, kk=kk, v=v, splits=(h1, h2, h3), bfull=bfull, btot=btot, bmid=bmid, worst=worst)


def _hgrn_fast(c, st, gm, reverse):
    q, kk, v, bfull, btot, bmid = c["q"], c["kk"], c["v"], c["bfull"], c["btot"], c["bmid"]
    lane_head = lax.broadcasted_iota(jnp.int32, (HCHUNK, GW), 1) // 64
    qi = (q * jnp.exp(bfull - bmid)).astype(BF16)
    ke = (kk * jnp.exp(bmid - bfull)).astype(BF16)
    qx = jnp.concatenate([jnp.where(lane_head == h, qi, jnp.zeros_like(qi)) for h in range(NH)], axis=0)
    a = _dot_nt(qx, ke)
    t_idx = lax.broadcasted_iota(jnp.int32, a.shape, 0) % HCHUNK
    s_idx = lax.broadcasted_iota(jnp.int32, a.shape, 1)
    seen = (s_idx >= t_idx) if reverse else (s_idx <= t_idx)
    a = jnp.where(seen, a, 0.0).astype(BF16)
    vb = v.astype(BF16)
    o_all = _dot(a, vb)
    o = _dot_nt((q * jnp.exp(bfull)).astype(BF16), st.astype(BF16))
    for h in range(NH):
        o = o + jnp.where(lane_head == h, o_all[h * HCHUNK:(h + 1) * HCHUNK, :], 0.0)
    kd = (kk * jnp.exp(btot - bfull)).astype(BF16)
    return o, st * jnp.exp(btot) + _dot_tn(vb, kd) * gm


def _hgrn_slow(c, st, gm, gmb, tri_sub, reverse, o_ref):
    q, kk, v = c["q"], c["kk"], c["v"]
    h1, h2, h3 = c["splits"]
    bsub = _dot(tri_sub, h1) + _dot(tri_sub, h2) + _dot(tri_sub, h3)
    row = lax.broadcasted_iota(jnp.int32, (HSUB, GW), 0)
    order = range(HCHUNK // HSUB - 1, -1, -1) if reverse else range(HCHUNK // HSUB)
    for blk in order:
        r0 = blk * HSUB
        b_i = bsub[r0:r0 + HSUB]
        q_i = q[r0:r0 + HSUB]
        k_i = kk[r0:r0 + HSUB]
        v_i = v[r0:r0 + HSUB]
        bt_i = b_i[0:1] if reverse else b_i[HSUB - 1:HSUB]
        qe = (q_i * jnp.exp(b_i)).astype(BF16)
        o_inter = _dot_nt(qe, st.astype(BF16))
        parts = []
        for tl in range(HSUB):
            dlt = b_i[tl:tl + 1] - b_i
            valid = (row >= tl) if reverse else (row <= tl)
            w = jnp.exp(jnp.where(valid, dlt, NEG))
            parts.append((q_i[tl:tl + 1] * w) * k_i)
        pmat = jnp.concatenate(parts, axis=0).astype(BF16)
        abar = _dot(pmat, gmb)
        o_diag = jnp.sum(abar.reshape(HSUB, HSUB, GW) * v_i[None], axis=1)
        o_ref[c["r0"] + r0:c["r0"] + r0 + HSUB, :] = o_inter + o_diag
        kd = (k_i * jnp.exp(bt_i - b_i)).astype(BF16)
        upd = _dot_tn(v_i.astype(BF16), kd)
        st = st * jnp.exp(bt_i) + upd * gm
    return st


def _hgrn_kernel(pf_ref, pb_ref, c_ref, trif_ref, trib_ref, gm_ref, gmb_ref, of_ref, ob_ref, stf, stb):
    @pl.when(pl.program_id(1) == 0)
    def _():
        stf[...] = jnp.zeros_like(stf)
        stb[...] = jnp.zeros_like(stb)

    gm = gm_ref[...]
    dirs = []
    for p_ref, o_ref, st_ref, tri_ref, zcol, crow, reverse in (
            (pf_ref, of_ref, stf, trif_ref, GW, 0, False), (pb_ref, ob_ref, stb, trib_ref, 2 * GW, 3, True)):
        offs = tuple(range(0, HSTEP, HCHUNK))
        offs = offs[::-1] if reverse else offs
        chunks = [_hgrn_prologue(p_ref, r0, zcol, c_ref, crow, tri_ref, reverse) for r0 in offs]
        dirs.append((o_ref, st_ref, tri_ref, reverse, chunks))
    worst = functools.reduce(jnp.maximum, [c["worst"] for d in dirs for c in d[4]])
    safe = jnp.max(worst) < HGRN_SAFE_DECAY

    @pl.when(safe)
    def _():
        for o_ref, st_ref, tri_ref, reverse, chunks in dirs:
            st = st_ref[...]
            for c in chunks:
                o, st = _hgrn_fast(c, st, gm, reverse)
                o_ref[c["r0"]:c["r0"] + HCHUNK, :] = o
            st_ref[...] = st

    @pl.when(jnp.logical_not(safe))
    def _():
        gmb = gmb_ref[...]
        for o_ref, st_ref, tri_ref, reverse, chunks in dirs:
            st = st_ref[...]
            for c in chunks:
                st = _hgrn_slow(c, st, gm, gmb, tri_ref[0], reverse, o_ref)
            st_ref[...] = st


def _hgrn_block(b, n, reverse):
    nctx = CTX // HSTEP
    nlat = L // HSTEP
    jc = (nctx - 1 - n) if reverse else n
    jl = (nlat - 1 - (n - nctx)) if reverse else (n - nctx)
    return jnp.where(n < nctx, T_LAT // HSTEP + b * nctx + jc, b * nlat + jl)


def _hgrn(p_hg, consts, trif, trib, gm, gmb):
    nsteps = (CTX + L) // HSTEP
    full = lambda shape: pl.BlockSpec(shape, lambda b, n: (0,) * len(shape))
    return pl.pallas_call(
        _hgrn_kernel,
        grid=(B, nsteps),
        in_specs=[
            pl.BlockSpec((HSTEP, 1280), lambda b, n: (_hgrn_block(b, n, False), 0)),
            pl.BlockSpec((HSTEP, 1280), lambda b, n: (_hgrn_block(b, n, True), 0)),
            full((8, GW)),
            full((2, HCHUNK, HCHUNK)),
            full((2, HCHUNK, HCHUNK)),
            full((GW, GW)),
            full((GW, GW)),
        ],
        out_specs=[
            pl.BlockSpec((HSTEP, GW), lambda b, n: (_hgrn_block(b, n, False), 0)),
            pl.BlockSpec((HSTEP, GW), lambda b, n: (_hgrn_block(b, n, True), 0)),
        ],
        out_shape=[jax.ShapeDtypeStruct((T_ALL, GW), F32), jax.ShapeDtypeStruct((T_ALL, GW), F32)],
        scratch_shapes=[pltpu.VMEM((GW, GW), F32), pltpu.VMEM((GW, GW), F32)],
        compiler_params=_cparams(("arbitrary", "arbitrary")),
        name="hgrn",
    )(p_hg, p_hg, consts, trif, trib, gm, gmb)


def _alt_sum(x):
    n, c = x.shape
    sgn = jnp.where((lax.broadcasted_iota(jnp.int32, (n, c), 0) & 1) == 0, 1.0, -1.0)
    return jnp.sum(x * sgn, axis=0, keepdims=True)


def _hyfilt_kernel(feats_ref, w1_ref, b1_ref, fr_ref, w2_ref, b2_ref, w3_ref, b3_ref, dec_ref,
                   e_ref, o_ref, nq_ref):
    fr = fr_ref[...]
    feats = feats_ref[...]
    h = jnp.sin(fr * (jnp.dot(feats, w1_ref[...], preferred_element_type=F32, precision=HI) + b1_ref[...]))
    h = jnp.sin(fr * (jnp.dot(h, w2_ref[...], preferred_element_type=F32, precision=HI) + b2_ref[...]))
    filt = jnp.dot(h, w3_ref[...], preferred_element_type=F32, precision=HI) + b3_ref[...]
    filt = filt * jnp.exp(-feats[:, 0:1] * dec_ref[...])
    n = filt.shape[0]
    row = lax.broadcasted_iota(jnp.int32, (n, GW), 0)
    for o in range(2):
        fwd = filt[:, (2 * o) * GW:(2 * o + 1) * GW]
        bwd = jnp.where(row >= 1, filt[:, (2 * o + 1) * GW:(2 * o + 2) * GW], 0.0)
        ssq = jnp.sum(fwd * fwd + bwd * bwd, axis=0, keepdims=True)
        scale = lax.rsqrt(ssq + EPS)
        ev = (fwd + bwd) * scale
        e_ref[:, o * GW:(o + 1) * GW] = ev
        o_ref[:, o * GW:(o + 1) * GW] = (fwd - bwd) * scale
        nq_ref[:, o * GW:(o + 1) * GW] = _alt_sum(ev) * (0.5 / n)


def _hyfilt(feats, w1p, b1, fr, w2, b2, w3, b3, dec):
    n = feats.shape[0]
    return pl.pallas_call(
        _hyfilt_kernel,
        out_shape=[jax.ShapeDtypeStruct((n, 2 * GW), F32), jax.ShapeDtypeStruct((n, 2 * GW), F32),
                   jax.ShapeDtypeStruct((1, 2 * GW), F32)],
        compiler_params=_cparams(None),
        name="hyfilt",
    )(feats, w1p, b1, fr, w2, b2, w3, b3, dec)


def _hyspec_kernel(chi_ref, clo_ref, shi_ref, slo_ref, e_ref, o_ref, kre_ref, kim_ref, *, n):
    eh, el = _split2(e_ref[...])
    oh, ol = _split2(o_ref[...])
    kre = _dot(chi_ref[...], eh) + _dot(chi_ref[...], el) + _dot(clo_ref[...], eh)
    kim = _dot(shi_ref[...], oh) + _dot(shi_ref[...], ol) + _dot(slo_ref[...], oh)
    tr = kre.shape[0]
    grow = lax.broadcasted_iota(jnp.int32, kre.shape, 0) + pl.program_id(0) * tr
    s2 = 1.0 / n
    kre_ref[...] = kre * jnp.where(grow == 0, 0.5 * s2, s2)
    kim_ref[...] = kim * s2


def _hyspec(chi, clo, shi, slo, e, o):
    n = e.shape[0]
    tr = min(256, n)
    rows = pl.BlockSpec((tr, n), lambda i: (i, 0))
    full = pl.BlockSpec((n, 2 * GW), lambda i: (0, 0))
    outb = pl.BlockSpec((tr, 2 * GW), lambda i: (i, 0))
    return pl.pallas_call(
        functools.partial(_hyspec_kernel, n=n),
        grid=(n // tr,),
        in_specs=[rows, rows, rows, rows, full, full],
        out_specs=[outb, outb],
        out_shape=[jax.ShapeDtypeStruct((n, 2 * GW), F32)] * 2,
        compiler_params=_cparams(("parallel",)),
        name="hyspec",
    )(chi, clo, shi, slo, e, o)


def _hyena_kernel(u_ref, sw_ref, sb_ref, db_ref, c_ref, s_ref, kre_ref, kim_ref, knq_ref, o_ref,
                  z_scr, zb_scr, y_scr):
    n = u_ref.shape[0]
    ft = min(512, n)
    rc = min(256, n)
    nchunks = n // rc
    lrow = lax.broadcasted_iota(jnp.int32, (rc, GW), 0)
    sgn = jnp.where((lrow & 1) == 0, 1.0, -1.0)

    def short_conv(part, c):
        sl = slice(part * GW, (part + 1) * GW)
        r0 = c * rc
        u = u_ref[r0:r0 + rc, sl]
        prev = u_ref[r0 - 1:r0, sl] if c > 0 else jnp.zeros((1, GW), F32)
        nxt = u_ref[r0 + rc:r0 + rc + 1, sl] if c < nchunks - 1 else jnp.zeros((1, GW), F32)
        up = jnp.where(lrow == 0, prev, pltpu.roll(u, 1, 0))
        un = jnp.where(lrow == rc - 1, nxt, pltpu.roll(u, rc - 1, 0))
        return sw_ref[0:1, sl] * up + sw_ref[1:2, sl] * u + sw_ref[2:3, sl] * un + sb_ref[:, sl]

    for c in range(nchunks):
        z_scr[c * rc:(c + 1) * rc, :] = short_conv(0, c)
    for o in range(2):
        cols = slice(o * GW, (o + 1) * GW)
        znq = jnp.zeros((1, GW), F32)
        for c in range(nchunks):
            zc = z_scr[c * rc:(c + 1) * rc, :]
            zb_scr[c * rc:(c + 1) * rc, :] = zc.astype(BF16)
            znq = znq + jnp.sum(zc * sgn, axis=0, keepdims=True)
        ynq = znq * knq_ref[:, cols]
        for c in range(nchunks):
            y_scr[c * rc:(c + 1) * rc, :] = sgn * ynq
        for f in range(n // ft):
            rs = slice(f * ft, (f + 1) * ft)
            zre = _dot(c_ref[rs, :], zb_scr[...])
            zim = _dot(s_ref[rs, :], zb_scr[...])
            kre = kre_ref[rs, cols]
            kim = kim_ref[rs, cols]
            yre = (zre * kre - zim * kim).astype(BF16)
            yim = (zre * kim + zim * kre).astype(BF16)
            y_scr[...] += _dot(c_ref[:, rs], yre) + _dot(s_ref[:, rs], yim)
        dst = o_ref if o == 1 else z_scr
        for c in range(nchunks):
            rows = slice(c * rc, (c + 1) * rc)
            zn = short_conv(o + 1, c) * (y_scr[rows, :] + db_ref[o:o + 1, :] * z_scr[rows, :])
            dst[rows, :] = zn.astype(dst.dtype)


def _hyena(u, blk0, nb, n, sw, sb, db, cm, sm, kre, kim, knq):
    whole = pl.BlockSpec(memory_space=pltpu.VMEM)
    return pl.pallas_call(
        _hyena_kernel,
        grid=(nb,),
        in_specs=[
            pl.BlockSpec((n, 3 * GW), lambda b: (blk0 + b, 0)),
            pl.BlockSpec((3, 3 * GW), lambda b: (0, 0)),
            pl.BlockSpec((1, 3 * GW), lambda b: (0, 0)),
            pl.BlockSpec((2, GW), lambda b: (0, 0)),
            whole, whole, whole, whole, whole,
        ],
        out_specs=pl.BlockSpec((n, GW), lambda b: (b, 0)),
        out_shape=jax.ShapeDtypeStruct((nb * n, GW), BF16),
        scratch_shapes=[pltpu.VMEM((n, GW), F32), pltpu.VMEM((n, GW), BF16), pltpu.VMEM((n, GW), F32)],
        compiler_params=_cparams(("arbitrary",)),
        name="hyena",
    )(u, sw, sb, db, cm, sm, kre, kim, knq)


def _dft_consts(n):
    kk = jnp.arange(n, dtype=jnp.int32)
    ph = (kk[:, None] * kk[None, :]) % (2 * n)
    ang = ph.astype(F32) * (math.pi / n)
    return jnp.cos(ang), -jnp.sin(ang)


def _hyena_feats(n):
    t = jnp.arange(n, dtype=F32)
    t_unit = jnp.linspace(0.0, 1.0, n, dtype=F32)
    bands = jnp.linspace(1e-4, HYENA_BANDS - 1, HYENA_BANDS, dtype=F32)
    ang = (2.0 * math.pi / n) * t[:, None] * bands[None, :]
    feats = jnp.concatenate([t_unit[:, None], jnp.cos(ang), -jnp.sin(ang)], axis=-1)
    return jnp.pad(feats, ((0, 0), (0, 128 - HYENA_EMB)))


def _naprep_kernel(p_ref, qg_ref, kg_ref, gm_ref, q_ref, k_ref, v_ref):
    p = p_ref[...]
    q = p[:, 0:GW]
    k = p[:, GW:2 * GW]
    gm = gm_ref[...]
    qn = q * lax.rsqrt(_group_sum(q * q, gm) * (1.0 / 64) + EPS) * qg_ref[...]
    kn = k * lax.rsqrt(_group_sum(k * k, gm) * (1.0 / 64) + EPS) * kg_ref[...]
    q_ref[...] = (qn * (64 ** -0.5)).astype(BF16)
    k_ref[...] = kn.astype(BF16)
    v_ref[...] = p[:, 2 * GW:3 * GW].astype(BF16)


def _naprep(p_na, qg, kg, gmb):
    tok = lambda w: pl.BlockSpec((TM, w), lambda i: (i, 0))
    full = lambda shape: pl.BlockSpec(shape, lambda i: (0,) * len(shape))
    return pl.pallas_call(
        _naprep_kernel,
        grid=(N_TILES,),
        in_specs=[tok(768), full((1, GW)), full((1, GW)), full((GW, GW))],
        out_specs=[tok(GW), tok(GW), tok(GW)],
        out_shape=[jax.ShapeDtypeStruct((T_ALL, GW), BF16)] * 3,
        compiler_params=_cparams(("parallel",)),
        name="naprep",
    )(p_na, qg, kg, gmb)


NA_RPS = 8


def _na_kernel(q_ref, k_ref, v_ref, kc_ref, vc_ref, bias_ref, o_ref):
    rows = L // GRID_W
    kc = kc_ref[...]
    vc = vc_ref[...]
    for j in range(NA_RPS):
        r = pl.program_id(1) * NA_RPS + j
        rs = jnp.clip(r - NA_ROWS // 2, 0, rows - NA_ROWS)
        variant = r - rs
        start = pl.multiple_of(rs * GRID_W, GRID_W)
        kw = k_ref[pl.ds(start, NA_ROWS * GRID_W), :]
        vw = v_ref[pl.ds(start, NA_ROWS * GRID_W), :]
        q = q_ref[j * GRID_W:(j + 1) * GRID_W, :]
        lane_head = lax.broadcasted_iota(jnp.int32, (GRID_W, GW), 1) // 64
        hmask = [lane_head == h for h in range(NH)]
        qx = jnp.concatenate([jnp.where(hmask[h], q, jnp.zeros_like(q)) for h in range(NH)], axis=0)
        s_loc = _dot_nt(qx, kw) + bias_ref[variant]
        s_ctx = _dot_nt(qx, kc)
        m = jnp.maximum(jnp.max(s_loc, axis=-1, keepdims=True), jnp.max(s_ctx, axis=-1, keepdims=True))
        p_loc = jnp.exp(s_loc - m)
        p_ctx = jnp.exp(s_ctx - m)
        den = jnp.sum(p_loc, axis=-1, keepdims=True) + jnp.sum(p_ctx, axis=-1, keepdims=True)
        o_all = (_dot(p_loc.astype(BF16), vw) + _dot(p_ctx.astype(BF16), vc)) / den
        o = jnp.zeros((GRID_W, GW), F32)
        for h in range(NH):
            o = o + jnp.where(hmask[h], o_all[h * GRID_W:(h + 1) * GRID_W, :], 0.0)
        o_ref[j * GRID_W:(j + 1) * GRID_W, :] = o.astype(o_ref.dtype)


def _na(qn, kn, vn, bias_t):
    steps = L // GRID_W // NA_RPS
    tq = NA_RPS * GRID_W
    ctx_blk = T_LAT // CTX
    return pl.pallas_call(
        _na_kernel,
        grid=(B, steps),
        in_specs=[
            pl.BlockSpec((tq, GW), lambda b, r: (b * steps + r, 0)),
            pl.BlockSpec((L, GW), lambda b, r: (b, 0)),
            pl.BlockSpec((L, GW), lambda b, r: (b, 0)),
            pl.BlockSpec((CTX, GW), lambda b, r: (ctx_blk + b, 0)),
            pl.BlockSpec((CTX, GW), lambda b, r: (ctx_blk + b, 0)),
            pl.BlockSpec((NA_ROWS, NH * GRID_W, NA_ROWS * GRID_W), lambda b, r: (0, 0, 0)),
        ],
        out_specs=pl.BlockSpec((tq, GW), lambda b, r: (b * steps + r, 0)),
        out_shape=jax.ShapeDtypeStruct((T_LAT, GW), BF16),
        compiler_params=_cparams(("arbitrary", "arbitrary")),
        name="na",
    )(qn, kn, vn, kn, vn, bias_t)


def _na_bias_table(rpb):
    cq = jnp.arange(GRID_W)
    cs = jnp.clip(cq - NA_COLS // 2, 0, GRID_W - NA_COLS)
    col_ok = (cq[None, :] >= cs[:, None]) & (cq[None, :] < cs[:, None] + NA_COLS)
    dc = jnp.clip(cq[None, :] - cq[:, None] + (NA_COLS - 1), 0, 2 * NA_COLS - 2)
    onehot = (dc[:, :, None] == jnp.arange(2 * NA_COLS - 1)[None, None, :]).astype(F32)
    full = jnp.einsum('qkc,hrc->hrqk', onehot, rpb.astype(F32), precision=HI)
    full = jnp.where(col_ok[None, None], full, NEG)
    tab = jnp.stack([full[:, NA_ROWS - 1 - a:2 * NA_ROWS - 1 - a] for a in range(NA_ROWS)], axis=0)
    return tab.transpose(0, 1, 3, 2, 4).reshape(NA_ROWS, NH * GRID_W, NA_ROWS * GRID_W)


def _attn_kernel(*refs, nkv, dq, dv):
    q = refs[0][...]
    ks = [refs[1 + 2 * j][...] for j in range(nkv)]
    vs = [refs[2 + 2 * j][...] for j in range(nkv)]
    o_ref = refs[1 + 2 * nkv]
    outs = []
    for h in range(NH):
        qh = q[:, h * dq:(h + 1) * dq]
        ss = [_dot_nt(qh, k[:, h * dq:(h + 1) * dq]) for k in ks]
        m = functools.reduce(jnp.maximum, [jnp.max(s, axis=-1, keepdims=True) for s in ss])
        ps = [jnp.exp(s - m) for s in ss]
        den = functools.reduce(lambda a, b2: a + b2, [jnp.sum(p, axis=-1, keepdims=True) for p in ps])
        o = functools.reduce(lambda a, b2: a + b2,
                             [_dot(p.astype(BF16), v[:, h * dv:(h + 1) * dv]) for p, v in zip(ps, vs)])
        outs.append(o / den)
    o_ref[...] = jnp.concatenate(outs, axis=-1).astype(o_ref.dtype)


def _attn_latent(q, k, v, dq, dv, tq):
    nq = L // tq
    ctx_blk = T_LAT // CTX
    return pl.pallas_call(
        functools.partial(_attn_kernel, nkv=2, dq=dq, dv=dv),
        grid=(B, nq),
        in_specs=[
            pl.BlockSpec((tq, NH * dq), lambda b, i: (b * nq + i, 0)),
            pl.BlockSpec((L, NH * dq), lambda b, i: (b, 0)),
            pl.BlockSpec((L, NH * dv), lambda b, i: (b, 0)),
            pl.BlockSpec((CTX, NH * dq), lambda b, i: (ctx_blk + b, 0)),
            pl.BlockSpec((CTX, NH * dv), lambda b, i: (ctx_blk + b, 0)),
        ],
        out_specs=pl.BlockSpec((tq, NH * dv), lambda b, i: (b * nq + i, 0)),
        out_shape=jax.ShapeDtypeStruct((T_LAT, NH * dv), BF16),
        compiler_params=_cparams(("arbitrary", "arbitrary")),
        name="attn_latent",
    )(q, k, v, k, v)


def _attn_ctx(q, k, v, dq, dv):
    ctx_blk = T_LAT // CTX
    return pl.pallas_call(
        functools.partial(_attn_kernel, nkv=1, dq=dq, dv=dv),
        grid=(B,),
        in_specs=[
            pl.BlockSpec((CTX, NH * dq), lambda b: (ctx_blk + b, 0)),
            pl.BlockSpec((CTX, NH * dq), lambda b: (ctx_blk + b, 0)),
            pl.BlockSpec((CTX, NH * dv), lambda b: (ctx_blk + b, 0)),
        ],
        out_specs=pl.BlockSpec((CTX, NH * dv), lambda b: (b, 0)),
        out_shape=jax.ShapeDtypeStruct((T_CTX, NH * dv), BF16),
        compiler_params=_cparams(("arbitrary",)),
        name="attn_ctx",
    )(q, k, v)


def _mlaprep_kernel(p_ref, qag_ref, kvag_ref, wq_ref, wk_ref, wv_ref, qg_ref, kg_ref,
                    cos_ref, sin_ref, q_ref, k_ref, v_ref):
    p = p_ref[...]
    cq = p[:, 0:MLA_Q_RANK]
    ckv = p[:, MLA_Q_RANK:MLA_Q_RANK + MLA_KV_RANK]
    krp = p[:, MLA_Q_RANK + MLA_KV_RANK:]
    cqn = cq * lax.rsqrt(jnp.mean(cq * cq, axis=-1, keepdims=True) + EPS) * qag_ref[...]
    ckvn = ckv * lax.rsqrt(jnp.mean(ckv * ckv, axis=-1, keepdims=True) + EPS) * kvag_ref[...]
    ckvb = ckvn.astype(BF16)
    q = _dot(cqn.astype(BF16), wq_ref[...])
    k = _dot(jnp.concatenate([ckvb, krp.astype(BF16)], axis=-1), wk_ref[...])
    v = _dot(ckvb, wv_ref[...])

    def head_norm(x, g):
        outs = []
        for h in range(NH):
            xh = x[:, h * LANE:(h + 1) * LANE]
            ms = jnp.sum(xh * xh, axis=-1, keepdims=True) * (1.0 / MLA_QK)
            outs.append(xh * lax.rsqrt(ms + EPS))
        return jnp.concatenate(outs, axis=-1) * g

    cos = cos_ref[...]
    sin = sin_ref[...]
    half = MLA_ROPE // 2
    lane = lax.broadcasted_iota(jnp.int32, cos.shape, 1)
    first = ((lane % LANE - MLA_NOPE) & (half - 1)) < half // 2

    def rope(x):
        width = x.shape[1]
        partner = jnp.where(first, pltpu.roll(x, width - half // 2, 1), pltpu.roll(x, half // 2, 1))
        return x * cos + partner * sin

    q = rope(head_norm(q, qg_ref[...]))
    k = rope(head_norm(k, kg_ref[...]))
    q_ref[...] = (q * (MLA_QK ** -0.5)).astype(BF16)
    k_ref[...] = k.astype(BF16)
    v_ref[...] = v.astype(BF16)


def _mlaprep(p_mla, qag, kvag, wq, wk, wv, qg, kg, cos_t, sin_t):
    tok = lambda w: pl.BlockSpec((TM, w), lambda i: (i, 0))
    full = lambda shape: pl.BlockSpec(shape, lambda i: (0,) * len(shape))
    pos = pl.BlockSpec((TM, 512), lambda i: (jnp.where(i < LAT_TILES, i % TILES_PER_SEQ, TILES_PER_SEQ), 0))
    return pl.pallas_call(
        _mlaprep_kernel,
        grid=(N_TILES,),
        in_specs=[tok(512), full((1, 256)), full((1, 128)), full((256, 512)), full((256, 512)),
                  full((128, 256)), full((1, 512)), full((1, 512)), pos, pos],
        out_specs=[tok(512), tok(512), tok(GW)],
        out_shape=[jax.ShapeDtypeStruct((T_ALL, 512), BF16), jax.ShapeDtypeStruct((T_ALL, 512), BF16),
                   jax.ShapeDtypeStruct((T_ALL, GW), BF16)],
        compiler_params=_cparams(("parallel",)),
        name="mlaprep",
    )(p_mla, qag, kvag, wq, wk, wv, qg, kg, cos_t, sin_t)


def _rope_tables():
    t = jnp.arange(L)
    rowp = (t // GRID_W).astype(F32)
    colp = (t % GRID_W).astype(F32)
    half = MLA_ROPE // 2
    inv = ROPE_BASE ** (-jnp.arange(0, half, 2, dtype=F32) / half)
    j = jnp.arange(MLA_ROPE)
    pos = jnp.where(j[None, :] < half, rowp[:, None], colp[:, None])
    ang = pos * inv[j % (half // 2)][None, :]
    first = (j % half) < (half // 2)
    cos32 = jnp.cos(ang)
    sin32 = jnp.where(first[None, :], -jnp.sin(ang), jnp.sin(ang))
    cos_h = jnp.concatenate([jnp.ones((L, MLA_NOPE), F32), cos32, jnp.ones((L, 32), F32)], axis=-1)
    sin_h = jnp.concatenate([jnp.zeros((L, MLA_NOPE), F32), sin32, jnp.zeros((L, 32), F32)], axis=-1)
    cos_t = jnp.concatenate([jnp.tile(cos_h, (1, NH)), jnp.ones((TM, 512), F32)], axis=0)
    sin_t = jnp.concatenate([jnp.tile(sin_h, (1, NH)), jnp.zeros((TM, 512), F32)], axis=0)
    return cos_t, sin_t


def _outproj_kernel(x_ref, of_ref, ob_ref, g_ref, hyl_ref, hyc_ref, nal_ref, nac_ref, mll_ref, mlc_ref,
                    mod_ref, ng_ref, gm_ref, w_ref, n2_ref, wr_ref, x1_ref, h2_ref, lg_ref):
    oa = of_ref[...] + ob_ref[...]
    ms = _group_sum(oa * oa, gm_ref[...]) * (1.0 / 64)
    g = g_ref[...]
    oa = oa * lax.rsqrt(ms + EPS) * ng_ref[...] * (g * jax.nn.sigmoid(g))
    lat = pl.program_id(0) < LAT_TILES
    hy = jnp.where(lat, hyl_ref[...], hyc_ref[...])
    na = jnp.where(lat, nal_ref[...], nac_ref[...])
    mla = jnp.where(lat, mll_ref[...], mlc_ref[...])
    mix = jnp.concatenate([oa.astype(BF16), hy, na, mla], axis=-1)
    x1 = x_ref[...] + mod_ref[2:3, :] * _dot(mix, w_ref[...])
    x1_ref[...] = x1
    ms2 = jnp.mean(x1 * x1, axis=-1, keepdims=True)
    h2 = x1 * lax.rsqrt(ms2 + EPS) * n2_ref[...] * (1.0 + mod_ref[4:5, :]) + mod_ref[3:4, :]
    _store_rowtiles(h2_ref, h2)
    hh, hl = _split2(h2)
    wr = wr_ref[...]
    lg_ref[...] = _dot(hh, wr[:, 0:128]) + _dot(hl, wr[:, 0:128]) + _dot(hh, wr[:, 128:256])


def _outproj(n_tiles, X, o_f, o_b, p_hg, hy, na, mla, mod_l, ng, gmb, w_bf, n2g, wr):
    tok = lambda w: pl.BlockSpec((TM, w), lambda i: (i, 0))
    full = lambda shape: pl.BlockSpec(shape, lambda i: (0,) * len(shape))
    latb = pl.BlockSpec((TM, GW), lambda i: (jnp.minimum(i, LAT_TILES - 1), 0))
    ctxb = pl.BlockSpec((TM, GW), lambda i: (jnp.maximum(i - LAT_TILES, 0), 0))
    nt = n_tiles * TM
    return pl.pallas_call(
        _outproj_kernel,
        grid=(n_tiles,),
        in_specs=[tok(D), tok(GW), tok(GW), pl.BlockSpec((TM, GW), lambda i: (i, 4)),
                  latb, ctxb, latb, ctxb, latb, ctxb,
                  pl.BlockSpec((None, 6, D), lambda i: (_seg_of_tile(i), 0, 0)),
                  full((1, GW)), full((GW, GW)), full((D, D)), full((1, D)), full((D, 256))],
        out_specs=[tok(D), pl.BlockSpec((TM * ROW_CH, LANE), lambda i: (i, 0)), tok(128)],
        out_shape=[jax.ShapeDtypeStruct((nt, D), F32), jax.ShapeDtypeStruct((nt * ROW_CH, LANE), F32),
                   jax.ShapeDtypeStruct((nt, 128), F32)],
        compiler_params=_cparams(("parallel",)),
        name="outproj",
    )(X, o_f, o_b, p_hg, hy[0], hy[1], na[0], na[1], mla[0], mla[1], mod_l, ng, gmb, w_bf, n2g, wr)


def _route_kernel(lg_ref, b_ref, ltri_ref, o_ref, cnt_ref, base):
    @pl.when(pl.program_id(0) == 0)
    def _():
        base[...] = jnp.zeros_like(base)

    l = lg_ref[...] + b_ref[...]
    lane_i = lax.broadcasted_iota(jnp.int32, l.shape, 1)
    lane = lane_i.astype(F32)

    def first_max(mask):
        v = jnp.max(jnp.where(mask, l, NEG), axis=-1, keepdims=True)
        i = jnp.min(jnp.where(mask & (l == v), lane, float(LANE)), axis=-1, keepdims=True)
        return v, i

    gmask = lane_i < N_GROUPS
    mg, g_sel = first_max(gmask)
    p_sel = 1.0 / jnp.sum(jnp.where(gmask, jnp.exp(jnp.where(gmask, l - mg, 0.0)), 0.0), axis=-1, keepdims=True)
    lane_group = jnp.right_shift(lane_i - N_GROUPS, EPG.bit_length() - 1).astype(F32)
    emask = (lane_i >= N_GROUPS) & (lane_i < N_GROUPS + N_EXPERTS) & (lane_group == g_sel)
    v1, i1 = first_max(emask)
    v2, i2 = first_max(emask & (lane != i1))
    r = jnp.exp(v2 - v1)
    w1 = p_sel / (1.0 + r)
    w2 = w1 * r
    hit1 = lane == i1
    hit2 = lane == i2
    cnt = jnp.where(hit1 | hit2, 1.0, 0.0)
    before = _dot(ltri_ref[...], cnt.astype(BF16)) + base[...]
    pos1 = jnp.sum(jnp.where(hit1, before, 0.0), axis=-1, keepdims=True)
    pos2 = jnp.sum(jnp.where(hit2, before, 0.0), axis=-1, keepdims=True)
    base[...] = base[...] + jnp.sum(cnt, axis=0, keepdims=True)
    cnt_ref[...] = jnp.broadcast_to(base[...], cnt_ref.shape)
    cols = [i1 - N_GROUPS, i2 - N_GROUPS, w1, w2, pos1, pos2]
    out = jnp.zeros(l.shape, F32)
    for j, c in enumerate(cols):
        out = jnp.where(lane_i == j, c, out)
    o_ref[...] = out


def _route(logits, bg, be):
    t = logits.shape[0]
    bias = jnp.pad(jnp.concatenate([bg, be]), (0, LANE - N_GROUPS - N_EXPERTS)).reshape(1, LANE)
    ltri = (jnp.arange(TM)[None, :] < jnp.arange(TM)[:, None]).astype(BF16)
    out, cnt = pl.pallas_call(
        _route_kernel,
        grid=(t // TM,),
        in_specs=[pl.BlockSpec((TM, LANE), lambda i: (i, 0)), pl.BlockSpec((1, LANE), lambda i: (0, 0)),
                  pl.BlockSpec((TM, TM), lambda i: (0, 0))],
        out_specs=[pl.BlockSpec((TM, LANE), lambda i: (i, 0)), pl.BlockSpec((8, LANE), lambda i: (0, 0))],
        out_shape=[jax.ShapeDtypeStruct((t, LANE), F32), jax.ShapeDtypeStruct((8, LANE), F32)],
        scratch_shapes=[pltpu.VMEM((1, LANE), F32)],
        compiler_params=_cparams(("arbitrary",)),
        name="route",
    )(logits, bias, ltri)
    eid = out[:, 0:2].astype(jnp.int32)
    pos = out[:, 4:6].astype(jnp.int32)
    counts = cnt[0, N_GROUPS:N_GROUPS + N_EXPERTS].astype(jnp.int32)
    return eid, out[:, 2:4], pos, counts


def _dispatch_tables(eid, pos, counts):
    t = eid.shape[0]
    n = 2 * t
    nb = n // TMOE + N_EXPERTS
    flat_e = eid.reshape(n)
    pos = pos.reshape(n)
    pcounts = ((counts + TMOE - 1) // TMOE) * TMOE
    pends = jnp.cumsum(pcounts)
    pstarts = pends - pcounts
    dest = (jnp.take(pstarts, flat_e) + pos).astype(jnp.int32)
    blk_start = jnp.arange(nb, dtype=jnp.int32) * TMOE
    block_e = jnp.minimum(jnp.sum((pends[None, :] <= blk_start[:, None]).astype(jnp.int32), axis=1),
                          N_EXPERTS - 1).astype(jnp.int32)
    nblk = (pends[-1] // TMOE).astype(jnp.int32).reshape(1)
    pads = jnp.concatenate([pstarts + counts, pcounts - counts, nblk]).astype(jnp.int32)
    return block_e, nblk, dest, pads


PAD_PIECES = tuple(1 << b for b in range(TMOE.bit_length() - 1))


def _rows(tok, n):
    if isinstance(tok, int):
        return pl.ds(tok * ROW_CH, n * ROW_CH)
    return pl.ds(pl.multiple_of(tok * ROW_CH, ROW_CH), n * ROW_CH)


def _dispatch_kernel(dest_ref, pad_ref, h_ref, xs_out, zbuf, sem, zsem):
    base = pl.program_id(0) * (2 * TM)

    @pl.when(pl.program_id(0) == 0)
    def _():
        zbuf[...] = jnp.zeros_like(zbuf)
        ztok = zbuf.shape[0] // ROW_CH
        for phase in range(2):
            for e in range(N_EXPERTS):
                off = pad_ref[e]
                npad = pad_ref[N_EXPERTS + e]
                for piece in PAD_PIECES:
                    has = (npad & piece) != 0

                    @pl.when(has)
                    def _(off=off, piece=piece):
                        cp = pltpu.make_async_copy(zbuf.at[pl.ds(0, piece * ROW_CH)], xs_out.at[_rows(off, piece)], zsem)
                        cp.start() if phase == 0 else cp.wait()

                    off = off + jnp.where(has, piece, 0)

            first = pad_ref[2 * N_EXPERTS] * (TMOE // ztok)

            def tail(j, carry):
                cp = pltpu.make_async_copy(zbuf, xs_out.at[_rows(j * ztok, ztok)], zsem)
                cp.start() if phase == 0 else cp.wait()
                return carry

            lax.fori_loop(first, xs_out.shape[0] // zbuf.shape[0], tail, 0)

    def issue(r, carry):
        for k in range(2):
            pltpu.make_async_copy(h_ref.at[_rows(r, 1)], xs_out.at[_rows(dest_ref[base + 2 * r + k], 1)], sem).start()
        return carry

    lax.fori_loop(0, TM, issue, 0, unroll=8)
    pltpu.make_async_copy(xs_out.at[_rows(0, 2 * TM)], xs_out.at[_rows(0, 2 * TM)], sem).wait()


def _dispatch(n_tiles, n_slots, dest, pads, h2):
    grid_spec = pltpu.PrefetchScalarGridSpec(
        num_scalar_prefetch=2,
        grid=(n_tiles,),
        in_specs=[pl.BlockSpec((TM * ROW_CH, LANE), lambda i, dst, pd: (i, 0))],
        out_specs=pl.BlockSpec(memory_space=pl.ANY),
        scratch_shapes=[pltpu.VMEM((TMOE // 2 * ROW_CH, LANE), F32), pltpu.SemaphoreType.DMA(()),
                        pltpu.SemaphoreType.DMA(())],
    )
    return pl.pallas_call(
        _dispatch_kernel,
        grid_spec=grid_spec,
        out_shape=jax.ShapeDtypeStruct((n_slots * ROW_CH, LANE), F32),
        compiler_params=_cparams(("arbitrary",)),
        name="dispatch",
    )(dest, pads, h2)


def _experts_kernel(be_ref, nblk_ref, xs_ref, wg_ref, wu_ref, wd_ref, ys_ref, wgb, wub, wdb):
    i = pl.program_id(0)

    @pl.when((i == 0) | (be_ref[i] != be_ref[jnp.maximum(i - 1, 0)]))
    def _():
        wgb[...] = wg_ref[...].astype(BF16)
        wub[...] = wu_ref[...].astype(BF16)
        wdb[...] = wd_ref[...].astype(BF16)

    @pl.when(i < nblk_ref[0])
    def _():
        x = _load_rowtiles(xs_ref, TMOE).astype(BF16)
        gate = _dot(x, wgb[...])
        up = _dot(x, wub[...])
        act = (gate * jax.nn.sigmoid(gate)) * up
        _store_rowtiles(ys_ref, _dot(act.astype(BF16), wdb[...]))

    @pl.when(i >= nblk_ref[0])
    def _():
        ys_ref[...] = jnp.zeros_like(ys_ref)


def _experts(layer, block_e, nblk, xs, w_gate, w_up, w_down):
    nb = block_e.shape[0]
    used = lambda i, nk: jnp.minimum(i, nk[0] - 1)
    grid_spec = pltpu.PrefetchScalarGridSpec(
        num_scalar_prefetch=2,
        grid=(nb,),
        in_specs=[
            pl.BlockSpec((TMOE * ROW_CH, LANE), lambda i, be, nk: (used(i, nk), 0)),
            pl.BlockSpec((None, None, D, D_EXPERT), lambda i, be, nk: (layer, be[i], 0, 0)),
            pl.BlockSpec((None, None, D, D_EXPERT), lambda i, be, nk: (layer, be[i], 0, 0)),
            pl.BlockSpec((None, None, D_EXPERT, D), lambda i, be, nk: (layer, be[i], 0, 0)),
        ],
        out_specs=pl.BlockSpec((TMOE * ROW_CH, LANE), lambda i, be, nk: (i, 0)),
        scratch_shapes=[pltpu.VMEM((D, D_EXPERT), BF16), pltpu.VMEM((D, D_EXPERT), BF16),
                        pltpu.VMEM((D_EXPERT, D), BF16)],
    )
    return pl.pallas_call(
        _experts_kernel,
        grid_spec=grid_spec,
        out_shape=jax.ShapeDtypeStruct(xs.shape, F32),
        compiler_params=_cparams(("arbitrary",)),
        name="experts",
    )(block_e, nblk, xs, w_gate, w_up, w_down)


def _combine_tile(dest_ref, x_ref, w_ref, mod_ref, ys_hbm, ybuf, sem):
    i = pl.program_id(0)

    def fetch(tile, slot):
        base = tile * (2 * TM)

        def issue(r, carry):
            for k in range(2):
                pltpu.make_async_copy(ys_hbm.at[_rows(dest_ref[base + 2 * r + k], 1)],
                                      ybuf.at[slot, k, _rows(r, 1)], sem.at[slot]).start()
            return carry

        lax.fori_loop(0, TM, issue, 0, unroll=8)

    @pl.when(i == 0)
    def _():
        fetch(0, 0)

    @pl.when(i + 1 < pl.num_programs(0))
    def _():
        fetch(i + 1, (i + 1) % 2)

    slot = i % 2
    for k in range(2):
        pltpu.make_async_copy(ys_hbm.at[_rows(0, TM)], ybuf.at[slot, k], sem.at[slot]).wait()
    w = w_ref[...]
    y = w[:, 0:1] * _load_rowtiles(ybuf.at[slot, 0], TM) + w[:, 1:2] * _load_rowtiles(ybuf.at[slot, 1], TM)
    return x_ref[...] + mod_ref[5:6, :] * y


def _combine_kernel(dest_ref, x_ref, w_ref, mod_ref, ys_hbm, o_ref, ybuf, sem):
    o_ref[...] = _combine_tile(dest_ref, x_ref, w_ref, mod_ref, ys_hbm, ybuf, sem)


def _combine_in_specs():
    return [pl.BlockSpec((TM, D), lambda i, dst: (i, 0)),
            pl.BlockSpec((TM, 2), lambda i, dst: (i, 0)),
            pl.BlockSpec((None, 6, D), lambda i, dst: (_seg_of_tile(i), 0, 0)),
            pl.BlockSpec(memory_space=pl.ANY)]


COMBINE_SCRATCH = [pltpu.VMEM((2, 2, TM * ROW_CH, LANE), F32), pltpu.SemaphoreType.DMA((2,))]


def _combine(n_tiles, dest, X1, wts, mod_l, ys):
    grid_spec = pltpu.PrefetchScalarGridSpec(
        num_scalar_prefetch=1,
        grid=(n_tiles,),
        in_specs=_combine_in_specs(),
        out_specs=pl.BlockSpec((TM, D), lambda i, dst: (i, 0)),
        scratch_shapes=COMBINE_SCRATCH,
    )
    return pl.pallas_call(
        _combine_kernel,
        grid_spec=grid_spec,
        out_shape=jax.ShapeDtypeStruct((n_tiles * TM, D), F32),
        compiler_params=_cparams(("arbitrary",)),
        name="combine",
    )(dest, X1, wts, mod_l, ys)


def _combine_inproj_kernel(dest_ref, x_ref, w_ref, modp_ref, ys_hbm, mod_ref, g_ref, win_ref,
                           x_out, o_hg, o_hy, o_na, o_mla, ybuf, sem):
    x = _combine_tile(dest_ref, x_ref, w_ref, modp_ref, ys_hbm, ybuf, sem)
    x_out[...] = x
    _inproj_tile(x, mod_ref, g_ref, win_ref, (o_hg, o_hy, o_na, o_mla))


def _combine_inproj(dest, X1, wts, mod_prev, ys, mod_l, g, w_bf):
    p_specs, p_shapes = _inproj_out_specs(lambda i, dst: (i, 0))
    grid_spec = pltpu.PrefetchScalarGridSpec(
        num_scalar_prefetch=1,
        grid=(N_TILES,),
        in_specs=_combine_in_specs() + [
            pl.BlockSpec((None, 6, D), lambda i, dst: (_seg_of_tile(i), 0, 0)),
            pl.BlockSpec((1, D), lambda i, dst: (0, 0)),
            pl.BlockSpec((D, D_IN_PAD), lambda i, dst: (0, 0)),
        ],
        out_specs=[pl.BlockSpec((TM, D), lambda i, dst: (i, 0))] + p_specs,
        scratch_shapes=COMBINE_SCRATCH,
    )
    return pl.pallas_call(
        _combine_inproj_kernel,
        grid_spec=grid_spec,
        out_shape=[jax.ShapeDtypeStruct((T_ALL, D), F32)] + p_shapes,
        compiler_params=_cparams(("arbitrary",)),
        name="combine_inproj",
    )(dest, X1, wts, mod_prev, ys, mod_l, g, w_bf)


def _group_mask(width, group):
    lane = jnp.arange(width)
    return (lane[:, None] // group == lane[None, :] // group)


def _hgrn_tri(reverse):
    t = jnp.arange(HCHUNK)
    same = (t[:, None] // HSUB) == (t[None, :] // HSUB)
    order = (t[None, :] >= t[:, None]) if reverse else (t[None, :] <= t[:, None])
    return jnp.stack([same & order, order]).astype(BF16)


def _mla_weights(w_uq, w_ukv, q_g, k_g):
    wq = jnp.pad(w_uq.reshape(MLA_Q_RANK, NH, MLA_QK), ((0, 0), (0, 0), (0, 128 - MLA_QK))).reshape(MLA_Q_RANK, 512)
    kv = w_ukv.reshape(MLA_KV_RANK, NH, MLA_NOPE + 64)
    wk_top = jnp.pad(kv[:, :, :MLA_NOPE], ((0, 0), (0, 0), (0, 128 - MLA_NOPE))).reshape(MLA_KV_RANK, 512)
    lane = jnp.arange(512)
    src = jnp.arange(128)
    place = ((lane[None, :] % 128) == (src[:, None] + MLA_NOPE)) & (src[:, None] < MLA_ROPE)
    wk = jnp.concatenate([wk_top, place.astype(F32)], axis=0)
    wv = kv[:, :, MLA_NOPE:].reshape(MLA_KV_RANK, GW)
    pad_g = lambda g: jnp.tile(jnp.pad(g, (0, 128 - MLA_QK)), NH).reshape(1, 512)
    return wq.astype(BF16), wk.astype(BF16), wv.astype(BF16), pad_g(q_g), pad_g(k_g)


def kernel(x, c, ctx, c_ctx, w_ada, b_ada, norm1_g, norm2_g, w_in, w_out, hgrn_lb_logits, hgrn_norm_g,
           hy_short_w, hy_short_b, hy_w1, hy_b1, hy_freq, hy_w2, hy_b2, hy_w3, hy_b3, hy_decay, hy_bias,
           na_rpb, na_q_g, na_k_g, mla_q_a_g, mla_kv_a_g, mla_w_uq, mla_w_ukv, mla_q_g, mla_k_g,
           moe_wg, moe_bg, moe_we, moe_be, moe_w_gate, moe_w_up, moe_w_down):
    X = jnp.concatenate([x.reshape(T_LAT, D), ctx.reshape(T_CTX, D)], axis=0)
    cmat = jnp.concatenate([c, c_ctx[None, :], jnp.zeros((16 - B - 1, D), F32)], axis=0)
    mod = _adaln(cmat, w_ada, b_ada).reshape(DEPTH, 16, 6, D)

    lb_cum = jnp.cumsum(jax.nn.softmax(hgrn_lb_logits.astype(F32), axis=0), axis=0)
    lower = lb_cum - lb_cum[0:1]

    gm64 = _group_mask(GW, 64)
    gm64_f = gm64.astype(F32)
    gm64_b = gm64.astype(BF16)
    trif = _hgrn_tri(False)
    trib = _hgrn_tri(True)
    cos_t, sin_t = _rope_tables()
    dft = {}
    for n in (L, CTX):
        cm, sm = _dft_consts(n)
        chi, clo = _split2(cm)
        shi, slo = _split2(sm)
        dft[n] = (chi, clo, shi, slo, _hyena_feats(n))

    pending = None
    for l in range(DEPTH):
        mod_l = mod[l]
        w_in_l = jnp.pad(w_in[l], ((0, 0), (0, D_IN_PAD - D_IN))).astype(BF16)
        if pending is None:
            p_hg, p_hy, p_na, p_mla = _inproj(X, mod_l, norm1_g[l].reshape(1, D), w_in_l)
        else:
            X, p_hg, p_hy, p_na, p_mla = _combine_inproj(*pending, mod_l, norm1_g[l].reshape(1, D), w_in_l)

        lb = lower[l]
        hconst = jnp.concatenate([
            jnp.stack([jnp.maximum(jnp.log(lb[d]), NEG), jnp.log1p(-lb[d]), 1.0 - lb[d]]) for d in range(2)
        ] + [jnp.zeros((2, GW), F32)], axis=0)
        o_f, o_b = _hgrn(p_hg, hconst, trif, trib, gm64_f, gm64_b)

        need_ctx = l < DEPTH - 1
        w1p = jnp.pad(hy_w1[l], ((0, 128 - HYENA_EMB), (0, 0)))
        o_hy = []
        for n, blk0 in ((L, 0), (CTX, T_LAT // CTX)):
            if n == CTX and not need_ctx:
                continue
            chi, clo, shi, slo, feats = dft[n]
            e, o, knq = _hyfilt(feats, w1p, hy_b1[l].reshape(1, -1), hy_freq[l].reshape(1, -1), hy_w2[l],
                                hy_b2[l].reshape(1, -1), hy_w3[l], hy_b3[l].reshape(1, -1),
                                hy_decay[l].reshape(1, 4 * GW))
            kre, kim = _hyspec(chi, clo, shi, slo, e, o)
            o_hy.append(_hyena(p_hy, blk0, B, n, hy_short_w[l], hy_short_b[l].reshape(1, -1), hy_bias[l],
                               chi, shi, kre, kim, knq))

        qn, kn, vn = _naprep(p_na, jnp.tile(na_q_g[l], NH).reshape(1, GW), jnp.tile(na_k_g[l], NH).reshape(1, GW),
                             gm64_b)
        o_na = [_na(qn, kn, vn, _na_bias_table(na_rpb[l]))]

        wq, wk, wv, qg, kg = _mla_weights(mla_w_uq[l], mla_w_ukv[l], mla_q_g[l], mla_k_g[l])
        mq, mk, mv = _mlaprep(p_mla, mla_q_a_g[l].reshape(1, -1), mla_kv_a_g[l].reshape(1, -1), wq, wk, wv,
                              qg, kg, cos_t, sin_t)
        o_mla = [_attn_latent(mq, mk, mv, 128, 64, 512)]
        if need_ctx:
            o_na.append(_attn_ctx(qn, kn, vn, 64, 64))
            o_mla.append(_attn_ctx(mq, mk, mv, 128, 64))
        else:
            o_hy.append(o_hy[0])
            o_na.append(o_na[0])
            o_mla.append(o_mla[0])
        n_tiles = N_TILES if need_ctx else LAT_TILES

        wr = jnp.pad(jnp.concatenate([moe_wg[l], moe_we[l]], axis=1), ((0, 0), (0, 128 - N_GROUPS - N_EXPERTS)))
        wr_hi, wr_lo = _split2(wr)
        X1, h2, logits = _outproj(n_tiles, X, o_f, o_b, p_hg, o_hy, o_na, o_mla, mod_l,
                                  jnp.tile(hgrn_norm_g[l], NH).reshape(1, GW), gm64_b,
                                  w_out[l].astype(BF16), norm2_g[l].reshape(1, D),
                                  jnp.concatenate([wr_hi, wr_lo], axis=1))

        eid, wts, pos, counts = _route(logits, moe_bg[l], moe_be[l])
        block_e, nblk, dest, pads = _dispatch_tables(eid, pos, counts)
        xs = _dispatch(n_tiles, block_e.shape[0] * TMOE, dest, pads, h2)
        ys = _experts(l, block_e, nblk, xs, moe_w_gate, moe_w_up, moe_w_down)
        pending = (dest, X1, wts, mod_l, ys)

    return _combine(LAT_TILES, *pending).reshape(B, L, D)
```

```python
import functools
import math

import jax
import jax.numpy as jnp
from jax import lax
from jax.experimental import pallas as pl
from jax.experimental.pallas import tpu as pltpu

F32 = jnp.float32
BF16 = jnp.bfloat16

D = 1024
B = 8
L = 2048
CTX = 256
DEPTH = 4
GRID_W = 64
EPS = 1e-6
GW = 256
NH = 4
HYENA_BANDS = 16
HYENA_EMB = 1 + 2 * HYENA_BANDS
HYENA_FFN = 64
NA_ROWS = 8
NA_COLS = 16
MLA_Q_RANK = 256
MLA_KV_RANK = 128
MLA_NOPE = 64
MLA_ROPE = 32
MLA_QK = MLA_NOPE + MLA_ROPE
ROPE_BASE = 10000.0
N_GROUPS = 4
EPG = 8
N_EXPERTS = N_GROUPS * EPG
D_EXPERT = 512
D_IN = 3232
D_IN_PAD = 3328

T_LAT = B * L
T_CTX = B * CTX
T_ALL = T_LAT + T_CTX

TM = 512
N_TILES = T_ALL // TM
LAT_TILES = T_LAT // TM
TILES_PER_SEQ = L // TM
HCHUNK = 64
HSUB = 16
HGRN_SAFE_DECAY = 80.0
TMOE = 256
VMEM_LIMIT_BYTES = 56 * 1024 * 1024
NEG = -1e30

HI = lax.Precision.HIGHEST


def _cparams(sem, vmem=VMEM_LIMIT_BYTES):
    return pltpu.CompilerParams(dimension_semantics=sem, vmem_limit_bytes=vmem)


def _seg_of_tile(i):
    return jnp.where(i < LAT_TILES, i // TILES_PER_SEQ, B)


def _dot(a, b):
    return jnp.dot(a, b, preferred_element_type=F32)


def _dot_nt(a, b):
    return lax.dot_general(a, b, (((1,), (1,)), ((), ())), preferred_element_type=F32)


def _dot_tn(a, b):
    return lax.dot_general(a, b, (((0,), (0,)), ((), ())), preferred_element_type=F32)


def _split2(x):
    hi = x.astype(BF16)
    lo = (x - hi.astype(F32)).astype(BF16)
    return hi, lo


def _split3(x):
    h1 = x.astype(BF16)
    r1 = x - h1.astype(F32)
    h2 = r1.astype(BF16)
    h3 = (r1 - h2.astype(F32)).astype(BF16)
    return h1, h2, h3


LANE = 128
ROW_CH = D // LANE


def _store_rowtiles(ref, val):
    n = val.shape[0]
    for j in range(ROW_CH):
        ref[pl.ds(j, n, stride=ROW_CH), :] = val[:, j * LANE:(j + 1) * LANE]


def _load_rowtiles(ref, n):
    return jnp.concatenate([ref[pl.ds(j, n, stride=ROW_CH), :] for j in range(ROW_CH)], axis=-1)


def _norm2(x1, n2_ref, mod_ref):
    ms2 = jnp.mean(x1 * x1, axis=-1, keepdims=True)
    return x1 * lax.rsqrt(ms2 + EPS) * n2_ref[...] * (1.0 + mod_ref[4:5, :]) + mod_ref[3:4, :]


def _group_sum(x, gm):
    hi, lo = _split2(x)
    return _dot(hi, gm) + _dot(lo, gm)


def _ada_kernel(c_ref, w_ref, b_ref, o_ref):
    cc = c_ref[...]
    sc = cc * jax.nn.sigmoid(cc)
    o_ref[0] = jnp.dot(sc, w_ref[0], preferred_element_type=F32, precision=HI) + b_ref[0]


def _adaln(cmat, w_ada, b_ada):
    tn = 1536
    return pl.pallas_call(
        _ada_kernel,
        grid=(DEPTH, 6 * D // tn),
        in_specs=[
            pl.BlockSpec((16, D), lambda l, j: (0, 0)),
            pl.BlockSpec((1, D, tn), lambda l, j: (l, 0, j)),
            pl.BlockSpec((1, 1, tn), lambda l, j: (l, 0, j)),
        ],
        out_specs=pl.BlockSpec((1, 16, tn), lambda l, j: (l, 0, j)),
        out_shape=jax.ShapeDtypeStruct((DEPTH, 16, 6 * D), F32),
        compiler_params=_cparams(("arbitrary", "arbitrary")),
        name="adaln",
    )(cmat, w_ada, b_ada.reshape(DEPTH, 1, 6 * D))


IN_WIDTHS = (1280, 768, 768, 512)


def _inproj_tile(x, mod_ref, g_ref, w_ref, outs):
    ms = jnp.mean(x * x, axis=-1, keepdims=True)
    y = x * lax.rsqrt(ms + EPS) * g_ref[...]
    h = y * (1.0 + mod_ref[1:2, :]) + mod_ref[0:1, :]
    p = _dot(h.astype(BF16), w_ref[...])
    c0 = 0
    for o_ref, w in zip(outs, IN_WIDTHS):
        o_ref[...] = p[:, c0:c0 + w]
        c0 += w


def _inproj_kernel(x_ref, mod_ref, g_ref, w_ref, o_hg, o_hy, o_na, o_mla):
    _inproj_tile(x_ref[...], mod_ref, g_ref, w_ref, (o_hg, o_hy, o_na, o_mla))


def _inproj_out_specs(index_map):
    specs = [pl.BlockSpec((TM, w), index_map) for w in IN_WIDTHS]
    shapes = [jax.ShapeDtypeStruct((T_ALL, w), F32) for w in IN_WIDTHS]
    return specs, shapes


def _inproj(X, mod_l, g, w_bf):
    out_specs, out_shape = _inproj_out_specs(lambda i: (i, 0))
    return pl.pallas_call(
        _inproj_kernel,
        grid=(N_TILES,),
        in_specs=[
            pl.BlockSpec((TM, D), lambda i: (i, 0)),
            pl.BlockSpec((None, 6, D), lambda i: (_seg_of_tile(i), 0, 0)),
            pl.BlockSpec((1, D), lambda i: (0, 0)),
            pl.BlockSpec((D, D_IN_PAD), lambda i: (0, 0)),
        ],
        out_specs=out_specs,
        out_shape=out_shape,
        compiler_params=_cparams(("parallel",)),
        name="inproj",
    )(X, mod_l, g, w_bf)


HSTEP = 4 * HCHUNK


def _hgrn_prologue(p_ref, r0, zcol, c_ref, crow, tri_ref, reverse):
    q = p_ref[r0:r0 + HCHUNK, 0:GW]
    z = p_ref[r0:r0 + HCHUNK, zcol:zcol + GW]
    v = p_ref[r0:r0 + HCHUNK, 3 * GW:4 * GW]
    la = c_ref[crow:crow + 1, :]
    l1 = c_ref[crow + 1:crow + 2, :]
    oml = c_ref[crow + 2:crow + 3, :]
    e = jnp.exp(-jnp.abs(z))
    ope = 1.0 + e
    ls = jnp.minimum(z, 0.0) - jnp.log(ope)
    c2 = l1 + ls
    logf = jnp.maximum(la, c2) + jnp.log(1.0 + jnp.exp(-jnp.abs(la - c2)))
    kk = oml * (jnp.where(z >= 0.0, e, 1.0) / ope)
    h1, h2, h3 = _split3(logf)
    tri_full = tri_ref[1]
    bfull = _dot(tri_full, h1) + _dot(tri_full, h2) + _dot(tri_full, h3)
    half = HCHUNK // 2
    first, mid, last = (HCHUNK - 1, half, 0) if reverse else (0, half - 1, HCHUNK - 1)
    btot = bfull[last:last + 1]
    bmid = bfull[mid:mid + 1]
    worst = jnp.maximum(bfull[first:first + 1] - bmid, bmid - btot)
    return dict(r0=r0, q=q, kk=kk, v=v, splits=(h1, h2, h3), bfull=bfull, btot=btot, bmid=bmid, worst=worst)


def _hgrn_fast(c, st, gm, reverse):
    q, kk, v, bfull, btot, bmid = c["q"], c["kk"], c["v"], c["bfull"], c["btot"], c["bmid"]
    lane_head = lax.broadcasted_iota(jnp.int32, (HCHUNK, GW), 1) // 64
    qi = (q * jnp.exp(bfull - bmid)).astype(BF16)
    ke = (kk * jnp.exp(bmid - bfull)).astype(BF16)
    qx = jnp.concatenate([jnp.where(lane_head == h, qi, jnp.zeros_like(qi)) for h in range(NH)], axis=0)
    a = _dot_nt(qx, ke)
    t_idx = lax.broadcasted_iota(jnp.int32, a.shape, 0) % HCHUNK
    s_idx = lax.broadcasted_iota(jnp.int32, a.shape, 1)
    seen = (s_idx >= t_idx) if reverse else (s_idx <= t_idx)
    a = jnp.where(seen, a, 0.0).astype(BF16)
    vb = v.astype(BF16)
    o_all = _dot(a, vb)
    o = _dot_nt((q * jnp.exp(bfull)).astype(BF16), st.astype(BF16))
    for h in range(NH):
        o = o + jnp.where(lane_head == h, o_all[h * HCHUNK:(h + 1) * HCHUNK, :], 0.0)
    kd = (kk * jnp.exp(btot - bfull)).astype(BF16)
    return o, st * jnp.exp(btot) + _dot_tn(vb, kd) * gm


def _hgrn_slow(c, st, gm, gmb, tri_sub, reverse, o_ref):
    q, kk, v = c["q"], c["kk"], c["v"]
    h1, h2, h3 = c["splits"]
    bsub = _dot(tri_sub, h1) + _dot(tri_sub, h2) + _dot(tri_sub, h3)
    row = lax.broadcasted_iota(jnp.int32, (HSUB, GW), 0)
    order = range(HCHUNK // HSUB - 1, -1, -1) if reverse else range(HCHUNK // HSUB)
    for blk in order:
        r0 = blk * HSUB
        b_i = bsub[r0:r0 + HSUB]
        q_i = q[r0:r0 + HSUB]
        k_i = kk[r0:r0 + HSUB]
        v_i = v[r0:r0 + HSUB]
        bt_i = b_i[0:1] if reverse else b_i[HSUB - 1:HSUB]
        qe = (q_i * jnp.exp(b_i)).astype(BF16)
        o_inter = _dot_nt(qe, st.astype(BF16))
        parts = []
        for tl in range(HSUB):
            dlt = b_i[tl:tl + 1] - b_i
            valid = (row >= tl) if reverse else (row <= tl)
            w = jnp.exp(jnp.where(valid, dlt, NEG))
            parts.append((q_i[tl:tl + 1] * w) * k_i)
        pmat = jnp.concatenate(parts, axis=0).astype(BF16)
        abar = _dot(pmat, gmb)
        o_diag = jnp.sum(abar.reshape(HSUB, HSUB, GW) * v_i[None], axis=1)
        o_ref[c["r0"] + r0:c["r0"] + r0 + HSUB, :] = o_inter + o_diag
        kd = (k_i * jnp.exp(bt_i - b_i)).astype(BF16)
        upd = _dot_tn(v_i.astype(BF16), kd)
        st = st * jnp.exp(bt_i) + upd * gm
    return st


def _hgrn_kernel(pf_ref, pb_ref, c_ref, trif_ref, trib_ref, gm_ref, gmb_ref, of_ref, ob_ref, stf, stb):
    @pl.when(pl.program_id(1) == 0)
    def _():
        stf[...] = jnp.zeros_like(stf)
        stb[...] = jnp.zeros_like(stb)

    gm = gm_ref[...]
    dirs = []
    for p_ref, o_ref, st_ref, tri_ref, zcol, crow, reverse in (
            (pf_ref, of_ref, stf, trif_ref, GW, 0, False), (pb_ref, ob_ref, stb, trib_ref, 2 * GW, 3, True)):
        offs = tuple(range(0, HSTEP, HCHUNK))
        offs = offs[::-1] if reverse else offs
        chunks = [_hgrn_prologue(p_ref, r0, zcol, c_ref, crow, tri_ref, reverse) for r0 in offs]
        dirs.append((o_ref, st_ref, tri_ref, reverse, chunks))
    worst = functools.reduce(jnp.maximum, [c["worst"] for d in dirs for c in d[4]])
    safe = jnp.max(worst) < HGRN_SAFE_DECAY

    @pl.when(safe)
    def _():
        for o_ref, st_ref, tri_ref, reverse, chunks in dirs:
            st = st_ref[...]
            for c in chunks:
                o, st = _hgrn_fast(c, st, gm, reverse)
                o_ref[c["r0"]:c["r0"] + HCHUNK, :] = o
            st_ref[...] = st

    @pl.when(jnp.logical_not(safe))
    def _():
        gmb = gmb_ref[...]
        for o_ref, st_ref, tri_ref, reverse, chunks in dirs:
            st = st_ref[...]
            for c in chunks:
                st = _hgrn_slow(c, st, gm, gmb, tri_ref[0], reverse, o_ref)
            st_ref[...] = st


def _hgrn_block(b, n, reverse):
    nctx = CTX // HSTEP
    nlat = L // HSTEP
    jc = (nctx - 1 - n) if reverse else n
    jl = (nlat - 1 - (n - nctx)) if reverse else (n - nctx)
    return jnp.where(n < nctx, T_LAT // HSTEP + b * nctx + jc, b * nlat + jl)


def _hgrn(p_hg, consts, trif, trib, gm, gmb):
    nsteps = (CTX + L) // HSTEP
    full = lambda shape: pl.BlockSpec(shape, lambda b, n: (0,) * len(shape))
    return pl.pallas_call(
        _hgrn_kernel,
        grid=(B, nsteps),
        in_specs=[
            pl.BlockSpec((HSTEP, 1280), lambda b, n: (_hgrn_block(b, n, False), 0)),
            pl.BlockSpec((HSTEP, 1280), lambda b, n: (_hgrn_block(b, n, True), 0)),
            full((8, GW)),
            full((2, HCHUNK, HCHUNK)),
            full((2, HCHUNK, HCHUNK)),
            full((GW, GW)),
            full((GW, GW)),
        ],
        out_specs=[
            pl.BlockSpec((HSTEP, GW), lambda b, n: (_hgrn_block(b, n, False), 0)),
            pl.BlockSpec((HSTEP, GW), lambda b, n: (_hgrn_block(b, n, True), 0)),
        ],
        out_shape=[jax.ShapeDtypeStruct((T_ALL, GW), F32), jax.ShapeDtypeStruct((T_ALL, GW), F32)],
        scratch_shapes=[pltpu.VMEM((GW, GW), F32), pltpu.VMEM((GW, GW), F32)],
        compiler_params=_cparams(("arbitrary", "arbitrary")),
        name="hgrn",
    )(p_hg, p_hg, consts, trif, trib, gm, gmb)


def _alt_sum(x):
    n, c = x.shape
    sgn = jnp.where((lax.broadcasted_iota(jnp.int32, (n, c), 0) & 1) == 0, 1.0, -1.0)
    return jnp.sum(x * sgn, axis=0, keepdims=True)


def _hyfilt_kernel(feats_ref, w1_ref, b1_ref, fr_ref, w2_ref, b2_ref, w3_ref, b3_ref, dec_ref,
                   e_ref, o_ref, nq_ref):
    fr = fr_ref[...]
    feats = feats_ref[...]
    h = jnp.sin(fr * (jnp.dot(feats, w1_ref[...], preferred_element_type=F32, precision=HI) + b1_ref[...]))
    h = jnp.sin(fr * (jnp.dot(h, w2_ref[...], preferred_element_type=F32, precision=HI) + b2_ref[...]))
    filt = jnp.dot(h, w3_ref[...], preferred_element_type=F32, precision=HI) + b3_ref[...]
    filt = filt * jnp.exp(-feats[:, 0:1] * dec_ref[...])
    n = filt.shape[0]
    row = lax.broadcasted_iota(jnp.int32, (n, GW), 0)
    for o in range(2):
        fwd = filt[:, (2 * o) * GW:(2 * o + 1) * GW]
        bwd = jnp.where(row >= 1, filt[:, (2 * o + 1) * GW:(2 * o + 2) * GW], 0.0)
        ssq = jnp.sum(fwd * fwd + bwd * bwd, axis=0, keepdims=True)
        scale = lax.rsqrt(ssq + EPS)
        ev = (fwd + bwd) * scale
        e_ref[:, o * GW:(o + 1) * GW] = ev
        o_ref[:, o * GW:(o + 1) * GW] = (fwd - bwd) * scale
        nq_ref[:, o * GW:(o + 1) * GW] = _alt_sum(ev) * (0.5 / n)


def _hyfilt(feats, w1p, b1, fr, w2, b2, w3, b3, dec):
    n = feats.shape[0]
    return pl.pallas_call(
        _hyfilt_kernel,
        out_shape=[jax.ShapeDtypeStruct((n, 2 * GW), F32), jax.ShapeDtypeStruct((n, 2 * GW), F32),
                   jax.ShapeDtypeStruct((1, 2 * GW), F32)],
        compiler_params=_cparams(None),
        name="hyfilt",
    )(feats, w1p, b1, fr, w2, b2, w3, b3, dec)


def _hyspec_kernel(chi_ref, clo_ref, shi_ref, slo_ref, e_ref, o_ref, kre_ref, kim_ref, *, n):
    eh, el = _split2(e_ref[...])
    oh, ol = _split2(o_ref[...])
    kre = _dot(chi_ref[...], eh) + _dot(chi_ref[...], el) + _dot(clo_ref[...], eh)
    kim = _dot(shi_ref[...], oh) + _dot(shi_ref[...], ol) + _dot(slo_ref[...], oh)
    tr = kre.shape[0]
    grow = lax.broadcasted_iota(jnp.int32, kre.shape, 0) + pl.program_id(0) * tr
    s2 = 1.0 / n
    kre_ref[...] = kre * jnp.where(grow == 0, 0.5 * s2, s2)
    kim_ref[...] = kim * s2


def _hyspec(chi, clo, shi, slo, e, o):
    n = e.shape[0]
    tr = min(256, n)
    rows = pl.BlockSpec((tr, n), lambda i: (i, 0))
    full = pl.BlockSpec((n, 2 * GW), lambda i: (0, 0))
    outb = pl.BlockSpec((tr, 2 * GW), lambda i: (i, 0))
    return pl.pallas_call(
        functools.partial(_hyspec_kernel, n=n),
        grid=(n // tr,),
        in_specs=[rows, rows, rows, rows, full, full],
        out_specs=[outb, outb],
        out_shape=[jax.ShapeDtypeStruct((n, 2 * GW), F32)] * 2,
        compiler_params=_cparams(("parallel",)),
        name="hyspec",
    )(chi, clo, shi, slo, e, o)


def _hyena_kernel(u_ref, sw_ref, sb_ref, db_ref, c_ref, s_ref, kre_ref, kim_ref, knq_ref, o_ref,
                  z_scr, zb_scr, y_scr):
    n = u_ref.shape[0]
    ft = min(512, n)
    rc = min(256, n)
    nchunks = n // rc
    lrow = lax.broadcasted_iota(jnp.int32, (rc, GW), 0)
    sgn = jnp.where((lrow & 1) == 0, 1.0, -1.0)

    def short_conv(part, c):
        sl = slice(part * GW, (part + 1) * GW)
        r0 = c * rc
        u = u_ref[r0:r0 + rc, sl]
        prev = u_ref[r0 - 1:r0, sl] if c > 0 else jnp.zeros((1, GW), F32)
        nxt = u_ref[r0 + rc:r0 + rc + 1, sl] if c < nchunks - 1 else jnp.zeros((1, GW), F32)
        up = jnp.where(lrow == 0, prev, pltpu.roll(u, 1, 0))
        un = jnp.where(lrow == rc - 1, nxt, pltpu.roll(u, rc - 1, 0))
        return sw_ref[0:1, sl] * up + sw_ref[1:2, sl] * u + sw_ref[2:3, sl] * un + sb_ref[:, sl]

    for c in range(nchunks):
        z_scr[c * rc:(c + 1) * rc, :] = short_conv(0, c)
    for o in range(2):
        cols = slice(o * GW, (o + 1) * GW)
        znq = jnp.zeros((1, GW), F32)
        for c in range(nchunks):
            zc = z_scr[c * rc:(c + 1) * rc, :]
            zb_scr[c * rc:(c + 1) * rc, :] = zc.astype(BF16)
            znq = znq + jnp.sum(zc * sgn, axis=0, keepdims=True)
        ynq = znq * knq_ref[:, cols]
        for c in range(nchunks):
            y_scr[c * rc:(c + 1) * rc, :] = sgn * ynq
        for f in range(n // ft):
            rs = slice(f * ft, (f + 1) * ft)
            zre = _dot(c_ref[rs, :], zb_scr[...])
            zim = _dot(s_ref[rs, :], zb_scr[...])
            kre = kre_ref[rs, cols]
            kim = kim_ref[rs, cols]
            yre = (zre * kre - zim * kim).astype(BF16)
            yim = (zre * kim + zim * kre).astype(BF16)
            y_scr[...] += _dot(c_ref[:, rs], yre) + _dot(s_ref[:, rs], yim)
        dst = o_ref if o == 1 else z_scr
        for c in range(nchunks):
            rows = slice(c * rc, (c + 1) * rc)
            zn = short_conv(o + 1, c) * (y_scr[rows, :] + db_ref[o:o + 1, :] * z_scr[rows, :])
            dst[rows, :] = zn.astype(dst.dtype)


def _hyena(u, blk0, nb, n, sw, sb, db, cm, sm, kre, kim, knq):
    whole = pl.BlockSpec(memory_space=pltpu.VMEM)
    return pl.pallas_call(
        _hyena_kernel,
        grid=(nb,),
        in_specs=[
            pl.BlockSpec((n, 3 * GW), lambda b: (blk0 + b, 0)),
            pl.BlockSpec((3, 3 * GW), lambda b: (0, 0)),
            pl.BlockSpec((1, 3 * GW), lambda b: (0, 0)),
            pl.BlockSpec((2, GW), lambda b: (0, 0)),
            whole, whole, whole, whole, whole,
        ],
        out_specs=pl.BlockSpec((n, GW), lambda b: (b, 0)),
        out_shape=jax.ShapeDtypeStruct((nb * n, GW), BF16),
        scratch_shapes=[pltpu.VMEM((n, GW), F32), pltpu.VMEM((n, GW), BF16), pltpu.VMEM((n, GW), F32)],
        compiler_params=_cparams(("arbitrary",)),
        name="hyena",
    )(u, sw, sb, db, cm, sm, kre, kim, knq)


def _dft_consts(n):
    kk = jnp.arange(n, dtype=jnp.int32)
    ph = (kk[:, None] * kk[None, :]) % (2 * n)
    ang = ph.astype(F32) * (math.pi / n)
    return jnp.cos(ang), -jnp.sin(ang)


def _hyena_feats(n):
    t = jnp.arange(n, dtype=F32)
    t_unit = jnp.linspace(0.0, 1.0, n, dtype=F32)
    bands = jnp.linspace(1e-4, HYENA_BANDS - 1, HYENA_BANDS, dtype=F32)
    ang = (2.0 * math.pi / n) * t[:, None] * bands[None, :]
    feats = jnp.concatenate([t_unit[:, None], jnp.cos(ang), -jnp.sin(ang)], axis=-1)
    return jnp.pad(feats, ((0, 0), (0, 128 - HYENA_EMB)))


def _naprep_kernel(p_ref, qg_ref, kg_ref, gm_ref, q_ref, k_ref, v_ref):
    p = p_ref[...]
    q = p[:, 0:GW]
    k = p[:, GW:2 * GW]
    gm = gm_ref[...]
    qn = q * lax.rsqrt(_group_sum(q * q, gm) * (1.0 / 64) + EPS) * qg_ref[...]
    kn = k * lax.rsqrt(_group_sum(k * k, gm) * (1.0 / 64) + EPS) * kg_ref[...]
    q_ref[...] = (qn * (64 ** -0.5)).astype(BF16)
    k_ref[...] = kn.astype(BF16)
    v_ref[...] = p[:, 2 * GW:3 * GW].astype(BF16)


def _naprep(p_na, qg, kg, gmb):
    tok = lambda w: pl.BlockSpec((TM, w), lambda i: (i, 0))
    full = lambda shape: pl.BlockSpec(shape, lambda i: (0,) * len(shape))
    return pl.pallas_call(
        _naprep_kernel,
        grid=(N_TILES,),
        in_specs=[tok(768), full((1, GW)), full((1, GW)), full((GW, GW))],
        out_specs=[tok(GW), tok(GW), tok(GW)],
        out_shape=[jax.ShapeDtypeStruct((T_ALL, GW), BF16)] * 3,
        compiler_params=_cparams(("parallel",)),
        name="naprep",
    )(p_na, qg, kg, gmb)


NA_RPS = 8


def _na_kernel(q_ref, k_ref, v_ref, kc_ref, vc_ref, bias_ref, o_ref):
    rows = L // GRID_W
    kc = kc_ref[...]
    vc = vc_ref[...]
    for j in range(NA_RPS):
        r = pl.program_id(1) * NA_RPS + j
        rs = jnp.clip(r - NA_ROWS // 2, 0, rows - NA_ROWS)
        variant = r - rs
        start = pl.multiple_of(rs * GRID_W, GRID_W)
        kw = k_ref[pl.ds(start, NA_ROWS * GRID_W), :]
        vw = v_ref[pl.ds(start, NA_ROWS * GRID_W), :]
        q = q_ref[j * GRID_W:(j + 1) * GRID_W, :]
        lane_head = lax.broadcasted_iota(jnp.int32, (GRID_W, GW), 1) // 64
        hmask = [lane_head == h for h in range(NH)]
        qx = jnp.concatenate([jnp.where(hmask[h], q, jnp.zeros_like(q)) for h in range(NH)], axis=0)
        s_loc = _dot_nt(qx, kw) + bias_ref[variant]
        s_ctx = _dot_nt(qx, kc)
        m = jnp.maximum(jnp.max(s_loc, axis=-1, keepdims=True), jnp.max(s_ctx, axis=-1, keepdims=True))
        p_loc = jnp.exp(s_loc - m)
        p_ctx = jnp.exp(s_ctx - m)
        den = jnp.sum(p_loc, axis=-1, keepdims=True) + jnp.sum(p_ctx, axis=-1, keepdims=True)
        o_all = (_dot(p_loc.astype(BF16), vw) + _dot(p_ctx.astype(BF16), vc)) / den
        o = jnp.zeros((GRID_W, GW), F32)
        for h in range(NH):
            o = o + jnp.where(hmask[h], o_all[h * GRID_W:(h + 1) * GRID_W, :], 0.0)
        o_ref[j * GRID_W:(j + 1) * GRID_W, :] = o.astype(o_ref.dtype)


def _na(qn, kn, vn, bias_t):
    steps = L // GRID_W // NA_RPS
    tq = NA_RPS * GRID_W
    ctx_blk = T_LAT // CTX
    return pl.pallas_call(
        _na_kernel,
        grid=(B, steps),
        in_specs=[
            pl.BlockSpec((tq, GW), lambda b, r: (b * steps + r, 0)),
            pl.BlockSpec((L, GW), lambda b, r: (b, 0)),
            pl.BlockSpec((L, GW), lambda b, r: (b, 0)),
            pl.BlockSpec((CTX, GW), lambda b, r: (ctx_blk + b, 0)),
            pl.BlockSpec((CTX, GW), lambda b, r: (ctx_blk + b, 0)),
            pl.BlockSpec((NA_ROWS, NH * GRID_W, NA_ROWS * GRID_W), lambda b, r: (0, 0, 0)),
        ],
        out_specs=pl.BlockSpec((tq, GW), lambda b, r: (b * steps + r, 0)),
        out_shape=jax.ShapeDtypeStruct((T_LAT, GW), BF16),
        compiler_params=_cparams(("arbitrary", "arbitrary")),
        name="na",
    )(qn, kn, vn, kn, vn, bias_t)


def _na_bias_table(rpb):
    cq = jnp.arange(GRID_W)
    cs = jnp.clip(cq - NA_COLS // 2, 0, GRID_W - NA_COLS)
    col_ok = (cq[None, :] >= cs[:, None]) & (cq[None, :] < cs[:, None] + NA_COLS)
    dc = jnp.clip(cq[None, :] - cq[:, None] + (NA_COLS - 1), 0, 2 * NA_COLS - 2)
    onehot = (dc[:, :, None] == jnp.arange(2 * NA_COLS - 1)[None, None, :]).astype(F32)
    full = jnp.einsum('qkc,hrc->hrqk', onehot, rpb.astype(F32), precision=HI)
    full = jnp.where(col_ok[None, None], full, NEG)
    tab = jnp.stack([full[:, NA_ROWS - 1 - a:2 * NA_ROWS - 1 - a] for a in range(NA_ROWS)], axis=0)
    return tab.transpose(0, 1, 3, 2, 4).reshape(NA_ROWS, NH * GRID_W, NA_ROWS * GRID_W)


def _attn_kernel(*refs, nkv, dq, dv):
    q = refs[0][...]
    ks = [refs[1 + 2 * j][...] for j in range(nkv)]
    vs = [refs[2 + 2 * j][...] for j in range(nkv)]
    o_ref = refs[1 + 2 * nkv]
    outs = []
    for h in range(NH):
        qh = q[:, h * dq:(h + 1) * dq]
        ss = [_dot_nt(qh, k[:, h * dq:(h + 1) * dq]) for k in ks]
        m = functools.reduce(jnp.maximum, [jnp.max(s, axis=-1, keepdims=True) for s in ss])
        ps = [jnp.exp(s - m) for s in ss]
        den = functools.reduce(lambda a, b2: a + b2, [jnp.sum(p, axis=-1, keepdims=True) for p in ps])
        o = functools.reduce(lambda a, b2: a + b2,
                             [_dot(p.astype(BF16), v[:, h * dv:(h + 1) * dv]) for p, v in zip(ps, vs)])
        outs.append(o / den)
    o_ref[...] = jnp.concatenate(outs, axis=-1).astype(o_ref.dtype)


def _attn_latent(q, k, v, dq, dv, tq):
    nq = L // tq
    ctx_blk = T_LAT // CTX
    return pl.pallas_call(
        functools.partial(_attn_kernel, nkv=2, dq=dq, dv=dv),
        grid=(B, nq),
        in_specs=[
            pl.BlockSpec((tq, NH * dq), lambda b, i: (b * nq + i, 0)),
            pl.BlockSpec((L, NH * dq), lambda b, i: (b, 0)),
            pl.BlockSpec((L, NH * dv), lambda b, i: (b, 0)),
            pl.BlockSpec((CTX, NH * dq), lambda b, i: (ctx_blk + b, 0)),
            pl.BlockSpec((CTX, NH * dv), lambda b, i: (ctx_blk + b, 0)),
        ],
        out_specs=pl.BlockSpec((tq, NH * dv), lambda b, i: (b * nq + i, 0)),
        out_shape=jax.ShapeDtypeStruct((T_LAT, NH * dv), BF16),
        compiler_params=_cparams(("arbitrary", "arbitrary")),
        name="attn_latent",
    )(q, k, v, k, v)


def _attn_ctx(q, k, v, dq, dv):
    ctx_blk = T_LAT // CTX
    return pl.pallas_call(
        functools.partial(_attn_kernel, nkv=1, dq=dq, dv=dv),
        grid=(B,),
        in_specs=[
            pl.BlockSpec((CTX, NH * dq), lambda b: (ctx_blk + b, 0)),
            pl.BlockSpec((CTX, NH * dq), lambda b: (ctx_blk + b, 0)),
            pl.BlockSpec((CTX, NH * dv), lambda b: (ctx_blk + b, 0)),
        ],
        out_specs=pl.BlockSpec((CTX, NH * dv), lambda b: (b, 0)),
        out_shape=jax.ShapeDtypeStruct((T_CTX, NH * dv), BF16),
        compiler_params=_cparams(("arbitrary",)),
        name="attn_ctx",
    )(q, k, v)


def _mlaprep_kernel(p_ref, qag_ref, kvag_ref, wq_ref, wk_ref, wv_ref, qg_ref, kg_ref,
                    cos_ref, sin_ref, q_ref, k_ref, v_ref):
    p = p_ref[...]
    cq = p[:, 0:MLA_Q_RANK]
    ckv = p[:, MLA_Q_RANK:MLA_Q_RANK + MLA_KV_RANK]
    krp = p[:, MLA_Q_RANK + MLA_KV_RANK:]
    cqn = cq * lax.rsqrt(jnp.mean(cq * cq, axis=-1, keepdims=True) + EPS) * qag_ref[...]
    ckvn = ckv * lax.rsqrt(jnp.mean(ckv * ckv, axis=-1, keepdims=True) + EPS) * kvag_ref[...]
    ckvb = ckvn.astype(BF16)
    q = _dot(cqn.astype(BF16), wq_ref[...])
    k = _dot(jnp.concatenate([ckvb, krp.astype(BF16)], axis=-1), wk_ref[...])
    v = _dot(ckvb, wv_ref[...])

    def head_norm(x, g):
        outs = []
        for h in range(NH):
            xh = x[:, h * LANE:(h + 1) * LANE]
            ms = jnp.sum(xh * xh, axis=-1, keepdims=True) * (1.0 / MLA_QK)
            outs.append(xh * lax.rsqrt(ms + EPS))
        return jnp.concatenate(outs, axis=-1) * g

    cos = cos_ref[...]
    sin = sin_ref[...]
    half = MLA_ROPE // 2
    lane = lax.broadcasted_iota(jnp.int32, cos.shape, 1)
    first = ((lane % LANE - MLA_NOPE) & (half - 1)) < half // 2

    def rope(x):
        width = x.shape[1]
        partner = jnp.where(first, pltpu.roll(x, width - half // 2, 1), pltpu.roll(x, half // 2, 1))
        return x * cos + partner * sin

    q = rope(head_norm(q, qg_ref[...]))
    k = rope(head_norm(k, kg_ref[...]))
    q_ref[...] = (q * (MLA_QK ** -0.5)).astype(BF16)
    k_ref[...] = k.astype(BF16)
    v_ref[...] = v.astype(BF16)


def _mlaprep(p_mla, qag, kvag, wq, wk, wv, qg, kg, cos_t, sin_t):
    tok = lambda w: pl.BlockSpec((TM, w), lambda i: (i, 0))
    full = lambda shape: pl.BlockSpec(shape, lambda i: (0,) * len(shape))
    pos = pl.BlockSpec((TM, 512), lambda i: (jnp.where(i < LAT_TILES, i % TILES_PER_SEQ, TILES_PER_SEQ), 0))
    return pl.pallas_call(
        _mlaprep_kernel,
        grid=(N_TILES,),
        in_specs=[tok(512), full((1, 256)), full((1, 128)), full((256, 512)), full((256, 512)),
                  full((128, 256)), full((1, 512)), full((1, 512)), pos, pos],
        out_specs=[tok(512), tok(512), tok(GW)],
        out_shape=[jax.ShapeDtypeStruct((T_ALL, 512), BF16), jax.ShapeDtypeStruct((T_ALL, 512), BF16),
                   jax.ShapeDtypeStruct((T_ALL, GW), BF16)],
        compiler_params=_cparams(("parallel",)),
        name="mlaprep",
    )(p_mla, qag, kvag, wq, wk, wv, qg, kg, cos_t, sin_t)


def _rope_tables():
    t = jnp.arange(L)
    rowp = (t // GRID_W).astype(F32)
    colp = (t % GRID_W).astype(F32)
    half = MLA_ROPE // 2
    inv = ROPE_BASE ** (-jnp.arange(0, half, 2, dtype=F32) / half)
    j = jnp.arange(MLA_ROPE)
    pos = jnp.where(j[None, :] < half, rowp[:, None], colp[:, None])
    ang = pos * inv[j % (half // 2)][None, :]
    first = (j % half) < (half // 2)
    cos32 = jnp.cos(ang)
    sin32 = jnp.where(first[None, :], -jnp.sin(ang), jnp.sin(ang))
    cos_h = jnp.concatenate([jnp.ones((L, MLA_NOPE), F32), cos32, jnp.ones((L, 32), F32)], axis=-1)
    sin_h = jnp.concatenate([jnp.zeros((L, MLA_NOPE), F32), sin32, jnp.zeros((L, 32), F32)], axis=-1)
    cos_t = jnp.concatenate([jnp.tile(cos_h, (1, NH)), jnp.ones((TM, 512), F32)], axis=0)
    sin_t = jnp.concatenate([jnp.tile(sin_h, (1, NH)), jnp.zeros((TM, 512), F32)], axis=0)
    return cos_t, sin_t


def _outproj_kernel(x_ref, of_ref, ob_ref, g_ref, hyl_ref, hyc_ref, nal_ref, nac_ref, mll_ref, mlc_ref,
                    mod_ref, ng_ref, gm_ref, w_ref, n2_ref, wr_ref, x1_ref, lg_ref):
    oa = of_ref[...] + ob_ref[...]
    ms = _group_sum(oa * oa, gm_ref[...]) * (1.0 / 64)
    g = g_ref[...]
    oa = oa * lax.rsqrt(ms + EPS) * ng_ref[...] * (g * jax.nn.sigmoid(g))
    lat = pl.program_id(0) < LAT_TILES
    hy = jnp.where(lat, hyl_ref[...], hyc_ref[...])
    na = jnp.where(lat, nal_ref[...], nac_ref[...])
    mla = jnp.where(lat, mll_ref[...], mlc_ref[...])
    mix = jnp.concatenate([oa.astype(BF16), hy, na, mla], axis=-1)
    x1 = x_ref[...] + mod_ref[2:3, :] * _dot(mix, w_ref[...])
    x1_ref[...] = x1
    hh, hl = _split2(_norm2(x1, n2_ref, mod_ref))
    wr = wr_ref[...]
    lg_ref[...] = _dot(hh, wr[:, 0:128]) + _dot(hl, wr[:, 0:128]) + _dot(hh, wr[:, 128:256])


def _outproj(n_tiles, X, o_f, o_b, p_hg, hy, na, mla, mod_l, ng, gmb, w_bf, n2g, wr):
    tok = lambda w: pl.BlockSpec((TM, w), lambda i: (i, 0))
    full = lambda shape: pl.BlockSpec(shape, lambda i: (0,) * len(shape))
    latb = pl.BlockSpec((TM, GW), lambda i: (jnp.minimum(i, LAT_TILES - 1), 0))
    ctxb = pl.BlockSpec((TM, GW), lambda i: (jnp.maximum(i - LAT_TILES, 0), 0))
    nt = n_tiles * TM
    return pl.pallas_call(
        _outproj_kernel,
        grid=(n_tiles,),
        in_specs=[tok(D), tok(GW), tok(GW), pl.BlockSpec((TM, GW), lambda i: (i, 4)),
                  latb, ctxb, latb, ctxb, latb, ctxb,
                  pl.BlockSpec((None, 6, D), lambda i: (_seg_of_tile(i), 0, 0)),
                  full((1, GW)), full((GW, GW)), full((D, D)), full((1, D)), full((D, 256))],
        out_specs=[tok(D), tok(128)],
        out_shape=[jax.ShapeDtypeStruct((nt, D), F32), jax.ShapeDtypeStruct((nt, 128), F32)],
        compiler_params=_cparams(("parallel",)),
        name="outproj",
    )(X, o_f, o_b, p_hg, hy[0], hy[1], na[0], na[1], mla[0], mla[1], mod_l, ng, gmb, w_bf, n2g, wr)


def _route_kernel(lg_ref, b_ref, ltri_ref, o_ref, cnt_ref, base):
    @pl.when(pl.program_id(0) == 0)
    def _():
        base[...] = jnp.zeros_like(base)

    l = lg_ref[...] + b_ref[...]
    lane_i = lax.broadcasted_iota(jnp.int32, l.shape, 1)
    lane = lane_i.astype(F32)

    def first_max(mask):
        v = jnp.max(jnp.where(mask, l, NEG), axis=-1, keepdims=True)
        i = jnp.min(jnp.where(mask & (l == v), lane, float(LANE)), axis=-1, keepdims=True)
        return v, i

    gmask = lane_i < N_GROUPS
    mg, g_sel = first_max(gmask)
    p_sel = 1.0 / jnp.sum(jnp.where(gmask, jnp.exp(jnp.where(gmask, l - mg, 0.0)), 0.0), axis=-1, keepdims=True)
    lane_group = jnp.right_shift(lane_i - N_GROUPS, EPG.bit_length() - 1).astype(F32)
    emask = (lane_i >= N_GROUPS) & (lane_i < N_GROUPS + N_EXPERTS) & (lane_group == g_sel)
    v1, i1 = first_max(emask)
    v2, i2 = first_max(emask & (lane != i1))
    r = jnp.exp(v2 - v1)
    w1 = p_sel / (1.0 + r)
    w2 = w1 * r
    hit1 = lane == i1
    hit2 = lane == i2
    cnt = jnp.where(hit1 | hit2, 1.0, 0.0)
    before = _dot(ltri_ref[...], cnt.astype(BF16)) + base[...]
    pos1 = jnp.sum(jnp.where(hit1, before, 0.0), axis=-1, keepdims=True)
    pos2 = jnp.sum(jnp.where(hit2, before, 0.0), axis=-1, keepdims=True)
    base[...] = base[...] + jnp.sum(cnt, axis=0, keepdims=True)
    cnt_ref[...] = jnp.broadcast_to(base[...], cnt_ref.shape)
    cols = [i1 - N_GROUPS, i2 - N_GROUPS, w1, w2, pos1, pos2]
    out = jnp.zeros(l.shape, F32)
    for j, c in enumerate(cols):
        out = jnp.where(lane_i == j, c, out)
    o_ref[...] = out


def _route(logits, bg, be):
    t = logits.shape[0]
    bias = jnp.pad(jnp.concatenate([bg, be]), (0, LANE - N_GROUPS - N_EXPERTS)).reshape(1, LANE)
    ltri = (jnp.arange(TM)[None, :] < jnp.arange(TM)[:, None]).astype(BF16)
    out, cnt = pl.pallas_call(
        _route_kernel,
        grid=(t // TM,),
        in_specs=[pl.BlockSpec((TM, LANE), lambda i: (i, 0)), pl.BlockSpec((1, LANE), lambda i: (0, 0)),
                  pl.BlockSpec((TM, TM), lambda i: (0, 0))],
        out_specs=[pl.BlockSpec((TM, LANE), lambda i: (i, 0)), pl.BlockSpec((8, LANE), lambda i: (0, 0))],
        out_shape=[jax.ShapeDtypeStruct((t, LANE), F32), jax.ShapeDtypeStruct((8, LANE), F32)],
        scratch_shapes=[pltpu.VMEM((1, LANE), F32)],
        compiler_params=_cparams(("arbitrary",)),
        name="route",
    )(logits, bias, ltri)
    eid = out[:, 0:2].astype(jnp.int32)
    pos = out[:, 4:6].astype(jnp.int32)
    counts = cnt[0, N_GROUPS:N_GROUPS + N_EXPERTS].astype(jnp.int32)
    return eid, out[:, 2:4], pos, counts


def _dispatch_tables(eid, pos, counts):
    t = eid.shape[0]
    n = 2 * t
    nb = n // TMOE + N_EXPERTS
    flat_e = eid.reshape(n)
    pos = pos.reshape(n)
    onehot = (flat_e[:, None] == jnp.arange(N_EXPERTS, dtype=jnp.int32)[None, :]).astype(jnp.int32)
    pcounts = ((counts + TMOE - 1) // TMOE) * TMOE
    pends = jnp.cumsum(pcounts)
    pstarts = pends - pcounts
    dest = (jnp.sum(jnp.where(onehot > 0, pstarts[None, :], 0), axis=1) + pos).astype(jnp.int32)
    blk_start = jnp.arange(nb, dtype=jnp.int32) * TMOE
    block_e = jnp.minimum(jnp.sum((pends[None, :] <= blk_start[:, None]).astype(jnp.int32), axis=1),
                          N_EXPERTS - 1).astype(jnp.int32)
    nblk = (pends[-1] // TMOE).astype(jnp.int32).reshape(1)
    pads = jnp.concatenate([pstarts + counts, pcounts - counts, nblk]).astype(jnp.int32)
    return block_e, nblk, dest, pads


PAD_PIECES = tuple(1 << b for b in range(TMOE.bit_length() - 1))


def _rows(tok, n):
    if isinstance(tok, int):
        return pl.ds(tok * ROW_CH, n * ROW_CH)
    return pl.ds(pl.multiple_of(tok * ROW_CH, ROW_CH), n * ROW_CH)


def _dispatch_kernel(dest_ref, pad_ref, x1_ref, mod_ref, n2_ref, xs_out, h_ref, zbuf, sem, zsem):
    base = pl.program_id(0) * (2 * TM)
    _store_rowtiles(h_ref, _norm2(x1_ref[...], n2_ref, mod_ref))

    @pl.when(pl.program_id(0) == 0)
    def _():
        zbuf[...] = jnp.zeros_like(zbuf)
        ztok = zbuf.shape[0] // ROW_CH
        for phase in range(2):
            for e in range(N_EXPERTS):
                off = pad_ref[e]
                npad = pad_ref[N_EXPERTS + e]
                for piece in PAD_PIECES:
                    has = (npad & piece) != 0

                    @pl.when(has)
                    def _(off=off, piece=piece):
                        cp = pltpu.make_async_copy(zbuf.at[pl.ds(0, piece * ROW_CH)], xs_out.at[_rows(off, piece)], zsem)
                        cp.start() if phase == 0 else cp.wait()

                    off = off + jnp.where(has, piece, 0)

            first = pad_ref[2 * N_EXPERTS] * (TMOE // ztok)

            def tail(j, carry):
                cp = pltpu.make_async_copy(zbuf, xs_out.at[_rows(j * ztok, ztok)], zsem)
                cp.start() if phase == 0 else cp.wait()
                return carry

            lax.fori_loop(first, xs_out.shape[0] // zbuf.shape[0], tail, 0)

    def issue(r, carry):
        for k in range(2):
            pltpu.make_async_copy(h_ref.at[_rows(r, 1)], xs_out.at[_rows(dest_ref[base + 2 * r + k], 1)], sem).start()
        return carry

    lax.fori_loop(0, TM, issue, 0, unroll=8)
    pltpu.make_async_copy(xs_out.at[_rows(0, 2 * TM)], xs_out.at[_rows(0, 2 * TM)], sem).wait()


def _dispatch(n_tiles, n_slots, dest, pads, X1, mod_l, n2g):
    grid_spec = pltpu.PrefetchScalarGridSpec(
        num_scalar_prefetch=2,
        grid=(n_tiles,),
        in_specs=[pl.BlockSpec((TM, D), lambda i, dst, pd: (i, 0)),
                  pl.BlockSpec((None, 6, D), lambda i, dst, pd: (_seg_of_tile(i), 0, 0)),
                  pl.BlockSpec((1, D), lambda i, dst, pd: (0, 0))],
        out_specs=pl.BlockSpec(memory_space=pl.ANY),
        scratch_shapes=[pltpu.VMEM((TM * ROW_CH, LANE), F32), pltpu.VMEM((TMOE // 2 * ROW_CH, LANE), F32),
                        pltpu.SemaphoreType.DMA(()), pltpu.SemaphoreType.DMA(())],
    )
    return pl.pallas_call(
        _dispatch_kernel,
        grid_spec=grid_spec,
        out_shape=jax.ShapeDtypeStruct((n_slots * ROW_CH, LANE), F32),
        compiler_params=_cparams(("arbitrary",)),
        name="dispatch",
    )(dest, pads, X1, mod_l, n2g)


def _experts_kernel(be_ref, nblk_ref, xs_ref, wg_ref, wu_ref, wd_ref, ys_ref, wgb, wub, wdb):
    i = pl.program_id(0)

    @pl.when((i == 0) | (be_ref[i] != be_ref[jnp.maximum(i - 1, 0)]))
    def _():
        wgb[...] = wg_ref[...].astype(BF16)
        wub[...] = wu_ref[...].astype(BF16)
        wdb[...] = wd_ref[...].astype(BF16)

    @pl.when(i < nblk_ref[0])
    def _():
        x = _load_rowtiles(xs_ref, TMOE).astype(BF16)
        gate = _dot(x, wgb[...])
        up = _dot(x, wub[...])
        act = (gate * jax.nn.sigmoid(gate)) * up
        _store_rowtiles(ys_ref, _dot(act.astype(BF16), wdb[...]))

    @pl.when(i >= nblk_ref[0])
    def _():
        ys_ref[...] = jnp.zeros_like(ys_ref)


def _experts(layer, block_e, nblk, xs, w_gate, w_up, w_down):
    nb = block_e.shape[0]
    used = lambda i, nk: jnp.minimum(i, nk[0] - 1)
    grid_spec = pltpu.PrefetchScalarGridSpec(
        num_scalar_prefetch=2,
        grid=(nb,),
        in_specs=[
            pl.BlockSpec((TMOE * ROW_CH, LANE), lambda i, be, nk: (used(i, nk), 0)),
            pl.BlockSpec((None, None, D, D_EXPERT), lambda i, be, nk: (layer, be[i], 0, 0)),
            pl.BlockSpec((None, None, D, D_EXPERT), lambda i, be, nk: (layer, be[i], 0, 0)),
            pl.BlockSpec((None, None, D_EXPERT, D), lambda i, be, nk: (layer, be[i], 0, 0)),
        ],
        out_specs=pl.BlockSpec((TMOE * ROW_CH, LANE), lambda i, be, nk: (i, 0)),
        scratch_shapes=[pltpu.VMEM((D, D_EXPERT), BF16), pltpu.VMEM((D, D_EXPERT), BF16),
                        pltpu.VMEM((D_EXPERT, D), BF16)],
    )
    return pl.pallas_call(
        _experts_kernel,
        grid_spec=grid_spec,
        out_shape=jax.ShapeDtypeStruct(xs.shape, F32),
        compiler_params=_cparams(("arbitrary",)),
        name="experts",
    )(block_e, nblk, xs, w_gate, w_up, w_down)


def _combine_tile(dest_ref, x_ref, w_ref, mod_ref, ys_hbm, ybuf, sem):
    i = pl.program_id(0)

    def fetch(tile, slot):
        base = tile * (2 * TM)

        def issue(r, carry):
            for k in range(2):
                pltpu.make_async_copy(ys_hbm.at[_rows(dest_ref[base + 2 * r + k], 1)],
                                      ybuf.at[slot, k, _rows(r, 1)], sem.at[slot]).start()
            return carry

        lax.fori_loop(0, TM, issue, 0, unroll=8)

    @pl.when(i == 0)
    def _():
        fetch(0, 0)

    @pl.when(i + 1 < pl.num_programs(0))
    def _():
        fetch(i + 1, (i + 1) % 2)

    slot = i % 2
    for k in range(2):
        pltpu.make_async_copy(ys_hbm.at[_rows(0, TM)], ybuf.at[slot, k], sem.at[slot]).wait()
    w = w_ref[...]
    y = w[:, 0:1] * _load_rowtiles(ybuf.at[slot, 0], TM) + w[:, 1:2] * _load_rowtiles(ybuf.at[slot, 1], TM)
    return x_ref[...] + mod_ref[5:6, :] * y


def _combine_kernel(dest_ref, x_ref, w_ref, mod_ref, ys_hbm, o_ref, ybuf, sem):
    o_ref[...] = _combine_tile(dest_ref, x_ref, w_ref, mod_ref, ys_hbm, ybuf, sem)


def _combine_in_specs():
    return [pl.BlockSpec((TM, D), lambda i, dst: (i, 0)),
            pl.BlockSpec((TM, 2), lambda i, dst: (i, 0)),
            pl.BlockSpec((None, 6, D), lambda i, dst: (_seg_of_tile(i), 0, 0)),
            pl.BlockSpec(memory_space=pl.ANY)]


COMBINE_SCRATCH = [pltpu.VMEM((2, 2, TM * ROW_CH, LANE), F32), pltpu.SemaphoreType.DMA((2,))]


def _combine(n_tiles, dest, X1, wts, mod_l, ys):
    grid_spec = pltpu.PrefetchScalarGridSpec(
        num_scalar_prefetch=1,
        grid=(n_tiles,),
        in_specs=_combine_in_specs(),
        out_specs=pl.BlockSpec((TM, D), lambda i, dst: (i, 0)),
        scratch_shapes=COMBINE_SCRATCH,
    )
    return pl.pallas_call(
        _combine_kernel,
        grid_spec=grid_spec,
        out_shape=jax.ShapeDtypeStruct((n_tiles * TM, D), F32),
        compiler_params=_cparams(("arbitrary",)),
        name="combine",
    )(dest, X1, wts, mod_l, ys)


def _combine_inproj_kernel(dest_ref, x_ref, w_ref, modp_ref, ys_hbm, mod_ref, g_ref, win_ref,
                           x_out, o_hg, o_hy, o_na, o_mla, ybuf, sem):
    x = _combine_tile(dest_ref, x_ref, w_ref, modp_ref, ys_hbm, ybuf, sem)
    x_out[...] = x
    _inproj_tile(x, mod_ref, g_ref, win_ref, (o_hg, o_hy, o_na, o_mla))


def _combine_inproj(dest, X1, wts, mod_prev, ys, mod_l, g, w_bf):
    p_specs, p_shapes = _inproj_out_specs(lambda i, dst: (i, 0))
    grid_spec = pltpu.PrefetchScalarGridSpec(
        num_scalar_prefetch=1,
        grid=(N_TILES,),
        in_specs=_combine_in_specs() + [
            pl.BlockSpec((None, 6, D), lambda i, dst: (_seg_of_tile(i), 0, 0)),
            pl.BlockSpec((1, D), lambda i, dst: (0, 0)),
            pl.BlockSpec((D, D_IN_PAD), lambda i, dst: (0, 0)),
        ],
        out_specs=[pl.BlockSpec((TM, D), lambda i, dst: (i, 0))] + p_specs,
        scratch_shapes=COMBINE_SCRATCH,
    )
    return pl.pallas_call(
        _combine_inproj_kernel,
        grid_spec=grid_spec,
        out_shape=[jax.ShapeDtypeStruct((T_ALL, D), F32)] + p_shapes,
        compiler_params=_cparams(("arbitrary",)),
        name="combine_inproj",
    )(dest, X1, wts, mod_prev, ys, mod_l, g, w_bf)


def _group_mask(width, group):
    lane = jnp.arange(width)
    return (lane[:, None] // group == lane[None, :] // group)


def _hgrn_tri(reverse):
    t = jnp.arange(HCHUNK)
    same = (t[:, None] // HSUB) == (t[None, :] // HSUB)
    order = (t[None, :] >= t[:, None]) if reverse else (t[None, :] <= t[:, None])
    return jnp.stack([same & order, order]).astype(BF16)


def _mla_weights(w_uq, w_ukv, q_g, k_g):
    wq = jnp.pad(w_uq.reshape(MLA_Q_RANK, NH, MLA_QK), ((0, 0), (0, 0), (0, 128 - MLA_QK))).reshape(MLA_Q_RANK, 512)
    kv = w_ukv.reshape(MLA_KV_RANK, NH, MLA_NOPE + 64)
    wk_top = jnp.pad(kv[:, :, :MLA_NOPE], ((0, 0), (0, 0), (0, 128 - MLA_NOPE))).reshape(MLA_KV_RANK, 512)
    lane = jnp.arange(512)
    src = jnp.arange(128)
    place = ((lane[None, :] % 128) == (src[:, None] + MLA_NOPE)) & (src[:, None] < MLA_ROPE)
    wk = jnp.concatenate([wk_top, place.astype(F32)], axis=0)
    wv = kv[:, :, MLA_NOPE:].reshape(MLA_KV_RANK, GW)
    pad_g = lambda g: jnp.tile(jnp.pad(g, (0, 128 - MLA_QK)), NH).reshape(1, 512)
    return wq.astype(BF16), wk.astype(BF16), wv.astype(BF16), pad_g(q_g), pad_g(k_g)


def kernel(x, c, ctx, c_ctx, w_ada, b_ada, norm1_g, norm2_g, w_in, w_out, hgrn_lb_logits, hgrn_norm_g,
           hy_short_w, hy_short_b, hy_w1, hy_b1, hy_freq, hy_w2, hy_b2, hy_w3, hy_b3, hy_decay, hy_bias,
           na_rpb, na_q_g, na_k_g, mla_q_a_g, mla_kv_a_g, mla_w_uq, mla_w_ukv, mla_q_g, mla_k_g,
           moe_wg, moe_bg, moe_we, moe_be, moe_w_gate, moe_w_up, moe_w_down):
    X = jnp.concatenate([x.reshape(T_LAT, D), ctx.reshape(T_CTX, D)], axis=0)
    cmat = jnp.concatenate([c, c_ctx[None, :], jnp.zeros((16 - B - 1, D), F32)], axis=0)
    mod = _adaln(cmat, w_ada, b_ada).reshape(DEPTH, 16, 6, D)

    lb_cum = jnp.cumsum(jax.nn.softmax(hgrn_lb_logits.astype(F32), axis=0), axis=0)
    lower = lb_cum - lb_cum[0:1]

    gm64 = _group_mask(GW, 64)
    gm64_f = gm64.astype(F32)
    gm64_b = gm64.astype(BF16)
    trif = _hgrn_tri(False)
    trib = _hgrn_tri(True)
    cos_t, sin_t = _rope_tables()
    dft = {}
    for n in (L, CTX):
        cm, sm = _dft_consts(n)
        chi, clo = _split2(cm)
        shi, slo = _split2(sm)
        dft[n] = (chi, clo, shi, slo, _hyena_feats(n))

    pending = None
    for l in range(DEPTH):
        mod_l = mod[l]
        w_in_l = jnp.pad(w_in[l], ((0, 0), (0, D_IN_PAD - D_IN))).astype(BF16)
        if pending is None:
            p_hg, p_hy, p_na, p_mla = _inproj(X, mod_l, norm1_g[l].reshape(1, D), w_in_l)
        else:
            X, p_hg, p_hy, p_na, p_mla = _combine_inproj(*pending, mod_l, norm1_g[l].reshape(1, D), w_in_l)

        lb = lower[l]
        hconst = jnp.concatenate([
            jnp.stack([jnp.maximum(jnp.log(lb[d]), NEG), jnp.log1p(-lb[d]), 1.0 - lb[d]]) for d in range(2)
        ] + [jnp.zeros((2, GW), F32)], axis=0)
        o_f, o_b = _hgrn(p_hg, hconst, trif, trib, gm64_f, gm64_b)

        need_ctx = l < DEPTH - 1
        w1p = jnp.pad(hy_w1[l], ((0, 128 - HYENA_EMB), (0, 0)))
        o_hy = []
        for n, blk0 in ((L, 0), (CTX, T_LAT // CTX)):
            if n == CTX and not need_ctx:
                continue
            chi, clo, shi, slo, feats = dft[n]
            e, o, knq = _hyfilt(feats, w1p, hy_b1[l].reshape(1, -1), hy_freq[l].reshape(1, -1), hy_w2[l],
                                hy_b2[l].reshape(1, -1), hy_w3[l], hy_b3[l].reshape(1, -1),
                                hy_decay[l].reshape(1, 4 * GW))
            kre, kim = _hyspec(chi, clo, shi, slo, e, o)
            o_hy.append(_hyena(p_hy, blk0, B, n, hy_short_w[l], hy_short_b[l].reshape(1, -1), hy_bias[l],
                               chi, shi, kre, kim, knq))

        qn, kn, vn = _naprep(p_na, jnp.tile(na_q_g[l], NH).reshape(1, GW), jnp.tile(na_k_g[l], NH).reshape(1, GW),
                             gm64_b)
        o_na = [_na(qn, kn, vn, _na_bias_table(na_rpb[l]))]

        wq, wk, wv, qg, kg = _mla_weights(mla_w_uq[l], mla_w_ukv[l], mla_q_g[l], mla_k_g[l])
        mq, mk, mv = _mlaprep(p_mla, mla_q_a_g[l].reshape(1, -1), mla_kv_a_g[l].reshape(1, -1), wq, wk, wv,
                              qg, kg, cos_t, sin_t)
        o_mla = [_attn_latent(mq, mk, mv, 128, 64, 512)]
        if need_ctx:
            o_na.append(_attn_ctx(qn, kn, vn, 64, 64))
            o_mla.append(_attn_ctx(mq, mk, mv, 128, 64))
        else:
            o_hy.append(o_hy[0])
            o_na.append(o_na[0])
            o_mla.append(o_mla[0])
        n_tiles = N_TILES if need_ctx else LAT_TILES

        wr = jnp.pad(jnp.concatenate([moe_wg[l], moe_we[l]], axis=1), ((0, 0), (0, 128 - N_GROUPS - N_EXPERTS)))
        wr_hi, wr_lo = _split2(wr)
        X1, logits = _outproj(n_tiles, X, o_f, o_b, p_hg, o_hy, o_na, o_mla, mod_l,
                                  jnp.tile(hgrn_norm_g[l], NH).reshape(1, GW), gm64_b,
                                  w_out[l].astype(BF16), norm2_g[l].reshape(1, D),
                                  jnp.concatenate([wr_hi, wr_lo], axis=1))

        eid, wts, pos, counts = _route(logits, moe_bg[l], moe_be[l])
        block_e, nblk, dest, pads = _dispatch_tables(eid, pos, counts)
        xs = _dispatch(n_tiles, block_e.shape[0] * TMOE, dest, pads, X1, mod_l, norm2_g[l].reshape(1, D))
        ys = _experts(l, block_e, nblk, xs, moe_w_gate, moe_w_up, moe_w_down)
        pending = (dest, X1, wts, mod_l, ys)

    return _combine(LAT_TILES, *pending).reshape(B, L, D)
```
